```python
import jax
import jax.numpy as jnp
from jax import lax
import numpy as np

D_MODEL = 1024
BATCH = 8
SEQ = 16384
DEPTH = 4

CHUNK = 64
N_META = 16
META_PAD = CHUNK - N_META
D_CONV = 512
CONV_WIDTH = 31
HG_HEADS = 4
HG_DK = 128
HG_DV = 128
D_HG_K = HG_HEADS * HG_DK
D_HG = HG_HEADS * HG_DV
F_FLOOR = 1e-30
ATT_Q_HEADS = 8
ATT_KV_HEADS = 2
ATT_HEAD_DIM = 64
ATT_GROUP = ATT_Q_HEADS // ATT_KV_HEADS
D_ATT = ATT_Q_HEADS * ATT_HEAD_DIM
D_KV = ATT_KV_HEADS * ATT_HEAD_DIM
WINDOW = 128
WINDOW_CHUNKS = WINDOW // CHUNK
N_BRANCH = 3
EPS = 1e-6
IN_SIZES = (2 * D_CONV, D_CONV, D_HG_K, D_HG_K, D_HG, D_HG, D_ATT, D_KV, D_KV, D_ATT, N_BRANCH * D_MODEL)
D_IN = 2 * D_CONV + D_CONV + 2 * D_HG_K + 2 * D_HG + 2 * D_ATT + 2 * D_KV + N_BRANCH * D_MODEL

kernel_name = "hybrid_conv_hgrn2_swa_meta_trunk"


def _rmsnorm(x, g):
    xf = x.astype(jnp.float32)
    return xf * lax.rsqrt(jnp.mean(xf * xf, axis=-1, keepdims=True) + EPS) * g.astype(jnp.float32)


def _conv_branch(a_in, a_gate, valid, conv_w, conv_b, ln_g, ln_b, w_o):
    a = a_in.astype(jnp.float32)
    u = a[..., :D_CONV] * jax.nn.sigmoid(a[..., D_CONV:])
    u = jnp.where(valid[None, :, None], u, 0.0)
    y = lax.conv_general_dilated(
        u, conv_w.astype(jnp.float32)[:, None, :], window_strides=(1,),
        padding=[(CONV_WIDTH - 1, 0)], dimension_numbers=("NWC", "WIO", "NWC"),
        feature_group_count=D_CONV)
    y = y + conv_b.astype(jnp.float32)
    mu = jnp.mean(y, axis=-1, keepdims=True)
    var = jnp.mean(jnp.square(y - mu), axis=-1, keepdims=True)
    y = (y - mu) * lax.rsqrt(var + EPS) * ln_g.astype(jnp.float32) + ln_b.astype(jnp.float32)
    y = jax.nn.silu(y) * jax.nn.silu(a_gate.astype(jnp.float32))
    return y.astype(w_o.dtype) @ w_o


def _hgrn2_branch(q, fz, i, gate, valid, lb, gn_g, w_o):
    B, L, _ = q.shape
    n_chunks = L // CHUNK
    qh = jax.nn.silu(q.astype(jnp.float32)).reshape(B, L, HG_HEADS, HG_DK)
    z = fz.astype(jnp.float32).reshape(B, L, HG_HEADS, HG_DK)
    lbh = lb.astype(jnp.float32).reshape(HG_HEADS, HG_DK)
    f = lbh + (1.0 - lbh) * jax.nn.sigmoid(z)
    logf = jnp.log(jnp.maximum(f, F_FLOOR))
    kh = (1.0 - lbh) * jax.nn.sigmoid(-z)
    vmask = valid[None, :, None, None]
    logf = jnp.where(vmask, logf, 0.0)
    kh = jnp.where(vmask, kh, 0.0)
    vh = i.astype(jnp.float32).reshape(B, L, HG_HEADS, HG_DV)

    def to_chunks(t):
        return t.reshape(B, n_chunks, CHUNK, HG_HEADS, -1).transpose(1, 0, 3, 2, 4)

    tri = jnp.tril(jnp.ones((CHUNK, CHUNK), dtype=bool))[:, :, None]

    def step(S, inp):
        qc, kc, vc, gc = inp
        b = jnp.cumsum(gc, axis=2)
        b_last = b[:, :, -1:, :]
        rel = b[:, :, :, None, :] - b[:, :, None, :, :]
        decay = jnp.exp(jnp.where(tri, rel, -jnp.inf))
        scores = jnp.einsum("bhtd,bhtsd,bhsd->bhts", qc, decay, kc)
        o = (jnp.einsum("bhts,bhsv->bhtv", scores, vc)
             + jnp.einsum("bhtd,bhdv->bhtv", qc * jnp.exp(b), S))
        S = (jnp.exp(b_last)[:, :, 0, :, None] * S
             + jnp.einsum("bhsd,bhsv->bhdv", kc * jnp.exp(b_last - b), vc))
        return S, o

    S0 = jnp.zeros((B, HG_HEADS, HG_DK, HG_DV), jnp.float32)
    _, o = lax.scan(step, S0, (to_chunks(qh), to_chunks(kh), to_chunks(vh), to_chunks(logf)))
    o = o.transpose(1, 0, 3, 2, 4).reshape(B, L, HG_HEADS, HG_DV)
    o = _rmsnorm(o, gn_g).reshape(B, L, D_HG) * jax.nn.silu(gate.astype(jnp.float32))
    return o.astype(w_o.dtype) @ w_o


def _swa_key_mask(n_chunks):
    c = jnp.arange(n_chunks)[:, None]
    j = jnp.arange(CHUNK)[None, :]
    meta_ok = (c > WINDOW_CHUNKS) & (j >= META_PAD)
    band = [((c - WINDOW_CHUNKS + r) * CHUNK + j) >= META_PAD for r in range(WINDOW_CHUNKS + 1)]
    return jnp.concatenate([meta_ok] + band, axis=1)


def _band(t, n_chunks):
    tp = jnp.pad(t, ((0, 0), (WINDOW_CHUNKS, 0), (0, 0), (0, 0), (0, 0)))
    meta = jnp.broadcast_to(t[:, :1], t.shape)
    return jnp.concatenate([meta] + [tp[:, r:r + n_chunks] for r in range(WINDOW_CHUNKS + 1)], axis=2)


def _swa_branch(q, k, v, gate, qn_g, kn_g, sinks, w_o, key_mask):
    B, L, _ = q.shape
    n_chunks = L // CHUNK
    qh = _rmsnorm(q.reshape(B, L, ATT_Q_HEADS, ATT_HEAD_DIM), qn_g).reshape(
        B, n_chunks, CHUNK, ATT_KV_HEADS, ATT_GROUP, ATT_HEAD_DIM)
    kh = _rmsnorm(k.reshape(B, L, ATT_KV_HEADS, ATT_HEAD_DIM), kn_g).reshape(
        B, n_chunks, CHUNK, ATT_KV_HEADS, ATT_HEAD_DIM)
    vh = v.astype(jnp.float32).reshape(B, n_chunks, CHUNK, ATT_KV_HEADS, ATT_HEAD_DIM)
    kb = _band(kh, n_chunks)
    vb = _band(vh, n_chunks)
    s = jnp.einsum("bnqhgd,bnkhd->bnhgqk", qh, kb) * (ATT_HEAD_DIM ** -0.5)
    s = jnp.where(key_mask[None, :, None, None, None, :], s, -jnp.inf)
    sink = sinks.astype(jnp.float32).reshape(ATT_KV_HEADS, ATT_GROUP)[None, None, :, :, None, None]
    m = jnp.maximum(jnp.max(s, axis=-1, keepdims=True), sink)
    p = jnp.exp(s - m)
    denom = jnp.sum(p, axis=-1, keepdims=True) + jnp.exp(sink - m)
    o = jnp.einsum("bnhgqk,bnkhd->bnqhgd", p / denom, vb).reshape(B, L, D_ATT)
    o = o * jax.nn.silu(gate.astype(jnp.float32))
    return o.astype(w_o.dtype) @ w_o


def _fwd_setup_inputs(seed: int = 0) -> dict:
    key = jax.random.key(seed)
    ks = jax.random.split(key, 20)
    f32 = jnp.float32
    nrm = lambda k, shape, scale: jax.random.normal(k, shape, f32) * scale
    return {
        "x": nrm(ks[0], (BATCH, SEQ, D_MODEL), 1.0),
        "meta_tokens": nrm(ks[1], (N_META, D_MODEL), 1.0),
        "norm_g": 1.0 + nrm(ks[2], (DEPTH, D_MODEL), 0.05),
        "w_in": nrm(ks[3], (DEPTH, D_MODEL, D_IN), D_MODEL ** -0.5),
        "conv_w": nrm(ks[4], (DEPTH, CONV_WIDTH, D_CONV), CONV_WIDTH ** -0.5),
        "conv_b": nrm(ks[5], (DEPTH, D_CONV), 0.02),
        "conv_ln_g": 1.0 + nrm(ks[6], (DEPTH, D_CONV), 0.05),
        "conv_ln_b": nrm(ks[7], (DEPTH, D_CONV), 0.02),
        "w_conv_out": nrm(ks[8], (DEPTH, D_CONV, D_MODEL), D_CONV ** -0.5),
        "hg_lower_bounds": nrm(ks[9], (DEPTH, D_HG_K), 0.1),
        "hg_norm_g": 1.0 + nrm(ks[10], (DEPTH, HG_DV), 0.05),
        "w_hg_out": nrm(ks[11], (DEPTH, D_HG, D_MODEL), D_HG ** -0.5),
        "q_norm_g": 1.0 + nrm(ks[12], (DEPTH, ATT_HEAD_DIM), 0.05),
        "k_norm_g": 1.0 + nrm(ks[13], (DEPTH, ATT_HEAD_DIM), 0.05),
        "attn_sinks": nrm(ks[14], (DEPTH, ATT_Q_HEADS), 0.5),
        "w_att_out": nrm(ks[15], (DEPTH, D_ATT, D_MODEL), D_ATT ** -0.5),
        "w_out": nrm(ks[16], (DEPTH, D_MODEL, D_MODEL), D_MODEL ** -0.5),
    }


def _fwd_reference(x, meta_tokens, norm_g, w_in, conv_w, conv_b, conv_ln_g, conv_ln_b, w_conv_out,
              hg_lower_bounds, hg_norm_g, w_hg_out, q_norm_g, k_norm_g, attn_sinks, w_att_out, w_out):
    B = x.shape[0]
    dt = x.dtype
    h = jnp.concatenate([
        jnp.zeros((B, META_PAD, D_MODEL), dt),
        jnp.broadcast_to(meta_tokens.astype(dt)[None], (B, N_META, D_MODEL)),
        x], axis=1)
    L = h.shape[1]
    n_chunks = L // CHUNK
    valid = jnp.arange(L) >= META_PAD
    key_mask = _swa_key_mask(n_chunks)
    lb_sm = jax.nn.softmax(hg_lower_bounds.astype(jnp.float32), axis=0)
    lb_all = jnp.clip(jnp.cumsum(lb_sm, axis=0) - lb_sm[0:1], 0.0, 1.0)
    split_idx = [int(v) for v in np.cumsum(IN_SIZES)[:-1]]

    for l in range(DEPTH):
        hn = _rmsnorm(h, norm_g[l]).astype(dt)
        u = hn @ w_in[l]
        (a_in, a_gate, b_q, b_f, b_i, b_gate, c_q, c_k, c_v, c_gate, g_logits) = jnp.split(u, split_idx, axis=-1)
        z_a = _conv_branch(a_in, a_gate, valid, conv_w[l], conv_b[l], conv_ln_g[l], conv_ln_b[l], w_conv_out[l])
        z_b = _hgrn2_branch(b_q, b_f, b_i, b_gate, valid, lb_all[l], hg_norm_g[l], w_hg_out[l])
        z_c = _swa_branch(c_q, c_k, c_v, c_gate, q_norm_g[l], k_norm_g[l], attn_sinks[l], w_att_out[l], key_mask)
        g = jax.nn.sigmoid(g_logits.astype(jnp.float32))
        mixed = (g[..., :D_MODEL] * z_a.astype(jnp.float32)
                 + g[..., D_MODEL:2 * D_MODEL] * z_b.astype(jnp.float32)
                 + g[..., 2 * D_MODEL:] * z_c.astype(jnp.float32))
        h = h + mixed.astype(dt) @ w_out[l]

    return h[:, CHUNK:]


import jax as _jax
import jax.numpy as _jnp

TWIN_FORMAT = 'train_step'
FWD_PARAMS = ['x', 'meta_tokens', 'norm_g', 'w_in', 'conv_w', 'conv_b', 'conv_ln_g', 'conv_ln_b', 'w_conv_out', 'hg_lower_bounds', 'hg_norm_g', 'w_hg_out', 'q_norm_g', 'k_norm_g', 'attn_sinks', 'w_att_out', 'w_out']
TWIN_WEIGHTS = ['meta_tokens', 'norm_g', 'w_in', 'conv_w', 'conv_b', 'conv_ln_g', 'conv_ln_b', 'w_conv_out', 'hg_lower_bounds', 'hg_norm_g', 'w_hg_out', 'q_norm_g', 'k_norm_g', 'attn_sinks', 'w_att_out', 'w_out']
TWIN_DIFF_INPUT = 'x'
TWIN_INPUTS = ['x', 'meta_tokens', 'norm_g', 'w_in', 'conv_w', 'conv_b', 'conv_ln_g', 'conv_ln_b', 'w_conv_out', 'hg_lower_bounds', 'hg_norm_g', 'w_hg_out', 'q_norm_g', 'k_norm_g', 'attn_sinks', 'w_att_out', 'w_out', 'loss_target', 'm_meta_tokens', 'm_norm_g', 'm_w_in', 'm_conv_w', 'm_conv_b', 'm_conv_ln_g', 'm_conv_ln_b', 'm_w_conv_out', 'm_hg_lower_bounds', 'm_hg_norm_g', 'm_w_hg_out', 'm_q_norm_g', 'm_k_norm_g', 'm_attn_sinks', 'm_w_att_out', 'm_w_out', 'v_meta_tokens', 'v_norm_g', 'v_w_in', 'v_conv_w', 'v_conv_b', 'v_conv_ln_g', 'v_conv_ln_b', 'v_w_conv_out', 'v_hg_lower_bounds', 'v_hg_norm_g', 'v_w_hg_out', 'v_q_norm_g', 'v_k_norm_g', 'v_attn_sinks', 'v_w_att_out', 'v_w_out']
TWIN_OUTPUTS = ['loss', 'grad_x', 'grad_meta_tokens', 'grad_norm_g', 'grad_w_in', 'grad_conv_w', 'grad_conv_b', 'grad_conv_ln_g', 'grad_conv_ln_b', 'grad_w_conv_out', 'grad_hg_lower_bounds', 'grad_hg_norm_g', 'grad_w_hg_out', 'grad_q_norm_g', 'grad_k_norm_g', 'grad_attn_sinks', 'grad_w_att_out', 'grad_w_out', 'delta_meta_tokens', 'delta_norm_g', 'delta_w_in', 'delta_conv_w', 'delta_conv_b', 'delta_conv_ln_g', 'delta_conv_ln_b', 'delta_w_conv_out', 'delta_hg_lower_bounds', 'delta_hg_norm_g', 'delta_w_hg_out', 'delta_q_norm_g', 'delta_k_norm_g', 'delta_attn_sinks', 'delta_w_att_out', 'delta_w_out', 'new_m_meta_tokens', 'new_m_norm_g', 'new_m_w_in', 'new_m_conv_w', 'new_m_conv_b', 'new_m_conv_ln_g', 'new_m_conv_ln_b', 'new_m_w_conv_out', 'new_m_hg_lower_bounds', 'new_m_hg_norm_g', 'new_m_w_hg_out', 'new_m_q_norm_g', 'new_m_k_norm_g', 'new_m_attn_sinks', 'new_m_w_att_out', 'new_m_w_out', 'new_v_meta_tokens', 'new_v_norm_g', 'new_v_w_in', 'new_v_conv_w', 'new_v_conv_b', 'new_v_conv_ln_g', 'new_v_conv_ln_b', 'new_v_w_conv_out', 'new_v_hg_lower_bounds', 'new_v_hg_norm_g', 'new_v_w_hg_out', 'new_v_q_norm_g', 'new_v_k_norm_g', 'new_v_attn_sinks', 'new_v_w_att_out', 'new_v_w_out']
TWIN_LEAF_KINDS = {'loss': 'loss', 'grad_x': 'grad_x', 'grad_meta_tokens': 'grad_w', 'grad_norm_g': 'grad_w', 'grad_w_in': 'grad_w', 'grad_conv_w': 'grad_w', 'grad_conv_b': 'grad_w', 'grad_conv_ln_g': 'grad_w', 'grad_conv_ln_b': 'grad_w', 'grad_w_conv_out': 'grad_w', 'grad_hg_lower_bounds': 'grad_w', 'grad_hg_norm_g': 'grad_w', 'grad_w_hg_out': 'grad_w', 'grad_q_norm_g': 'grad_w', 'grad_k_norm_g': 'grad_w', 'grad_attn_sinks': 'grad_w', 'grad_w_att_out': 'grad_w', 'grad_w_out': 'grad_w', 'delta_meta_tokens': 'delta_w', 'delta_norm_g': 'delta_w', 'delta_w_in': 'delta_w', 'delta_conv_w': 'delta_w', 'delta_conv_b': 'delta_w', 'delta_conv_ln_g': 'delta_w', 'delta_conv_ln_b': 'delta_w', 'delta_w_conv_out': 'delta_w', 'delta_hg_lower_bounds': 'delta_w', 'delta_hg_norm_g': 'delta_w', 'delta_w_hg_out': 'delta_w', 'delta_q_norm_g': 'delta_w', 'delta_k_norm_g': 'delta_w', 'delta_attn_sinks': 'delta_w', 'delta_w_att_out': 'delta_w', 'delta_w_out': 'delta_w', 'new_m_meta_tokens': 'new_m', 'new_m_norm_g': 'new_m', 'new_m_w_in': 'new_m', 'new_m_conv_w': 'new_m', 'new_m_conv_b': 'new_m', 'new_m_conv_ln_g': 'new_m', 'new_m_conv_ln_b': 'new_m', 'new_m_w_conv_out': 'new_m', 'new_m_hg_lower_bounds': 'new_m', 'new_m_hg_norm_g': 'new_m', 'new_m_w_hg_out': 'new_m', 'new_m_q_norm_g': 'new_m', 'new_m_k_norm_g': 'new_m', 'new_m_attn_sinks': 'new_m', 'new_m_w_att_out': 'new_m', 'new_m_w_out': 'new_m', 'new_v_meta_tokens': 'new_v', 'new_v_norm_g': 'new_v', 'new_v_w_in': 'new_v', 'new_v_conv_w': 'new_v', 'new_v_conv_b': 'new_v', 'new_v_conv_ln_g': 'new_v', 'new_v_conv_ln_b': 'new_v', 'new_v_w_conv_out': 'new_v', 'new_v_hg_lower_bounds': 'new_v', 'new_v_hg_norm_g': 'new_v', 'new_v_w_hg_out': 'new_v', 'new_v_q_norm_g': 'new_v', 'new_v_k_norm_g': 'new_v', 'new_v_attn_sinks': 'new_v', 'new_v_w_att_out': 'new_v', 'new_v_w_out': 'new_v'}


def _forward(args):
    return _fwd_reference(*[args[k] for k in FWD_PARAMS])


def _output_shape():
    def fwd():
        inp = _fwd_setup_inputs(0)
        return _fwd_reference(*[inp[k] for k in FWD_PARAMS])
    out = _jax.eval_shape(fwd)
    return out.shape, out.dtype

N_MICROBATCH = 1
ADAM_LR = 0.001
ADAM_B1 = 0.9
ADAM_B2 = 0.999
ADAM_EPS = 1e-08
ADAM_WD = 0.01
ADAM_STEP = 10
PER_EXAMPLE_BATCH_AXIS = {'x': 0, 'loss_target': 0}
SHARED_INPUTS = []
_WEIGHT_DTYPES = {'meta_tokens': _jnp.float32, 'norm_g': _jnp.float32, 'w_in': _jnp.float32, 'conv_w': _jnp.float32, 'conv_b': _jnp.float32, 'conv_ln_g': _jnp.float32, 'conv_ln_b': _jnp.float32, 'w_conv_out': _jnp.float32, 'hg_lower_bounds': _jnp.float32, 'hg_norm_g': _jnp.float32, 'w_hg_out': _jnp.float32, 'q_norm_g': _jnp.float32, 'k_norm_g': _jnp.float32, 'attn_sinks': _jnp.float32, 'w_att_out': _jnp.float32, 'w_out': _jnp.float32}
MOMENT_SCALE = {'meta_tokens': 3.149623e-02, 'norm_g': 2.328320e+01, 'w_in': 2.398678e-01, 'conv_w': 2.950727e-01, 'conv_b': 3.437216e+00, 'conv_ln_g': 1.149464e+01, 'conv_ln_b': 7.619041e+00, 'w_conv_out': 3.367834e-01, 'hg_lower_bounds': 3.562123e-02, 'hg_norm_g': 1.082679e+02, 'w_hg_out': 5.014502e-01, 'q_norm_g': 1.460956e+00, 'k_norm_g': 1.492975e+00, 'attn_sinks': 3.230504e-02, 'w_att_out': 3.921291e-02, 'w_out': 5.793316e-01}


def _to_microbatches(a, axis):
    t = _jnp.moveaxis(a, axis, 0)
    t = t.reshape((N_MICROBATCH, t.shape[0] // N_MICROBATCH) + t.shape[1:])
    return _jnp.moveaxis(t, 1, axis + 1)


def setup_inputs(seed: int = 0) -> dict:
    inp = _fwd_setup_inputs(seed)
    key = _jax.random.fold_in(_jax.random.key(seed), 7919)
    shape, _ = _output_shape()
    out = dict(inp)
    out["loss_target"] = _jax.random.normal(_jax.random.fold_in(key, 0), shape, _jnp.float32)
    for i, name in enumerate(TWIN_WEIGHTS):
        w = inp[name].astype(_jnp.float32)
        if MOMENT_SCALE is None:
            s = _jnp.sqrt(_jnp.mean(_jnp.square(w)) + 1e-30)
        else:
            s = MOMENT_SCALE[name]
        km, kv = _jax.random.split(_jax.random.fold_in(key, i + 1))
        out[name] = w
        out["m_" + name] = s * _jax.random.normal(km, w.shape, _jnp.float32)
        out["v_" + name] = (s * s) * _jax.random.uniform(kv, w.shape, _jnp.float32, 0.5, 1.5)
    if N_MICROBATCH > 1:
        for name, axis in PER_EXAMPLE_BATCH_AXIS.items():
            out[name] = _to_microbatches(out[name], axis)
    return {'x': out['x'], 'meta_tokens': out['meta_tokens'], 'norm_g': out['norm_g'], 'w_in': out['w_in'], 'conv_w': out['conv_w'], 'conv_b': out['conv_b'], 'conv_ln_g': out['conv_ln_g'], 'conv_ln_b': out['conv_ln_b'], 'w_conv_out': out['w_conv_out'], 'hg_lower_bounds': out['hg_lower_bounds'], 'hg_norm_g': out['hg_norm_g'], 'w_hg_out': out['w_hg_out'], 'q_norm_g': out['q_norm_g'], 'k_norm_g': out['k_norm_g'], 'attn_sinks': out['attn_sinks'], 'w_att_out': out['w_att_out'], 'w_out': out['w_out'], 'loss_target': out['loss_target'], 'm_meta_tokens': out['m_meta_tokens'], 'm_norm_g': out['m_norm_g'], 'm_w_in': out['m_w_in'], 'm_conv_w': out['m_conv_w'], 'm_conv_b': out['m_conv_b'], 'm_conv_ln_g': out['m_conv_ln_g'], 'm_conv_ln_b': out['m_conv_ln_b'], 'm_w_conv_out': out['m_w_conv_out'], 'm_hg_lower_bounds': out['m_hg_lower_bounds'], 'm_hg_norm_g': out['m_hg_norm_g'], 'm_w_hg_out': out['m_w_hg_out'], 'm_q_norm_g': out['m_q_norm_g'], 'm_k_norm_g': out['m_k_norm_g'], 'm_attn_sinks': out['m_attn_sinks'], 'm_w_att_out': out['m_w_att_out'], 'm_w_out': out['m_w_out'], 'v_meta_tokens': out['v_meta_tokens'], 'v_norm_g': out['v_norm_g'], 'v_w_in': out['v_w_in'], 'v_conv_w': out['v_conv_w'], 'v_conv_b': out['v_conv_b'], 'v_conv_ln_g': out['v_conv_ln_g'], 'v_conv_ln_b': out['v_conv_ln_b'], 'v_w_conv_out': out['v_w_conv_out'], 'v_hg_lower_bounds': out['v_hg_lower_bounds'], 'v_hg_norm_g': out['v_hg_norm_g'], 'v_w_hg_out': out['v_w_hg_out'], 'v_q_norm_g': out['v_q_norm_g'], 'v_k_norm_g': out['v_k_norm_g'], 'v_attn_sinks': out['v_attn_sinks'], 'v_w_att_out': out['v_w_att_out'], 'v_w_out': out['v_w_out']}


def _loss(weights, diff, rest, loss_target):
    with _jax.named_scope("forward"):
        args = {**rest, TWIN_DIFF_INPUT: diff, **{k: w.astype(_WEIGHT_DTYPES[k]) for k, w in weights.items()}}
        y = _forward(args)
    with _jax.named_scope("loss_head"):
        err = _jnp.square(y.astype(_jnp.float32) - loss_target)
        return 0.5 * _jnp.sum(_jnp.mean(err, axis=-1)) if err.ndim else 0.5 * err


def _adamw(w, g, m, v):
    m = ADAM_B1 * m + (1.0 - ADAM_B1) * g
    v = ADAM_B2 * v + (1.0 - ADAM_B2) * _jnp.square(g)
    m_hat = m / (1.0 - ADAM_B1 ** ADAM_STEP)
    v_hat = v / (1.0 - ADAM_B2 ** ADAM_STEP)
    delta = -ADAM_LR * (m_hat / (_jnp.sqrt(v_hat) + ADAM_EPS) + ADAM_WD * w)
    return delta, m, v


def reference(x, meta_tokens, norm_g, w_in, conv_w, conv_b, conv_ln_g, conv_ln_b, w_conv_out, hg_lower_bounds, hg_norm_g, w_hg_out, q_norm_g, k_norm_g, attn_sinks, w_att_out, w_out, loss_target, m_meta_tokens, m_norm_g, m_w_in, m_conv_w, m_conv_b, m_conv_ln_g, m_conv_ln_b, m_w_conv_out, m_hg_lower_bounds, m_hg_norm_g, m_w_hg_out, m_q_norm_g, m_k_norm_g, m_attn_sinks, m_w_att_out, m_w_out, v_meta_tokens, v_norm_g, v_w_in, v_conv_w, v_conv_b, v_conv_ln_g, v_conv_ln_b, v_w_conv_out, v_hg_lower_bounds, v_hg_norm_g, v_w_hg_out, v_q_norm_g, v_k_norm_g, v_attn_sinks, v_w_att_out, v_w_out):
    given = dict(x=x, meta_tokens=meta_tokens, norm_g=norm_g, w_in=w_in, conv_w=conv_w, conv_b=conv_b, conv_ln_g=conv_ln_g, conv_ln_b=conv_ln_b, w_conv_out=w_conv_out, hg_lower_bounds=hg_lower_bounds, hg_norm_g=hg_norm_g, w_hg_out=w_hg_out, q_norm_g=q_norm_g, k_norm_g=k_norm_g, attn_sinks=attn_sinks, w_att_out=w_att_out, w_out=w_out, loss_target=loss_target, m_meta_tokens=m_meta_tokens, m_norm_g=m_norm_g, m_w_in=m_w_in, m_conv_w=m_conv_w, m_conv_b=m_conv_b, m_conv_ln_g=m_conv_ln_g, m_conv_ln_b=m_conv_ln_b, m_w_conv_out=m_w_conv_out, m_hg_lower_bounds=m_hg_lower_bounds, m_hg_norm_g=m_hg_norm_g, m_w_hg_out=m_w_hg_out, m_q_norm_g=m_q_norm_g, m_k_norm_g=m_k_norm_g, m_attn_sinks=m_attn_sinks, m_w_att_out=m_w_att_out, m_w_out=m_w_out, v_meta_tokens=v_meta_tokens, v_norm_g=v_norm_g, v_w_in=v_w_in, v_conv_w=v_conv_w, v_conv_b=v_conv_b, v_conv_ln_g=v_conv_ln_g, v_conv_ln_b=v_conv_ln_b, v_w_conv_out=v_w_conv_out, v_hg_lower_bounds=v_hg_lower_bounds, v_hg_norm_g=v_hg_norm_g, v_w_hg_out=v_w_hg_out, v_q_norm_g=v_q_norm_g, v_k_norm_g=v_k_norm_g, v_attn_sinks=v_attn_sinks, v_w_att_out=v_w_att_out, v_w_out=v_w_out)
    weights = {n: given[n] for n in TWIN_WEIGHTS}
    shared = {n: given[n] for n in SHARED_INPUTS}
    per_example = {n: given[n] for n in ['x']}
    grad_fn = _jax.value_and_grad(_loss, argnums=(0, 1))

    def one_microbatch(ex, loss_target):
        ex = dict(ex)
        diff = ex.pop(TWIN_DIFF_INPUT)
        return grad_fn(weights, diff, {**shared, **ex}, loss_target)

    if N_MICROBATCH == 1:
        loss, (grad_w, grad_x) = one_microbatch(per_example, given["loss_target"])
    else:
        def body(carry, xs):
            loss_sum, grad_sum = carry
            l_k, (gw_k, gx_k) = one_microbatch(xs[0], xs[1])
            with _jax.named_scope("update"):
                return (loss_sum + l_k, _jax.tree.map(_jnp.add, grad_sum, gw_k)), gx_k

        init = (_jnp.zeros((), _jnp.float32), _jax.tree.map(_jnp.zeros_like, weights))
        (loss, grad_w), grad_x = _jax.lax.scan(body, init, (per_example, given["loss_target"]))
    with _jax.named_scope("update"):
        delta_w, new_m, new_v = {}, {}, {}
        for n in TWIN_WEIGHTS:
            delta_w[n], new_m[n], new_v[n] = _adamw(weights[n], grad_w[n], given["m_" + n], given["v_" + n])
    return (loss, grad_x, *[grad_w[n] for n in TWIN_WEIGHTS], *[delta_w[n] for n in TWIN_WEIGHTS],
            *[new_m[n] for n in TWIN_WEIGHTS], *[new_v[n] for n in TWIN_WEIGHTS])
```

```python
import functools

import jax
import jax.numpy as jnp
from jax import lax
from jax.experimental import pallas as pl
from jax.experimental.pallas import tpu as pltpu

F32 = jnp.float32
MXU_DTYPE = jnp.bfloat16

D_MODEL = 1024
CHUNK = 64
N_META = 16
META_PAD = CHUNK - N_META
D_CONV = 512
CONV_WIDTH = 31
HG_HEADS = 4
HG_D = 128
D_HG = HG_HEADS * HG_D
F_FLOOR = 1e-30
ATT_Q_HEADS = 8
ATT_HEAD_DIM = 64
D_ATT = 512
D_KV = 128
WINDOW_CHUNKS = 2
EPS = 1e-6
D_IN = 7936
N_DEV = 8

ADAM_LR = 0.001
ADAM_B1 = 0.9
ADAM_B2 = 0.999
ADAM_EPS = 1e-08
ADAM_WD = 0.01
ADAM_STEP = 10

NP = 8192
OFF_B, W_B = 0, 2048
OFF_A, W_A = 2048, 1024
OFF_G, W_G = 3072, 3072
OFF_C, W_C = 6144, 1536
OFF_AG, W_AG = 7680, 512
_PACK = ((0, 1536, 2048), (2048, 0, 1024), (3072, 4864, 3072), (6144, 3584, 512), (6656, 4352, 512),
         (7168, 4096, 256), (7680, 1024, 512))
_PAD_AT, _PAD_W = 7424, 256

TM_MM = 1280
TM_BR = 256
HALO = 32
EXP_CLAMP = 80.0
VMEM_LIMIT = 56 * 1024 * 1024

_HI = lax.Precision.HIGHEST


def _cp(sem):
    return pltpu.CompilerParams(dimension_semantics=sem, vmem_limit_bytes=VMEM_LIMIT)


def _sig(x):
    return 1.0 / (1.0 + jnp.exp(-x))


def _dot(a, b):
    return jnp.dot(a.astype(MXU_DTYPE), b.astype(MXU_DTYPE), preferred_element_type=F32)


def _dot_nt(a, b):
    return lax.dot_general(a.astype(MXU_DTYPE), b.astype(MXU_DTYPE), (((1,), (1,)), ((), ())),
                           preferred_element_type=F32)


def _dot_tn(a, b):
    return lax.dot_general(a.astype(MXU_DTYPE), b.astype(MXU_DTYPE), (((0,), (0,)), ((), ())),
                           preferred_element_type=F32)


def _iota(shape, dim):
    return lax.broadcasted_iota(jnp.int32, shape, dim)


def _full_spec(shape):
    nd = len(shape)
    return pl.BlockSpec(shape, lambda *_: (0,) * nd)


def _inproj_fwd(h, g, wp):
    lp = h.shape[0]
    tm, tn = TM_MM, 512

    def body(h_ref, g_ref, w_ref, u_ref, hn_ref, hs_ref):
        @pl.when(pl.program_id(1) == 0)
        def _():
            x = h_ref[...]
            r = lax.rsqrt(jnp.mean(x * x, axis=-1, keepdims=True) + EPS)
            hn = (x * r * g_ref[...]).astype(MXU_DTYPE)
            hs_ref[...] = hn
            hn_ref[...] = hn
        u_ref[...] = jnp.dot(hs_ref[...], w_ref[...], preferred_element_type=F32)

    return pl.pallas_call(
        body, name="inproj_fwd", grid=(lp // tm, NP // tn),
        in_specs=[pl.BlockSpec((tm, D_MODEL), lambda i, j: (i, 0)), pl.BlockSpec((1, D_MODEL), lambda i, j: (0, 0)),
                  pl.BlockSpec((D_MODEL, tn), lambda i, j: (0, j))],
        out_specs=[pl.BlockSpec((tm, tn), lambda i, j: (i, j)), pl.BlockSpec((tm, D_MODEL), lambda i, j: (i, 0))],
        out_shape=[jax.ShapeDtypeStruct((lp, NP), F32), jax.ShapeDtypeStruct((lp, D_MODEL), MXU_DTYPE)],
        scratch_shapes=[pltpu.VMEM((tm, D_MODEL), MXU_DTYPE)],
        compiler_params=_cp(("parallel", "arbitrary")),
    )(h, g, wp)


def _inproj_bwd_dh(du, wpt, h, g, dh_next):
    lp = h.shape[0]
    tm, tk = TM_MM, 1024
    nk = NP // tk

    def body(du_ref, w_ref, h_ref, g_ref, dhn_ref, dh_ref, dg_ref, acc_ref):
        i, k = pl.program_id(0), pl.program_id(1)

        @pl.when(k == 0)
        def _():
            acc_ref[...] = jnp.zeros_like(acc_ref)

        @pl.when((i == 0) & (k == 0))
        def _():
            dg_ref[...] = jnp.zeros_like(dg_ref)

        acc_ref[...] += jnp.dot(du_ref[...], w_ref[...], preferred_element_type=F32)

        @pl.when(k == nk - 1)
        def _():
            dhn = acc_ref[...]
            x = h_ref[...]
            r = lax.rsqrt(jnp.mean(x * x, axis=-1, keepdims=True) + EPS)
            xh = x * r
            dg_ref[...] += jnp.sum(dhn * xh, axis=0, keepdims=True)
            dxh = dhn * g_ref[...]
            dx = r * (dxh - xh * jnp.mean(dxh * xh, axis=-1, keepdims=True))
            dh_ref[...] = dhn_ref[...] + dx

    return pl.pallas_call(
        body, name="inproj_bwd_dh", grid=(lp // tm, nk),
        in_specs=[pl.BlockSpec((tm, tk), lambda i, k: (i, k)), pl.BlockSpec((tk, D_MODEL), lambda i, k: (k, 0)),
                  pl.BlockSpec((tm, D_MODEL), lambda i, k: (i, 0)), pl.BlockSpec((1, D_MODEL), lambda i, k: (0, 0)),
                  pl.BlockSpec((tm, D_MODEL), lambda i, k: (i, 0))],
        out_specs=[pl.BlockSpec((tm, D_MODEL), lambda i, k: (i, 0)), pl.BlockSpec((1, D_MODEL), lambda i, k: (0, 0))],
        out_shape=[jax.ShapeDtypeStruct((lp, D_MODEL), F32), jax.ShapeDtypeStruct((1, D_MODEL), F32)],
        scratch_shapes=[pltpu.VMEM((tm, D_MODEL), F32)],
        compiler_params=_cp(("arbitrary", "arbitrary")),
    )(du, wpt, h, g, dh_next)


def _inproj_bwd_dw(hn, du):
    lp = hn.shape[0]
    tm, tn = TM_MM, 1024

    def body(hn_ref, du_ref, dw_ref):
        @pl.when(pl.program_id(1) == 0)
        def _():
            dw_ref[...] = jnp.zeros_like(dw_ref)
        dw_ref[...] += _dot_tn(hn_ref[...], du_ref[...])

    return pl.pallas_call(
        body, name="inproj_bwd_dw", grid=(NP // tn, lp // tm),
        in_specs=[pl.BlockSpec((tm, D_MODEL), lambda j, m: (m, 0)), pl.BlockSpec((tm, tn), lambda j, m: (m, j))],
        out_specs=pl.BlockSpec((D_MODEL, tn), lambda j, m: (0, j)),
        out_shape=jax.ShapeDtypeStruct((D_MODEL, NP), F32),
        compiler_params=_cp(("parallel", "arbitrary")),
    )(hn, du)


def _conv_common(a_ref, ah_ref, w_ref, b_ref, ext_ref, i, tm):
    rows = i * tm + _iota((tm, 1), 0)
    a = a_ref[...]
    u0 = jnp.where(rows >= META_PAD, a[:, :D_CONV] * _sig(a[:, D_CONV:]), 0.0)
    ah = ah_ref[...]
    uh = jnp.where(i > 0, ah[:, :D_CONV] * _sig(ah[:, D_CONV:]), 0.0)
    ext_ref[0:HALO, :] = uh
    ext_ref[HALO:HALO + tm, :] = u0
    y = jnp.zeros((tm, D_CONV), F32) + b_ref[...]
    base = HALO - (CONV_WIDTH - 1)
    for k in range(CONV_WIDTH):
        y = y + w_ref[k:k + 1, :] * ext_ref[pl.ds(base + k, tm), :]
    return y


def _layernorm_stats(y):
    mu = jnp.mean(y, axis=-1, keepdims=True)
    yc = y - mu
    rstd = lax.rsqrt(jnp.mean(yc * yc, axis=-1, keepdims=True) + EPS)
    return yc * rstd, rstd


def _conv_specs(tm):
    hb = tm // HALO
    return [pl.BlockSpec((tm, W_A), lambda i: (i, OFF_A // W_A)),
            pl.BlockSpec((HALO, W_A), lambda i: (jnp.maximum(i * hb - 1, 0), OFF_A // W_A)),
            pl.BlockSpec((tm, W_AG), lambda i: (i, OFF_AG // W_AG))]


def _conv_fwd(u, cw, cb, lg, lb_):
    lp = u.shape[0]
    tm = TM_BR

    def body(a_ref, ah_ref, ag_ref, w_ref, b_ref, lg_ref, lb_ref, y_ref, ext_ref):
        i = pl.program_id(0)
        y = _conv_common(a_ref, ah_ref, w_ref, b_ref, ext_ref, i, tm)
        xh, _ = _layernorm_stats(y)
        yn = xh * lg_ref[...] + lb_ref[...]
        gt = ag_ref[...]
        y_ref[...] = (yn * _sig(yn) * gt * _sig(gt)).astype(MXU_DTYPE)

    return pl.pallas_call(
        body, name="conv_fwd", grid=(lp // tm,),
        in_specs=_conv_specs(tm) + [_full_spec((CONV_WIDTH, D_CONV))] + [_full_spec((1, D_CONV))] * 3,
        out_specs=pl.BlockSpec((tm, D_CONV), lambda i: (i, 0)),
        out_shape=jax.ShapeDtypeStruct((lp, D_CONV), MXU_DTYPE),
        scratch_shapes=[pltpu.VMEM((HALO + tm, D_CONV), F32)],
        compiler_params=_cp(("arbitrary",)),
    )(u, u, u, cw, cb, lg, lb_)


def _dsilu(x, s):
    return s * (1.0 + x * (1.0 - s))


def _conv_bwd1(u, dya, du, cw, cb, lg, lb_):
    lp = u.shape[0]
    tm = TM_BR

    def body(a_ref, ah_ref, ag_ref, dya_ref, w_ref, b_ref, lg_ref, lb_ref, du_in, du_ref, dy_ref, dlg_ref, dlb_ref,
             ext_ref):
        del du_in
        i = pl.program_id(0)

        @pl.when(i == 0)
        def _():
            dlg_ref[...] = jnp.zeros_like(dlg_ref)
            dlb_ref[...] = jnp.zeros_like(dlb_ref)

        y = _conv_common(a_ref, ah_ref, w_ref, b_ref, ext_ref, i, tm)
        xh, rstd = _layernorm_stats(y)
        yn = xh * lg_ref[...] + lb_ref[...]
        s1 = _sig(yn)
        gt = ag_ref[...]
        s2 = _sig(gt)
        do = dya_ref[...].astype(F32)
        du_ref[...] = (do * (yn * s1) * _dsilu(gt, s2)).astype(MXU_DTYPE)
        dyn = do * (gt * s2) * _dsilu(yn, s1)
        dlg_ref[...] += jnp.sum(dyn * xh, axis=0, keepdims=True)
        dlb_ref[...] += jnp.sum(dyn, axis=0, keepdims=True)
        dxh = dyn * lg_ref[...]
        dy_ref[...] = rstd * (dxh - jnp.mean(dxh, axis=-1, keepdims=True)
                              - xh * jnp.mean(dxh * xh, axis=-1, keepdims=True))

    return pl.pallas_call(
        body, name="conv_bwd1", grid=(lp // tm,),
        in_specs=_conv_specs(tm) + [pl.BlockSpec((tm, D_CONV), lambda i: (i, 0)), _full_spec((CONV_WIDTH, D_CONV))]
        + [_full_spec((1, D_CONV))] * 3 + [pl.BlockSpec(memory_space=pl.ANY)],
        out_specs=[pl.BlockSpec((tm, W_AG), lambda i: (i, OFF_AG // W_AG)), pl.BlockSpec((tm, D_CONV), lambda i: (i, 0)),
                   _full_spec((1, D_CONV)), _full_spec((1, D_CONV))],
        out_shape=[jax.ShapeDtypeStruct(du.shape, du.dtype), jax.ShapeDtypeStruct((lp, D_CONV), F32),
                   jax.ShapeDtypeStruct((1, D_CONV), F32), jax.ShapeDtypeStruct((1, D_CONV), F32)],
        scratch_shapes=[pltpu.VMEM((HALO + tm, D_CONV), F32)],
        input_output_aliases={8: 0},
        compiler_params=_cp(("arbitrary",)),
    )(u, u, u, dya, cw, cb, lg, lb_, du)


def _conv_bwd2(u, dy, du, cw):
    lp = u.shape[0]
    tm = TM_BR
    nb = lp // tm
    hb = tm // HALO

    def body(a_ref, ah_ref, dy_ref, dyn_ref, w_ref, du_in, du_ref, dw_ref, db_ref, ext_ref, edy_ref):
        del du_in
        i = pl.program_id(0)

        @pl.when(i == 0)
        def _():
            dw_ref[...] = jnp.zeros_like(dw_ref)
            db_ref[...] = jnp.zeros_like(db_ref)

        rows = i * tm + _iota((tm, 1), 0)
        a = a_ref[...]
        p, sq = a[:, :D_CONV], _sig(a[:, D_CONV:])
        valid = rows >= META_PAD
        ah = ah_ref[...]
        ext_ref[0:HALO, :] = jnp.where(i > 0, ah[:, :D_CONV] * _sig(ah[:, D_CONV:]), 0.0)
        ext_ref[HALO:HALO + tm, :] = jnp.where(valid, p * sq, 0.0)
        dy = dy_ref[...]
        edy_ref[0:tm, :] = dy
        edy_ref[tm:tm + HALO, :] = jnp.where(i < nb - 1, dyn_ref[...], 0.0)
        db_ref[...] += jnp.sum(dy, axis=0, keepdims=True)
        base = HALO - (CONV_WIDTH - 1)
        du0 = jnp.zeros((tm, D_CONV), F32)
        for k in range(CONV_WIDTH):
            du0 = du0 + w_ref[k:k + 1, :] * edy_ref[pl.ds(CONV_WIDTH - 1 - k, tm), :]
            dw_ref[k:k + 1, :] += jnp.sum(dy * ext_ref[pl.ds(base + k, tm), :], axis=0, keepdims=True)
        du0 = jnp.where(valid, du0, 0.0)
        du_ref[...] = jnp.concatenate([du0 * sq, du0 * p * sq * (1.0 - sq)], axis=1).astype(MXU_DTYPE)

    return pl.pallas_call(
        body, name="conv_bwd2", grid=(nb,),
        in_specs=_conv_specs(tm)[:2] + [pl.BlockSpec((tm, D_CONV), lambda i: (i, 0)),
                                        pl.BlockSpec((HALO, D_CONV), lambda i: (jnp.minimum((i + 1) * hb, nb * hb - 1), 0)),
                                        _full_spec((CONV_WIDTH, D_CONV)), pl.BlockSpec(memory_space=pl.ANY)],
        out_specs=[pl.BlockSpec((tm, W_A), lambda i: (i, OFF_A // W_A)), _full_spec((CONV_WIDTH, D_CONV)),
                   _full_spec((1, D_CONV))],
        out_shape=[jax.ShapeDtypeStruct(du.shape, du.dtype), jax.ShapeDtypeStruct((CONV_WIDTH, D_CONV), F32),
                   jax.ShapeDtypeStruct((1, D_CONV), F32)],
        scratch_shapes=[pltpu.VMEM((HALO + tm, D_CONV), F32), pltpu.VMEM((tm + HALO, D_CONV), F32)],
        input_output_aliases={5: 0},
        compiler_params=_cp(("arbitrary",)),
    )(u, u, dy, dy, cw, du)


def _hg_chunk_fwd(blk, lb, valid, tri):
    bq, bf, v = blk[:, 0:512], blk[:, 512:1024], blk[:, 1024:1536]
    sgq = _sig(bq)
    qt = bq * sgq
    sz = _sig(bf)
    f = lb + (1.0 - lb) * sz
    g = jnp.where(valid, jnp.log(jnp.maximum(f, F_FLOOR)), 0.0)
    k = jnp.where(valid, (1.0 - lb) * (1.0 - sz), 0.0)
    b = jnp.dot(tri, g, precision=_HI, preferred_element_type=F32)
    ridx = _iota((CHUNK, 1), 0)
    ref = jnp.sum(jnp.where(ridx == CHUNK // 2 - 1, b, 0.0), axis=0, keepdims=True)
    bl = jnp.sum(jnp.where(ridx == CHUNK - 1, b, 0.0), axis=0, keepdims=True)
    eq = jnp.exp(jnp.minimum(b - ref, EXP_CLAMP))
    ek = jnp.exp(jnp.minimum(ref - b, EXP_CLAMP))
    e = jnp.exp(b)
    ekl = jnp.exp(bl - b)
    el = jnp.exp(bl)
    return dict(bq=bq, sgq=sgq, qt=qt, sz=sz, f=f, k=k, v=v, eq=eq, ek=ek, e=e, ekl=ekl, el=el,
                qd=qt * eq, kd=k * ek, qe=qt * e, kl=k * ekl)


def _hgrn_fwd(u, lb, gg):
    lp = u.shape[0]
    tm = TM_BR
    cpb = tm // CHUNK

    def body(u_ref, lb_ref, gg_ref, y_ref, st_ref, s_ref):
        i = pl.program_id(0)

        @pl.when(i == 0)
        def _():
            s_ref[...] = jnp.zeros_like(s_ref)

        lbv = lb_ref[...]
        ggv = gg_ref[...]
        tri = (_iota((CHUNK, CHUNK), 0) >= _iota((CHUNK, CHUNK), 1)).astype(F32)

        def chunk(c, carry):
            r0 = pl.multiple_of(c * CHUNK, CHUNK)
            blk = u_ref[pl.ds(r0, CHUNK), :]
            valid = (i * tm + r0 + _iota((CHUNK, 1), 0)) >= META_PAD
            q = _hg_chunk_fwd(blk, lbv, valid, tri)
            outs = []
            for hh in range(HG_HEADS):
                sl = slice(hh * HG_D, (hh + 1) * HG_D)
                a = jnp.where(tri > 0, _dot_nt(q["qd"][:, sl], q["kd"][:, sl]), 0.0)
                st = s_ref[hh]
                st_ref[c, hh] = st
                o = _dot(a, q["v"][:, sl]) + _dot_nt(q["qe"][:, sl], st)
                s_ref[hh] = st * q["el"][:, sl] + _dot_tn(q["v"][:, sl], q["kl"][:, sl])
                rs = lax.rsqrt(jnp.mean(o * o, axis=-1, keepdims=True) + EPS)
                outs.append(o * rs * ggv)
            on = jnp.concatenate(outs, axis=1)
            bg = blk[:, 1536:2048]
            y_ref[pl.ds(r0, CHUNK), :] = (on * bg * _sig(bg)).astype(MXU_DTYPE)
            return carry

        lax.fori_loop(0, cpb, chunk, 0)

    return pl.pallas_call(
        body, name="hgrn_fwd", grid=(lp // tm,),
        in_specs=[pl.BlockSpec((tm, W_B), lambda i: (i, 0)), _full_spec((1, D_HG)), _full_spec((1, HG_D))],
        out_specs=[pl.BlockSpec((tm, D_HG), lambda i: (i, 0)),
                   pl.BlockSpec((cpb, HG_HEADS, HG_D, HG_D), lambda i: (i, 0, 0, 0))],
        out_shape=[jax.ShapeDtypeStruct((lp, D_HG), MXU_DTYPE),
                   jax.ShapeDtypeStruct((lp // CHUNK, HG_HEADS, HG_D, HG_D), F32)],
        scratch_shapes=[pltpu.VMEM((HG_HEADS, HG_D, HG_D), F32)],
        compiler_params=_cp(("arbitrary",)),
    )(u, lb, gg)


def _hgrn_bwd(u, dyb, states, du, lb, gg):
    lp = u.shape[0]
    tm = TM_BR
    cpb = tm // CHUNK
    nb = lp // tm

    def body(u_ref, dy_ref, st_ref, lb_ref, gg_ref, du_in, du_ref, dlb_ref, dgg_ref, ds_ref):
        del du_in
        ii = pl.program_id(0)
        i = nb - 1 - ii

        @pl.when(ii == 0)
        def _():
            ds_ref[...] = jnp.zeros_like(ds_ref)
            dlb_ref[...] = jnp.zeros_like(dlb_ref)
            dgg_ref[...] = jnp.zeros_like(dgg_ref)

        lbv = lb_ref[...]
        ggv = gg_ref[...]
        lower = _iota((CHUNK, CHUNK), 0) >= _iota((CHUNK, CHUNK), 1)
        tri = lower.astype(F32)
        triu = (_iota((CHUNK, CHUNK), 0) <= _iota((CHUNK, CHUNK), 1)).astype(F32)
        ridx = _iota((CHUNK, 1), 0)

        def chunk(cc, carry):
            c = cpb - 1 - cc
            r0 = pl.multiple_of(c * CHUNK, CHUNK)
            blk = u_ref[pl.ds(r0, CHUNK), :]
            valid = (i * tm + r0 + _iota((CHUNK, 1), 0)) >= META_PAD
            q = _hg_chunk_fwd(blk, lbv, valid, tri)
            bg = blk[:, 1536:2048]
            sg = _sig(bg)
            dy = dy_ref[pl.ds(r0, CHUNK), :].astype(F32)
            don_all = dy * bg * sg
            dqt_l, dk_l, dv_l, db_l, dbl_l, on_l = [], [], [], [], [], []
            dgg = jnp.zeros((1, HG_D), F32)
            for hh in range(HG_HEADS):
                sl = slice(hh * HG_D, (hh + 1) * HG_D)
                qd, kd, qe, kl, v = q["qd"][:, sl], q["kd"][:, sl], q["qe"][:, sl], q["kl"][:, sl], q["v"][:, sl]
                el = q["el"][:, sl]
                a = jnp.where(lower, _dot_nt(qd, kd), 0.0)
                st = st_ref[c, hh]
                o = _dot(a, v) + _dot_nt(qe, st)
                rs = lax.rsqrt(jnp.mean(o * o, axis=-1, keepdims=True) + EPS)
                xh = o * rs
                on_l.append(xh * ggv)
                don = don_all[:, sl]
                dgg = dgg + jnp.sum(don * xh, axis=0, keepdims=True)
                dxh = don * ggv
                do = rs * (dxh - xh * jnp.mean(dxh * xh, axis=-1, keepdims=True))
                dst = ds_ref[hh]
                dv = _dot_tn(a, do) + _dot_nt(kl, dst)
                da = jnp.where(lower, _dot_nt(do, v), 0.0)
                dqe = _dot(do, st)
                dkl = _dot(v, dst)
                d_el = jnp.sum(st * dst, axis=0, keepdims=True)
                ds_ref[hh] = _dot_tn(do, qe) + dst * el
                dqd = _dot(da, kd)
                dkd = _dot_tn(da, qd)
                dqt_l.append(dqd * q["eq"][:, sl] + dqe * q["e"][:, sl])
                dk_l.append(dkd * q["ek"][:, sl] + dkl * q["ekl"][:, sl])
                dv_l.append(dv)
                db_l.append(dqd * qd - dkd * kd + dqe * qe - dkl * kl)
                dbl_l.append(jnp.sum(dkl * kl, axis=0, keepdims=True) + d_el * el)
            dqt = jnp.concatenate(dqt_l, axis=1)
            dk = jnp.concatenate(dk_l, axis=1)
            dv = jnp.concatenate(dv_l, axis=1)
            db = jnp.concatenate(db_l, axis=1) + jnp.where(ridx == CHUNK - 1, jnp.concatenate(dbl_l, axis=1), 0.0)
            on = jnp.concatenate(on_l, axis=1)
            dg = jnp.dot(triu, db, precision=_HI, preferred_element_type=F32)
            sz, f = q["sz"], q["f"]
            df = jnp.where(valid & (f > F_FLOOR), dg / f, 0.0)
            dkv = jnp.where(valid, dk, 0.0)
            t = (1.0 - sz) * (df - dkv)
            dlb_ref[...] += jnp.sum(t, axis=0, keepdims=True)
            dz = (1.0 - lbv) * (df - dkv) * sz * (1.0 - sz)
            dbq = dqt * _dsilu(q["bq"], q["sgq"])
            dbg = dy * on * _dsilu(bg, sg)
            dgg_ref[...] += dgg
            du_ref[pl.ds(r0, CHUNK), :] = jnp.concatenate([dbq, dz, dv, dbg], axis=1).astype(MXU_DTYPE)
            return carry

        lax.fori_loop(0, cpb, chunk, 0)

    return pl.pallas_call(
        body, name="hgrn_bwd", grid=(nb,),
        in_specs=[pl.BlockSpec((tm, W_B), lambda ii: (nb - 1 - ii, 0)), pl.BlockSpec((tm, D_HG), lambda ii: (nb - 1 - ii, 0)),
                  pl.BlockSpec((cpb, HG_HEADS, HG_D, HG_D), lambda ii: (nb - 1 - ii, 0, 0, 0)),
                  _full_spec((1, D_HG)), _full_spec((1, HG_D)), pl.BlockSpec(memory_space=pl.ANY)],
        out_specs=[pl.BlockSpec((tm, W_B), lambda ii: (nb - 1 - ii, 0)), _full_spec((1, D_HG)), _full_spec((1, HG_D))],
        out_shape=[jax.ShapeDtypeStruct(du.shape, du.dtype), jax.ShapeDtypeStruct((1, D_HG), F32),
                   jax.ShapeDtypeStruct((1, HG_D), F32)],
        scratch_shapes=[pltpu.VMEM((HG_HEADS, HG_D, HG_D), F32)],
        input_output_aliases={5: 0},
        compiler_params=_cp(("arbitrary",)),
    )(u, dyb, states, lb, gg, du)


N_KEYS = 2 * TM_BR
PREV_ROWS = N_KEYS - CHUNK - TM_BR
ATT_SCALE = ATT_HEAD_DIM ** -0.5
NEG = -1e30


def _half_sum(x, lo):
    a = jnp.sum(jnp.where(lo, x, 0.0), axis=1, keepdims=True)
    b = jnp.sum(jnp.where(lo, 0.0, x), axis=1, keepdims=True)
    return jnp.where(lo, a, b)


def _half_rms(x, lo):
    return lax.rsqrt(_half_sum(x * x, lo) * (1.0 / ATT_HEAD_DIM) + EPS)


def _swa_mask(i, tm):
    tq = i * tm + _iota((tm, N_KEYS), 0)
    s = _iota((tm, N_KEYS), 1)
    nq = tq >> 6
    kr = i * tm + s - (N_KEYS - tm)
    kc = kr >> 6
    band = (kr >= META_PAD) & (kc >= nq - WINDOW_CHUNKS) & (kc <= nq)
    meta = (nq > WINDOW_CHUNKS) & (s >= META_PAD)
    return ((s < CHUNK) & meta) | ((s >= CHUNK) & band)


def _swa_keys(own_kv, prev_ref, meta_ref, kg, tm):
    kv = jnp.concatenate([meta_ref[...], prev_ref[tm - PREV_ROWS:tm, :], own_kv], axis=0)
    k_raw, v = kv[:, :D_KV], kv[:, D_KV:]
    lo = _iota((1, D_KV), 1) < ATT_HEAD_DIM
    kr = _half_rms(k_raw, lo)
    kn = k_raw * kr * kg
    return k_raw, kr, kn, v, lo


def _placed(x, lo):
    xr = pltpu.roll(x, ATT_HEAD_DIM, 1)
    z = jnp.zeros_like(x)
    return [[jnp.where(lo, x, z).astype(MXU_DTYPE), jnp.where(lo, z, xr).astype(MXU_DTYPE)],
            [jnp.where(lo, xr, z).astype(MXU_DTYPE), jnp.where(lo, z, x).astype(MXU_DTYPE)]]


def _swa_specs(tm, order):
    kvb = (OFF_C + 1024) // 256
    return [pl.BlockSpec((tm, W_C), lambda i: (order(i), OFF_C // W_C)),
            pl.BlockSpec((tm, 256), lambda i: (jnp.maximum(order(i) - 1, 0), kvb)),
            pl.BlockSpec((CHUNK, 256), lambda i: (0, kvb)),
            _full_spec((1, D_KV)), _full_spec((1, D_KV)), pl.BlockSpec(memory_space=pltpu.SMEM)]


def _swa_fwd(u, qg, kg, sinks):
    lp = u.shape[0]
    tm = TM_BR

    def body(own_ref, prev_ref, meta_ref, qg_ref, kg_ref, sink_ref, y_ref):
        i = pl.program_id(0)
        own = own_ref[...]
        _, _, kn, v, lo = _swa_keys(own[:, 1024:1280], prev_ref, meta_ref, kg_ref[...], tm)
        kuse, vuse = _placed(kn, lo), _placed(v, lo)
        mask = _swa_mask(i, tm)
        for gi in range(ATT_Q_HEADS // 2):
            j = gi // 2
            sl = slice(gi * 128, (gi + 1) * 128)
            qraw = own[:, sl]
            qn = qraw * _half_rms(qraw, lo) * qg_ref[...]
            og = jnp.zeros((tm, 128), F32)
            for e in range(2):
                qm = jnp.where(lo if e == 0 else ~lo, qn, 0.0)
                s = jnp.where(mask, _dot_nt(qm, kuse[j][e]) * ATT_SCALE, NEG)
                sk = sink_ref[2 * gi + e]
                m = jnp.maximum(jnp.max(s, axis=-1, keepdims=True), sk)
                p = jnp.exp(s - m)
                den = jnp.sum(p, axis=-1, keepdims=True) + jnp.exp(sk - m)
                og = og + _dot(p / den, vuse[j][e])
            gt = own[:, 512 + gi * 128:512 + (gi + 1) * 128]
            y_ref[:, sl] = (og * gt * _sig(gt)).astype(MXU_DTYPE)

    return pl.pallas_call(
        body, name="swa_fwd", grid=(lp // tm,),
        in_specs=_swa_specs(tm, lambda i: i),
        out_specs=pl.BlockSpec((tm, D_ATT), lambda i: (i, 0)),
        out_shape=jax.ShapeDtypeStruct((lp, D_ATT), MXU_DTYPE),
        compiler_params=_cp(("arbitrary",)),
    )(u, u, u, qg, kg, sinks)


def _swa_bwd(u, dyc, du, qg, kg, sinks):
    lp = u.shape[0]
    tm = TM_BR
    nb = lp // tm
    order = lambda ii: nb - 1 - ii

    def body(own_ref, prev_ref, meta_ref, qg_ref, kg_ref, sink_ref, dy_ref, du_in, du_ref, dqg_ref, dkg_ref, dsk_ref,
             carry_ref, macc_ref):
        del du_in
        ii = pl.program_id(0)
        i = nb - 1 - ii

        @pl.when(ii == 0)
        def _():
            carry_ref[...] = jnp.zeros_like(carry_ref)
            macc_ref[...] = jnp.zeros_like(macc_ref)
            dqg_ref[...] = jnp.zeros_like(dqg_ref)
            dkg_ref[...] = jnp.zeros_like(dkg_ref)
            dsk_ref[...] = jnp.zeros_like(dsk_ref)

        own = own_ref[...]
        k_raw, krs, kn, v, lo = _swa_keys(own[:, 1024:1280], prev_ref, meta_ref, kg_ref[...], tm)
        kuse, vuse = _placed(kn, lo), _placed(v, lo)
        mask = _swa_mask(i, tm)
        dkn = jnp.zeros((N_KEYS, D_KV), F32)
        dvn = jnp.zeros((N_KEYS, D_KV), F32)
        for gi in range(ATT_Q_HEADS // 2):
            j = gi // 2
            sl = slice(gi * 128, (gi + 1) * 128)
            qraw = own[:, sl]
            qr = _half_rms(qraw, lo)
            qxh = qraw * qr
            qn = qxh * qg_ref[...]
            ps, pk, qms = [], [], []
            og = jnp.zeros((tm, 128), F32)
            for e in range(2):
                qm = jnp.where(lo if e == 0 else ~lo, qn, 0.0)
                s = jnp.where(mask, _dot_nt(qm, kuse[j][e]) * ATT_SCALE, NEG)
                sk = sink_ref[2 * gi + e]
                m = jnp.maximum(jnp.max(s, axis=-1, keepdims=True), sk)
                p = jnp.exp(s - m)
                den = jnp.sum(p, axis=-1, keepdims=True) + jnp.exp(sk - m)
                p = p / den
                ps.append(p)
                pk.append(jnp.exp(sk - m) / den)
                qms.append(qm)
                og = og + _dot(p, vuse[j][e])
            gt = own[:, 512 + gi * 128:512 + (gi + 1) * 128]
            sg = _sig(gt)
            dy = dy_ref[:, sl].astype(F32)
            dgt = dy * og * _dsilu(gt, sg)
            dog = dy * gt * sg
            dqn = jnp.zeros((tm, 128), F32)
            for e in range(2):
                half = lo if e == 0 else ~lo
                dog_m = jnp.where(half, dog, 0.0)
                dl = jnp.sum(dog_m * og, axis=1, keepdims=True)
                dp = _dot_nt(dog_m, vuse[j][e])
                ds = ps[e] * (dp - dl) * ATT_SCALE
                hsk = 2 * gi + e
                dsk_ref[hsk:hsk + 1, :] += jnp.zeros((1, 128), F32) - jnp.sum(pk[e] * dl, axis=0, keepdims=True)
                dqn = dqn + _dot(ds, kuse[j][e])
                dk_e = _dot_tn(ds, qms[e])
                dv_e = _dot_tn(ps[e], dog_m)
                if j != e:
                    dk_e = pltpu.roll(dk_e, ATT_HEAD_DIM, 1)
                    dv_e = pltpu.roll(dv_e, ATT_HEAD_DIM, 1)
                dkn = dkn + dk_e
                dvn = dvn + dv_e
            dqg_ref[...] += jnp.sum(dqn * qxh, axis=0, keepdims=True)
            dqx = dqn * qg_ref[...]
            dq = qr * (dqx - qxh * _half_sum(dqx * qxh, lo) * (1.0 / ATT_HEAD_DIM))
            du_ref[:, sl] = dq.astype(MXU_DTYPE)
            du_ref[:, 512 + gi * 128:512 + (gi + 1) * 128] = dgt.astype(MXU_DTYPE)

        macc_ref[...] += jnp.concatenate([dkn[0:CHUNK], dvn[0:CHUNK]], axis=1)
        own0 = N_KEYS - tm
        tot = jnp.concatenate([dkn[own0:], dvn[own0:]], axis=1) + carry_ref[...]
        carry_ref[0:tm - PREV_ROWS, :] = jnp.zeros((tm - PREV_ROWS, 2 * D_KV), F32)
        carry_ref[tm - PREV_ROWS:tm, :] = jnp.concatenate([dkn[CHUNK:own0], dvn[CHUNK:own0]], axis=1)
        first = jnp.where((i == 0) & (_iota((tm, 1), 0) < CHUNK), 1.0, 0.0)
        tot = tot + first * jnp.concatenate([macc_ref[...], jnp.zeros((tm - CHUNK, 2 * D_KV), F32)], axis=0)
        dkn_own, dv_own = tot[:, :D_KV], tot[:, D_KV:]
        kx = k_raw[own0:] * krs[own0:]
        dkg_ref[...] += jnp.sum(dkn_own * kx, axis=0, keepdims=True)
        dkx = dkn_own * kg_ref[...]
        dk = krs[own0:] * (dkx - kx * _half_sum(dkx * kx, lo) * (1.0 / ATT_HEAD_DIM))
        du_ref[:, 1024:1152] = dk.astype(MXU_DTYPE)
        du_ref[:, 1152:1280] = dv_own.astype(MXU_DTYPE)
        du_ref[:, 1280:W_C] = jnp.zeros((tm, W_C - 1280), MXU_DTYPE)

    return pl.pallas_call(
        body, name="swa_bwd", grid=(nb,),
        in_specs=_swa_specs(tm, order) + [pl.BlockSpec((tm, D_ATT), lambda ii: (order(ii), 0)),
                                          pl.BlockSpec(memory_space=pl.ANY)],
        out_specs=[pl.BlockSpec((tm, W_C), lambda ii: (order(ii), OFF_C // W_C)), _full_spec((1, 128)),
                   _full_spec((1, 128)), _full_spec((ATT_Q_HEADS, 128))],
        out_shape=[jax.ShapeDtypeStruct(du.shape, du.dtype), jax.ShapeDtypeStruct((1, 128), F32),
                   jax.ShapeDtypeStruct((1, 128), F32), jax.ShapeDtypeStruct((ATT_Q_HEADS, 128), F32)],
        scratch_shapes=[pltpu.VMEM((tm, 2 * D_KV), F32), pltpu.VMEM((CHUNK, 2 * D_KV), F32)],
        input_output_aliases={7: 0},
        compiler_params=_cp(("arbitrary",)),
    )(u, u, u, qg, kg, sinks, dyc, du)


def _load_once(pairs, first):
    @pl.when(first)
    def _():
        for src, dst in pairs:
            pltpu.sync_copy(src, dst)


def _mix_fwd(h, u, ya, yb, yc, wa, wb, wc, wo):
    lp = h.shape[0]
    tm = TM_BR

    def body(h_ref, g_ref, ya_ref, yb_ref, yc_ref, wa_hbm, wb_hbm, wc_hbm, wo_hbm, out_ref, wa_ref, wb_ref, wc_ref,
             wo_ref):
        _load_once(((wa_hbm, wa_ref), (wb_hbm, wb_ref), (wc_hbm, wc_ref), (wo_hbm, wo_ref)), pl.program_id(0) == 0)
        mixed = jnp.zeros((tm, D_MODEL), F32)
        for n, (y_ref, w_ref) in enumerate(((ya_ref, wa_ref), (yb_ref, wb_ref), (yc_ref, wc_ref))):
            z = jnp.dot(y_ref[...], w_ref[...], preferred_element_type=F32)
            mixed = mixed + _sig(g_ref[:, n * D_MODEL:(n + 1) * D_MODEL]) * z
        out_ref[...] = h_ref[...] + _dot(mixed, wo_ref[...])

    ybs = pl.BlockSpec((tm, 512), lambda i: (i, 0))
    anyspec = pl.BlockSpec(memory_space=pl.ANY)
    return pl.pallas_call(
        body, name="mix_fwd", grid=(lp // tm,),
        in_specs=[pl.BlockSpec((tm, D_MODEL), lambda i: (i, 0)), pl.BlockSpec((tm, W_G), lambda i: (i, OFF_G // W_G)),
                  ybs, ybs, ybs, anyspec, anyspec, anyspec, anyspec],
        out_specs=pl.BlockSpec((tm, D_MODEL), lambda i: (i, 0)),
        out_shape=jax.ShapeDtypeStruct((lp, D_MODEL), F32),
        scratch_shapes=[pltpu.VMEM((512, D_MODEL), MXU_DTYPE)] * 3 + [pltpu.VMEM((D_MODEL, D_MODEL), MXU_DTYPE)],
        compiler_params=_cp(("arbitrary",)),
    )(h, u, ya, yb, yc, wa, wb, wc, wo)


def _mix_bwd(dh, u, ya, yb, yc, wa, wb, wc, wo):
    lp = dh.shape[0]
    tm = TM_BR
    nb = lp // tm

    def body(dh_ref, g_ref, ya_ref, yb_ref, yc_ref, wa_hbm, wb_hbm, wc_hbm, wo_hbm,
             du_ref, dya_ref, dyb_ref, dyc_ref, dwa_hbm, dwb_hbm, dwc_hbm, dwo_hbm,
             wa_ref, wb_ref, wc_ref, wo_ref, dwa_ref, dwb_ref, dwc_ref, dwo_ref):
        i = pl.program_id(0)
        _load_once(((wa_hbm, wa_ref), (wb_hbm, wb_ref), (wc_hbm, wc_ref), (wo_hbm, wo_ref)), i == 0)

        @pl.when(i == 0)
        def _():
            for r in (dwa_ref, dwb_ref, dwc_ref, dwo_ref):
                r[...] = jnp.zeros_like(r)

        dh_b = dh_ref[...].astype(MXU_DTYPE)
        dmixed = _dot_nt(dh_b, wo_ref[...])
        mixed = jnp.zeros((tm, D_MODEL), F32)
        for n, (y_ref, w_ref, dy_ref, dw_ref) in enumerate(((ya_ref, wa_ref, dya_ref, dwa_ref),
                                                            (yb_ref, wb_ref, dyb_ref, dwb_ref),
                                                            (yc_ref, wc_ref, dyc_ref, dwc_ref))):
            y = y_ref[...]
            z = jnp.dot(y, w_ref[...], preferred_element_type=F32)
            gate = _sig(g_ref[:, n * D_MODEL:(n + 1) * D_MODEL])
            mixed = mixed + gate * z
            du_ref[:, n * D_MODEL:(n + 1) * D_MODEL] = (z * dmixed * gate * (1.0 - gate)).astype(MXU_DTYPE)
            dz = (gate * dmixed).astype(MXU_DTYPE)
            dy_ref[...] = _dot_nt(dz, w_ref[...]).astype(MXU_DTYPE)
            dw_ref[...] += _dot_tn(y, dz)
        dwo_ref[...] += _dot_tn(mixed, dh_b)

        @pl.when(i == nb - 1)
        def _():
            for src, dst in ((dwa_ref, dwa_hbm), (dwb_ref, dwb_hbm), (dwc_ref, dwc_hbm), (dwo_ref, dwo_hbm)):
                pltpu.sync_copy(src, dst)

    ybs = pl.BlockSpec((tm, 512), lambda i: (i, 0))
    anyspec = pl.BlockSpec(memory_space=pl.ANY)
    wsh = jax.ShapeDtypeStruct((512, D_MODEL), F32)
    return pl.pallas_call(
        body, name="mix_bwd", grid=(nb,),
        in_specs=[pl.BlockSpec((tm, D_MODEL), lambda i: (i, 0)), pl.BlockSpec((tm, W_G), lambda i: (i, OFF_G // W_G)),
                  ybs, ybs, ybs, anyspec, anyspec, anyspec, anyspec],
        out_specs=[pl.BlockSpec((tm, W_G), lambda i: (i, OFF_G // W_G)), ybs, ybs, ybs, anyspec, anyspec, anyspec, anyspec],
        out_shape=[jax.ShapeDtypeStruct((lp, NP), MXU_DTYPE)] + [jax.ShapeDtypeStruct((lp, 512), MXU_DTYPE)] * 3
        + [wsh, wsh, wsh, jax.ShapeDtypeStruct((D_MODEL, D_MODEL), F32)],
        scratch_shapes=[pltpu.VMEM((512, D_MODEL), MXU_DTYPE)] * 3 + [pltpu.VMEM((D_MODEL, D_MODEL), MXU_DTYPE)]
        + [pltpu.VMEM((512, D_MODEL), F32)] * 3 + [pltpu.VMEM((D_MODEL, D_MODEL), F32)],
        compiler_params=_cp(("arbitrary",)),
    )(dh, u, ya, yb, yc, wa, wb, wc, wo)


def _loss_head(h, target_p, seq):
    lp = h.shape[0]
    tm = TM_BR

    def body(h_ref, t_ref, dh_ref, loss_ref):
        i = pl.program_id(0)

        @pl.when(i == 0)
        def _():
            loss_ref[...] = jnp.zeros_like(loss_ref)

        rows = i * tm + _iota((tm, 1), 0)
        e = jnp.where((rows >= CHUNK) & (rows < CHUNK + seq), h_ref[...] - t_ref[...], 0.0)
        dh_ref[...] = e * (1.0 / D_MODEL)
        part = jnp.sum(jnp.mean(e * e, axis=-1, keepdims=True), axis=0, keepdims=True)
        loss_ref[...] += 0.5 * part

    return pl.pallas_call(
        body, name="loss_head", grid=(lp // tm,),
        in_specs=[pl.BlockSpec((tm, D_MODEL), lambda i: (i, 0))] * 2,
        out_specs=[pl.BlockSpec((tm, D_MODEL), lambda i: (i, 0)), _full_spec((1, 128))],
        out_shape=[jax.ShapeDtypeStruct((lp, D_MODEL), F32), jax.ShapeDtypeStruct((1, 128), F32)],
        compiler_params=_cp(("arbitrary",)),
    )(h, target_p)


def _lb_rows(p_ref):
    depth = p_ref.shape[0]
    rows = [p_ref[l:l + 1, :] for l in range(depth)]
    mx = functools.reduce(jnp.maximum, rows)
    ex = [jnp.exp(r - mx) for r in rows]
    tot = functools.reduce(jnp.add, ex)
    sm = [e / tot for e in ex]
    cs, run = [], jnp.zeros_like(sm[0])
    for l in range(depth):
        run = run + sm[l]
        cs.append(run)
    return sm, [c - sm[0] for c in cs]


def _lb_fwd(p):
    def body(p_ref, o_ref):
        _, xs = _lb_rows(p_ref)
        for l, xl in enumerate(xs):
            o_ref[l:l + 1, :] = jnp.clip(xl, 0.0, 1.0)

    return pl.pallas_call(body, name="lb_fwd", out_shape=jax.ShapeDtypeStruct(p.shape, F32))(p)


def _lb_bwd(p, dlb):
    def body(p_ref, d_ref, o_ref):
        sm, xs = _lb_rows(p_ref)
        depth = len(xs)
        dx = []
        for l in range(depth):
            x = xs[l]
            g0 = jnp.where(x > 0.0, 1.0, jnp.where(x == 0.0, 0.5, 0.0))
            y = jnp.maximum(x, 0.0)
            g1 = jnp.where(y < 1.0, 1.0, jnp.where(y == 1.0, 0.5, 0.0))
            dx.append(d_ref[l:l + 1, :] * g0 * g1)
        dsm = [functools.reduce(jnp.add, dx[jj:]) for jj in range(depth)]
        dsm[0] = dsm[0] - functools.reduce(jnp.add, dx)
        inner = functools.reduce(jnp.add, [a * b for a, b in zip(sm, dsm)])
        for l in range(depth):
            o_ref[l:l + 1, :] = sm[l] * (dsm[l] - inner)

    return pl.pallas_call(body, name="lb_bwd", out_shape=jax.ShapeDtypeStruct(p.shape, F32))(p, dlb)


def _my_index():
    return 4 * lax.axis_index("x") + 2 * lax.axis_index("y") + lax.axis_index("c")


def _mesh_id(p):
    return (p >> 2, (p >> 1) & 1, p & 1)


def _exchange(xs, name, scatter):
    n = len(xs)

    def body(*refs):
        x_refs, o_refs = refs[:n], refs[n:2 * n]
        send_sems, recv_sems, loc_sems = refs[2 * n:]
        me = _my_index()
        local = []
        for a in range(n):
            src = x_refs[a].at[me] if scatter else x_refs[a]
            cp = pltpu.make_async_copy(src, o_refs[a].at[me], loc_sems.at[a])
            cp.start()
            local.append(cp)
        sends = []
        for k in range(1, N_DEV):
            peer = (me + k) % N_DEV
            for a in range(n):
                src = x_refs[a].at[peer] if scatter else x_refs[a]
                cp = pltpu.make_async_remote_copy(src_ref=src, dst_ref=o_refs[a].at[me], send_sem=send_sems.at[a, k - 1],
                                                  recv_sem=recv_sems.at[a, k - 1], device_id=_mesh_id(peer),
                                                  device_id_type=pl.DeviceIdType.MESH)
                cp.start()
                sends.append(cp)
        for k in range(1, N_DEV):
            frm = (me + N_DEV - k) % N_DEV
            for a in range(n):
                src = x_refs[a].at[me] if scatter else x_refs[a]
                pltpu.make_async_remote_copy(src_ref=src, dst_ref=o_refs[a].at[frm], send_sem=send_sems.at[a, k - 1],
                                             recv_sem=recv_sems.at[a, k - 1], device_id=_mesh_id(frm),
                                             device_id_type=pl.DeviceIdType.MESH).wait_recv()
        for cp in sends:
            cp.wait_send()
        for cp in local:
            cp.wait()

    out_shape = [jax.ShapeDtypeStruct(x.shape if scatter else (N_DEV,) + x.shape, x.dtype) for x in xs]
    anyspec = pl.BlockSpec(memory_space=pl.ANY)
    return pl.pallas_call(
        body, name=name, in_specs=[anyspec] * n, out_specs=[anyspec] * n, out_shape=out_shape,
        scratch_shapes=[pltpu.SemaphoreType.DMA((n, N_DEV - 1)), pltpu.SemaphoreType.DMA((n, N_DEV - 1)),
                        pltpu.SemaphoreType.DMA((n,))],
        compiler_params=pltpu.CompilerParams(has_side_effects=True),
    )(*xs)


def _adamw(gp, w, m, v, name):
    r, cc = w.shape
    tr = 256 if r % 256 == 0 else r

    def body(g_ref, w_ref, m_ref, v_ref, go_ref, d_ref, mo_ref, vo_ref):
        g = g_ref[0]
        for s in range(1, N_DEV):
            g = g + g_ref[s]
        go_ref[...] = g
        mn = ADAM_B1 * m_ref[...] + (1.0 - ADAM_B1) * g
        vn = ADAM_B2 * v_ref[...] + (1.0 - ADAM_B2) * (g * g)
        m_hat = mn / (1.0 - ADAM_B1 ** ADAM_STEP)
        v_hat = vn / (1.0 - ADAM_B2 ** ADAM_STEP)
        d_ref[...] = -ADAM_LR * (m_hat / (jnp.sqrt(v_hat) + ADAM_EPS) + ADAM_WD * w_ref[...])
        mo_ref[...] = mn
        vo_ref[...] = vn

    bs = pl.BlockSpec((tr, cc), lambda i: (i, 0))
    sh = jax.ShapeDtypeStruct((r, cc), F32)
    return pl.pallas_call(
        body, name=name, grid=(r // tr,),
        in_specs=[pl.BlockSpec((N_DEV, tr, cc), lambda i: (0, i, 0)), bs, bs, bs],
        out_specs=[bs, bs, bs, bs], out_shape=[sh, sh, sh, sh],
        compiler_params=_cp(("parallel",)),
    )(gp, w, m, v)


def _pack_cols(w):
    parts, pos = [], 0
    for pstart, ostart, width in _PACK:
        if pstart != pos:
            parts.append(jnp.zeros(w.shape[:-1] + (pstart - pos,), w.dtype))
        parts.append(w[..., ostart:ostart + width])
        pos = pstart + width
    return jnp.concatenate(parts, axis=-1)


def _unpack_cols(wp):
    by_orig = sorted(_PACK, key=lambda t: t[1])
    return jnp.concatenate([wp[..., p:p + wd] for p, _, wd in by_orig], axis=-1)


def _local_step(x, target, meta, params):
    seq = x.shape[0]
    depth = params["norm_g"].shape[0]
    l_real = seq + CHUNK
    lp = -(-l_real // TM_MM) * TM_MM
    tail = lp - l_real
    zeros = lambda n: jnp.zeros((n, D_MODEL), F32)
    h = jnp.concatenate([zeros(META_PAD), meta, x, zeros(tail)], axis=0)
    target_p = jnp.concatenate([zeros(CHUNK), target, zeros(tail)], axis=0)

    lb_all = _lb_fwd(params["hg_lower_bounds"])
    tile2 = lambda a: jnp.concatenate([a, a], axis=-1)
    row = lambda a, l: a[l][None, :]
    wp = _pack_cols(params["w_in"].astype(MXU_DTYPE))
    wpt = jnp.swapaxes(wp, 1, 2)
    wa, wb, wc, wo = (params[k].astype(MXU_DTYPE) for k in ("w_conv_out", "w_hg_out", "w_att_out", "w_out"))

    saved = []
    for l in range(depth):
        u, hn = _inproj_fwd(h, row(params["norm_g"], l), wp[l])
        ya = _conv_fwd(u, params["conv_w"][l], row(params["conv_b"], l), row(params["conv_ln_g"], l),
                       row(params["conv_ln_b"], l))
        yb, states = _hgrn_fwd(u, row(lb_all, l), row(params["hg_norm_g"], l))
        yc = _swa_fwd(u, tile2(row(params["q_norm_g"], l)), tile2(row(params["k_norm_g"], l)), params["attn_sinks"][l])
        h_next = _mix_fwd(h, u, ya, yb, yc, wa[l], wb[l], wc[l], wo[l])
        saved.append((h, u, hn, ya, yb, yc, states))
        h = h_next

    dh, loss_row = _loss_head(h, target_p, seq)
    loss = loss_row[0, 0]

    grads = {k: [None] * depth for k in params}
    for l in reversed(range(depth)):
        h_l, u, hn, ya, yb, yc, states = saved[l]
        du, dya, dyb, dyc, dwa, dwb, dwc, dwo = _mix_bwd(dh, u, ya, yb, yc, wa[l], wb[l], wc[l], wo[l])
        cw = params["conv_w"][l]
        du, dy, dlg, dlb_ = _conv_bwd1(u, dya, du, cw, row(params["conv_b"], l), row(params["conv_ln_g"], l),
                                       row(params["conv_ln_b"], l))
        du, dcw, dcb = _conv_bwd2(u, dy, du, cw)
        du, dlbl, dgg = _hgrn_bwd(u, dyb, states, du, row(lb_all, l), row(params["hg_norm_g"], l))
        du, dqg, dkg, dsk = _swa_bwd(u, dyc, du, tile2(row(params["q_norm_g"], l)), tile2(row(params["k_norm_g"], l)),
                                     params["attn_sinks"][l])
        dwp = _inproj_bwd_dw(hn, du)
        dh, dng = _inproj_bwd_dh(du, wpt[l], h_l, row(params["norm_g"], l), dh)
        fold = lambda a: a[0, :ATT_HEAD_DIM] + a[0, ATT_HEAD_DIM:]
        for k, val in (("w_in", _unpack_cols(dwp)), ("w_conv_out", dwa), ("w_hg_out", dwb), ("w_att_out", dwc),
                       ("w_out", dwo), ("conv_w", dcw), ("conv_b", dcb[0]), ("conv_ln_g", dlg[0]), ("conv_ln_b", dlb_[0]),
                       ("hg_lower_bounds", dlbl[0]), ("hg_norm_g", dgg[0]), ("q_norm_g", fold(dqg)),
                       ("k_norm_g", fold(dkg)), ("attn_sinks", dsk[:, 0]), ("norm_g", dng[0])):
            grads[k][l] = val
    grads = {k: jnp.stack(v) for k, v in grads.items()}
    grads["hg_lower_bounds"] = _lb_bwd(params["hg_lower_bounds"], grads["hg_lower_bounds"])
    grad_x = dh[CHUNK:CHUNK + seq]
    grad_meta = dh[META_PAD:CHUNK]
    return loss, grad_x, grad_meta, grads


_SHARDED = ("meta_tokens", "w_in", "conv_w", "w_conv_out", "w_hg_out", "w_att_out", "w_out")
_REPLICATED = ("norm_g", "conv_b", "conv_ln_g", "conv_ln_b", "hg_lower_bounds", "hg_norm_g", "q_norm_g", "k_norm_g",
               "attn_sinks")
_WEIGHTS = ("meta_tokens", "norm_g", "w_in", "conv_w", "conv_b", "conv_ln_g", "conv_ln_b", "w_conv_out",
            "hg_lower_bounds", "hg_norm_g", "w_hg_out", "q_norm_g", "k_norm_g", "attn_sinks", "w_att_out", "w_out")
_ROW_SHARDED = ("w_out",)


def _assemble(name, g):
    if name in _ROW_SHARDED:
        return jnp.moveaxis(g, 0, 1).reshape(g.shape[1], N_DEV * g.shape[2], g.shape[3])
    full = jnp.moveaxis(g, 0, -2)
    return full.reshape(full.shape[:-2] + (N_DEV * full.shape[-1],))


def _split(name, full):
    if name in _ROW_SHARDED:
        d, r, c = full.shape
        return jnp.moveaxis(full.reshape(d, N_DEV, r // N_DEV, c), 1, 0)
    c = full.shape[-1] // N_DEV
    return jnp.moveaxis(full.reshape(full.shape[:-1] + (N_DEV, c)), -2, 0)


def _as2d(a):
    return a.reshape((-1, a.shape[-1]))


def kernel(x, meta_tokens, norm_g, w_in, conv_w, conv_b, conv_ln_g, conv_ln_b, w_conv_out, hg_lower_bounds, hg_norm_g, w_hg_out, q_norm_g, k_norm_g, attn_sinks, w_att_out, w_out, loss_target, m_meta_tokens, m_norm_g, m_w_in, m_conv_w, m_conv_b, m_conv_ln_g, m_conv_ln_b, m_w_conv_out, m_hg_lower_bounds, m_hg_norm_g, m_w_hg_out, m_q_norm_g, m_k_norm_g, m_attn_sinks, m_w_att_out, m_w_out, v_meta_tokens, v_norm_g, v_w_in, v_conv_w, v_conv_b, v_conv_ln_g, v_conv_ln_b, v_w_conv_out, v_hg_lower_bounds, v_hg_norm_g, v_w_hg_out, v_q_norm_g, v_k_norm_g, v_attn_sinks, v_w_att_out, v_w_out):
    w = dict(meta_tokens=meta_tokens, norm_g=norm_g, w_in=w_in, conv_w=conv_w, conv_b=conv_b, conv_ln_g=conv_ln_g,
             conv_ln_b=conv_ln_b, w_conv_out=w_conv_out, hg_lower_bounds=hg_lower_bounds, hg_norm_g=hg_norm_g,
             w_hg_out=w_hg_out, q_norm_g=q_norm_g, k_norm_g=k_norm_g, attn_sinks=attn_sinks, w_att_out=w_att_out,
             w_out=w_out)
    m = dict(meta_tokens=m_meta_tokens, norm_g=m_norm_g, w_in=m_w_in, conv_w=m_conv_w, conv_b=m_conv_b,
             conv_ln_g=m_conv_ln_g, conv_ln_b=m_conv_ln_b, w_conv_out=m_w_conv_out, hg_lower_bounds=m_hg_lower_bounds,
             hg_norm_g=m_hg_norm_g, w_hg_out=m_w_hg_out, q_norm_g=m_q_norm_g, k_norm_g=m_k_norm_g,
             attn_sinks=m_attn_sinks, w_att_out=m_w_att_out, w_out=m_w_out)
    v = dict(meta_tokens=v_meta_tokens, norm_g=v_norm_g, w_in=v_w_in, conv_w=v_conv_w, conv_b=v_conv_b,
             conv_ln_g=v_conv_ln_g, conv_ln_b=v_conv_ln_b, w_conv_out=v_w_conv_out, hg_lower_bounds=v_hg_lower_bounds,
             hg_norm_g=v_hg_norm_g, w_hg_out=v_w_hg_out, q_norm_g=v_q_norm_g, k_norm_g=v_k_norm_g,
             attn_sinks=v_attn_sinks, w_att_out=v_w_att_out, w_out=v_w_out)

    narrow = ("w_in", "w_conv_out", "w_hg_out", "w_att_out", "w_out")
    send = [w[k].astype(MXU_DTYPE) if k in narrow else w[k] for k in _SHARDED]
    gathered = _exchange(send, "gather_weights", scatter=False)
    full = {k: _assemble(k, g) for k, g in zip(_SHARDED, gathered)}
    params = {k: full[k] for k in _SHARDED if k != "meta_tokens"}
    params.update({k: w[k] for k in _REPLICATED})

    loss, grad_x, grad_meta, grads = _local_step(x[0], loss_target[0], full["meta_tokens"], params)
    grads["meta_tokens"] = grad_meta
    loss = lax.psum(loss, ("x", "y", "c"))

    pieces = _exchange([_split(k, grads[k]) for k in _SHARDED], "scatter_grads", scatter=True)
    small = jnp.concatenate([grads[k].reshape(-1) for k in _REPLICATED])
    n_small = small.shape[0]
    pad = -n_small % 128
    small = jnp.concatenate([small, jnp.zeros((pad,), F32)]).reshape(-1, 128)
    small_all = _exchange([small], "gather_small_grads", scatter=False)[0].reshape(N_DEV, -1)

    out_g, out_d, out_m, out_v = {}, {}, {}, {}
    for k, gp in zip(_SHARDED, pieces):
        shp = w[k].shape
        res = _adamw(gp.reshape((N_DEV,) + _as2d(w[k]).shape), _as2d(w[k]), _as2d(m[k]), _as2d(v[k]), "adamw_" + k)
        out_g[k], out_d[k], out_m[k], out_v[k] = (r.reshape(shp) for r in res)
    off = 0
    for k in _REPLICATED:
        shp = w[k].shape
        n = w[k].size
        gp = small_all[:, off:off + n].reshape((N_DEV,) + shp)
        off += n
        res = _adamw(gp, w[k], m[k], v[k], "adamw_" + k)
        out_g[k], out_d[k], out_m[k], out_v[k] = res

    return (loss, grad_x[None], *[out_g[k] for k in _WEIGHTS], *[out_d[k] for k in _WEIGHTS],
            *[out_m[k] for k in _WEIGHTS], *[out_v[k] for k in _WEIGHTS])
```

```python
import functools

import jax
import jax.numpy as jnp
from jax import lax
from jax.experimental import pallas as pl
from jax.experimental.pallas import tpu as pltpu

F32 = jnp.float32
MXU_DTYPE = jnp.bfloat16

D_MODEL = 1024
CHUNK = 64
N_META = 16
META_PAD = CHUNK - N_META
D_CONV = 512
CONV_WIDTH = 31
HG_HEADS = 4
HG_D = 128
D_HG = HG_HEADS * HG_D
F_FLOOR = 1e-30
ATT_Q_HEADS = 8
ATT_HEAD_DIM = 64
D_ATT = 512
D_KV = 128
WINDOW_CHUNKS = 2
EPS = 1e-6
D_IN = 7936
N_DEV = 8

ADAM_LR = 0.001
ADAM_B1 = 0.9
ADAM_B2 = 0.999
ADAM_EPS = 1e-08
ADAM_WD = 0.01
ADAM_STEP = 10

NP = 8192
OFF_B, W_B = 0, 2048
OFF_A, W_A = 2048, 1024
OFF_G, W_G = 3072, 3072
OFF_C, W_C = 6144, 1536
OFF_AG, W_AG = 7680, 512
_PACK = ((0, 1536, 2048), (2048, 0, 1024), (3072, 4864, 3072), (6144, 3584, 512), (6656, 4352, 512),
         (7168, 4096, 256), (7680, 1024, 512))
_PAD_AT, _PAD_W = 7424, 256

TM_MM = 1280
TM_BR = 256
HALO = 32
EXP_CLAMP = 80.0
VMEM_LIMIT = 56 * 1024 * 1024

_HI = lax.Precision.HIGHEST


def _cp(sem):
    return pltpu.CompilerParams(dimension_semantics=sem, vmem_limit_bytes=VMEM_LIMIT)


def _sig(x):
    return 1.0 / (1.0 + jnp.exp(-x))


def _dot(a, b):
    return jnp.dot(a.astype(MXU_DTYPE), b.astype(MXU_DTYPE), preferred_element_type=F32)


def _dot_nt(a, b):
    return lax.dot_general(a.astype(MXU_DTYPE), b.astype(MXU_DTYPE), (((1,), (1,)), ((), ())),
                           preferred_element_type=F32)


def _dot_tn(a, b):
    return lax.dot_general(a.astype(MXU_DTYPE), b.astype(MXU_DTYPE), (((0,), (0,)), ((), ())),
                           preferred_element_type=F32)


def _iota(shape, dim):
    return lax.broadcasted_iota(jnp.int32, shape, dim)


def _full_spec(shape):
    nd = len(shape)
    return pl.BlockSpec(shape, lambda *_: (0,) * nd)


def _my_index():
    return 4 * lax.axis_index("x") + 2 * lax.axis_index("y") + lax.axis_index("c")


def _mesh_id(p):
    return (p >> 2, (p >> 1) & 1, p & 1)


class _Exchange:
    def __init__(self, x_refs, o_refs, send_sems, recv_sems, loc_sems, scatter, dst=lambda o, s: o.at[s]):
        me = _my_index()
        self.local, self.sends, self.recvs = [], [], []
        for a, (x, o) in enumerate(zip(x_refs, o_refs)):
            mine = x.at[me] if scatter else x
            self.local.append(pltpu.make_async_copy(mine, dst(o, me), loc_sems.at[a]))
            for k in range(1, N_DEV):
                to, frm = (me + k) % N_DEV, (me + N_DEV - k) % N_DEV
                sems = dict(send_sem=send_sems.at[a, k - 1], recv_sem=recv_sems.at[a, k - 1],
                            device_id_type=pl.DeviceIdType.MESH)
                self.sends.append(pltpu.make_async_remote_copy(
                    src_ref=x.at[to] if scatter else x, dst_ref=dst(o, me), device_id=_mesh_id(to), **sems))
                self.recvs.append(pltpu.make_async_remote_copy(
                    src_ref=mine, dst_ref=dst(o, frm), device_id=_mesh_id(frm), **sems))

    def start(self):
        for cp in self.local + self.sends:
            cp.start()

    def finish(self):
        for cp in self.recvs:
            cp.wait_recv()
        for cp in self.sends:
            cp.wait_send()
        for cp in self.local:
            cp.wait()


def _exchange_sems(n):
    return [pltpu.SemaphoreType.DMA((n, N_DEV - 1)), pltpu.SemaphoreType.DMA((n, N_DEV - 1)),
            pltpu.SemaphoreType.DMA((n,))]


_ANY = pl.BlockSpec(memory_space=pl.ANY)


def _inproj_fwd(h, g, wp, gather=()):
    lp = h.shape[0]
    tm, tn = TM_MM, 1024
    ni, nj = lp // tm, NP // tn
    n = len(gather)

    def body(h_ref, g_ref, w_ref, *rest):
        x_refs, (u_ref, hn_ref), o_refs = rest[:n], rest[n:n + 2], rest[n + 2:2 * n + 2]
        hs_ref, sems = rest[2 * n + 2], rest[2 * n + 3:]
        i, j = pl.program_id(0), pl.program_id(1)
        if n:
            @pl.when((i == 0) & (j == 0))
            def _():
                _Exchange(x_refs, o_refs, *sems, scatter=False).start()

        @pl.when(j == 0)
        def _():
            x = h_ref[...]
            r = lax.rsqrt(jnp.mean(x * x, axis=-1, keepdims=True) + EPS)
            hn = (x * r * g_ref[...]).astype(MXU_DTYPE)
            hs_ref[...] = hn
            hn_ref[...] = hn
        u_ref[...] = jnp.dot(hs_ref[...], w_ref[...], preferred_element_type=F32)
        if n:
            @pl.when((i == ni - 1) & (j == nj - 1))
            def _():
                _Exchange(x_refs, o_refs, *sems, scatter=False).finish()

    res = pl.pallas_call(
        body, name="inproj_fwd_gather" if n else "inproj_fwd", grid=(ni, nj),
        in_specs=[pl.BlockSpec((tm, D_MODEL), lambda i, j: (i, 0)), pl.BlockSpec((1, D_MODEL), lambda i, j: (0, 0)),
                  pl.BlockSpec((D_MODEL, tn), lambda i, j: (0, j))] + [_ANY] * n,
        out_specs=[pl.BlockSpec((tm, tn), lambda i, j: (i, j)), pl.BlockSpec((tm, D_MODEL), lambda i, j: (i, 0))]
        + [_ANY] * n,
        out_shape=[jax.ShapeDtypeStruct((lp, NP), F32), jax.ShapeDtypeStruct((lp, D_MODEL), MXU_DTYPE)]
        + [jax.ShapeDtypeStruct((N_DEV,) + x.shape, x.dtype) for x in gather],
        scratch_shapes=[pltpu.VMEM((tm, D_MODEL), MXU_DTYPE)] + (_exchange_sems(n) if n else []),
        compiler_params=_cp(("arbitrary", "arbitrary")),
    )(h, g, wp, *gather)
    return res[0], res[1], list(res[2:])


def _inproj_bwd_dh(du, wpt, h, g, dh_next, pieces, stacked, layer, depth):
    lp = h.shape[0]
    tm, tk = TM_MM, 1024
    ni, nk = lp // tm, NP // tk
    n = len(pieces)
    n_acc = 0 if stacked is None else n

    def body(du_ref, w_ref, h_ref, g_ref, dhn_ref, *rest):
        x_refs, (dh_ref, dg_ref), o_refs = rest[:n], rest[n + n_acc:n + n_acc + 2], rest[n + n_acc + 2:2 * n + n_acc + 2]
        acc_ref, sems = rest[2 * n + n_acc + 2], rest[2 * n + n_acc + 3:]
        i, k = pl.program_id(0), pl.program_id(1)
        slot = lambda o, s: o.at[s, layer]

        @pl.when((i == 0) & (k == 0))
        def _():
            _Exchange(x_refs, o_refs, *sems, scatter=True, dst=slot).start()
            dg_ref[...] = jnp.zeros_like(dg_ref)

        @pl.when(k == 0)
        def _():
            acc_ref[...] = jnp.zeros_like(acc_ref)

        acc_ref[...] += jnp.dot(du_ref[...], w_ref[...], preferred_element_type=F32)

        @pl.when(k == nk - 1)
        def _():
            dhn = acc_ref[...]
            x = h_ref[...]
            r = lax.rsqrt(jnp.mean(x * x, axis=-1, keepdims=True) + EPS)
            xh = x * r
            dg_ref[...] += jnp.sum(dhn * xh, axis=0, keepdims=True)
            dxh = dhn * g_ref[...]
            dx = r * (dxh - xh * jnp.mean(dxh * xh, axis=-1, keepdims=True))
            dh_ref[...] = dhn_ref[...] + dx

        @pl.when((i == ni - 1) & (k == nk - 1))
        def _():
            _Exchange(x_refs, o_refs, *sems, scatter=True, dst=slot).finish()

    acc_in = [] if stacked is None else list(stacked)
    res = pl.pallas_call(
        body, name="inproj_bwd_dh_scatter", grid=(ni, nk),
        in_specs=[pl.BlockSpec((tm, tk), lambda i, k: (i, k)), pl.BlockSpec((tk, D_MODEL), lambda i, k: (k, 0)),
                  pl.BlockSpec((tm, D_MODEL), lambda i, k: (i, 0)), pl.BlockSpec((1, D_MODEL), lambda i, k: (0, 0)),
                  pl.BlockSpec((tm, D_MODEL), lambda i, k: (i, 0))] + [_ANY] * (n + n_acc),
        out_specs=[pl.BlockSpec((tm, D_MODEL), lambda i, k: (i, 0)), pl.BlockSpec((1, D_MODEL), lambda i, k: (0, 0))]
        + [_ANY] * n,
        out_shape=[jax.ShapeDtypeStruct((lp, D_MODEL), F32), jax.ShapeDtypeStruct((1, D_MODEL), F32)]
        + [jax.ShapeDtypeStruct((N_DEV, depth) + p.shape[1:], p.dtype) for p in pieces],
        scratch_shapes=[pltpu.VMEM((tm, D_MODEL), F32)] + _exchange_sems(n),
        input_output_aliases={5 + n + a: 2 + a for a in range(n_acc)},
        compiler_params=_cp(("arbitrary", "arbitrary")),
    )(du, wpt, h, g, dh_next, *pieces, *acc_in)
    return res[0], res[1], list(res[2:])


def _inproj_bwd_dw(hn, du):
    lp = hn.shape[0]
    tm, tn = TM_MM, 1024

    def body(hn_ref, du_ref, dw_ref):
        @pl.when(pl.program_id(1) == 0)
        def _():
            dw_ref[...] = jnp.zeros_like(dw_ref)
        dw_ref[...] += _dot_tn(hn_ref[...], du_ref[...])

    return pl.pallas_call(
        body, name="inproj_bwd_dw", grid=(NP // tn, lp // tm),
        in_specs=[pl.BlockSpec((tm, D_MODEL), lambda j, m: (m, 0)), pl.BlockSpec((tm, tn), lambda j, m: (m, j))],
        out_specs=pl.BlockSpec((D_MODEL, tn), lambda j, m: (0, j)),
        out_shape=jax.ShapeDtypeStruct((D_MODEL, NP), F32),
        compiler_params=_cp(("parallel", "arbitrary")),
    )(hn, du)


N_SHIFT = 8
CONV_SUB = 32


def _shift_copies(src_ref, sh_ref):
    n = sh_ref.shape[1]
    for b in range(1, N_SHIFT):
        sh_ref[b - 1, :, :] = src_ref[pl.ds(b, n), :]


def _window(src_ref, sh_ref, off, r0, n):
    a, b = divmod(off, N_SHIFT)
    start = pl.multiple_of(r0 + a * N_SHIFT, N_SHIFT)
    if b == 0:
        return src_ref[pl.ds(start, n), :]
    return sh_ref[b - 1, pl.ds(start, n), :]


def _shift_scratch(tm):
    return pltpu.VMEM((N_SHIFT - 1, tm + HALO - N_SHIFT, D_CONV), F32)


def _glu_ext(a_ref, ah_ref, ext_ref, sh_ref, i, tm):
    rows = i * tm + _iota((tm, 1), 0)
    a = a_ref[...]
    p, sq = a[:, :D_CONV], _sig(a[:, D_CONV:])
    valid = rows >= META_PAD
    ah = ah_ref[...]
    ext_ref[0:HALO, :] = jnp.where(i > 0, ah[:, :D_CONV] * _sig(ah[:, D_CONV:]), 0.0)
    ext_ref[HALO:HALO + tm, :] = jnp.where(valid, p * sq, 0.0)
    _shift_copies(ext_ref, sh_ref)
    return p, sq, valid


def _layernorm_stats(y):
    mu = jnp.mean(y, axis=-1, keepdims=True)
    yc = y - mu
    rstd = lax.rsqrt(jnp.mean(yc * yc, axis=-1, keepdims=True) + EPS)
    return yc * rstd, rstd


def _conv_specs(tm):
    hb = tm // HALO
    return [pl.BlockSpec((tm, W_A), lambda i: (i, OFF_A // W_A)),
            pl.BlockSpec((HALO, W_A), lambda i: (jnp.maximum(i * hb - 1, 0), OFF_A // W_A)),
            pl.BlockSpec((tm, W_AG), lambda i: (i, OFF_AG // W_AG))]


def _conv_fwd(u, cw, cb, lg, lb_):
    lp = u.shape[0]
    tm = TM_BR

    def body(a_ref, ah_ref, ag_ref, w_ref, b_ref, lg_ref, lb_ref, ya_ref, y_ref, ext_ref, sh_ref):
        i = pl.program_id(0)
        _glu_ext(a_ref, ah_ref, ext_ref, sh_ref, i, tm)
        base = HALO - (CONV_WIDTH - 1)

        y = jnp.zeros((tm, D_CONV), F32) + b_ref[...]
        for k in range(CONV_WIDTH):
            y = y + w_ref[k:k + 1, :] * _window(ext_ref, sh_ref, base + k, 0, tm)
        y_ref[...] = y
        xh, _ = _layernorm_stats(y)
        yn = xh * lg_ref[...] + lb_ref[...]
        gt = ag_ref[...]
        ya_ref[...] = (yn * _sig(yn) * gt * _sig(gt)).astype(MXU_DTYPE)

    rowspec = pl.BlockSpec((tm, D_CONV), lambda i: (i, 0))
    return pl.pallas_call(
        body, name="conv_fwd", grid=(lp // tm,),
        in_specs=_conv_specs(tm) + [_full_spec((CONV_WIDTH, D_CONV))] + [_full_spec((1, D_CONV))] * 3,
        out_specs=[rowspec, rowspec],
        out_shape=[jax.ShapeDtypeStruct((lp, D_CONV), MXU_DTYPE), jax.ShapeDtypeStruct((lp, D_CONV), F32)],
        scratch_shapes=[pltpu.VMEM((HALO + tm, D_CONV), F32), _shift_scratch(tm)],
        compiler_params=_cp(("arbitrary",)),
    )(u, u, u, cw, cb, lg, lb_)


def _dsilu(x, s):
    return s * (1.0 + x * (1.0 - s))


def _conv_bwd1(u, y, dya, du, lg, lb_):
    lp = u.shape[0]
    tm = TM_BR

    def body(ag_ref, y_ref, dya_ref, lg_ref, lb_ref, du_in, du_ref, dy_ref, dlg_ref, dlb_ref):
        del du_in
        i = pl.program_id(0)

        @pl.when(i == 0)
        def _():
            dlg_ref[...] = jnp.zeros_like(dlg_ref)
            dlb_ref[...] = jnp.zeros_like(dlb_ref)

        xh, rstd = _layernorm_stats(y_ref[...])
        yn = xh * lg_ref[...] + lb_ref[...]
        s1 = _sig(yn)
        gt = ag_ref[...]
        s2 = _sig(gt)
        do = dya_ref[...].astype(F32)
        du_ref[...] = (do * (yn * s1) * _dsilu(gt, s2)).astype(MXU_DTYPE)
        dyn = do * (gt * s2) * _dsilu(yn, s1)
        dlg_ref[...] += jnp.sum(dyn * xh, axis=0, keepdims=True)
        dlb_ref[...] += jnp.sum(dyn, axis=0, keepdims=True)
        dxh = dyn * lg_ref[...]
        dy_ref[...] = rstd * (dxh - jnp.mean(dxh, axis=-1, keepdims=True)
                              - xh * jnp.mean(dxh * xh, axis=-1, keepdims=True))

    rowspec = pl.BlockSpec((tm, D_CONV), lambda i: (i, 0))
    return pl.pallas_call(
        body, name="conv_bwd1", grid=(lp // tm,),
        in_specs=[_conv_specs(tm)[2], rowspec, rowspec, _full_spec((1, D_CONV)), _full_spec((1, D_CONV)),
                  pl.BlockSpec(memory_space=pl.ANY)],
        out_specs=[pl.BlockSpec((tm, W_AG), lambda i: (i, OFF_AG // W_AG)), rowspec,
                   _full_spec((1, D_CONV)), _full_spec((1, D_CONV))],
        out_shape=[jax.ShapeDtypeStruct(du.shape, du.dtype), jax.ShapeDtypeStruct((lp, D_CONV), F32),
                   jax.ShapeDtypeStruct((1, D_CONV), F32), jax.ShapeDtypeStruct((1, D_CONV), F32)],
        input_output_aliases={5: 0},
        compiler_params=_cp(("arbitrary",)),
    )(u, y, dya, lg, lb_, du)


def _conv_bwd2(u, dy, du, cw):
    lp = u.shape[0]
    tm = TM_BR
    nb = lp // tm
    hb = tm // HALO

    def body(a_ref, ah_ref, dy_ref, dyn_ref, w_ref, du_in, du_ref, dw_ref, db_ref, ext_ref, sh_ref, edy_ref, shd_ref,
             dwp_ref):
        del du_in
        i = pl.program_id(0)

        @pl.when(i == 0)
        def _():
            dwp_ref[...] = jnp.zeros_like(dwp_ref)
            db_ref[...] = jnp.zeros_like(db_ref)

        _glu_ext(a_ref, ah_ref, ext_ref, sh_ref, i, tm)
        dy_all = dy_ref[...]
        edy_ref[0:tm, :] = dy_all
        edy_ref[tm:tm + HALO, :] = jnp.where(i < nb - 1, dyn_ref[...], 0.0)
        _shift_copies(edy_ref, shd_ref)
        db_ref[...] += jnp.sum(dy_all, axis=0, keepdims=True)
        base = HALO - (CONV_WIDTH - 1)

        def fold8(x):
            parts = [x[s:s + N_SHIFT] for s in range(0, CONV_SUB, N_SHIFT)]
            return functools.reduce(jnp.add, parts)

        def sub(r, carry):
            r0 = pl.multiple_of(r * CONV_SUB, CONV_SUB)
            dy = dy_ref[pl.ds(r0, CONV_SUB), :]
            du0 = jnp.zeros((CONV_SUB, D_CONV), F32)
            for k in range(CONV_WIDTH):
                du0 = du0 + w_ref[k:k + 1, :] * _window(edy_ref, shd_ref, CONV_WIDTH - 1 - k, r0, CONV_SUB)
                dwp_ref[k] += fold8(dy * _window(ext_ref, sh_ref, base + k, r0, CONV_SUB))
            a = a_ref[pl.ds(r0, CONV_SUB), :]
            p, sq = a[:, :D_CONV], _sig(a[:, D_CONV:])
            valid = (i * tm + r0 + _iota((CONV_SUB, 1), 0)) >= META_PAD
            du0 = jnp.where(valid, du0, 0.0)
            du_ref[pl.ds(r0, CONV_SUB), :] = jnp.concatenate([du0 * sq, du0 * p * sq * (1.0 - sq)],
                                                             axis=1).astype(MXU_DTYPE)
            return carry

        lax.fori_loop(0, tm // CONV_SUB, sub, 0)

        @pl.when(i == nb - 1)
        def _():
            dw_ref[...] = jnp.sum(dwp_ref[...], axis=1)

    return pl.pallas_call(
        body, name="conv_bwd2", grid=(nb,),
        in_specs=_conv_specs(tm)[:2] + [pl.BlockSpec((tm, D_CONV), lambda i: (i, 0)),
                                        pl.BlockSpec((HALO, D_CONV), lambda i: (jnp.minimum((i + 1) * hb, nb * hb - 1), 0)),
                                        _full_spec((CONV_WIDTH, D_CONV)), pl.BlockSpec(memory_space=pl.ANY)],
        out_specs=[pl.BlockSpec((tm, W_A), lambda i: (i, OFF_A // W_A)), _full_spec((CONV_WIDTH, D_CONV)),
                   _full_spec((1, D_CONV))],
        out_shape=[jax.ShapeDtypeStruct(du.shape, du.dtype), jax.ShapeDtypeStruct((CONV_WIDTH, D_CONV), F32),
                   jax.ShapeDtypeStruct((1, D_CONV), F32)],
        scratch_shapes=[pltpu.VMEM((HALO + tm, D_CONV), F32), _shift_scratch(tm),
                        pltpu.VMEM((tm + HALO, D_CONV), F32), _shift_scratch(tm),
                        pltpu.VMEM((CONV_WIDTH, N_SHIFT, D_CONV), F32)],
        input_output_aliases={5: 0},
        compiler_params=_cp(("arbitrary",)),
    )(u, u, dy, dy, cw, du)


def _hg_chunk_fwd(blk, lb, valid, tri):
    bq, bf, v = blk[:, 0:512], blk[:, 512:1024], blk[:, 1024:1536]
    sgq = _sig(bq)
    qt = bq * sgq
    sz = _sig(bf)
    f = lb + (1.0 - lb) * sz
    g = jnp.where(valid, jnp.log(jnp.maximum(f, F_FLOOR)), 0.0)
    k = jnp.where(valid, (1.0 - lb) * (1.0 - sz), 0.0)
    b = jnp.dot(tri, g, precision=_HI, preferred_element_type=F32)
    ridx = _iota((CHUNK, 1), 0)
    ref = jnp.sum(jnp.where(ridx == CHUNK // 2 - 1, b, 0.0), axis=0, keepdims=True)
    bl = jnp.sum(jnp.where(ridx == CHUNK - 1, b, 0.0), axis=0, keepdims=True)
    eq = jnp.exp(jnp.minimum(b - ref, EXP_CLAMP))
    ek = jnp.exp(jnp.minimum(ref - b, EXP_CLAMP))
    e = jnp.exp(b)
    ekl = jnp.exp(bl - b)
    el = jnp.exp(bl)
    return dict(bq=bq, sgq=sgq, qt=qt, sz=sz, f=f, k=k, v=v, eq=eq, ek=ek, e=e, ekl=ekl, el=el,
                qd=qt * eq, kd=k * ek, qe=qt * e, kl=k * ekl)


def _hgrn_fwd(u, lb, gg):
    lp = u.shape[0]
    tm = TM_BR
    cpb = tm // CHUNK

    def body(u_ref, lb_ref, gg_ref, y_ref, st_ref, s_ref):
        i = pl.program_id(0)

        @pl.when(i == 0)
        def _():
            s_ref[...] = jnp.zeros_like(s_ref)

        lbv = lb_ref[...]
        ggv = gg_ref[...]
        tri = (_iota((CHUNK, CHUNK), 0) >= _iota((CHUNK, CHUNK), 1)).astype(F32)

        def chunk(c, carry):
            r0 = pl.multiple_of(c * CHUNK, CHUNK)
            blk = u_ref[pl.ds(r0, CHUNK), :]
            valid = (i * tm + r0 + _iota((CHUNK, 1), 0)) >= META_PAD
            q = _hg_chunk_fwd(blk, lbv, valid, tri)
            outs = []
            for hh in range(HG_HEADS):
                sl = slice(hh * HG_D, (hh + 1) * HG_D)
                a = jnp.where(tri > 0, _dot_nt(q["qd"][:, sl], q["kd"][:, sl]), 0.0)
                st = s_ref[hh]
                st_ref[c, hh] = st
                o = _dot(a, q["v"][:, sl]) + _dot_nt(q["qe"][:, sl], st)
                s_ref[hh] = st * q["el"][:, sl] + _dot_tn(q["v"][:, sl], q["kl"][:, sl])
                rs = lax.rsqrt(jnp.mean(o * o, axis=-1, keepdims=True) + EPS)
                outs.append(o * rs * ggv)
            on = jnp.concatenate(outs, axis=1)
            bg = blk[:, 1536:2048]
            y_ref[pl.ds(r0, CHUNK), :] = (on * bg * _sig(bg)).astype(MXU_DTYPE)
            return carry

        lax.fori_loop(0, cpb, chunk, 0, unroll=2)

    return pl.pallas_call(
        body, name="hgrn_fwd", grid=(lp // tm,),
        in_specs=[pl.BlockSpec((tm, W_B), lambda i: (i, 0)), _full_spec((1, D_HG)), _full_spec((1, HG_D))],
        out_specs=[pl.BlockSpec((tm, D_HG), lambda i: (i, 0)),
                   pl.BlockSpec((cpb, HG_HEADS, HG_D, HG_D), lambda i: (i, 0, 0, 0))],
        out_shape=[jax.ShapeDtypeStruct((lp, D_HG), MXU_DTYPE),
                   jax.ShapeDtypeStruct((lp // CHUNK, HG_HEADS, HG_D, HG_D), F32)],
        scratch_shapes=[pltpu.VMEM((HG_HEADS, HG_D, HG_D), F32)],
        compiler_params=_cp(("arbitrary",)),
    )(u, lb, gg)


def _hgrn_bwd(u, dyb, states, du, lb, gg):
    lp = u.shape[0]
    tm = TM_BR
    cpb = tm // CHUNK
    nb = lp // tm

    def body(u_ref, dy_ref, st_ref, lb_ref, gg_ref, du_in, du_ref, dlb_ref, dgg_ref, ds_ref):
        del du_in
        ii = pl.program_id(0)
        i = nb - 1 - ii

        @pl.when(ii == 0)
        def _():
            ds_ref[...] = jnp.zeros_like(ds_ref)
            dlb_ref[...] = jnp.zeros_like(dlb_ref)
            dgg_ref[...] = jnp.zeros_like(dgg_ref)

        lbv = lb_ref[...]
        ggv = gg_ref[...]
        lower = _iota((CHUNK, CHUNK), 0) >= _iota((CHUNK, CHUNK), 1)
        tri = lower.astype(F32)
        triu = (_iota((CHUNK, CHUNK), 0) <= _iota((CHUNK, CHUNK), 1)).astype(F32)
        ridx = _iota((CHUNK, 1), 0)

        def chunk(cc, carry):
            c = cpb - 1 - cc
            r0 = pl.multiple_of(c * CHUNK, CHUNK)
            blk = u_ref[pl.ds(r0, CHUNK), :]
            valid = (i * tm + r0 + _iota((CHUNK, 1), 0)) >= META_PAD
            q = _hg_chunk_fwd(blk, lbv, valid, tri)
            bg = blk[:, 1536:2048]
            sg = _sig(bg)
            dy = dy_ref[pl.ds(r0, CHUNK), :].astype(F32)
            don_all = dy * bg * sg
            dqt_l, dk_l, dv_l, db_l, dbl_l, on_l = [], [], [], [], [], []
            dgg = jnp.zeros((1, HG_D), F32)
            for hh in range(HG_HEADS):
                sl = slice(hh * HG_D, (hh + 1) * HG_D)
                qd, kd, qe, kl, v = q["qd"][:, sl], q["kd"][:, sl], q["qe"][:, sl], q["kl"][:, sl], q["v"][:, sl]
                el = q["el"][:, sl]
                a = jnp.where(lower, _dot_nt(qd, kd), 0.0)
                st = st_ref[c, hh]
                o = _dot(a, v) + _dot_nt(qe, st)
                rs = lax.rsqrt(jnp.mean(o * o, axis=-1, keepdims=True) + EPS)
                xh = o * rs
                on_l.append(xh * ggv)
                don = don_all[:, sl]
                dgg = dgg + jnp.sum(don * xh, axis=0, keepdims=True)
                dxh = don * ggv
                do = rs * (dxh - xh * jnp.mean(dxh * xh, axis=-1, keepdims=True))
                dst = ds_ref[hh]
                dv = _dot_tn(a, do) + _dot_nt(kl, dst)
                da = jnp.where(lower, _dot_nt(do, v), 0.0)
                dqe = _dot(do, st)
                dkl = _dot(v, dst)
                d_el = jnp.sum(st * dst, axis=0, keepdims=True)
                ds_ref[hh] = _dot_tn(do, qe) + dst * el
                dqd = _dot(da, kd)
                dkd = _dot_tn(da, qd)
                dqt_l.append(dqd * q["eq"][:, sl] + dqe * q["e"][:, sl])
                dk_l.append(dkd * q["ek"][:, sl] + dkl * q["ekl"][:, sl])
                dv_l.append(dv)
                db_l.append(dqd * qd - dkd * kd + dqe * qe - dkl * kl)
                dbl_l.append(jnp.sum(dkl * kl, axis=0, keepdims=True) + d_el * el)
            dqt = jnp.concatenate(dqt_l, axis=1)
            dk = jnp.concatenate(dk_l, axis=1)
            dv = jnp.concatenate(dv_l, axis=1)
            db = jnp.concatenate(db_l, axis=1) + jnp.where(ridx == CHUNK - 1, jnp.concatenate(dbl_l, axis=1), 0.0)
            on = jnp.concatenate(on_l, axis=1)
            dg = jnp.dot(triu, db, precision=_HI, preferred_element_type=F32)
            sz, f = q["sz"], q["f"]
            df = jnp.where(valid & (f > F_FLOOR), dg / f, 0.0)
            dkv = jnp.where(valid, dk, 0.0)
            t = (1.0 - sz) * (df - dkv)
            dlb_ref[...] += jnp.sum(t, axis=0, keepdims=True)
            dz = (1.0 - lbv) * (df - dkv) * sz * (1.0 - sz)
            dbq = dqt * _dsilu(q["bq"], q["sgq"])
            dbg = dy * on * _dsilu(bg, sg)
            dgg_ref[...] += dgg
            du_ref[pl.ds(r0, CHUNK), :] = jnp.concatenate([dbq, dz, dv, dbg], axis=1).astype(MXU_DTYPE)
            return carry

        lax.fori_loop(0, cpb, chunk, 0, unroll=2)

    return pl.pallas_call(
        body, name="hgrn_bwd", grid=(nb,),
        in_specs=[pl.BlockSpec((tm, W_B), lambda ii: (nb - 1 - ii, 0)), pl.BlockSpec((tm, D_HG), lambda ii: (nb - 1 - ii, 0)),
                  pl.BlockSpec((cpb, HG_HEADS, HG_D, HG_D), lambda ii: (nb - 1 - ii, 0, 0, 0)),
                  _full_spec((1, D_HG)), _full_spec((1, HG_D)), pl.BlockSpec(memory_space=pl.ANY)],
        out_specs=[pl.BlockSpec((tm, W_B), lambda ii: (nb - 1 - ii, 0)), _full_spec((1, D_HG)), _full_spec((1, HG_D))],
        out_shape=[jax.ShapeDtypeStruct(du.shape, du.dtype), jax.ShapeDtypeStruct((1, D_HG), F32),
                   jax.ShapeDtypeStruct((1, HG_D), F32)],
        scratch_shapes=[pltpu.VMEM((HG_HEADS, HG_D, HG_D), F32)],
        input_output_aliases={5: 0},
        compiler_params=_cp(("arbitrary",)),
    )(u, dyb, states, lb, gg, du)


N_KEYS = 2 * TM_BR
PREV_ROWS = N_KEYS - CHUNK - TM_BR
ATT_SCALE = ATT_HEAD_DIM ** -0.5
NEG = -1e30


def _half_sum(x, lo):
    a = jnp.sum(jnp.where(lo, x, 0.0), axis=1, keepdims=True)
    b = jnp.sum(jnp.where(lo, 0.0, x), axis=1, keepdims=True)
    return jnp.where(lo, a, b)


def _half_rms(x, lo):
    return lax.rsqrt(_half_sum(x * x, lo) * (1.0 / ATT_HEAD_DIM) + EPS)


def _swa_mask(i, tm):
    tq = i * tm + _iota((tm, N_KEYS), 0)
    s = _iota((tm, N_KEYS), 1)
    nq = tq >> 6
    kr = i * tm + s - (N_KEYS - tm)
    kc = kr >> 6
    band = (kr >= META_PAD) & (kc >= nq - WINDOW_CHUNKS) & (kc <= nq)
    meta = (nq > WINDOW_CHUNKS) & (s >= META_PAD)
    return ((s < CHUNK) & meta) | ((s >= CHUNK) & band)


def _swa_keys(own_kv, prev_ref, meta_ref, kg, tm):
    kv = jnp.concatenate([meta_ref[...], prev_ref[tm - PREV_ROWS:tm, :], own_kv], axis=0)
    k_raw, v = kv[:, :D_KV], kv[:, D_KV:]
    lo = _iota((1, D_KV), 1) < ATT_HEAD_DIM
    kr = _half_rms(k_raw, lo)
    kn = k_raw * kr * kg
    return k_raw, kr, kn, v, lo


def _placed(x, lo):
    xr = pltpu.roll(x, ATT_HEAD_DIM, 1)
    z = jnp.zeros_like(x)
    return [[jnp.where(lo, x, z).astype(MXU_DTYPE), jnp.where(lo, z, xr).astype(MXU_DTYPE)],
            [jnp.where(lo, xr, z).astype(MXU_DTYPE), jnp.where(lo, z, x).astype(MXU_DTYPE)]]


def _swa_specs(tm, order):
    kvb = (OFF_C + 1024) // 256
    return [pl.BlockSpec((tm, W_C), lambda i: (order(i), OFF_C // W_C)),
            pl.BlockSpec((tm, 256), lambda i: (jnp.maximum(order(i) - 1, 0), kvb)),
            pl.BlockSpec((CHUNK, 256), lambda i: (0, kvb)),
            _full_spec((1, D_KV)), _full_spec((1, D_KV)), pl.BlockSpec(memory_space=pltpu.SMEM)]


def _swa_fwd(u, qg, kg, sinks):
    lp = u.shape[0]
    tm = TM_BR

    def body(own_ref, prev_ref, meta_ref, qg_ref, kg_ref, sink_ref, y_ref):
        i = pl.program_id(0)
        own = own_ref[...]
        _, _, kn, v, lo = _swa_keys(own[:, 1024:1280], prev_ref, meta_ref, kg_ref[...], tm)
        kuse, vuse = _placed(kn, lo), _placed(v, lo)
        mask = _swa_mask(i, tm)
        for gi in range(ATT_Q_HEADS // 2):
            j = gi // 2
            sl = slice(gi * 128, (gi + 1) * 128)
            qraw = own[:, sl]
            qn = qraw * _half_rms(qraw, lo) * qg_ref[...]
            og = jnp.zeros((tm, 128), F32)
            for e in range(2):
                qm = jnp.where(lo if e == 0 else ~lo, qn, 0.0)
                s = jnp.where(mask, _dot_nt(qm, kuse[j][e]) * ATT_SCALE, NEG)
                sk = sink_ref[2 * gi + e]
                m = jnp.maximum(jnp.max(s, axis=-1, keepdims=True), sk)
                p = jnp.exp(s - m)
                den = jnp.sum(p, axis=-1, keepdims=True) + jnp.exp(sk - m)
                og = og + _dot(p / den, vuse[j][e])
            gt = own[:, 512 + gi * 128:512 + (gi + 1) * 128]
            y_ref[:, sl] = (og * gt * _sig(gt)).astype(MXU_DTYPE)

    return pl.pallas_call(
        body, name="swa_fwd", grid=(lp // tm,),
        in_specs=_swa_specs(tm, lambda i: i),
        out_specs=pl.BlockSpec((tm, D_ATT), lambda i: (i, 0)),
        out_shape=jax.ShapeDtypeStruct((lp, D_ATT), MXU_DTYPE),
        compiler_params=_cp(("arbitrary",)),
    )(u, u, u, qg, kg, sinks)


def _swa_bwd(u, dyc, du, qg, kg, sinks):
    lp = u.shape[0]
    tm = TM_BR
    nb = lp // tm
    order = lambda ii: nb - 1 - ii

    def body(own_ref, prev_ref, meta_ref, qg_ref, kg_ref, sink_ref, dy_ref, du_in, du_ref, dqg_ref, dkg_ref, dsk_ref,
             carry_ref, macc_ref):
        del du_in
        ii = pl.program_id(0)
        i = nb - 1 - ii

        @pl.when(ii == 0)
        def _():
            carry_ref[...] = jnp.zeros_like(carry_ref)
            macc_ref[...] = jnp.zeros_like(macc_ref)
            dqg_ref[...] = jnp.zeros_like(dqg_ref)
            dkg_ref[...] = jnp.zeros_like(dkg_ref)
            dsk_ref[...] = jnp.zeros_like(dsk_ref)

        own = own_ref[...]
        k_raw, krs, kn, v, lo = _swa_keys(own[:, 1024:1280], prev_ref, meta_ref, kg_ref[...], tm)
        kuse, vuse = _placed(kn, lo), _placed(v, lo)
        mask = _swa_mask(i, tm)
        dkn = jnp.zeros((N_KEYS, D_KV), F32)
        dvn = jnp.zeros((N_KEYS, D_KV), F32)
        for gi in range(ATT_Q_HEADS // 2):
            j = gi // 2
            sl = slice(gi * 128, (gi + 1) * 128)
            qraw = own[:, sl]
            qr = _half_rms(qraw, lo)
            qxh = qraw * qr
            qn = qxh * qg_ref[...]
            ps, pk, qms = [], [], []
            og = jnp.zeros((tm, 128), F32)
            for e in range(2):
                qm = jnp.where(lo if e == 0 else ~lo, qn, 0.0)
                s = jnp.where(mask, _dot_nt(qm, kuse[j][e]) * ATT_SCALE, NEG)
                sk = sink_ref[2 * gi + e]
                m = jnp.maximum(jnp.max(s, axis=-1, keepdims=True), sk)
                p = jnp.exp(s - m)
                den = jnp.sum(p, axis=-1, keepdims=True) + jnp.exp(sk - m)
                p = p / den
                ps.append(p)
                pk.append(jnp.exp(sk - m) / den)
                qms.append(qm)
                og = og + _dot(p, vuse[j][e])
            gt = own[:, 512 + gi * 128:512 + (gi + 1) * 128]
            sg = _sig(gt)
            dy = dy_ref[:, sl].astype(F32)
            dgt = dy * og * _dsilu(gt, sg)
            dog = dy * gt * sg
            dqn = jnp.zeros((tm, 128), F32)
            for e in range(2):
                half = lo if e == 0 else ~lo
                dog_m = jnp.where(half, dog, 0.0)
                dl = jnp.sum(dog_m * og, axis=1, keepdims=True)
                dp = _dot_nt(dog_m, vuse[j][e])
                ds = ps[e] * (dp - dl) * ATT_SCALE
                hsk = 2 * gi + e
                dsk_ref[hsk:hsk + 1, :] += jnp.zeros((1, 128), F32) - jnp.sum(pk[e] * dl, axis=0, keepdims=True)
                dqn = dqn + _dot(ds, kuse[j][e])
                dk_e = _dot_tn(ds, qms[e])
                dv_e = _dot_tn(ps[e], dog_m)
                if j != e:
                    dk_e = pltpu.roll(dk_e, ATT_HEAD_DIM, 1)
                    dv_e = pltpu.roll(dv_e, ATT_HEAD_DIM, 1)
                dkn = dkn + dk_e
                dvn = dvn + dv_e
            dqg_ref[...] += jnp.sum(dqn * qxh, axis=0, keepdims=True)
            dqx = dqn * qg_ref[...]
            dq = qr * (dqx - qxh * _half_sum(dqx * qxh, lo) * (1.0 / ATT_HEAD_DIM))
            du_ref[:, sl] = dq.astype(MXU_DTYPE)
            du_ref[:, 512 + gi * 128:512 + (gi + 1) * 128] = dgt.astype(MXU_DTYPE)

        macc_ref[...] += jnp.concatenate([dkn[0:CHUNK], dvn[0:CHUNK]], axis=1)
        own0 = N_KEYS - tm
        tot = jnp.concatenate([dkn[own0:], dvn[own0:]], axis=1) + carry_ref[...]
        carry_ref[0:tm - PREV_ROWS, :] = jnp.zeros((tm - PREV_ROWS, 2 * D_KV), F32)
        carry_ref[tm - PREV_ROWS:tm, :] = jnp.concatenate([dkn[CHUNK:own0], dvn[CHUNK:own0]], axis=1)
        first = jnp.where((i == 0) & (_iota((tm, 1), 0) < CHUNK), 1.0, 0.0)
        tot = tot + first * jnp.concatenate([macc_ref[...], jnp.zeros((tm - CHUNK, 2 * D_KV), F32)], axis=0)
        dkn_own, dv_own = tot[:, :D_KV], tot[:, D_KV:]
        kx = k_raw[own0:] * krs[own0:]
        dkg_ref[...] += jnp.sum(dkn_own * kx, axis=0, keepdims=True)
        dkx = dkn_own * kg_ref[...]
        dk = krs[own0:] * (dkx - kx * _half_sum(dkx * kx, lo) * (1.0 / ATT_HEAD_DIM))
        du_ref[:, 1024:1152] = dk.astype(MXU_DTYPE)
        du_ref[:, 1152:1280] = dv_own.astype(MXU_DTYPE)
        du_ref[:, 1280:W_C] = jnp.zeros((tm, W_C - 1280), MXU_DTYPE)

    return pl.pallas_call(
        body, name="swa_bwd", grid=(nb,),
        in_specs=_swa_specs(tm, order) + [pl.BlockSpec((tm, D_ATT), lambda ii: (order(ii), 0)),
                                          pl.BlockSpec(memory_space=pl.ANY)],
        out_specs=[pl.BlockSpec((tm, W_C), lambda ii: (order(ii), OFF_C // W_C)), _full_spec((1, 128)),
                   _full_spec((1, 128)), _full_spec((ATT_Q_HEADS, 128))],
        out_shape=[jax.ShapeDtypeStruct(du.shape, du.dtype), jax.ShapeDtypeStruct((1, 128), F32),
                   jax.ShapeDtypeStruct((1, 128), F32), jax.ShapeDtypeStruct((ATT_Q_HEADS, 128), F32)],
        scratch_shapes=[pltpu.VMEM((tm, 2 * D_KV), F32), pltpu.VMEM((CHUNK, 2 * D_KV), F32)],
        input_output_aliases={7: 0},
        compiler_params=_cp(("arbitrary",)),
    )(u, u, u, qg, kg, sinks, dyc, du)


def _load_once(pairs, first):
    @pl.when(first)
    def _():
        for src, dst in pairs:
            pltpu.sync_copy(src, dst)


def _mix_fwd(h, u, ya, yb, yc, wa, wb, wc, wo):
    lp = h.shape[0]
    tm = TM_BR

    def body(h_ref, g_ref, ya_ref, yb_ref, yc_ref, wa_hbm, wb_hbm, wc_hbm, wo_hbm, out_ref, wa_ref, wb_ref, wc_ref,
             wo_ref):
        _load_once(((wa_hbm, wa_ref), (wb_hbm, wb_ref), (wc_hbm, wc_ref), (wo_hbm, wo_ref)), pl.program_id(0) == 0)
        mixed = jnp.zeros((tm, D_MODEL), F32)
        for n, (y_ref, w_ref) in enumerate(((ya_ref, wa_ref), (yb_ref, wb_ref), (yc_ref, wc_ref))):
            z = jnp.dot(y_ref[...], w_ref[...], preferred_element_type=F32)
            mixed = mixed + _sig(g_ref[:, n * D_MODEL:(n + 1) * D_MODEL]) * z
        out_ref[...] = h_ref[...] + _dot(mixed, wo_ref[...])

    ybs = pl.BlockSpec((tm, 512), lambda i: (i, 0))
    anyspec = pl.BlockSpec(memory_space=pl.ANY)
    return pl.pallas_call(
        body, name="mix_fwd", grid=(lp // tm,),
        in_specs=[pl.BlockSpec((tm, D_MODEL), lambda i: (i, 0)), pl.BlockSpec((tm, W_G), lambda i: (i, OFF_G // W_G)),
                  ybs, ybs, ybs, anyspec, anyspec, anyspec, anyspec],
        out_specs=pl.BlockSpec((tm, D_MODEL), lambda i: (i, 0)),
        out_shape=jax.ShapeDtypeStruct((lp, D_MODEL), F32),
        scratch_shapes=[pltpu.VMEM((512, D_MODEL), MXU_DTYPE)] * 3 + [pltpu.VMEM((D_MODEL, D_MODEL), MXU_DTYPE)],
        compiler_params=_cp(("arbitrary",)),
    )(h, u, ya, yb, yc, wa, wb, wc, wo)


def _mix_bwd(dh, u, ya, yb, yc, wa, wb, wc, wo):
    lp = dh.shape[0]
    tm = TM_BR
    nb = lp // tm

    def body(dh_ref, g_ref, ya_ref, yb_ref, yc_ref, wa_hbm, wb_hbm, wc_hbm, wo_hbm,
             du_ref, dya_ref, dyb_ref, dyc_ref, dwa_hbm, dwb_hbm, dwc_hbm, dwo_hbm,
             wa_ref, wb_ref, wc_ref, wo_ref, dwa_ref, dwb_ref, dwc_ref, dwo_ref):
        i = pl.program_id(0)
        _load_once(((wa_hbm, wa_ref), (wb_hbm, wb_ref), (wc_hbm, wc_ref), (wo_hbm, wo_ref)), i == 0)

        @pl.when(i == 0)
        def _():
            for r in (dwa_ref, dwb_ref, dwc_ref, dwo_ref):
                r[...] = jnp.zeros_like(r)

        dh_b = dh_ref[...].astype(MXU_DTYPE)
        dmixed = _dot_nt(dh_b, wo_ref[...])
        mixed = jnp.zeros((tm, D_MODEL), F32)
        for n, (y_ref, w_ref, dy_ref, dw_ref) in enumerate(((ya_ref, wa_ref, dya_ref, dwa_ref),
                                                            (yb_ref, wb_ref, dyb_ref, dwb_ref),
                                                            (yc_ref, wc_ref, dyc_ref, dwc_ref))):
            y = y_ref[...]
            z = jnp.dot(y, w_ref[...], preferred_element_type=F32)
            gate = _sig(g_ref[:, n * D_MODEL:(n + 1) * D_MODEL])
            mixed = mixed + gate * z
            du_ref[:, n * D_MODEL:(n + 1) * D_MODEL] = (z * dmixed * gate * (1.0 - gate)).astype(MXU_DTYPE)
            dz = (gate * dmixed).astype(MXU_DTYPE)
            dy_ref[...] = _dot_nt(dz, w_ref[...]).astype(MXU_DTYPE)
            dw_ref[...] += _dot_tn(y, dz)
        dwo_ref[...] += _dot_tn(mixed, dh_b)

        @pl.when(i == nb - 1)
        def _():
            for src, dst in ((dwa_ref, dwa_hbm), (dwb_ref, dwb_hbm), (dwc_ref, dwc_hbm), (dwo_ref, dwo_hbm)):
                pltpu.sync_copy(src, dst)

    ybs = pl.BlockSpec((tm, 512), lambda i: (i, 0))
    anyspec = pl.BlockSpec(memory_space=pl.ANY)
    wsh = jax.ShapeDtypeStruct((512, D_MODEL), F32)
    return pl.pallas_call(
        body, name="mix_bwd", grid=(nb,),
        in_specs=[pl.BlockSpec((tm, D_MODEL), lambda i: (i, 0)), pl.BlockSpec((tm, W_G), lambda i: (i, OFF_G // W_G)),
                  ybs, ybs, ybs, anyspec, anyspec, anyspec, anyspec],
        out_specs=[pl.BlockSpec((tm, W_G), lambda i: (i, OFF_G // W_G)), ybs, ybs, ybs, anyspec, anyspec, anyspec, anyspec],
        out_shape=[jax.ShapeDtypeStruct((lp, NP), MXU_DTYPE)] + [jax.ShapeDtypeStruct((lp, 512), MXU_DTYPE)] * 3
        + [wsh, wsh, wsh, jax.ShapeDtypeStruct((D_MODEL, D_MODEL), F32)],
        scratch_shapes=[pltpu.VMEM((512, D_MODEL), MXU_DTYPE)] * 3 + [pltpu.VMEM((D_MODEL, D_MODEL), MXU_DTYPE)]
        + [pltpu.VMEM((512, D_MODEL), F32)] * 3 + [pltpu.VMEM((D_MODEL, D_MODEL), F32)],
        compiler_params=_cp(("arbitrary",)),
    )(dh, u, ya, yb, yc, wa, wb, wc, wo)


def _loss_head(h, target_p, seq):
    lp = h.shape[0]
    tm = TM_BR

    def body(h_ref, t_ref, dh_ref, loss_ref):
        i = pl.program_id(0)

        @pl.when(i == 0)
        def _():
            loss_ref[...] = jnp.zeros_like(loss_ref)

        rows = i * tm + _iota((tm, 1), 0)
        e = jnp.where((rows >= CHUNK) & (rows < CHUNK + seq), h_ref[...] - t_ref[...], 0.0)
        dh_ref[...] = e * (1.0 / D_MODEL)
        part = jnp.sum(jnp.mean(e * e, axis=-1, keepdims=True), axis=0, keepdims=True)
        loss_ref[...] += 0.5 * part

    return pl.pallas_call(
        body, name="loss_head", grid=(lp // tm,),
        in_specs=[pl.BlockSpec((tm, D_MODEL), lambda i: (i, 0))] * 2,
        out_specs=[pl.BlockSpec((tm, D_MODEL), lambda i: (i, 0)), _full_spec((1, 128))],
        out_shape=[jax.ShapeDtypeStruct((lp, D_MODEL), F32), jax.ShapeDtypeStruct((1, 128), F32)],
        compiler_params=_cp(("arbitrary",)),
    )(h, target_p)


def _lb_rows(p_ref):
    depth = p_ref.shape[0]
    rows = [p_ref[l:l + 1, :] for l in range(depth)]
    mx = functools.reduce(jnp.maximum, rows)
    ex = [jnp.exp(r - mx) for r in rows]
    tot = functools.reduce(jnp.add, ex)
    sm = [e / tot for e in ex]
    cs, run = [], jnp.zeros_like(sm[0])
    for l in range(depth):
        run = run + sm[l]
        cs.append(run)
    return sm, [c - sm[0] for c in cs]


def _lb_fwd(p):
    def body(p_ref, o_ref):
        _, xs = _lb_rows(p_ref)
        for l, xl in enumerate(xs):
            o_ref[l:l + 1, :] = jnp.clip(xl, 0.0, 1.0)

    return pl.pallas_call(body, name="lb_fwd", out_shape=jax.ShapeDtypeStruct(p.shape, F32))(p)


def _lb_bwd(p, dlb):
    def body(p_ref, d_ref, o_ref):
        sm, xs = _lb_rows(p_ref)
        depth = len(xs)
        dx = []
        for l in range(depth):
            x = xs[l]
            g0 = jnp.where(x > 0.0, 1.0, jnp.where(x == 0.0, 0.5, 0.0))
            y = jnp.maximum(x, 0.0)
            g1 = jnp.where(y < 1.0, 1.0, jnp.where(y == 1.0, 0.5, 0.0))
            dx.append(d_ref[l:l + 1, :] * g0 * g1)
        dsm = [functools.reduce(jnp.add, dx[jj:]) for jj in range(depth)]
        dsm[0] = dsm[0] - functools.reduce(jnp.add, dx)
        inner = functools.reduce(jnp.add, [a * b for a, b in zip(sm, dsm)])
        for l in range(depth):
            o_ref[l:l + 1, :] = sm[l] * (dsm[l] - inner)

    return pl.pallas_call(body, name="lb_bwd", out_shape=jax.ShapeDtypeStruct(p.shape, F32))(p, dlb)


def _exchange(gather, scatter, name):
    ng, ns = len(gather), len(scatter)
    n = ng + ns

    def body(*refs):
        x_refs, o_refs, sems = refs[:n], refs[n:2 * n], refs[2 * n:]
        exs = []
        if ng:
            exs.append(_Exchange(x_refs[:ng], o_refs[:ng], *sems[:3], scatter=False))
        if ns:
            exs.append(_Exchange(x_refs[ng:], o_refs[ng:], *sems[-3:], scatter=True))
        for ex in exs:
            ex.start()
        for ex in exs:
            ex.finish()

    out_shape = [jax.ShapeDtypeStruct((N_DEV,) + x.shape, x.dtype) for x in gather]
    out_shape += [jax.ShapeDtypeStruct(x.shape, x.dtype) for x in scatter]
    return pl.pallas_call(
        body, name=name, in_specs=[_ANY] * n, out_specs=[_ANY] * n, out_shape=out_shape,
        scratch_shapes=(_exchange_sems(ng) if ng else []) + (_exchange_sems(ns) if ns else []),
        compiler_params=pltpu.CompilerParams(has_side_effects=True),
    )(*gather, *scatter)


def _adamw(gp, w, m, v, name):
    r, cc = w.shape
    tr = 256 if r % 256 == 0 else r

    def body(g_ref, w_ref, m_ref, v_ref, go_ref, d_ref, mo_ref, vo_ref):
        g = g_ref[0]
        for s in range(1, N_DEV):
            g = g + g_ref[s]
        go_ref[...] = g
        mn = ADAM_B1 * m_ref[...] + (1.0 - ADAM_B1) * g
        vn = ADAM_B2 * v_ref[...] + (1.0 - ADAM_B2) * (g * g)
        m_hat = mn / (1.0 - ADAM_B1 ** ADAM_STEP)
        v_hat = vn / (1.0 - ADAM_B2 ** ADAM_STEP)
        d_ref[...] = -ADAM_LR * (m_hat / (jnp.sqrt(v_hat) + ADAM_EPS) + ADAM_WD * w_ref[...])
        mo_ref[...] = mn
        vo_ref[...] = vn

    bs = pl.BlockSpec((tr, cc), lambda i: (i, 0))
    sh = jax.ShapeDtypeStruct((r, cc), F32)
    return pl.pallas_call(
        body, name=name, grid=(r // tr,),
        in_specs=[pl.BlockSpec((N_DEV, tr, cc), lambda i: (0, i, 0)), bs, bs, bs],
        out_specs=[bs, bs, bs, bs], out_shape=[sh, sh, sh, sh],
        compiler_params=_cp(("parallel",)),
    )(gp, w, m, v)


def _pack_cols(w):
    parts, pos = [], 0
    for pstart, ostart, width in _PACK:
        if pstart != pos:
            parts.append(jnp.zeros(w.shape[:-1] + (pstart - pos,), w.dtype))
        parts.append(w[..., ostart:ostart + width])
        pos = pstart + width
    return jnp.concatenate(parts, axis=-1)


def _unpack_cols(wp):
    by_orig = sorted(_PACK, key=lambda t: t[1])
    return jnp.concatenate([wp[..., p:p + wd] for p, _, wd in by_orig], axis=-1)


_LAYER_SHARDED = ("w_in", "conv_w", "w_conv_out", "w_hg_out", "w_att_out", "w_out")
_NARROW = ("w_in", "w_conv_out", "w_hg_out", "w_att_out", "w_out")
_REPLICATED = ("norm_g", "conv_b", "conv_ln_g", "conv_ln_b", "hg_lower_bounds", "hg_norm_g", "q_norm_g", "k_norm_g",
               "attn_sinks")
_WEIGHTS = ("meta_tokens", "norm_g", "w_in", "conv_w", "conv_b", "conv_ln_g", "conv_ln_b", "w_conv_out",
            "hg_lower_bounds", "hg_norm_g", "w_hg_out", "q_norm_g", "k_norm_g", "attn_sinks", "w_att_out", "w_out")
_ROW_SHARDED = ("w_out",)


def _assemble(name, g):
    if name in _ROW_SHARDED:
        return g.reshape((N_DEV * g.shape[1],) + g.shape[2:])
    full = jnp.moveaxis(g, 0, -2)
    return full.reshape(full.shape[:-2] + (N_DEV * full.shape[-1],))


def _split(name, full):
    if name in _ROW_SHARDED:
        return full.reshape((N_DEV, full.shape[0] // N_DEV) + full.shape[1:])
    c = full.shape[-1] // N_DEV
    return jnp.moveaxis(full.reshape(full.shape[:-1] + (N_DEV, c)), -2, 0)


def _layer_weights(gathered):
    full = {k: _assemble(k, g) for k, g in zip(_LAYER_SHARDED, gathered)}
    wp = _pack_cols(full["w_in"])
    return dict(wp=wp, wpt=wp.T, cw=full["conv_w"], wa=full["w_conv_out"], wb=full["w_hg_out"],
                wc=full["w_att_out"], wo=full["w_out"])


def _layer_fwd(h, lw, sp, gather):
    u, hn, gathered = _inproj_fwd(h, sp["norm_g"], lw["wp"], gather)
    ya, y_conv = _conv_fwd(u, lw["cw"], sp["conv_b"], sp["conv_ln_g"], sp["conv_ln_b"])
    yb, states = _hgrn_fwd(u, sp["lb"], sp["hg_norm_g"])
    yc = _swa_fwd(u, sp["qg"], sp["kg"], sp["sinks"])
    h_next = _mix_fwd(h, u, ya, yb, yc, lw["wa"], lw["wb"], lw["wc"], lw["wo"])
    return h_next, (h, u, hn, ya, yb, yc, states, y_conv), gathered


def _layer_bwd(dh, saved, lw, sp, stacked, layer, depth):
    h_l, u, hn, ya, yb, yc, states, y_conv = saved
    du, dya, dyb, dyc, dwa, dwb, dwc, dwo = _mix_bwd(dh, u, ya, yb, yc, lw["wa"], lw["wb"], lw["wc"], lw["wo"])
    du, dy, dlg, dlb_ = _conv_bwd1(u, y_conv, dya, du, sp["conv_ln_g"], sp["conv_ln_b"])
    du, dcw, dcb = _conv_bwd2(u, dy, du, lw["cw"])
    du, dlbl, dgg = _hgrn_bwd(u, dyb, states, du, sp["lb"], sp["hg_norm_g"])
    du, dqg, dkg, dsk = _swa_bwd(u, dyc, du, sp["qg"], sp["kg"], sp["sinks"])
    dwp = _inproj_bwd_dw(hn, du)
    full = dict(w_in=_unpack_cols(dwp), conv_w=dcw, w_conv_out=dwa, w_hg_out=dwb, w_att_out=dwc, w_out=dwo)
    pieces = [_split(k, full[k]) for k in _LAYER_SHARDED]
    dh, dng, stacked = _inproj_bwd_dh(du, lw["wpt"], h_l, sp["norm_g"], dh, pieces, stacked, layer, depth)
    fold = lambda a: a[0, :ATT_HEAD_DIM] + a[0, ATT_HEAD_DIM:]
    small = dict(norm_g=dng[0], conv_b=dcb[0], conv_ln_g=dlg[0], conv_ln_b=dlb_[0], hg_lower_bounds=dlbl[0],
                 hg_norm_g=dgg[0], q_norm_g=fold(dqg), k_norm_g=fold(dkg), attn_sinks=dsk[:, 0])
    return dh, small, stacked


def _as2d(a):
    return a.reshape((-1, a.shape[-1]))


def kernel(x, meta_tokens, norm_g, w_in, conv_w, conv_b, conv_ln_g, conv_ln_b, w_conv_out, hg_lower_bounds, hg_norm_g, w_hg_out, q_norm_g, k_norm_g, attn_sinks, w_att_out, w_out, loss_target, m_meta_tokens, m_norm_g, m_w_in, m_conv_w, m_conv_b, m_conv_ln_g, m_conv_ln_b, m_w_conv_out, m_hg_lower_bounds, m_hg_norm_g, m_w_hg_out, m_q_norm_g, m_k_norm_g, m_attn_sinks, m_w_att_out, m_w_out, v_meta_tokens, v_norm_g, v_w_in, v_conv_w, v_conv_b, v_conv_ln_g, v_conv_ln_b, v_w_conv_out, v_hg_lower_bounds, v_hg_norm_g, v_w_hg_out, v_q_norm_g, v_k_norm_g, v_attn_sinks, v_w_att_out, v_w_out):
    w = dict(meta_tokens=meta_tokens, norm_g=norm_g, w_in=w_in, conv_w=conv_w, conv_b=conv_b, conv_ln_g=conv_ln_g,
             conv_ln_b=conv_ln_b, w_conv_out=w_conv_out, hg_lower_bounds=hg_lower_bounds, hg_norm_g=hg_norm_g,
             w_hg_out=w_hg_out, q_norm_g=q_norm_g, k_norm_g=k_norm_g, attn_sinks=attn_sinks, w_att_out=w_att_out,
             w_out=w_out)
    m = dict(meta_tokens=m_meta_tokens, norm_g=m_norm_g, w_in=m_w_in, conv_w=m_conv_w, conv_b=m_conv_b,
             conv_ln_g=m_conv_ln_g, conv_ln_b=m_conv_ln_b, w_conv_out=m_w_conv_out, hg_lower_bounds=m_hg_lower_bounds,
             hg_norm_g=m_hg_norm_g, w_hg_out=m_w_hg_out, q_norm_g=m_q_norm_g, k_norm_g=m_k_norm_g,
             attn_sinks=m_attn_sinks, w_att_out=m_w_att_out, w_out=m_w_out)
    v = dict(meta_tokens=v_meta_tokens, norm_g=v_norm_g, w_in=v_w_in, conv_w=v_conv_w, conv_b=v_conv_b,
             conv_ln_g=v_conv_ln_g, conv_ln_b=v_conv_ln_b, w_conv_out=v_w_conv_out, hg_lower_bounds=v_hg_lower_bounds,
             hg_norm_g=v_hg_norm_g, w_hg_out=v_w_hg_out, q_norm_g=v_q_norm_g, k_norm_g=v_k_norm_g,
             attn_sinks=v_attn_sinks, w_att_out=v_w_att_out, w_out=v_w_out)

    depth = norm_g.shape[0]
    seq = x.shape[1]
    lp = -(-(seq + CHUNK) // TM_MM) * TM_MM
    tail = lp - seq - CHUNK
    zeros = lambda n: jnp.zeros((n, D_MODEL), F32)

    def shards(l):
        return [w[k][l].astype(MXU_DTYPE) if k in _NARROW else w[k][l] for k in _LAYER_SHARDED]

    first = _exchange(shards(0) + [meta_tokens], [], "gather_first")
    gathered, meta_full = first[:-1], _assemble("meta_tokens", first[-1])
    h = jnp.concatenate([zeros(META_PAD), meta_full, x[0], zeros(tail)], axis=0)
    target_p = jnp.concatenate([zeros(CHUNK), loss_target[0], zeros(tail)], axis=0)

    lb_all = _lb_fwd(hg_lower_bounds)
    tile2 = lambda a: jnp.concatenate([a, a], axis=-1)
    row = lambda a, l: a[l][None, :]

    def small_rows(l):
        sp = {k: row(w[k], l) for k in ("norm_g", "conv_b", "conv_ln_g", "conv_ln_b", "hg_norm_g")}
        sp.update(lb=row(lb_all, l), qg=tile2(row(q_norm_g, l)), kg=tile2(row(k_norm_g, l)), sinks=attn_sinks[l])
        return sp

    layer_w, saved = [], []
    for l in range(depth):
        layer_w.append(_layer_weights(gathered))
        h, sv, gathered = _layer_fwd(h, layer_w[l], small_rows(l), shards(l + 1) if l + 1 < depth else [])
        saved.append(sv)

    dh, loss_row = _loss_head(h, target_p, seq)
    loss = lax.psum(loss_row[0, 0], ("x", "y", "c"))

    stacked, small_grads = None, [None] * depth
    for l in reversed(range(depth)):
        dh, small_grads[l], stacked = _layer_bwd(dh, saved[l], layer_w[l], small_rows(l), stacked, l, depth)
    grad_x = dh[CHUNK:CHUNK + seq]
    grads = {k: jnp.stack([small_grads[l][k] for l in range(depth)]) for k in _REPLICATED}
    grads["hg_lower_bounds"] = _lb_bwd(hg_lower_bounds, grads["hg_lower_bounds"])

    small = jnp.concatenate([grads[k].reshape(-1) for k in _REPLICATED])
    small = jnp.concatenate([small, jnp.zeros((-small.shape[0] % 128,), F32)]).reshape(-1, 128)
    small_all, meta_pieces = _exchange([small], [_split("meta_tokens", dh[META_PAD:CHUNK])], "exchange_small_grads")
    small_all = small_all.reshape(N_DEV, -1)

    out_g, out_d, out_m, out_v = {}, {}, {}, {}
    for k, gp in zip(("meta_tokens",) + _LAYER_SHARDED, [meta_pieces] + stacked):
        shp = w[k].shape
        res = _adamw(gp.reshape((N_DEV,) + _as2d(w[k]).shape), _as2d(w[k]), _as2d(m[k]), _as2d(v[k]), "adamw_" + k)
        out_g[k], out_d[k], out_m[k], out_v[k] = (r.reshape(shp) for r in res)
    off = 0
    for k in _REPLICATED:
        shp = w[k].shape
        n = w[k].size
        gp = small_all[:, off:off + n].reshape((N_DEV,) + shp)
        off += n
        res = _adamw(gp, w[k], m[k], v[k], "adamw_" + k)
        out_g[k], out_d[k], out_m[k], out_v[k] = res

    return (loss, grad_x[None], *[out_g[k] for k in _WEIGHTS], *[out_d[k] for k in _WEIGHTS],
            *[out_m[k] for k in _WEIGHTS], *[out_v[k] for k in _WEIGHTS])
```

```python
import functools

import jax
import jax.numpy as jnp
from jax import lax
from jax.experimental import pallas as pl
from jax.experimental.pallas import tpu as pltpu

F32 = jnp.float32
MXU_DTYPE = jnp.bfloat16
WIRE_DTYPE = jnp.bfloat16

D_MODEL = 1024
CHUNK = 64
N_META = 16
META_PAD = CHUNK - N_META
D_CONV = 512
CONV_WIDTH = 31
HG_HEADS = 4
HG_D = 128
D_HG = HG_HEADS * HG_D
F_FLOOR = 1e-30
ATT_Q_HEADS = 8
ATT_HEAD_DIM = 64
D_ATT = 512
D_KV = 128
WINDOW_CHUNKS = 2
EPS = 1e-6
D_IN = 7936
N_DEV = 8

ADAM_LR = 0.001
ADAM_B1 = 0.9
ADAM_B2 = 0.999
ADAM_EPS = 1e-08
ADAM_WD = 0.01
ADAM_STEP = 10

NP = 8192
OFF_B, W_B = 0, 2048
OFF_A, W_A = 2048, 1024
OFF_G, W_G = 3072, 3072
OFF_C, W_C = 6144, 1536
OFF_AG, W_AG = 7680, 512
_PACK = ((0, 1536, 2048), (2048, 0, 1024), (3072, 4864, 3072), (6144, 3584, 512), (6656, 4352, 512),
         (7168, 4096, 256), (7680, 1024, 512))
_PAD_AT, _PAD_W = 7424, 256

TM_MM = 1280
TM_BR = 256
HALO = 32
EXP_CLAMP = 80.0
VMEM_LIMIT = 56 * 1024 * 1024

_HI = lax.Precision.HIGHEST


def _cp(sem):
    return pltpu.CompilerParams(dimension_semantics=sem, vmem_limit_bytes=VMEM_LIMIT)


def _sig(x):
    return 1.0 / (1.0 + jnp.exp(-x))


def _dot(a, b):
    return jnp.dot(a.astype(MXU_DTYPE), b.astype(MXU_DTYPE), preferred_element_type=F32)


def _dot_nt(a, b):
    return lax.dot_general(a.astype(MXU_DTYPE), b.astype(MXU_DTYPE), (((1,), (1,)), ((), ())),
                           preferred_element_type=F32)


def _dot_tn(a, b):
    return lax.dot_general(a.astype(MXU_DTYPE), b.astype(MXU_DTYPE), (((0,), (0,)), ((), ())),
                           preferred_element_type=F32)


def _split_hi_lo(a):
    hi = a.astype(MXU_DTYPE)
    return hi, (a - hi.astype(F32)).astype(MXU_DTYPE)


def _dot3(a, b, dims=(((1,), (0,)), ((), ()))):
    (ah, al), (bh, bl) = _split_hi_lo(a), _split_hi_lo(b)
    dg = lambda x, y: lax.dot_general(x, y, dims, preferred_element_type=F32)
    return dg(ah, bh) + dg(al, bh) + dg(ah, bl)


def _iota(shape, dim):
    return lax.broadcasted_iota(jnp.int32, shape, dim)


def _full_spec(shape):
    nd = len(shape)
    return pl.BlockSpec(shape, lambda *_: (0,) * nd)


def _my_index():
    return 4 * lax.axis_index("x") + 2 * lax.axis_index("y") + lax.axis_index("c")


def _mesh_id(p):
    return (p >> 2, (p >> 1) & 1, p & 1)


class _Exchange:
    def __init__(self, x_refs, o_refs, send_sems, recv_sems, loc_sems, scatter, dst=lambda o, s: o.at[s]):
        me = _my_index()
        self.local, self.sends, self.recvs = [], [], []
        for a, (x, o) in enumerate(zip(x_refs, o_refs)):
            mine = x.at[me] if scatter else x
            self.local.append(pltpu.make_async_copy(mine, dst(o, me), loc_sems.at[a]))
            for k in range(1, N_DEV):
                to, frm = (me + k) % N_DEV, (me + N_DEV - k) % N_DEV
                sems = dict(send_sem=send_sems.at[a, k - 1], recv_sem=recv_sems.at[a, k - 1],
                            device_id_type=pl.DeviceIdType.MESH)
                self.sends.append(pltpu.make_async_remote_copy(
                    src_ref=x.at[to] if scatter else x, dst_ref=dst(o, me), device_id=_mesh_id(to), **sems))
                self.recvs.append(pltpu.make_async_remote_copy(
                    src_ref=mine, dst_ref=dst(o, frm), device_id=_mesh_id(frm), **sems))

    def start(self):
        for cp in self.local + self.sends:
            cp.start()

    def finish(self):
        for cp in self.recvs:
            cp.wait_recv()
        for cp in self.sends:
            cp.wait_send()
        for cp in self.local:
            cp.wait()


def _exchange_sems(n):
    return [pltpu.SemaphoreType.DMA((n, N_DEV - 1)), pltpu.SemaphoreType.DMA((n, N_DEV - 1)),
            pltpu.SemaphoreType.DMA((n,))]


_ANY = pl.BlockSpec(memory_space=pl.ANY)


def _inproj_fwd(h, g, wp, gather=()):
    lp = h.shape[0]
    tm, tn = TM_MM, 1024
    ni, nj = lp // tm, NP // tn
    n = len(gather)

    def body(h_ref, g_ref, w_ref, *rest):
        x_refs, (u_ref, hn_ref), o_refs = rest[:n], rest[n:n + 2], rest[n + 2:2 * n + 2]
        hs_ref, sems = rest[2 * n + 2], rest[2 * n + 3:]
        i, j = pl.program_id(0), pl.program_id(1)
        if n:
            @pl.when((i == 0) & (j == 0))
            def _():
                _Exchange(x_refs, o_refs, *sems, scatter=False).start()

        @pl.when(j == 0)
        def _():
            x = h_ref[...]
            r = lax.rsqrt(jnp.mean(x * x, axis=-1, keepdims=True) + EPS)
            hn = (x * r * g_ref[...]).astype(MXU_DTYPE)
            hs_ref[...] = hn
            hn_ref[...] = hn
        u_ref[...] = jnp.dot(hs_ref[...], w_ref[...], preferred_element_type=F32)
        if n:
            @pl.when((i == ni - 1) & (j == nj - 1))
            def _():
                _Exchange(x_refs, o_refs, *sems, scatter=False).finish()

    res = pl.pallas_call(
        body, name="inproj_fwd_gather" if n else "inproj_fwd", grid=(ni, nj),
        in_specs=[pl.BlockSpec((tm, D_MODEL), lambda i, j: (i, 0)), pl.BlockSpec((1, D_MODEL), lambda i, j: (0, 0)),
                  pl.BlockSpec((D_MODEL, tn), lambda i, j: (0, j))] + [_ANY] * n,
        out_specs=[pl.BlockSpec((tm, tn), lambda i, j: (i, j)), pl.BlockSpec((tm, D_MODEL), lambda i, j: (i, 0))]
        + [_ANY] * n,
        out_shape=[jax.ShapeDtypeStruct((lp, NP), F32), jax.ShapeDtypeStruct((lp, D_MODEL), MXU_DTYPE)]
        + [jax.ShapeDtypeStruct((N_DEV,) + x.shape, x.dtype) for x in gather],
        scratch_shapes=[pltpu.VMEM((tm, D_MODEL), MXU_DTYPE)] + (_exchange_sems(n) if n else []),
        compiler_params=_cp(("arbitrary", "arbitrary")),
    )(h, g, wp, *gather)
    return res[0], res[1], list(res[2:])


def _inproj_bwd_dh(du, wpt, h, g, dh_next, pieces, stacked, layer, depth):
    lp = h.shape[0]
    tm, tk = TM_MM, 1024
    ni, nk = lp // tm, NP // tk
    n = len(pieces)
    n_acc = 0 if stacked is None else n

    def body(du_ref, w_ref, h_ref, g_ref, dhn_ref, *rest):
        x_refs, (dh_ref, dg_ref), o_refs = rest[:n], rest[n + n_acc:n + n_acc + 2], rest[n + n_acc + 2:2 * n + n_acc + 2]
        acc_ref, sems = rest[2 * n + n_acc + 2], rest[2 * n + n_acc + 3:]
        i, k = pl.program_id(0), pl.program_id(1)
        slot = lambda o, s: o.at[s, layer]

        @pl.when((i == 0) & (k == 0))
        def _():
            _Exchange(x_refs, o_refs, *sems, scatter=True, dst=slot).start()
            dg_ref[...] = jnp.zeros_like(dg_ref)

        @pl.when(k == 0)
        def _():
            acc_ref[...] = jnp.zeros_like(acc_ref)

        acc_ref[...] += jnp.dot(du_ref[...], w_ref[...], preferred_element_type=F32)

        @pl.when(k == nk - 1)
        def _():
            dhn = acc_ref[...]
            x = h_ref[...]
            r = lax.rsqrt(jnp.mean(x * x, axis=-1, keepdims=True) + EPS)
            xh = x * r
            dg_ref[...] += jnp.sum(dhn * xh, axis=0, keepdims=True)
            dxh = dhn * g_ref[...]
            dx = r * (dxh - xh * jnp.mean(dxh * xh, axis=-1, keepdims=True))
            dh_ref[...] = dhn_ref[...] + dx

        @pl.when((i == ni - 1) & (k == nk - 1))
        def _():
            _Exchange(x_refs, o_refs, *sems, scatter=True, dst=slot).finish()

    acc_in = [] if stacked is None else list(stacked)
    res = pl.pallas_call(
        body, name="inproj_bwd_dh_scatter", grid=(ni, nk),
        in_specs=[pl.BlockSpec((tm, tk), lambda i, k: (i, k)), pl.BlockSpec((tk, D_MODEL), lambda i, k: (k, 0)),
                  pl.BlockSpec((tm, D_MODEL), lambda i, k: (i, 0)), pl.BlockSpec((1, D_MODEL), lambda i, k: (0, 0)),
                  pl.BlockSpec((tm, D_MODEL), lambda i, k: (i, 0))] + [_ANY] * (n + n_acc),
        out_specs=[pl.BlockSpec((tm, D_MODEL), lambda i, k: (i, 0)), pl.BlockSpec((1, D_MODEL), lambda i, k: (0, 0))]
        + [_ANY] * n,
        out_shape=[jax.ShapeDtypeStruct((lp, D_MODEL), F32), jax.ShapeDtypeStruct((1, D_MODEL), F32)]
        + [jax.ShapeDtypeStruct((N_DEV, depth) + p.shape[1:], p.dtype) for p in pieces],
        scratch_shapes=[pltpu.VMEM((tm, D_MODEL), F32)] + _exchange_sems(n),
        input_output_aliases={5 + n + a: 2 + a for a in range(n_acc)},
        compiler_params=_cp(("arbitrary", "arbitrary")),
    )(du, wpt, h, g, dh_next, *pieces, *acc_in)
    return res[0], res[1], list(res[2:])


def _inproj_bwd_dw(hn, du):
    lp = hn.shape[0]
    tm, tn = TM_MM, 1024

    def body(hn_ref, du_ref, dw_ref):
        @pl.when(pl.program_id(1) == 0)
        def _():
            dw_ref[...] = jnp.zeros_like(dw_ref)
        dw_ref[...] += _dot_tn(hn_ref[...], du_ref[...])

    return pl.pallas_call(
        body, name="inproj_bwd_dw", grid=(NP // tn, lp // tm),
        in_specs=[pl.BlockSpec((tm, D_MODEL), lambda j, m: (m, 0)), pl.BlockSpec((tm, tn), lambda j, m: (m, j))],
        out_specs=pl.BlockSpec((D_MODEL, tn), lambda j, m: (0, j)),
        out_shape=jax.ShapeDtypeStruct((D_MODEL, NP), F32),
        compiler_params=_cp(("parallel", "arbitrary")),
    )(hn, du)


N_SHIFT = 8
CONV_SUB = 32


def _shift_copies(src_ref, sh_ref):
    n = sh_ref.shape[1]
    for b in range(1, N_SHIFT):
        sh_ref[b - 1, :, :] = src_ref[pl.ds(b, n), :]


def _window(src_ref, sh_ref, off, r0, n):
    a, b = divmod(off, N_SHIFT)
    start = pl.multiple_of(r0 + a * N_SHIFT, N_SHIFT)
    if b == 0:
        return src_ref[pl.ds(start, n), :]
    return sh_ref[b - 1, pl.ds(start, n), :]


def _shift_scratch(tm):
    return pltpu.VMEM((N_SHIFT - 1, tm + HALO - N_SHIFT, D_CONV), F32)


def _glu_ext(a_ref, ah_ref, ext_ref, sh_ref, i, tm):
    rows = i * tm + _iota((tm, 1), 0)
    a = a_ref[...]
    p, sq = a[:, :D_CONV], _sig(a[:, D_CONV:])
    valid = rows >= META_PAD
    ah = ah_ref[...]
    ext_ref[0:HALO, :] = jnp.where(i > 0, ah[:, :D_CONV] * _sig(ah[:, D_CONV:]), 0.0)
    ext_ref[HALO:HALO + tm, :] = jnp.where(valid, p * sq, 0.0)
    _shift_copies(ext_ref, sh_ref)
    return p, sq, valid


def _layernorm_stats(y):
    mu = jnp.mean(y, axis=-1, keepdims=True)
    yc = y - mu
    rstd = lax.rsqrt(jnp.mean(yc * yc, axis=-1, keepdims=True) + EPS)
    return yc * rstd, rstd


def _conv_specs(tm):
    hb = tm // HALO
    return [pl.BlockSpec((tm, W_A), lambda i: (i, OFF_A // W_A)),
            pl.BlockSpec((HALO, W_A), lambda i: (jnp.maximum(i * hb - 1, 0), OFF_A // W_A)),
            pl.BlockSpec((tm, W_AG), lambda i: (i, OFF_AG // W_AG))]


def _conv_fwd(u, cw, cb, lg, lb_):
    lp = u.shape[0]
    tm = TM_BR

    def body(a_ref, ah_ref, ag_ref, w_ref, b_ref, lg_ref, lb_ref, ya_ref, y_ref, ext_ref, sh_ref):
        i = pl.program_id(0)
        _glu_ext(a_ref, ah_ref, ext_ref, sh_ref, i, tm)
        base = HALO - (CONV_WIDTH - 1)

        y = jnp.zeros((tm, D_CONV), F32) + b_ref[...]
        for k in range(CONV_WIDTH):
            y = y + w_ref[k:k + 1, :] * _window(ext_ref, sh_ref, base + k, 0, tm)
        y_ref[...] = y
        xh, _ = _layernorm_stats(y)
        yn = xh * lg_ref[...] + lb_ref[...]
        gt = ag_ref[...]
        ya_ref[...] = (yn * _sig(yn) * gt * _sig(gt)).astype(MXU_DTYPE)

    rowspec = pl.BlockSpec((tm, D_CONV), lambda i: (i, 0))
    return pl.pallas_call(
        body, name="conv_fwd", grid=(lp // tm,),
        in_specs=_conv_specs(tm) + [_full_spec((CONV_WIDTH, D_CONV))] + [_full_spec((1, D_CONV))] * 3,
        out_specs=[rowspec, rowspec],
        out_shape=[jax.ShapeDtypeStruct((lp, D_CONV), MXU_DTYPE), jax.ShapeDtypeStruct((lp, D_CONV), F32)],
        scratch_shapes=[pltpu.VMEM((HALO + tm, D_CONV), F32), _shift_scratch(tm)],
        compiler_params=_cp(("arbitrary",)),
    )(u, u, u, cw, cb, lg, lb_)


def _dsilu(x, s):
    return s * (1.0 + x * (1.0 - s))


def _conv_bwd1(u, y, dya, du, lg, lb_):
    lp = u.shape[0]
    tm = TM_BR

    def body(ag_ref, y_ref, dya_ref, lg_ref, lb_ref, du_in, du_ref, dy_ref, dlg_ref, dlb_ref):
        del du_in
        i = pl.program_id(0)

        @pl.when(i == 0)
        def _():
            dlg_ref[...] = jnp.zeros_like(dlg_ref)
            dlb_ref[...] = jnp.zeros_like(dlb_ref)

        xh, rstd = _layernorm_stats(y_ref[...])
        yn = xh * lg_ref[...] + lb_ref[...]
        s1 = _sig(yn)
        gt = ag_ref[...]
        s2 = _sig(gt)
        do = dya_ref[...].astype(F32)
        du_ref[...] = (do * (yn * s1) * _dsilu(gt, s2)).astype(MXU_DTYPE)
        dyn = do * (gt * s2) * _dsilu(yn, s1)
        dlg_ref[...] += jnp.sum(dyn * xh, axis=0, keepdims=True)
        dlb_ref[...] += jnp.sum(dyn, axis=0, keepdims=True)
        dxh = dyn * lg_ref[...]
        dy_ref[...] = rstd * (dxh - jnp.mean(dxh, axis=-1, keepdims=True)
                              - xh * jnp.mean(dxh * xh, axis=-1, keepdims=True))

    rowspec = pl.BlockSpec((tm, D_CONV), lambda i: (i, 0))
    return pl.pallas_call(
        body, name="conv_bwd1", grid=(lp // tm,),
        in_specs=[_conv_specs(tm)[2], rowspec, rowspec, _full_spec((1, D_CONV)), _full_spec((1, D_CONV)),
                  pl.BlockSpec(memory_space=pl.ANY)],
        out_specs=[pl.BlockSpec((tm, W_AG), lambda i: (i, OFF_AG // W_AG)), rowspec,
                   _full_spec((1, D_CONV)), _full_spec((1, D_CONV))],
        out_shape=[jax.ShapeDtypeStruct(du.shape, du.dtype), jax.ShapeDtypeStruct((lp, D_CONV), F32),
                   jax.ShapeDtypeStruct((1, D_CONV), F32), jax.ShapeDtypeStruct((1, D_CONV), F32)],
        input_output_aliases={5: 0},
        compiler_params=_cp(("arbitrary",)),
    )(u, y, dya, lg, lb_, du)


def _conv_bwd2(u, dy, du, cw):
    lp = u.shape[0]
    tm = TM_BR
    nb = lp // tm
    hb = tm // HALO

    def body(a_ref, ah_ref, dy_ref, dyn_ref, w_ref, du_in, du_ref, dw_ref, db_ref, ext_ref, sh_ref, edy_ref, shd_ref,
             dwp_ref):
        del du_in
        i = pl.program_id(0)

        @pl.when(i == 0)
        def _():
            dwp_ref[...] = jnp.zeros_like(dwp_ref)
            db_ref[...] = jnp.zeros_like(db_ref)

        _glu_ext(a_ref, ah_ref, ext_ref, sh_ref, i, tm)
        dy_all = dy_ref[...]
        edy_ref[0:tm, :] = dy_all
        edy_ref[tm:tm + HALO, :] = jnp.where(i < nb - 1, dyn_ref[...], 0.0)
        _shift_copies(edy_ref, shd_ref)
        db_ref[...] += jnp.sum(dy_all, axis=0, keepdims=True)
        base = HALO - (CONV_WIDTH - 1)

        def fold8(x):
            parts = [x[s:s + N_SHIFT] for s in range(0, CONV_SUB, N_SHIFT)]
            return functools.reduce(jnp.add, parts)

        def sub(r, carry):
            r0 = pl.multiple_of(r * CONV_SUB, CONV_SUB)
            dy = dy_ref[pl.ds(r0, CONV_SUB), :]
            du0 = jnp.zeros((CONV_SUB, D_CONV), F32)
            for k in range(CONV_WIDTH):
                du0 = du0 + w_ref[k:k + 1, :] * _window(edy_ref, shd_ref, CONV_WIDTH - 1 - k, r0, CONV_SUB)
                dwp_ref[k] += fold8(dy * _window(ext_ref, sh_ref, base + k, r0, CONV_SUB))
            a = a_ref[pl.ds(r0, CONV_SUB), :]
            p, sq = a[:, :D_CONV], _sig(a[:, D_CONV:])
            valid = (i * tm + r0 + _iota((CONV_SUB, 1), 0)) >= META_PAD
            du0 = jnp.where(valid, du0, 0.0)
            du_ref[pl.ds(r0, CONV_SUB), :] = jnp.concatenate([du0 * sq, du0 * p * sq * (1.0 - sq)],
                                                             axis=1).astype(MXU_DTYPE)
            return carry

        lax.fori_loop(0, tm // CONV_SUB, sub, 0)

        @pl.when(i == nb - 1)
        def _():
            dw_ref[...] = jnp.sum(dwp_ref[...], axis=1)

    return pl.pallas_call(
        body, name="conv_bwd2", grid=(nb,),
        in_specs=_conv_specs(tm)[:2] + [pl.BlockSpec((tm, D_CONV), lambda i: (i, 0)),
                                        pl.BlockSpec((HALO, D_CONV), lambda i: (jnp.minimum((i + 1) * hb, nb * hb - 1), 0)),
                                        _full_spec((CONV_WIDTH, D_CONV)), pl.BlockSpec(memory_space=pl.ANY)],
        out_specs=[pl.BlockSpec((tm, W_A), lambda i: (i, OFF_A // W_A)), _full_spec((CONV_WIDTH, D_CONV)),
                   _full_spec((1, D_CONV))],
        out_shape=[jax.ShapeDtypeStruct(du.shape, du.dtype), jax.ShapeDtypeStruct((CONV_WIDTH, D_CONV), F32),
                   jax.ShapeDtypeStruct((1, D_CONV), F32)],
        scratch_shapes=[pltpu.VMEM((HALO + tm, D_CONV), F32), _shift_scratch(tm),
                        pltpu.VMEM((tm + HALO, D_CONV), F32), _shift_scratch(tm),
                        pltpu.VMEM((CONV_WIDTH, N_SHIFT, D_CONV), F32)],
        input_output_aliases={5: 0},
        compiler_params=_cp(("arbitrary",)),
    )(u, u, dy, dy, cw, du)


HG_T = 128
HG_HALF = HG_T // 2


def _hg_chunk_fwd(blk, lb, valid, tri):
    bq, bf, v = blk[:, 0:512], blk[:, 512:1024], blk[:, 1024:1536]
    sgq = _sig(bq)
    qt = bq * sgq
    sz = _sig(bf)
    f = lb + (1.0 - lb) * sz
    g = jnp.where(valid, jnp.log(jnp.maximum(f, F_FLOOR)), 0.0)
    k = jnp.where(valid, (1.0 - lb) * (1.0 - sz), 0.0)
    b = jnp.dot(tri, g, precision=_HI, preferred_element_type=F32)
    ridx = _iota((HG_T, 1), 0)
    pick = lambda r: jnp.sum(jnp.where(ridx == r, b, 0.0), axis=0, keepdims=True)
    top = ridx < HG_HALF
    rx = pick(HG_HALF - 1)
    rd = jnp.where(top, pick(HG_HALF // 2 - 1), pick(HG_HALF + HG_HALF // 2 - 1))
    bl = pick(HG_T - 1)
    eqx = jnp.where(top, 0.0, jnp.exp(jnp.minimum(b - rx, 0.0)))
    ekx = jnp.where(top, jnp.exp(jnp.minimum(rx - b, 0.0)), 0.0)
    eqd = jnp.exp(jnp.minimum(b - rd, EXP_CLAMP))
    ekd = jnp.exp(jnp.minimum(rd - b, EXP_CLAMP))
    e = jnp.exp(b)
    ekl = jnp.exp(bl - b)
    el = jnp.exp(bl)
    return dict(bq=bq, sgq=sgq, qt=qt, sz=sz, f=f, k=k, v=v, top=top, eqx=eqx, ekx=ekx, eqd=eqd, ekd=ekd, e=e,
                ekl=ekl, el=el, qx=qt * eqx, kx=k * ekx, qd=qt * eqd, kd=k * ekd, qe=qt * e, kl=k * ekl)


def _hg_factors(q, sl):
    top, qd, kd = q["top"], q["qd"][:, sl], q["kd"][:, sl]
    qcat = jnp.concatenate([q["qx"][:, sl], jnp.where(top, qd, 0.0), jnp.where(top, 0.0, qd)], axis=1)
    kcat = jnp.concatenate([q["kx"][:, sl], jnp.where(top, kd, 0.0), jnp.where(top, 0.0, kd)], axis=1)
    return qcat, kcat


def _hgrn_fwd(u, lb, gg):
    lp = u.shape[0]
    tm = TM_BR
    cpb = tm // HG_T

    def body(u_ref, lb_ref, gg_ref, y_ref, st_ref, s_ref):
        i = pl.program_id(0)

        @pl.when(i == 0)
        def _():
            s_ref[...] = jnp.zeros_like(s_ref)

        lbv = lb_ref[...]
        ggv = gg_ref[...]
        tri = (_iota((HG_T, HG_T), 0) >= _iota((HG_T, HG_T), 1)).astype(F32)

        def chunk(c, carry):
            r0 = pl.multiple_of(c * HG_T, HG_T)
            blk = u_ref[pl.ds(r0, HG_T), :]
            valid = (i * tm + r0 + _iota((HG_T, 1), 0)) >= META_PAD
            q = _hg_chunk_fwd(blk, lbv, valid, tri)
            outs = []
            for hh in range(HG_HEADS):
                sl = slice(hh * HG_D, (hh + 1) * HG_D)
                qcat, kcat = _hg_factors(q, sl)
                a = jnp.where(tri > 0, _dot_nt(qcat, kcat), 0.0)
                st = s_ref[hh]
                st_ref[c, hh] = st
                o = _dot(a, q["v"][:, sl]) + _dot_nt(q["qe"][:, sl], st)
                s_ref[hh] = st * q["el"][:, sl] + _dot_tn(q["v"][:, sl], q["kl"][:, sl])
                rs = lax.rsqrt(jnp.mean(o * o, axis=-1, keepdims=True) + EPS)
                outs.append(o * rs * ggv)
            on = jnp.concatenate(outs, axis=1)
            bg = blk[:, 1536:2048]
            y_ref[pl.ds(r0, HG_T), :] = (on * bg * _sig(bg)).astype(MXU_DTYPE)
            return carry

        lax.fori_loop(0, cpb, chunk, 0, unroll=2)

    return pl.pallas_call(
        body, name="hgrn_fwd", grid=(lp // tm,),
        in_specs=[pl.BlockSpec((tm, W_B), lambda i: (i, 0)), _full_spec((1, D_HG)), _full_spec((1, HG_D))],
        out_specs=[pl.BlockSpec((tm, D_HG), lambda i: (i, 0)),
                   pl.BlockSpec((cpb, HG_HEADS, HG_D, HG_D), lambda i: (i, 0, 0, 0))],
        out_shape=[jax.ShapeDtypeStruct((lp, D_HG), MXU_DTYPE),
                   jax.ShapeDtypeStruct((lp // HG_T, HG_HEADS, HG_D, HG_D), F32)],
        scratch_shapes=[pltpu.VMEM((HG_HEADS, HG_D, HG_D), F32)],
        compiler_params=_cp(("arbitrary",)),
    )(u, lb, gg)


def _hgrn_bwd(u, dyb, states, du, lb, gg):
    lp = u.shape[0]
    tm = TM_BR
    cpb = tm // HG_T
    nb = lp // tm

    def body(u_ref, dy_ref, st_ref, lb_ref, gg_ref, du_in, du_ref, dlb_ref, dgg_ref, ds_ref):
        del du_in
        ii = pl.program_id(0)
        i = nb - 1 - ii

        @pl.when(ii == 0)
        def _():
            ds_ref[...] = jnp.zeros_like(ds_ref)
            dlb_ref[...] = jnp.zeros_like(dlb_ref)
            dgg_ref[...] = jnp.zeros_like(dgg_ref)

        lbv = lb_ref[...]
        ggv = gg_ref[...]
        lower = _iota((HG_T, HG_T), 0) >= _iota((HG_T, HG_T), 1)
        tri = lower.astype(F32)
        triu = (_iota((HG_T, HG_T), 0) <= _iota((HG_T, HG_T), 1)).astype(F32)
        ridx = _iota((HG_T, 1), 0)

        def chunk(cc, carry):
            c = cpb - 1 - cc
            r0 = pl.multiple_of(c * HG_T, HG_T)
            blk = u_ref[pl.ds(r0, HG_T), :]
            valid = (i * tm + r0 + _iota((HG_T, 1), 0)) >= META_PAD
            q = _hg_chunk_fwd(blk, lbv, valid, tri)
            top = q["top"]
            bg = blk[:, 1536:2048]
            sg = _sig(bg)
            dy = dy_ref[pl.ds(r0, HG_T), :].astype(F32)
            don_all = dy * bg * sg
            dqt_l, dk_l, dv_l, db_l, dbl_l, on_l = [], [], [], [], [], []
            dgg = jnp.zeros((1, HG_D), F32)
            for hh in range(HG_HEADS):
                sl = slice(hh * HG_D, (hh + 1) * HG_D)
                qe, kl, v = q["qe"][:, sl], q["kl"][:, sl], q["v"][:, sl]
                el = q["el"][:, sl]
                qcat, kcat = _hg_factors(q, sl)
                a = jnp.where(lower, _dot_nt(qcat, kcat), 0.0)
                st = st_ref[c, hh]
                o = _dot(a, v) + _dot_nt(qe, st)
                rs = lax.rsqrt(jnp.mean(o * o, axis=-1, keepdims=True) + EPS)
                xh = o * rs
                on_l.append(xh * ggv)
                don = don_all[:, sl]
                dgg = dgg + jnp.sum(don * xh, axis=0, keepdims=True)
                dxh = don * ggv
                do = rs * (dxh - xh * jnp.mean(dxh * xh, axis=-1, keepdims=True))
                dst = ds_ref[hh]
                dv = _dot_tn(a, do) + _dot_nt(kl, dst)
                da = jnp.where(lower, _dot_nt(do, v), 0.0)
                dqe = _dot(do, st)
                dkl = _dot(v, dst)
                d_el = jnp.sum(st * dst, axis=0, keepdims=True)
                ds_ref[hh] = _dot_tn(do, qe) + dst * el
                dqc = _dot3(da, kcat)
                dkc = _dot3(da, qcat, (((0,), (0,)), ((), ())))
                dqx, dqd = dqc[:, :HG_D], jnp.where(top, dqc[:, HG_D:2 * HG_D], dqc[:, 2 * HG_D:])
                dkx, dkd = dkc[:, :HG_D], jnp.where(top, dkc[:, HG_D:2 * HG_D], dkc[:, 2 * HG_D:])
                dqt_l.append(dqx * q["eqx"][:, sl] + dqd * q["eqd"][:, sl] + dqe * q["e"][:, sl])
                dk_l.append(dkx * q["ekx"][:, sl] + dkd * q["ekd"][:, sl] + dkl * q["ekl"][:, sl])
                dv_l.append(dv)
                db_l.append(dqx * q["qx"][:, sl] - dkx * q["kx"][:, sl] + dqd * q["qd"][:, sl] - dkd * q["kd"][:, sl]
                            + dqe * qe - dkl * kl)
                dbl_l.append(jnp.sum(dkl * kl, axis=0, keepdims=True) + d_el * el)
            dqt = jnp.concatenate(dqt_l, axis=1)
            dk = jnp.concatenate(dk_l, axis=1)
            dv = jnp.concatenate(dv_l, axis=1)
            db = jnp.concatenate(db_l, axis=1) + jnp.where(ridx == HG_T - 1, jnp.concatenate(dbl_l, axis=1), 0.0)
            on = jnp.concatenate(on_l, axis=1)
            dg = jnp.dot(triu, db, precision=_HI, preferred_element_type=F32)
            sz, f = q["sz"], q["f"]
            df = jnp.where(valid & (f > F_FLOOR), dg / f, 0.0)
            dkv = jnp.where(valid, dk, 0.0)
            t = (1.0 - sz) * (df - dkv)
            dlb_ref[...] += jnp.sum(t, axis=0, keepdims=True)
            dz = (1.0 - lbv) * (df - dkv) * sz * (1.0 - sz)
            dbq = dqt * _dsilu(q["bq"], q["sgq"])
            dbg = dy * on * _dsilu(bg, sg)
            dgg_ref[...] += dgg
            du_ref[pl.ds(r0, HG_T), :] = jnp.concatenate([dbq, dz, dv, dbg], axis=1).astype(MXU_DTYPE)
            return carry

        lax.fori_loop(0, cpb, chunk, 0, unroll=2)

    return pl.pallas_call(
        body, name="hgrn_bwd", grid=(nb,),
        in_specs=[pl.BlockSpec((tm, W_B), lambda ii: (nb - 1 - ii, 0)), pl.BlockSpec((tm, D_HG), lambda ii: (nb - 1 - ii, 0)),
                  pl.BlockSpec((cpb, HG_HEADS, HG_D, HG_D), lambda ii: (nb - 1 - ii, 0, 0, 0)),
                  _full_spec((1, D_HG)), _full_spec((1, HG_D)), pl.BlockSpec(memory_space=pl.ANY)],
        out_specs=[pl.BlockSpec((tm, W_B), lambda ii: (nb - 1 - ii, 0)), _full_spec((1, D_HG)), _full_spec((1, HG_D))],
        out_shape=[jax.ShapeDtypeStruct(du.shape, du.dtype), jax.ShapeDtypeStruct((1, D_HG), F32),
                   jax.ShapeDtypeStruct((1, HG_D), F32)],
        scratch_shapes=[pltpu.VMEM((HG_HEADS, HG_D, HG_D), F32)],
        input_output_aliases={5: 0},
        compiler_params=_cp(("arbitrary",)),
    )(u, dyb, states, lb, gg, du)


N_KEYS = 2 * TM_BR
PREV_ROWS = N_KEYS - CHUNK - TM_BR
ATT_SCALE = ATT_HEAD_DIM ** -0.5
NEG = -1e30


def _half_sum(x, lo):
    a = jnp.sum(jnp.where(lo, x, 0.0), axis=1, keepdims=True)
    b = jnp.sum(jnp.where(lo, 0.0, x), axis=1, keepdims=True)
    return jnp.where(lo, a, b)


def _half_rms(x, lo):
    return lax.rsqrt(_half_sum(x * x, lo) * (1.0 / ATT_HEAD_DIM) + EPS)


def _swa_mask(i, tm):
    tq = i * tm + _iota((tm, N_KEYS), 0)
    s = _iota((tm, N_KEYS), 1)
    nq = tq >> 6
    kr = i * tm + s - (N_KEYS - tm)
    kc = kr >> 6
    band = (kr >= META_PAD) & (kc >= nq - WINDOW_CHUNKS) & (kc <= nq)
    meta = (nq > WINDOW_CHUNKS) & (s >= META_PAD)
    return ((s < CHUNK) & meta) | ((s >= CHUNK) & band)


def _swa_keys(own_kv, prev_ref, meta_ref, kg, tm):
    kv = jnp.concatenate([meta_ref[...], prev_ref[tm - PREV_ROWS:tm, :], own_kv], axis=0)
    k_raw, v = kv[:, :D_KV], kv[:, D_KV:]
    lo = _iota((1, D_KV), 1) < ATT_HEAD_DIM
    kr = _half_rms(k_raw, lo)
    kn = k_raw * kr * kg
    return k_raw, kr, kn, v, lo


def _placed(x, lo):
    xr = pltpu.roll(x, ATT_HEAD_DIM, 1)
    z = jnp.zeros_like(x)
    return [[jnp.where(lo, x, z).astype(MXU_DTYPE), jnp.where(lo, z, xr).astype(MXU_DTYPE)],
            [jnp.where(lo, xr, z).astype(MXU_DTYPE), jnp.where(lo, z, x).astype(MXU_DTYPE)]]


def _swa_specs(tm, order):
    kvb = (OFF_C + 1024) // 256
    return [pl.BlockSpec((tm, W_C), lambda i: (order(i), OFF_C // W_C)),
            pl.BlockSpec((tm, 256), lambda i: (jnp.maximum(order(i) - 1, 0), kvb)),
            pl.BlockSpec((CHUNK, 256), lambda i: (0, kvb)),
            _full_spec((1, D_KV)), _full_spec((1, D_KV)), pl.BlockSpec(memory_space=pltpu.SMEM)]


def _swa_fwd(u, qg, kg, sinks):
    lp = u.shape[0]
    tm = TM_BR

    def body(own_ref, prev_ref, meta_ref, qg_ref, kg_ref, sink_ref, y_ref):
        i = pl.program_id(0)
        own = own_ref[...]
        _, _, kn, v, lo = _swa_keys(own[:, 1024:1280], prev_ref, meta_ref, kg_ref[...], tm)
        kuse, vuse = _placed(kn, lo), _placed(v, lo)
        bias = jnp.where(_swa_mask(i, tm), 0.0, NEG)
        for gi in range(ATT_Q_HEADS // 2):
            j = gi // 2
            sl = slice(gi * 128, (gi + 1) * 128)
            qraw = own[:, sl]
            qs = qraw * _half_rms(qraw, lo) * (qg_ref[...] * ATT_SCALE)
            og = jnp.zeros((tm, 128), F32)
            for e in range(2):
                qm = jnp.where(lo if e == 0 else ~lo, qs, 0.0)
                s = _dot_nt(qm, kuse[j][e]) + bias
                sk = sink_ref[2 * gi + e]
                m = jnp.maximum(jnp.max(s, axis=-1, keepdims=True), sk)
                p = jnp.exp(s - m)
                inv = 1.0 / (jnp.sum(p, axis=-1, keepdims=True) + jnp.exp(sk - m))
                og = og + _dot(p, vuse[j][e]) * inv
            gt = own[:, 512 + gi * 128:512 + (gi + 1) * 128]
            y_ref[:, sl] = (og * gt * _sig(gt)).astype(MXU_DTYPE)

    return pl.pallas_call(
        body, name="swa_fwd", grid=(lp // tm,),
        in_specs=_swa_specs(tm, lambda i: i),
        out_specs=pl.BlockSpec((tm, D_ATT), lambda i: (i, 0)),
        out_shape=jax.ShapeDtypeStruct((lp, D_ATT), MXU_DTYPE),
        compiler_params=_cp(("arbitrary",)),
    )(u, u, u, qg, kg, sinks)


def _swa_bwd(u, dyc, du, qg, kg, sinks):
    lp = u.shape[0]
    tm = TM_BR
    nb = lp // tm
    order = lambda ii: nb - 1 - ii

    def body(own_ref, prev_ref, meta_ref, qg_ref, kg_ref, sink_ref, dy_ref, du_in, du_ref, dqg_ref, dkg_ref, dsk_ref,
             carry_ref, macc_ref):
        del du_in
        ii = pl.program_id(0)
        i = nb - 1 - ii

        @pl.when(ii == 0)
        def _():
            carry_ref[...] = jnp.zeros_like(carry_ref)
            macc_ref[...] = jnp.zeros_like(macc_ref)
            dqg_ref[...] = jnp.zeros_like(dqg_ref)
            dkg_ref[...] = jnp.zeros_like(dkg_ref)
            dsk_ref[...] = jnp.zeros_like(dsk_ref)

        own = own_ref[...]
        k_raw, krs, kn, v, lo = _swa_keys(own[:, 1024:1280], prev_ref, meta_ref, kg_ref[...], tm)
        kuse, vuse = _placed(kn, lo), _placed(v, lo)
        bias = jnp.where(_swa_mask(i, tm), 0.0, NEG)
        dkn_t = jnp.zeros((D_KV, N_KEYS), F32)
        dvn_t = jnp.zeros((D_KV, N_KEYS), F32)
        for gi in range(ATT_Q_HEADS // 2):
            j = gi // 2
            sl = slice(gi * 128, (gi + 1) * 128)
            qraw = own[:, sl]
            qr = _half_rms(qraw, lo)
            qxh = qraw * qr
            qs = qxh * (qg_ref[...] * ATT_SCALE)
            ps, invs, pk, qms = [], [], [], []
            og = jnp.zeros((tm, 128), F32)
            for e in range(2):
                qm = jnp.where(lo if e == 0 else ~lo, qs, 0.0)
                s = _dot_nt(qm, kuse[j][e]) + bias
                sk = sink_ref[2 * gi + e]
                m = jnp.maximum(jnp.max(s, axis=-1, keepdims=True), sk)
                p = jnp.exp(s - m)
                inv = 1.0 / (jnp.sum(p, axis=-1, keepdims=True) + jnp.exp(sk - m))
                ps.append(p)
                invs.append(inv)
                pk.append(jnp.exp(sk - m) * inv)
                qms.append(qm)
                og = og + _dot(p, vuse[j][e]) * inv
            gt = own[:, 512 + gi * 128:512 + (gi + 1) * 128]
            sg = _sig(gt)
            dy = dy_ref[:, sl].astype(F32)
            dgt = dy * og * _dsilu(gt, sg)
            dog = dy * gt * sg
            dqn = jnp.zeros((tm, 128), F32)
            for e in range(2):
                half = lo if e == 0 else ~lo
                dog_m = jnp.where(half, dog, 0.0)
                dl = jnp.sum(dog_m * og, axis=1, keepdims=True)
                dp = _dot_nt(dog_m, vuse[j][e])
                ds = ps[e] * ((dp - dl) * invs[e])
                hsk = 2 * gi + e
                dsk_ref[hsk:hsk + 1, :] += jnp.zeros((1, 128), F32) - jnp.sum(pk[e] * dl, axis=0, keepdims=True)
                dqn = dqn + _dot(ds, kuse[j][e])
                dk_e = _dot_tn(qms[e], ds)
                dv_e = _dot_tn(dog_m * invs[e], ps[e])
                if j != e:
                    dk_e = pltpu.roll(dk_e, ATT_HEAD_DIM, 0)
                    dv_e = pltpu.roll(dv_e, ATT_HEAD_DIM, 0)
                dkn_t = dkn_t + dk_e
                dvn_t = dvn_t + dv_e
            dqn = dqn * ATT_SCALE
            dqg_ref[...] += jnp.sum(dqn * qxh, axis=0, keepdims=True)
            dqx = dqn * qg_ref[...]
            dq = qr * (dqx - qxh * _half_sum(dqx * qxh, lo) * (1.0 / ATT_HEAD_DIM))
            du_ref[:, sl] = dq.astype(MXU_DTYPE)
            du_ref[:, 512 + gi * 128:512 + (gi + 1) * 128] = dgt.astype(MXU_DTYPE)

        dkn, dvn = dkn_t.T, dvn_t.T
        macc_ref[...] += jnp.concatenate([dkn[0:CHUNK], dvn[0:CHUNK]], axis=1)
        own0 = N_KEYS - tm
        tot = jnp.concatenate([dkn[own0:], dvn[own0:]], axis=1) + carry_ref[...]
        carry_ref[0:tm - PREV_ROWS, :] = jnp.zeros((tm - PREV_ROWS, 2 * D_KV), F32)
        carry_ref[tm - PREV_ROWS:tm, :] = jnp.concatenate([dkn[CHUNK:own0], dvn[CHUNK:own0]], axis=1)
        first = jnp.where((i == 0) & (_iota((tm, 1), 0) < CHUNK), 1.0, 0.0)
        tot = tot + first * jnp.concatenate([macc_ref[...], jnp.zeros((tm - CHUNK, 2 * D_KV), F32)], axis=0)
        dkn_own, dv_own = tot[:, :D_KV], tot[:, D_KV:]
        kx = k_raw[own0:] * krs[own0:]
        dkg_ref[...] += jnp.sum(dkn_own * kx, axis=0, keepdims=True)
        dkx = dkn_own * kg_ref[...]
        dk = krs[own0:] * (dkx - kx * _half_sum(dkx * kx, lo) * (1.0 / ATT_HEAD_DIM))
        du_ref[:, 1024:1152] = dk.astype(MXU_DTYPE)
        du_ref[:, 1152:1280] = dv_own.astype(MXU_DTYPE)
        du_ref[:, 1280:W_C] = jnp.zeros((tm, W_C - 1280), MXU_DTYPE)

    return pl.pallas_call(
        body, name="swa_bwd", grid=(nb,),
        in_specs=_swa_specs(tm, order) + [pl.BlockSpec((tm, D_ATT), lambda ii: (order(ii), 0)),
                                          pl.BlockSpec(memory_space=pl.ANY)],
        out_specs=[pl.BlockSpec((tm, W_C), lambda ii: (order(ii), OFF_C // W_C)), _full_spec((1, 128)),
                   _full_spec((1, 128)), _full_spec((ATT_Q_HEADS, 128))],
        out_shape=[jax.ShapeDtypeStruct(du.shape, du.dtype), jax.ShapeDtypeStruct((1, 128), F32),
                   jax.ShapeDtypeStruct((1, 128), F32), jax.ShapeDtypeStruct((ATT_Q_HEADS, 128), F32)],
        scratch_shapes=[pltpu.VMEM((tm, 2 * D_KV), F32), pltpu.VMEM((CHUNK, 2 * D_KV), F32)],
        input_output_aliases={7: 0},
        compiler_params=_cp(("arbitrary",)),
    )(u, u, u, qg, kg, sinks, dyc, du)


def _load_once(pairs, first):
    @pl.when(first)
    def _():
        for src, dst in pairs:
            pltpu.sync_copy(src, dst)


def _mix_fwd(h, u, ya, yb, yc, wa, wb, wc, wo):
    lp = h.shape[0]
    tm = TM_BR

    def body(h_ref, g_ref, ya_ref, yb_ref, yc_ref, wa_hbm, wb_hbm, wc_hbm, wo_hbm, out_ref, wa_ref, wb_ref, wc_ref,
             wo_ref):
        _load_once(((wa_hbm, wa_ref), (wb_hbm, wb_ref), (wc_hbm, wc_ref), (wo_hbm, wo_ref)), pl.program_id(0) == 0)
        mixed = jnp.zeros((tm, D_MODEL), F32)
        for n, (y_ref, w_ref) in enumerate(((ya_ref, wa_ref), (yb_ref, wb_ref), (yc_ref, wc_ref))):
            z = jnp.dot(y_ref[...], w_ref[...], preferred_element_type=F32)
            mixed = mixed + _sig(g_ref[:, n * D_MODEL:(n + 1) * D_MODEL]) * z
        out_ref[...] = h_ref[...] + _dot(mixed, wo_ref[...])

    ybs = pl.BlockSpec((tm, 512), lambda i: (i, 0))
    anyspec = pl.BlockSpec(memory_space=pl.ANY)
    return pl.pallas_call(
        body, name="mix_fwd", grid=(lp // tm,),
        in_specs=[pl.BlockSpec((tm, D_MODEL), lambda i: (i, 0)), pl.BlockSpec((tm, W_G), lambda i: (i, OFF_G // W_G)),
                  ybs, ybs, ybs, anyspec, anyspec, anyspec, anyspec],
        out_specs=pl.BlockSpec((tm, D_MODEL), lambda i: (i, 0)),
        out_shape=jax.ShapeDtypeStruct((lp, D_MODEL), F32),
        scratch_shapes=[pltpu.VMEM((512, D_MODEL), MXU_DTYPE)] * 3 + [pltpu.VMEM((D_MODEL, D_MODEL), MXU_DTYPE)],
        compiler_params=_cp(("arbitrary",)),
    )(h, u, ya, yb, yc, wa, wb, wc, wo)


def _mix_bwd(dh, u, ya, yb, yc, wa, wb, wc, wo):
    lp = dh.shape[0]
    tm = TM_BR
    nb = lp // tm

    def body(dh_ref, g_ref, ya_ref, yb_ref, yc_ref, wa_hbm, wb_hbm, wc_hbm, wo_hbm,
             du_ref, dya_ref, dyb_ref, dyc_ref, dwa_hbm, dwb_hbm, dwc_hbm, dwo_hbm,
             wa_ref, wb_ref, wc_ref, wo_ref, dwa_ref, dwb_ref, dwc_ref, dwo_ref):
        i = pl.program_id(0)
        _load_once(((wa_hbm, wa_ref), (wb_hbm, wb_ref), (wc_hbm, wc_ref), (wo_hbm, wo_ref)), i == 0)

        @pl.when(i == 0)
        def _():
            for r in (dwa_ref, dwb_ref, dwc_ref, dwo_ref):
                r[...] = jnp.zeros_like(r)

        dh_b = dh_ref[...].astype(MXU_DTYPE)
        dmixed = _dot_nt(dh_b, wo_ref[...])
        mixed = jnp.zeros((tm, D_MODEL), F32)
        for n, (y_ref, w_ref, dy_ref, dw_ref) in enumerate(((ya_ref, wa_ref, dya_ref, dwa_ref),
                                                            (yb_ref, wb_ref, dyb_ref, dwb_ref),
                                                            (yc_ref, wc_ref, dyc_ref, dwc_ref))):
            y = y_ref[...]
            z = jnp.dot(y, w_ref[...], preferred_element_type=F32)
            gate = _sig(g_ref[:, n * D_MODEL:(n + 1) * D_MODEL])
            mixed = mixed + gate * z
            du_ref[:, n * D_MODEL:(n + 1) * D_MODEL] = (z * dmixed * gate * (1.0 - gate)).astype(MXU_DTYPE)
            dz = (gate * dmixed).astype(MXU_DTYPE)
            dy_ref[...] = _dot_nt(dz, w_ref[...]).astype(MXU_DTYPE)
            dw_ref[...] += _dot_tn(y, dz)
        dwo_ref[...] += _dot_tn(mixed, dh_b)

        @pl.when(i == nb - 1)
        def _():
            for src, dst in ((dwa_ref, dwa_hbm), (dwb_ref, dwb_hbm), (dwc_ref, dwc_hbm), (dwo_ref, dwo_hbm)):
                pltpu.sync_copy(src, dst)

    ybs = pl.BlockSpec((tm, 512), lambda i: (i, 0))
    anyspec = pl.BlockSpec(memory_space=pl.ANY)
    wsh = jax.ShapeDtypeStruct((512, D_MODEL), F32)
    return pl.pallas_call(
        body, name="mix_bwd", grid=(nb,),
        in_specs=[pl.BlockSpec((tm, D_MODEL), lambda i: (i, 0)), pl.BlockSpec((tm, W_G), lambda i: (i, OFF_G // W_G)),
                  ybs, ybs, ybs, anyspec, anyspec, anyspec, anyspec],
        out_specs=[pl.BlockSpec((tm, W_G), lambda i: (i, OFF_G // W_G)), ybs, ybs, ybs, anyspec, anyspec, anyspec, anyspec],
        out_shape=[jax.ShapeDtypeStruct((lp, NP), MXU_DTYPE)] + [jax.ShapeDtypeStruct((lp, 512), MXU_DTYPE)] * 3
        + [wsh, wsh, wsh, jax.ShapeDtypeStruct((D_MODEL, D_MODEL), F32)],
        scratch_shapes=[pltpu.VMEM((512, D_MODEL), MXU_DTYPE)] * 3 + [pltpu.VMEM((D_MODEL, D_MODEL), MXU_DTYPE)]
        + [pltpu.VMEM((512, D_MODEL), F32)] * 3 + [pltpu.VMEM((D_MODEL, D_MODEL), F32)],
        compiler_params=_cp(("arbitrary",)),
    )(dh, u, ya, yb, yc, wa, wb, wc, wo)


def _loss_head(h, target_p, seq):
    lp = h.shape[0]
    tm = TM_BR

    def body(h_ref, t_ref, dh_ref, loss_ref):
        i = pl.program_id(0)

        @pl.when(i == 0)
        def _():
            loss_ref[...] = jnp.zeros_like(loss_ref)

        rows = i * tm + _iota((tm, 1), 0)
        e = jnp.where((rows >= CHUNK) & (rows < CHUNK + seq), h_ref[...] - t_ref[...], 0.0)
        dh_ref[...] = e * (1.0 / D_MODEL)
        part = jnp.sum(jnp.mean(e * e, axis=-1, keepdims=True), axis=0, keepdims=True)
        loss_ref[...] += 0.5 * part

    return pl.pallas_call(
        body, name="loss_head", grid=(lp // tm,),
        in_specs=[pl.BlockSpec((tm, D_MODEL), lambda i: (i, 0))] * 2,
        out_specs=[pl.BlockSpec((tm, D_MODEL), lambda i: (i, 0)), _full_spec((1, 128))],
        out_shape=[jax.ShapeDtypeStruct((lp, D_MODEL), F32), jax.ShapeDtypeStruct((1, 128), F32)],
        compiler_params=_cp(("arbitrary",)),
    )(h, target_p)


def _lb_rows(p_ref):
    depth = p_ref.shape[0]
    rows = [p_ref[l:l + 1, :] for l in range(depth)]
    mx = functools.reduce(jnp.maximum, rows)
    ex = [jnp.exp(r - mx) for r in rows]
    tot = functools.reduce(jnp.add, ex)
    sm = [e / tot for e in ex]
    cs, run = [], jnp.zeros_like(sm[0])
    for l in range(depth):
        run = run + sm[l]
        cs.append(run)
    return sm, [c - sm[0] for c in cs]


def _lb_fwd(p):
    def body(p_ref, o_ref):
        _, xs = _lb_rows(p_ref)
        for l, xl in enumerate(xs):
            o_ref[l:l + 1, :] = jnp.clip(xl, 0.0, 1.0)

    return pl.pallas_call(body, name="lb_fwd", out_shape=jax.ShapeDtypeStruct(p.shape, F32))(p)


def _lb_bwd(p, dlb):
    def body(p_ref, d_ref, o_ref):
        sm, xs = _lb_rows(p_ref)
        depth = len(xs)
        dx = []
        for l in range(depth):
            x = xs[l]
            g0 = jnp.where(x > 0.0, 1.0, jnp.where(x == 0.0, 0.5, 0.0))
            y = jnp.maximum(x, 0.0)
            g1 = jnp.where(y < 1.0, 1.0, jnp.where(y == 1.0, 0.5, 0.0))
            dx.append(d_ref[l:l + 1, :] * g0 * g1)
        dsm = [functools.reduce(jnp.add, dx[jj:]) for jj in range(depth)]
        dsm[0] = dsm[0] - functools.reduce(jnp.add, dx)
        inner = functools.reduce(jnp.add, [a * b for a, b in zip(sm, dsm)])
        for l in range(depth):
            o_ref[l:l + 1, :] = sm[l] * (dsm[l] - inner)

    return pl.pallas_call(body, name="lb_bwd", out_shape=jax.ShapeDtypeStruct(p.shape, F32))(p, dlb)


def _exchange(gather, scatter, name):
    ng, ns = len(gather), len(scatter)
    n = ng + ns

    def body(*refs):
        x_refs, o_refs, sems = refs[:n], refs[n:2 * n], refs[2 * n:]
        exs = []
        if ng:
            exs.append(_Exchange(x_refs[:ng], o_refs[:ng], *sems[:3], scatter=False))
        if ns:
            exs.append(_Exchange(x_refs[ng:], o_refs[ng:], *sems[-3:], scatter=True))
        for ex in exs:
            ex.start()
        for ex in exs:
            ex.finish()

    out_shape = [jax.ShapeDtypeStruct((N_DEV,) + x.shape, x.dtype) for x in gather]
    out_shape += [jax.ShapeDtypeStruct(x.shape, x.dtype) for x in scatter]
    return pl.pallas_call(
        body, name=name, in_specs=[_ANY] * n, out_specs=[_ANY] * n, out_shape=out_shape,
        scratch_shapes=(_exchange_sems(ng) if ng else []) + (_exchange_sems(ns) if ns else []),
        compiler_params=pltpu.CompilerParams(has_side_effects=True),
    )(*gather, *scatter)


def _adamw(gp, w, m, v, name):
    r, cc = w.shape
    tr = 256 if r % 256 == 0 else r

    def body(g_ref, w_ref, m_ref, v_ref, go_ref, d_ref, mo_ref, vo_ref):
        g = g_ref[0].astype(F32)
        for s in range(1, N_DEV):
            g = g + g_ref[s].astype(F32)
        go_ref[...] = g
        mn = ADAM_B1 * m_ref[...] + (1.0 - ADAM_B1) * g
        vn = ADAM_B2 * v_ref[...] + (1.0 - ADAM_B2) * (g * g)
        m_hat = mn / (1.0 - ADAM_B1 ** ADAM_STEP)
        v_hat = vn / (1.0 - ADAM_B2 ** ADAM_STEP)
        d_ref[...] = -ADAM_LR * (m_hat / (jnp.sqrt(v_hat) + ADAM_EPS) + ADAM_WD * w_ref[...])
        mo_ref[...] = mn
        vo_ref[...] = vn

    bs = pl.BlockSpec((tr, cc), lambda i: (i, 0))
    sh = jax.ShapeDtypeStruct((r, cc), F32)
    return pl.pallas_call(
        body, name=name, grid=(r // tr,),
        in_specs=[pl.BlockSpec((N_DEV, tr, cc), lambda i: (0, i, 0)), bs, bs, bs],
        out_specs=[bs, bs, bs, bs], out_shape=[sh, sh, sh, sh],
        compiler_params=_cp(("parallel",)),
    )(gp, w, m, v)


def _pack_cols(w):
    parts, pos = [], 0
    for pstart, ostart, width in _PACK:
        if pstart != pos:
            parts.append(jnp.zeros(w.shape[:-1] + (pstart - pos,), w.dtype))
        parts.append(w[..., ostart:ostart + width])
        pos = pstart + width
    return jnp.concatenate(parts, axis=-1)


def _unpack_cols(wp):
    by_orig = sorted(_PACK, key=lambda t: t[1])
    return jnp.concatenate([wp[..., p:p + wd] for p, _, wd in by_orig], axis=-1)


_LAYER_SHARDED = ("w_in", "conv_w", "w_conv_out", "w_hg_out", "w_att_out", "w_out")
_NARROW = ("w_in", "w_conv_out", "w_hg_out", "w_att_out", "w_out")
_REPLICATED = ("norm_g", "conv_b", "conv_ln_g", "conv_ln_b", "hg_lower_bounds", "hg_norm_g", "q_norm_g", "k_norm_g",
               "attn_sinks")
_WEIGHTS = ("meta_tokens", "norm_g", "w_in", "conv_w", "conv_b", "conv_ln_g", "conv_ln_b", "w_conv_out",
            "hg_lower_bounds", "hg_norm_g", "w_hg_out", "q_norm_g", "k_norm_g", "attn_sinks", "w_att_out", "w_out")
_ROW_SHARDED = ("w_out",)


def _assemble(name, g):
    if name in _ROW_SHARDED:
        return g.reshape((N_DEV * g.shape[1],) + g.shape[2:])
    full = jnp.moveaxis(g, 0, -2)
    return full.reshape(full.shape[:-2] + (N_DEV * full.shape[-1],))


def _split(name, full):
    if name in _ROW_SHARDED:
        return full.reshape((N_DEV, full.shape[0] // N_DEV) + full.shape[1:])
    c = full.shape[-1] // N_DEV
    return jnp.moveaxis(full.reshape(full.shape[:-1] + (N_DEV, c)), -2, 0)


def _layer_weights(gathered):
    full = {k: _assemble(k, g) for k, g in zip(_LAYER_SHARDED, gathered)}
    wp = _pack_cols(full["w_in"])
    return dict(wp=wp, wpt=wp.T, cw=full["conv_w"], wa=full["w_conv_out"], wb=full["w_hg_out"],
                wc=full["w_att_out"], wo=full["w_out"])


def _layer_fwd(h, lw, sp, gather):
    u, hn, gathered = _inproj_fwd(h, sp["norm_g"], lw["wp"], gather)
    ya, y_conv = _conv_fwd(u, lw["cw"], sp["conv_b"], sp["conv_ln_g"], sp["conv_ln_b"])
    yb, states = _hgrn_fwd(u, sp["lb"], sp["hg_norm_g"])
    yc = _swa_fwd(u, sp["qg"], sp["kg"], sp["sinks"])
    h_next = _mix_fwd(h, u, ya, yb, yc, lw["wa"], lw["wb"], lw["wc"], lw["wo"])
    return h_next, (h, u, hn, ya, yb, yc, states, y_conv), gathered


def _layer_bwd(dh, saved, lw, sp, stacked, layer, depth):
    h_l, u, hn, ya, yb, yc, states, y_conv = saved
    du, dya, dyb, dyc, dwa, dwb, dwc, dwo = _mix_bwd(dh, u, ya, yb, yc, lw["wa"], lw["wb"], lw["wc"], lw["wo"])
    du, dy, dlg, dlb_ = _conv_bwd1(u, y_conv, dya, du, sp["conv_ln_g"], sp["conv_ln_b"])
    du, dcw, dcb = _conv_bwd2(u, dy, du, lw["cw"])
    du, dlbl, dgg = _hgrn_bwd(u, dyb, states, du, sp["lb"], sp["hg_norm_g"])
    du, dqg, dkg, dsk = _swa_bwd(u, dyc, du, sp["qg"], sp["kg"], sp["sinks"])
    dwp = _inproj_bwd_dw(hn, du)
    full = dict(w_in=_unpack_cols(dwp), conv_w=dcw, w_conv_out=dwa, w_hg_out=dwb, w_att_out=dwc, w_out=dwo)
    pieces = [_split(k, full[k]).astype(WIRE_DTYPE) for k in _LAYER_SHARDED]
    dh, dng, stacked = _inproj_bwd_dh(du, lw["wpt"], h_l, sp["norm_g"], dh, pieces, stacked, layer, depth)
    fold = lambda a: a[0, :ATT_HEAD_DIM] + a[0, ATT_HEAD_DIM:]
    small = dict(norm_g=dng[0], conv_b=dcb[0], conv_ln_g=dlg[0], conv_ln_b=dlb_[0], hg_lower_bounds=dlbl[0],
                 hg_norm_g=dgg[0], q_norm_g=fold(dqg), k_norm_g=fold(dkg), attn_sinks=dsk[:, 0])
    return dh, small, stacked


def _as2d(a):
    return a.reshape((-1, a.shape[-1]))


def kernel(x, meta_tokens, norm_g, w_in, conv_w, conv_b, conv_ln_g, conv_ln_b, w_conv_out, hg_lower_bounds, hg_norm_g, w_hg_out, q_norm_g, k_norm_g, attn_sinks, w_att_out, w_out, loss_target, m_meta_tokens, m_norm_g, m_w_in, m_conv_w, m_conv_b, m_conv_ln_g, m_conv_ln_b, m_w_conv_out, m_hg_lower_bounds, m_hg_norm_g, m_w_hg_out, m_q_norm_g, m_k_norm_g, m_attn_sinks, m_w_att_out, m_w_out, v_meta_tokens, v_norm_g, v_w_in, v_conv_w, v_conv_b, v_conv_ln_g, v_conv_ln_b, v_w_conv_out, v_hg_lower_bounds, v_hg_norm_g, v_w_hg_out, v_q_norm_g, v_k_norm_g, v_attn_sinks, v_w_att_out, v_w_out):
    w = dict(meta_tokens=meta_tokens, norm_g=norm_g, w_in=w_in, conv_w=conv_w, conv_b=conv_b, conv_ln_g=conv_ln_g,
             conv_ln_b=conv_ln_b, w_conv_out=w_conv_out, hg_lower_bounds=hg_lower_bounds, hg_norm_g=hg_norm_g,
             w_hg_out=w_hg_out, q_norm_g=q_norm_g, k_norm_g=k_norm_g, attn_sinks=attn_sinks, w_att_out=w_att_out,
             w_out=w_out)
    m = dict(meta_tokens=m_meta_tokens, norm_g=m_norm_g, w_in=m_w_in, conv_w=m_conv_w, conv_b=m_conv_b,
             conv_ln_g=m_conv_ln_g, conv_ln_b=m_conv_ln_b, w_conv_out=m_w_conv_out, hg_lower_bounds=m_hg_lower_bounds,
             hg_norm_g=m_hg_norm_g, w_hg_out=m_w_hg_out, q_norm_g=m_q_norm_g, k_norm_g=m_k_norm_g,
             attn_sinks=m_attn_sinks, w_att_out=m_w_att_out, w_out=m_w_out)
    v = dict(meta_tokens=v_meta_tokens, norm_g=v_norm_g, w_in=v_w_in, conv_w=v_conv_w, conv_b=v_conv_b,
             conv_ln_g=v_conv_ln_g, conv_ln_b=v_conv_ln_b, w_conv_out=v_w_conv_out, hg_lower_bounds=v_hg_lower_bounds,
             hg_norm_g=v_hg_norm_g, w_hg_out=v_w_hg_out, q_norm_g=v_q_norm_g, k_norm_g=v_k_norm_g,
             attn_sinks=v_attn_sinks, w_att_out=v_w_att_out, w_out=v_w_out)

    depth = norm_g.shape[0]
    seq = x.shape[1]
    lp = -(-(seq + CHUNK) // TM_MM) * TM_MM
    tail = lp - seq - CHUNK
    zeros = lambda n: jnp.zeros((n, D_MODEL), F32)

    def shards(l):
        return [w[k][l].astype(MXU_DTYPE) if k in _NARROW else w[k][l] for k in _LAYER_SHARDED]

    first = _exchange(shards(0) + [meta_tokens], [], "gather_first")
    gathered, meta_full = first[:-1], _assemble("meta_tokens", first[-1])
    h = jnp.concatenate([zeros(META_PAD), meta_full, x[0], zeros(tail)], axis=0)
    target_p = jnp.concatenate([zeros(CHUNK), loss_target[0], zeros(tail)], axis=0)

    lb_all = _lb_fwd(hg_lower_bounds)
    tile2 = lambda a: jnp.concatenate([a, a], axis=-1)
    row = lambda a, l: a[l][None, :]

    def small_rows(l):
        sp = {k: row(w[k], l) for k in ("norm_g", "conv_b", "conv_ln_g", "conv_ln_b", "hg_norm_g")}
        sp.update(lb=row(lb_all, l), qg=tile2(row(q_norm_g, l)), kg=tile2(row(k_norm_g, l)), sinks=attn_sinks[l])
        return sp

    layer_w, saved = [], []
    for l in range(depth):
        layer_w.append(_layer_weights(gathered))
        h, sv, gathered = _layer_fwd(h, layer_w[l], small_rows(l), shards(l + 1) if l + 1 < depth else [])
        saved.append(sv)

    dh, loss_row = _loss_head(h, target_p, seq)
    loss = lax.psum(loss_row[0, 0], ("x", "y", "c"))

    stacked, small_grads = None, [None] * depth
    for l in reversed(range(depth)):
        dh, small_grads[l], stacked = _layer_bwd(dh, saved[l], layer_w[l], small_rows(l), stacked, l, depth)
    grad_x = dh[CHUNK:CHUNK + seq]
    grads = {k: jnp.stack([small_grads[l][k] for l in range(depth)]) for k in _REPLICATED}
    grads["hg_lower_bounds"] = _lb_bwd(hg_lower_bounds, grads["hg_lower_bounds"])

    small = jnp.concatenate([grads[k].reshape(-1) for k in _REPLICATED])
    small = jnp.concatenate([small, jnp.zeros((-small.shape[0] % 128,), F32)]).reshape(-1, 128)
    small_all, meta_pieces = _exchange([small], [_split("meta_tokens", dh[META_PAD:CHUNK])], "exchange_small_grads")
    small_all = small_all.reshape(N_DEV, -1)

    out_g, out_d, out_m, out_v = {}, {}, {}, {}
    for k, gp in zip(("meta_tokens",) + _LAYER_SHARDED, [meta_pieces] + stacked):
        shp = w[k].shape
        res = _adamw(gp.reshape((N_DEV,) + _as2d(w[k]).shape), _as2d(w[k]), _as2d(m[k]), _as2d(v[k]), "adamw_" + k)
        out_g[k], out_d[k], out_m[k], out_v[k] = (r.reshape(shp) for r in res)
    off = 0
    for k in _REPLICATED:
        shp = w[k].shape
        n = w[k].size
        gp = small_all[:, off:off + n].reshape((N_DEV,) + shp)
        off += n
        res = _adamw(gp, w[k], m[k], v[k], "adamw_" + k)
        out_g[k], out_d[k], out_m[k], out_v[k] = res

    return (loss, grad_x[None], *[out_g[k] for k in _WEIGHTS], *[out_d[k] for k in _WEIGHTS],
            *[out_m[k] for k in _WEIGHTS], *[out_v[k] for k in _WEIGHTS])
```

```python
import functools

import jax
import jax.numpy as jnp
from jax import lax
from jax.experimental import pallas as pl
from jax.experimental.pallas import tpu as pltpu

F32 = jnp.float32
MXU_DTYPE = jnp.bfloat16
WIRE_DTYPE = jnp.bfloat16

D_MODEL = 1024
CHUNK = 64
N_META = 16
META_PAD = CHUNK - N_META
D_CONV = 512
CONV_WIDTH = 31
HG_HEADS = 4
HG_D = 128
D_HG = HG_HEADS * HG_D
F_FLOOR = 1e-30
ATT_Q_HEADS = 8
ATT_HEAD_DIM = 64
D_ATT = 512
D_KV = 128
WINDOW_CHUNKS = 2
EPS = 1e-6
D_IN = 7936
N_DEV = 8

ADAM_LR = 0.001
ADAM_B1 = 0.9
ADAM_B2 = 0.999
ADAM_EPS = 1e-08
ADAM_WD = 0.01
ADAM_STEP = 10

NP = 8192
OFF_B, W_B = 0, 2048
OFF_A, W_A = 2048, 1024
OFF_G, W_G = 3072, 3072
OFF_C, W_C = 6144, 1536
OFF_AG, W_AG = 7680, 512
_PACK = ((0, 1536, 2048), (2048, 0, 1024), (3072, 4864, 3072), (6144, 3584, 512), (6656, 4352, 512),
         (7168, 4096, 256), (7680, 1024, 512))
_PAD_AT, _PAD_W = 7424, 256

TM_MM = 1280
TM_BR = 256
TM_MIX = 640
HALO = 32
EXP_CLAMP = 80.0
VMEM_LIMIT = 56 * 1024 * 1024

_HI = lax.Precision.HIGHEST


def _cp(sem):
    return pltpu.CompilerParams(dimension_semantics=sem, vmem_limit_bytes=VMEM_LIMIT)


def _sig(x):
    return 1.0 / (1.0 + jnp.exp(-x))


def _dot(a, b):
    return jnp.dot(a.astype(MXU_DTYPE), b.astype(MXU_DTYPE), preferred_element_type=F32)


def _dot_nt(a, b):
    return lax.dot_general(a.astype(MXU_DTYPE), b.astype(MXU_DTYPE), (((1,), (1,)), ((), ())),
                           preferred_element_type=F32)


def _dot_tn(a, b):
    return lax.dot_general(a.astype(MXU_DTYPE), b.astype(MXU_DTYPE), (((0,), (0,)), ((), ())),
                           preferred_element_type=F32)


def _rnd(x):
    return x.astype(MXU_DTYPE).astype(F32)


def _iota(shape, dim):
    return lax.broadcasted_iota(jnp.int32, shape, dim)


def _full_spec(shape):
    nd = len(shape)
    return pl.BlockSpec(shape, lambda *_: (0,) * nd)


def _my_index():
    return 4 * lax.axis_index("x") + 2 * lax.axis_index("y") + lax.axis_index("c")


def _mesh_id(p):
    return (p >> 2, (p >> 1) & 1, p & 1)


class _Exchange:
    def __init__(self, x_refs, o_refs, send_sems, recv_sems, loc_sems, scatter, dst=lambda o, s: o.at[s]):
        me = _my_index()
        self.local, self.sends, self.recvs = [], [], []
        for a, (x, o) in enumerate(zip(x_refs, o_refs)):
            mine = x.at[me] if scatter else x
            self.local.append(pltpu.make_async_copy(mine, dst(o, me), loc_sems.at[a]))
            for k in range(1, N_DEV):
                to, frm = (me + k) % N_DEV, (me + N_DEV - k) % N_DEV
                sems = dict(send_sem=send_sems.at[a, k - 1], recv_sem=recv_sems.at[a, k - 1],
                            device_id_type=pl.DeviceIdType.MESH)
                self.sends.append(pltpu.make_async_remote_copy(
                    src_ref=x.at[to] if scatter else x, dst_ref=dst(o, me), device_id=_mesh_id(to), **sems))
                self.recvs.append(pltpu.make_async_remote_copy(
                    src_ref=mine, dst_ref=dst(o, frm), device_id=_mesh_id(frm), **sems))

    def start(self):
        for cp in self.local + self.sends:
            cp.start()

    def finish(self):
        for cp in self.recvs:
            cp.wait_recv()
        for cp in self.sends:
            cp.wait_send()
        for cp in self.local:
            cp.wait()


def _exchange_sems(n):
    return [pltpu.SemaphoreType.DMA((n, N_DEV - 1)), pltpu.SemaphoreType.DMA((n, N_DEV - 1)),
            pltpu.SemaphoreType.DMA((n,))]


_ANY = pl.BlockSpec(memory_space=pl.ANY)


def _inproj_fwd(h, g, wp, gather=()):
    lp = h.shape[0]
    tm, tn = TM_MM, 1024
    ni, nj = lp // tm, NP // tn
    n = len(gather)

    def body(h_ref, g_ref, w_ref, *rest):
        x_refs, (u_ref, hn_ref), o_refs = rest[:n], rest[n:n + 2], rest[n + 2:2 * n + 2]
        hs_ref, sems = rest[2 * n + 2], rest[2 * n + 3:]
        i, j = pl.program_id(0), pl.program_id(1)
        if n:
            @pl.when((i == 0) & (j == 0))
            def _():
                _Exchange(x_refs, o_refs, *sems, scatter=False).start()

        @pl.when(j == 0)
        def _():
            x = h_ref[...]
            r = lax.rsqrt(jnp.mean(x * x, axis=-1, keepdims=True) + EPS)
            hn = (x * r * g_ref[...]).astype(MXU_DTYPE)
            hs_ref[...] = hn
            hn_ref[...] = hn
        u_ref[...] = jnp.dot(hs_ref[...], w_ref[...], preferred_element_type=F32)
        if n:
            @pl.when((i == ni - 1) & (j == nj - 1))
            def _():
                _Exchange(x_refs, o_refs, *sems, scatter=False).finish()

    res = pl.pallas_call(
        body, name="inproj_fwd_gather" if n else "inproj_fwd", grid=(ni, nj),
        in_specs=[pl.BlockSpec((tm, D_MODEL), lambda i, j: (i, 0)), pl.BlockSpec((1, D_MODEL), lambda i, j: (0, 0)),
                  pl.BlockSpec((D_MODEL, tn), lambda i, j: (0, j))] + [_ANY] * n,
        out_specs=[pl.BlockSpec((tm, tn), lambda i, j: (i, j)), pl.BlockSpec((tm, D_MODEL), lambda i, j: (i, 0))]
        + [_ANY] * n,
        out_shape=[jax.ShapeDtypeStruct((lp, NP), F32), jax.ShapeDtypeStruct((lp, D_MODEL), MXU_DTYPE)]
        + [jax.ShapeDtypeStruct((N_DEV,) + x.shape, x.dtype) for x in gather],
        scratch_shapes=[pltpu.VMEM((tm, D_MODEL), MXU_DTYPE)] + (_exchange_sems(n) if n else []),
        compiler_params=_cp(("arbitrary", "arbitrary")),
    )(h, g, wp, *gather)
    return res[0], res[1], list(res[2:])


def _inproj_bwd_dh(du, wpt, h, g, dh_next, pieces, stacked, layer, depth):
    lp = h.shape[0]
    tm, tk = TM_MM, 1024
    ni, nk = lp // tm, NP // tk
    n = len(pieces)
    n_acc = 0 if stacked is None else n

    def body(du_ref, w_ref, h_ref, g_ref, dhn_ref, *rest):
        x_refs, (dh_ref, dg_ref), o_refs = rest[:n], rest[n + n_acc:n + n_acc + 2], rest[n + n_acc + 2:2 * n + n_acc + 2]
        acc_ref, sems = rest[2 * n + n_acc + 2], rest[2 * n + n_acc + 3:]
        i, k = pl.program_id(0), pl.program_id(1)
        slot = lambda o, s: o.at[s, layer]

        @pl.when((i == 0) & (k == 0))
        def _():
            _Exchange(x_refs, o_refs, *sems, scatter=True, dst=slot).start()
            dg_ref[...] = jnp.zeros_like(dg_ref)

        @pl.when(k == 0)
        def _():
            acc_ref[...] = jnp.zeros_like(acc_ref)

        acc_ref[...] += jnp.dot(du_ref[...], w_ref[...], preferred_element_type=F32)

        @pl.when(k == nk - 1)
        def _():
            dhn = acc_ref[...]
            x = h_ref[...]
            r = lax.rsqrt(jnp.mean(x * x, axis=-1, keepdims=True) + EPS)
            xh = x * r
            dg_ref[...] += jnp.sum(dhn * xh, axis=0, keepdims=True)
            dxh = dhn * g_ref[...]
            dx = r * (dxh - xh * jnp.mean(dxh * xh, axis=-1, keepdims=True))
            dh_ref[...] = dhn_ref[...] + dx

        @pl.when((i == ni - 1) & (k == nk - 1))
        def _():
            _Exchange(x_refs, o_refs, *sems, scatter=True, dst=slot).finish()

    acc_in = [] if stacked is None else list(stacked)
    res = pl.pallas_call(
        body, name="inproj_bwd_dh_scatter", grid=(ni, nk),
        in_specs=[pl.BlockSpec((tm, tk), lambda i, k: (i, k)), pl.BlockSpec((tk, D_MODEL), lambda i, k: (k, 0)),
                  pl.BlockSpec((tm, D_MODEL), lambda i, k: (i, 0)), pl.BlockSpec((1, D_MODEL), lambda i, k: (0, 0)),
                  pl.BlockSpec((tm, D_MODEL), lambda i, k: (i, 0))] + [_ANY] * (n + n_acc),
        out_specs=[pl.BlockSpec((tm, D_MODEL), lambda i, k: (i, 0)), pl.BlockSpec((1, D_MODEL), lambda i, k: (0, 0))]
        + [_ANY] * n,
        out_shape=[jax.ShapeDtypeStruct((lp, D_MODEL), F32), jax.ShapeDtypeStruct((1, D_MODEL), F32)]
        + [jax.ShapeDtypeStruct((N_DEV, depth) + p.shape[1:], p.dtype) for p in pieces],
        scratch_shapes=[pltpu.VMEM((tm, D_MODEL), F32)] + _exchange_sems(n),
        input_output_aliases={5 + n + a: 2 + a for a in range(n_acc)},
        compiler_params=_cp(("arbitrary", "arbitrary")),
    )(du, wpt, h, g, dh_next, *pieces, *acc_in)
    return res[0], res[1], list(res[2:])


def _inproj_bwd_dw(hn, du):
    lp = hn.shape[0]
    tm, tn = TM_MM, 1024

    def body(hn_ref, du_ref, dw_ref):
        @pl.when(pl.program_id(1) == 0)
        def _():
            dw_ref[...] = jnp.zeros_like(dw_ref)
        dw_ref[...] += _dot_tn(hn_ref[...], du_ref[...])

    return pl.pallas_call(
        body, name="inproj_bwd_dw", grid=(NP // tn, lp // tm),
        in_specs=[pl.BlockSpec((tm, D_MODEL), lambda j, m: (m, 0)), pl.BlockSpec((tm, tn), lambda j, m: (m, j))],
        out_specs=pl.BlockSpec((D_MODEL, tn), lambda j, m: (0, j)),
        out_shape=jax.ShapeDtypeStruct((D_MODEL, NP), F32),
        compiler_params=_cp(("parallel", "arbitrary")),
    )(hn, du)


N_SHIFT = 8
CONV_SUB = 32


def _shift_copies(src_ref, sh_ref):
    n = sh_ref.shape[1]
    for b in range(1, N_SHIFT):
        sh_ref[b - 1, :, :] = src_ref[pl.ds(b, n), :]


def _window(src_ref, sh_ref, off, r0, n):
    a, b = divmod(off, N_SHIFT)
    start = pl.multiple_of(r0 + a * N_SHIFT, N_SHIFT)
    if b == 0:
        return src_ref[pl.ds(start, n), :]
    return sh_ref[b - 1, pl.ds(start, n), :]


def _shift_scratch(tm):
    return pltpu.VMEM((N_SHIFT - 1, tm + HALO - N_SHIFT, D_CONV), F32)


def _glu_ext(a_ref, ah_ref, ext_ref, sh_ref, i, tm):
    rows = i * tm + _iota((tm, 1), 0)
    a = a_ref[...]
    p, sq = a[:, :D_CONV], _sig(a[:, D_CONV:])
    valid = rows >= META_PAD
    ah = ah_ref[...]
    ext_ref[0:HALO, :] = jnp.where(i > 0, ah[:, :D_CONV] * _sig(ah[:, D_CONV:]), 0.0)
    ext_ref[HALO:HALO + tm, :] = jnp.where(valid, p * sq, 0.0)
    _shift_copies(ext_ref, sh_ref)
    return p, sq, valid


def _layernorm_stats(y):
    mu = jnp.mean(y, axis=-1, keepdims=True)
    yc = y - mu
    rstd = lax.rsqrt(jnp.mean(yc * yc, axis=-1, keepdims=True) + EPS)
    return yc * rstd, rstd


def _conv_specs(tm):
    hb = tm // HALO
    return [pl.BlockSpec((tm, W_A), lambda i: (i, OFF_A // W_A)),
            pl.BlockSpec((HALO, W_A), lambda i: (jnp.maximum(i * hb - 1, 0), OFF_A // W_A)),
            pl.BlockSpec((tm, W_AG), lambda i: (i, OFF_AG // W_AG))]


def _conv_fwd_body(tm):
    def body(a_ref, ah_ref, ag_ref, w_ref, b_ref, lg_ref, lb_ref, ya_ref, y_ref, ext_ref, sh_ref):
        i = pl.program_id(0)
        _glu_ext(a_ref, ah_ref, ext_ref, sh_ref, i, tm)
        base = HALO - (CONV_WIDTH - 1)

        y = jnp.zeros((tm, D_CONV), F32) + b_ref[...]
        for k in range(CONV_WIDTH):
            y = y + w_ref[k:k + 1, :] * _window(ext_ref, sh_ref, base + k, 0, tm)
        y_ref[...] = y
        xh, _ = _layernorm_stats(y)
        yn = xh * lg_ref[...] + lb_ref[...]
        gt = ag_ref[...]
        ya_ref[...] = (yn * _sig(yn) * gt * _sig(gt)).astype(MXU_DTYPE)

    return body


def _dsilu(x, s):
    return s * (1.0 + x * (1.0 - s))


def _conv_bwd1(u, y, dya, du, lg, lb_):
    lp = u.shape[0]
    tm = TM_BR

    def body(ag_ref, y_ref, dya_ref, lg_ref, lb_ref, du_in, du_ref, dy_ref, dlg_ref, dlb_ref):
        del du_in
        i = pl.program_id(0)

        @pl.when(i == 0)
        def _():
            dlg_ref[...] = jnp.zeros_like(dlg_ref)
            dlb_ref[...] = jnp.zeros_like(dlb_ref)

        xh, rstd = _layernorm_stats(y_ref[...])
        yn = xh * lg_ref[...] + lb_ref[...]
        s1 = _sig(yn)
        gt = ag_ref[...]
        s2 = _sig(gt)
        do = dya_ref[...].astype(F32)
        du_ref[...] = (do * (yn * s1) * _dsilu(gt, s2)).astype(MXU_DTYPE)
        dyn = do * (gt * s2) * _dsilu(yn, s1)
        dlg_ref[...] += jnp.sum(dyn * xh, axis=0, keepdims=True)
        dlb_ref[...] += jnp.sum(dyn, axis=0, keepdims=True)
        dxh = dyn * lg_ref[...]
        dy_ref[...] = rstd * (dxh - jnp.mean(dxh, axis=-1, keepdims=True)
                              - xh * jnp.mean(dxh * xh, axis=-1, keepdims=True))

    rowspec = pl.BlockSpec((tm, D_CONV), lambda i: (i, 0))
    return pl.pallas_call(
        body, name="conv_bwd1", grid=(lp // tm,),
        in_specs=[_conv_specs(tm)[2], rowspec, rowspec, _full_spec((1, D_CONV)), _full_spec((1, D_CONV)),
                  pl.BlockSpec(memory_space=pl.ANY)],
        out_specs=[pl.BlockSpec((tm, W_AG), lambda i: (i, OFF_AG // W_AG)), rowspec,
                   _full_spec((1, D_CONV)), _full_spec((1, D_CONV))],
        out_shape=[jax.ShapeDtypeStruct(du.shape, du.dtype), jax.ShapeDtypeStruct((lp, D_CONV), F32),
                   jax.ShapeDtypeStruct((1, D_CONV), F32), jax.ShapeDtypeStruct((1, D_CONV), F32)],
        input_output_aliases={5: 0},
        compiler_params=_cp(("arbitrary",)),
    )(u, y, dya, lg, lb_, du)


def _conv_bwd2(u, dy, du, cw):
    lp = u.shape[0]
    tm = TM_BR
    nb = lp // tm
    hb = tm // HALO

    def body(a_ref, ah_ref, dy_ref, dyn_ref, w_ref, du_in, du_ref, dw_ref, db_ref, ext_ref, sh_ref, edy_ref, shd_ref,
             dwp_ref):
        del du_in
        i = pl.program_id(0)

        @pl.when(i == 0)
        def _():
            dwp_ref[...] = jnp.zeros_like(dwp_ref)
            db_ref[...] = jnp.zeros_like(db_ref)

        _glu_ext(a_ref, ah_ref, ext_ref, sh_ref, i, tm)
        dy_all = dy_ref[...]
        edy_ref[0:tm, :] = dy_all
        edy_ref[tm:tm + HALO, :] = jnp.where(i < nb - 1, dyn_ref[...], 0.0)
        _shift_copies(edy_ref, shd_ref)
        db_ref[...] += jnp.sum(dy_all, axis=0, keepdims=True)
        base = HALO - (CONV_WIDTH - 1)

        def fold8(x):
            parts = [x[s:s + N_SHIFT] for s in range(0, CONV_SUB, N_SHIFT)]
            return functools.reduce(jnp.add, parts)

        def sub(r, carry):
            r0 = pl.multiple_of(r * CONV_SUB, CONV_SUB)
            dy = dy_ref[pl.ds(r0, CONV_SUB), :]
            du0 = jnp.zeros((CONV_SUB, D_CONV), F32)
            for k in range(CONV_WIDTH):
                du0 = du0 + w_ref[k:k + 1, :] * _window(edy_ref, shd_ref, CONV_WIDTH - 1 - k, r0, CONV_SUB)
                dwp_ref[k] += fold8(dy * _window(ext_ref, sh_ref, base + k, r0, CONV_SUB))
            a = a_ref[pl.ds(r0, CONV_SUB), :]
            p, sq = a[:, :D_CONV], _sig(a[:, D_CONV:])
            valid = (i * tm + r0 + _iota((CONV_SUB, 1), 0)) >= META_PAD
            du0 = jnp.where(valid, du0, 0.0)
            du_ref[pl.ds(r0, CONV_SUB), :] = jnp.concatenate([du0 * sq, du0 * p * sq * (1.0 - sq)],
                                                             axis=1).astype(MXU_DTYPE)
            return carry

        lax.fori_loop(0, tm // CONV_SUB, sub, 0)

        @pl.when(i == nb - 1)
        def _():
            dw_ref[...] = jnp.sum(dwp_ref[...], axis=1)

    return pl.pallas_call(
        body, name="conv_bwd2", grid=(nb,),
        in_specs=_conv_specs(tm)[:2] + [pl.BlockSpec((tm, D_CONV), lambda i: (i, 0)),
                                        pl.BlockSpec((HALO, D_CONV), lambda i: (jnp.minimum((i + 1) * hb, nb * hb - 1), 0)),
                                        _full_spec((CONV_WIDTH, D_CONV)), pl.BlockSpec(memory_space=pl.ANY)],
        out_specs=[pl.BlockSpec((tm, W_A), lambda i: (i, OFF_A // W_A)), _full_spec((CONV_WIDTH, D_CONV)),
                   _full_spec((1, D_CONV))],
        out_shape=[jax.ShapeDtypeStruct(du.shape, du.dtype), jax.ShapeDtypeStruct((CONV_WIDTH, D_CONV), F32),
                   jax.ShapeDtypeStruct((1, D_CONV), F32)],
        scratch_shapes=[pltpu.VMEM((HALO + tm, D_CONV), F32), _shift_scratch(tm),
                        pltpu.VMEM((tm + HALO, D_CONV), F32), _shift_scratch(tm),
                        pltpu.VMEM((CONV_WIDTH, N_SHIFT, D_CONV), F32)],
        input_output_aliases={5: 0},
        compiler_params=_cp(("arbitrary",)),
    )(u, u, dy, dy, cw, du)


HG_T = 128
TM_HG = 256
HG_HALF = HG_T // 2


def _hg_chunk_fwd(blk, lb, valid, tri):
    bq, bf, v = blk[:, 0:512], blk[:, 512:1024], blk[:, 1024:1536]
    sgq = _sig(bq)
    qt = bq * sgq
    sz = _sig(bf)
    f = lb + (1.0 - lb) * sz
    g = jnp.where(valid, jnp.log(jnp.maximum(f, F_FLOOR)), 0.0)
    k = jnp.where(valid, (1.0 - lb) * (1.0 - sz), 0.0)
    b = jnp.dot(tri, g, precision=_HI, preferred_element_type=F32)
    ridx = _iota((HG_T, 1), 0)
    pick = lambda r: jnp.sum(jnp.where(ridx == r, b, 0.0), axis=0, keepdims=True)
    top = ridx < HG_HALF
    rx = pick(HG_HALF - 1)
    rd = jnp.where(top, pick(HG_HALF // 2 - 1), pick(HG_HALF + HG_HALF // 2 - 1))
    bl = pick(HG_T - 1)
    eqx = jnp.where(top, 0.0, jnp.exp(jnp.minimum(b - rx, 0.0)))
    ekx = jnp.where(top, jnp.exp(jnp.minimum(rx - b, 0.0)), 0.0)
    eqd = jnp.exp(jnp.minimum(b - rd, EXP_CLAMP))
    ekd = jnp.exp(jnp.minimum(rd - b, EXP_CLAMP))
    e = jnp.exp(b)
    ekl = jnp.exp(bl - b)
    el = jnp.exp(bl)
    return dict(bq=bq, sgq=sgq, qt=qt, sz=sz, f=f, k=k, v=v, top=top, eqx=eqx, ekx=ekx, eqd=eqd, ekd=ekd, e=e,
                ekl=ekl, el=el, qx=_rnd(qt * eqx), kx=_rnd(k * ekx), qd=_rnd(qt * eqd), kd=_rnd(k * ekd), qe=qt * e,
                kl=k * ekl)


def _hg_factors(q, sl):
    top, qd, kd = q["top"], q["qd"][:, sl], q["kd"][:, sl]
    qcat = jnp.concatenate([q["qx"][:, sl], jnp.where(top, qd, 0.0), jnp.where(top, 0.0, qd)], axis=1)
    kcat = jnp.concatenate([q["kx"][:, sl], jnp.where(top, kd, 0.0), jnp.where(top, 0.0, kd)], axis=1)
    return qcat, kcat


def _hgrn_fwd_body(tm):
    cpb = tm // HG_T

    def body(u_ref, lb_ref, gg_ref, y_ref, st_ref, s_ref):
        i = pl.program_id(0)

        @pl.when(i == 0)
        def _():
            s_ref[...] = jnp.zeros_like(s_ref)

        lbv = lb_ref[...]
        ggv = gg_ref[...]
        tri = (_iota((HG_T, HG_T), 0) >= _iota((HG_T, HG_T), 1)).astype(F32)

        def chunk(c, carry):
            r0 = pl.multiple_of(c * HG_T, HG_T)
            blk = u_ref[pl.ds(r0, HG_T), :]
            valid = (i * tm + r0 + _iota((HG_T, 1), 0)) >= META_PAD
            q = _hg_chunk_fwd(blk, lbv, valid, tri)
            outs = []
            for hh in range(HG_HEADS):
                sl = slice(hh * HG_D, (hh + 1) * HG_D)
                qcat, kcat = _hg_factors(q, sl)
                a = jnp.where(tri > 0, _dot_nt(qcat, kcat), 0.0)
                st = s_ref[hh]
                st_ref[c, hh] = st
                o = _dot(a, q["v"][:, sl]) + _dot_nt(q["qe"][:, sl], st)
                s_ref[hh] = st * q["el"][:, sl] + _dot_tn(q["v"][:, sl], q["kl"][:, sl])
                rs = lax.rsqrt(jnp.mean(o * o, axis=-1, keepdims=True) + EPS)
                outs.append(o * rs * ggv)
            on = jnp.concatenate(outs, axis=1)
            bg = blk[:, 1536:2048]
            y_ref[pl.ds(r0, HG_T), :] = (on * bg * _sig(bg)).astype(MXU_DTYPE)
            return carry

        lax.fori_loop(0, cpb, chunk, 0, unroll=True)

    return body


def _conv_hgrn_fwd(u, cw, cb, lg, lb_, hlb, gg):
    lp = u.shape[0]
    tm = TM_BR
    cpb = tm // HG_T
    conv_body, hgrn_body = _conv_fwd_body(tm), _hgrn_fwd_body(tm)

    def body(a_ref, ah_ref, ag_ref, w_ref, b_ref, lg_ref, lb_ref, ub_ref, hlb_ref, gg_ref,
             ya_ref, y_ref, yb_ref, st_ref, ext_ref, sh_ref, s_ref):
        hgrn_body(ub_ref, hlb_ref, gg_ref, yb_ref, st_ref, s_ref)
        conv_body(a_ref, ah_ref, ag_ref, w_ref, b_ref, lg_ref, lb_ref, ya_ref, y_ref, ext_ref, sh_ref)

    rowspec = pl.BlockSpec((tm, D_CONV), lambda i: (i, 0))
    return pl.pallas_call(
        body, name="conv_hgrn_fwd", grid=(lp // tm,),
        in_specs=_conv_specs(tm) + [_full_spec((CONV_WIDTH, D_CONV))] + [_full_spec((1, D_CONV))] * 3
        + [pl.BlockSpec((tm, W_B), lambda i: (i, 0)), _full_spec((1, D_HG)), _full_spec((1, HG_D))],
        out_specs=[rowspec, rowspec, pl.BlockSpec((tm, D_HG), lambda i: (i, 0)),
                   pl.BlockSpec((cpb, HG_HEADS, HG_D, HG_D), lambda i: (i, 0, 0, 0))],
        out_shape=[jax.ShapeDtypeStruct((lp, D_CONV), MXU_DTYPE), jax.ShapeDtypeStruct((lp, D_CONV), F32),
                   jax.ShapeDtypeStruct((lp, D_HG), MXU_DTYPE),
                   jax.ShapeDtypeStruct((lp // HG_T, HG_HEADS, HG_D, HG_D), F32)],
        scratch_shapes=[pltpu.VMEM((HALO + tm, D_CONV), F32), _shift_scratch(tm),
                        pltpu.VMEM((HG_HEADS, HG_D, HG_D), F32)],
        compiler_params=_cp(("arbitrary",)),
    )(u, u, u, cw, cb, lg, lb_, u, hlb, gg)


def _hgrn_bwd(u, dyb, states, du, lb, gg):
    lp = u.shape[0]
    tm = TM_HG
    cpb = tm // HG_T
    nb = lp // tm

    def body(u_ref, dy_ref, st_ref, lb_ref, gg_ref, du_in, du_ref, dlb_ref, dgg_ref, ds_ref):
        del du_in
        ii = pl.program_id(0)
        i = nb - 1 - ii

        @pl.when(ii == 0)
        def _():
            ds_ref[...] = jnp.zeros_like(ds_ref)
            dlb_ref[...] = jnp.zeros_like(dlb_ref)
            dgg_ref[...] = jnp.zeros_like(dgg_ref)

        lbv = lb_ref[...]
        ggv = gg_ref[...]
        lower = _iota((HG_T, HG_T), 0) >= _iota((HG_T, HG_T), 1)
        tri = lower.astype(F32)
        triu = (_iota((HG_T, HG_T), 0) <= _iota((HG_T, HG_T), 1)).astype(F32)
        ridx = _iota((HG_T, 1), 0)

        def chunk(cc, carry):
            c = cpb - 1 - cc
            r0 = pl.multiple_of(c * HG_T, HG_T)
            blk = u_ref[pl.ds(r0, HG_T), :]
            valid = (i * tm + r0 + _iota((HG_T, 1), 0)) >= META_PAD
            q = _hg_chunk_fwd(blk, lbv, valid, tri)
            top = q["top"]
            bg = blk[:, 1536:2048]
            sg = _sig(bg)
            dy = dy_ref[pl.ds(r0, HG_T), :].astype(F32)
            don_all = dy * bg * sg
            dqt_l, dk_l, dv_l, db_l, dbl_l, on_l = [], [], [], [], [], []
            dgg = jnp.zeros((1, HG_D), F32)
            for hh in range(HG_HEADS):
                sl = slice(hh * HG_D, (hh + 1) * HG_D)
                qe, kl, v = q["qe"][:, sl], q["kl"][:, sl], q["v"][:, sl]
                el = q["el"][:, sl]
                qcat, kcat = _hg_factors(q, sl)
                a = jnp.where(lower, _dot_nt(qcat, kcat), 0.0)
                st = st_ref[c, hh]
                o = _dot(a, v) + _dot_nt(qe, st)
                rs = lax.rsqrt(jnp.mean(o * o, axis=-1, keepdims=True) + EPS)
                xh = o * rs
                on_l.append(xh * ggv)
                don = don_all[:, sl]
                dgg = dgg + jnp.sum(don * xh, axis=0, keepdims=True)
                dxh = don * ggv
                do = rs * (dxh - xh * jnp.mean(dxh * xh, axis=-1, keepdims=True))
                dst = ds_ref[hh]
                dv = _dot_tn(a, do) + _dot_nt(kl, dst)
                da = jnp.where(lower, _dot_nt(do, v), 0.0)
                dqe = _dot(do, st)
                dkl = _dot(v, dst)
                d_el = jnp.sum(st * dst, axis=0, keepdims=True)
                ds_ref[hh] = _dot_tn(do, qe) + dst * el
                dqc = _dot(da, kcat)
                dkc = _dot_tn(da, qcat)
                dqx, dqd = dqc[:, :HG_D], jnp.where(top, dqc[:, HG_D:2 * HG_D], dqc[:, 2 * HG_D:])
                dkx, dkd = dkc[:, :HG_D], jnp.where(top, dkc[:, HG_D:2 * HG_D], dkc[:, 2 * HG_D:])
                dqt_l.append(dqx * q["eqx"][:, sl] + dqd * q["eqd"][:, sl] + dqe * q["e"][:, sl])
                dk_l.append(dkx * q["ekx"][:, sl] + dkd * q["ekd"][:, sl] + dkl * q["ekl"][:, sl])
                dv_l.append(dv)
                db_l.append(dqx * q["qx"][:, sl] - dkx * q["kx"][:, sl] + dqd * q["qd"][:, sl] - dkd * q["kd"][:, sl]
                            + dqe * qe - dkl * kl)
                dbl_l.append(jnp.sum(dkl * kl, axis=0, keepdims=True) + d_el * el)
            dqt = jnp.concatenate(dqt_l, axis=1)
            dk = jnp.concatenate(dk_l, axis=1)
            dv = jnp.concatenate(dv_l, axis=1)
            db = jnp.concatenate(db_l, axis=1) + jnp.where(ridx == HG_T - 1, jnp.concatenate(dbl_l, axis=1), 0.0)
            on = jnp.concatenate(on_l, axis=1)
            dg = jnp.dot(triu, db, precision=_HI, preferred_element_type=F32)
            sz, f = q["sz"], q["f"]
            df = jnp.where(valid & (f > F_FLOOR), dg / f, 0.0)
            dkv = jnp.where(valid, dk, 0.0)
            t = (1.0 - sz) * (df - dkv)
            dlb_ref[...] += jnp.sum(t, axis=0, keepdims=True)
            dz = (1.0 - lbv) * (df - dkv) * sz * (1.0 - sz)
            dbq = dqt * _dsilu(q["bq"], q["sgq"])
            dbg = dy * on * _dsilu(bg, sg)
            dgg_ref[...] += dgg
            du_ref[pl.ds(r0, HG_T), :] = jnp.concatenate([dbq, dz, dv, dbg], axis=1).astype(MXU_DTYPE)
            return carry

        lax.fori_loop(0, cpb, chunk, 0, unroll=True)

    return pl.pallas_call(
        body, name="hgrn_bwd", grid=(nb,),
        in_specs=[pl.BlockSpec((tm, W_B), lambda ii: (nb - 1 - ii, 0)), pl.BlockSpec((tm, D_HG), lambda ii: (nb - 1 - ii, 0)),
                  pl.BlockSpec((cpb, HG_HEADS, HG_D, HG_D), lambda ii: (nb - 1 - ii, 0, 0, 0)),
                  _full_spec((1, D_HG)), _full_spec((1, HG_D)), pl.BlockSpec(memory_space=pl.ANY)],
        out_specs=[pl.BlockSpec((tm, W_B), lambda ii: (nb - 1 - ii, 0)), _full_spec((1, D_HG)), _full_spec((1, HG_D))],
        out_shape=[jax.ShapeDtypeStruct(du.shape, du.dtype), jax.ShapeDtypeStruct((1, D_HG), F32),
                   jax.ShapeDtypeStruct((1, HG_D), F32)],
        scratch_shapes=[pltpu.VMEM((HG_HEADS, HG_D, HG_D), F32)],
        input_output_aliases={5: 0},
        compiler_params=_cp(("arbitrary",)),
    )(u, dyb, states, lb, gg, du)


N_KEYS = 2 * TM_BR
PREV_ROWS = N_KEYS - CHUNK - TM_BR
LOG2E = 1.4426950408889634
ATT_SCALE2 = ATT_HEAD_DIM ** -0.5 * LOG2E
NEG = -1e30


def _half_sum(x, lo):
    a = jnp.sum(jnp.where(lo, x, 0.0), axis=1, keepdims=True)
    b = jnp.sum(jnp.where(lo, 0.0, x), axis=1, keepdims=True)
    return jnp.where(lo, a, b)


def _half_rms(x, lo):
    return lax.rsqrt(_half_sum(x * x, lo) * (1.0 / ATT_HEAD_DIM) + EPS)


def _swa_mask(i, tm):
    tq = i * tm + _iota((tm, N_KEYS), 0)
    s = _iota((tm, N_KEYS), 1)
    nq = tq >> 6
    kr = i * tm + s - (N_KEYS - tm)
    kc = kr >> 6
    band = (kr >= META_PAD) & (kc >= nq - WINDOW_CHUNKS) & (kc <= nq)
    meta = (nq > WINDOW_CHUNKS) & (s >= META_PAD)
    return ((s < CHUNK) & meta) | ((s >= CHUNK) & band)


def _swa_keys(own_kv, prev_ref, meta_ref, kg, tm):
    kv = jnp.concatenate([meta_ref[...], prev_ref[tm - PREV_ROWS:tm, :], own_kv], axis=0)
    k_raw, v = kv[:, :D_KV], kv[:, D_KV:]
    lo = _iota((1, D_KV), 1) < ATT_HEAD_DIM
    kr = _half_rms(k_raw, lo)
    kn = k_raw * kr * kg
    return k_raw, kr, kn, v, lo


def _placed(x, lo):
    xr = pltpu.roll(x, ATT_HEAD_DIM, 1)
    z = jnp.zeros_like(x)
    return [[jnp.where(lo, x, z).astype(MXU_DTYPE), jnp.where(lo, z, xr).astype(MXU_DTYPE)],
            [jnp.where(lo, xr, z).astype(MXU_DTYPE), jnp.where(lo, z, x).astype(MXU_DTYPE)]]


def _swa_specs(tm, order):
    kvb = (OFF_C + 1024) // 256
    return [pl.BlockSpec((tm, W_C), lambda i: (order(i), OFF_C // W_C)),
            pl.BlockSpec((tm, 256), lambda i: (jnp.maximum(order(i) - 1, 0), kvb)),
            pl.BlockSpec((CHUNK, 256), lambda i: (0, kvb)),
            _full_spec((1, D_KV)), _full_spec((1, D_KV)), pl.BlockSpec(memory_space=pltpu.SMEM)]


def _swa_fwd(u, qg, kg, sinks):
    lp = u.shape[0]
    tm = TM_BR

    def body(own_ref, prev_ref, meta_ref, qg_ref, kg_ref, sink_ref, y_ref):
        i = pl.program_id(0)
        own = own_ref[...]
        _, _, kn, v, lo = _swa_keys(own[:, 1024:1280], prev_ref, meta_ref, kg_ref[...], tm)
        kuse, vuse = _placed(kn, lo), _placed(v, lo)
        bias = jnp.where(_swa_mask(i, tm), 0.0, NEG)
        for gi in range(ATT_Q_HEADS // 2):
            j = gi // 2
            sl = slice(gi * 128, (gi + 1) * 128)
            qraw = own[:, sl]
            qs = qraw * _half_rms(qraw, lo) * (qg_ref[...] * ATT_SCALE2)
            og = jnp.zeros((tm, 128), F32)
            for e in range(2):
                qm = jnp.where(lo if e == 0 else ~lo, qs, 0.0)
                s = _dot_nt(qm, kuse[j][e]) + bias
                sk = sink_ref[2 * gi + e] * LOG2E
                m = jnp.maximum(jnp.max(s, axis=-1, keepdims=True), sk)
                p = jnp.exp2(s - m)
                inv = 1.0 / (jnp.sum(p, axis=-1, keepdims=True) + jnp.exp2(sk - m))
                og = og + _dot(p, vuse[j][e]) * inv
            gt = own[:, 512 + gi * 128:512 + (gi + 1) * 128]
            y_ref[:, sl] = (og * gt * _sig(gt)).astype(MXU_DTYPE)

    return pl.pallas_call(
        body, name="swa_fwd", grid=(lp // tm,),
        in_specs=_swa_specs(tm, lambda i: i),
        out_specs=pl.BlockSpec((tm, D_ATT), lambda i: (i, 0)),
        out_shape=jax.ShapeDtypeStruct((lp, D_ATT), MXU_DTYPE),
        compiler_params=_cp(("arbitrary",)),
    )(u, u, u, qg, kg, sinks)


def _swa_bwd(u, dyc, du, qg, kg, sinks):
    lp = u.shape[0]
    tm = TM_BR
    nb = lp // tm
    order = lambda ii: nb - 1 - ii

    def body(own_ref, prev_ref, meta_ref, qg_ref, kg_ref, sink_ref, dy_ref, du_in, du_ref, dqg_ref, dkg_ref, dsk_ref,
             carry_ref, macc_ref):
        del du_in
        ii = pl.program_id(0)
        i = nb - 1 - ii

        @pl.when(ii == 0)
        def _():
            carry_ref[...] = jnp.zeros_like(carry_ref)
            macc_ref[...] = jnp.zeros_like(macc_ref)
            dqg_ref[...] = jnp.zeros_like(dqg_ref)
            dkg_ref[...] = jnp.zeros_like(dkg_ref)
            dsk_ref[...] = jnp.zeros_like(dsk_ref)

        own = own_ref[...]
        k_raw, krs, kn, v, lo = _swa_keys(own[:, 1024:1280], prev_ref, meta_ref, kg_ref[...], tm)
        kuse, vuse = _placed(kn, lo), _placed(v, lo)
        bias = jnp.where(_swa_mask(i, tm), 0.0, NEG)
        dkn_t = jnp.zeros((D_KV, N_KEYS), F32)
        dvn_t = jnp.zeros((D_KV, N_KEYS), F32)
        for gi in range(ATT_Q_HEADS // 2):
            j = gi // 2
            sl = slice(gi * 128, (gi + 1) * 128)
            qraw = own[:, sl]
            qr = _half_rms(qraw, lo)
            qxh = qraw * qr
            qs = qxh * (qg_ref[...] * ATT_SCALE2)
            ps, invs, pk, qms = [], [], [], []
            og = jnp.zeros((tm, 128), F32)
            for e in range(2):
                qm = jnp.where(lo if e == 0 else ~lo, qs, 0.0)
                s = _dot_nt(qm, kuse[j][e]) + bias
                sk = sink_ref[2 * gi + e] * LOG2E
                m = jnp.maximum(jnp.max(s, axis=-1, keepdims=True), sk)
                p = jnp.exp2(s - m)
                inv = 1.0 / (jnp.sum(p, axis=-1, keepdims=True) + jnp.exp2(sk - m))
                ps.append(p)
                invs.append(inv)
                pk.append(jnp.exp2(sk - m) * inv)
                qms.append(qm)
                og = og + _dot(p, vuse[j][e]) * inv
            gt = own[:, 512 + gi * 128:512 + (gi + 1) * 128]
            sg = _sig(gt)
            dy = dy_ref[:, sl].astype(F32)
            dgt = dy * og * _dsilu(gt, sg)
            dog = dy * gt * sg
            dqn = jnp.zeros((tm, 128), F32)
            for e in range(2):
                half = lo if e == 0 else ~lo
                dog_m = jnp.where(half, dog, 0.0)
                dl = jnp.sum(dog_m * og, axis=1, keepdims=True)
                dp = _dot_nt(dog_m, vuse[j][e])
                ds = ps[e] * ((dp - dl) * (invs[e] * (1.0 / LOG2E)))
                hsk = 2 * gi + e
                dsk_ref[hsk:hsk + 1, :] += jnp.zeros((1, 128), F32) - jnp.sum(pk[e] * dl, axis=0, keepdims=True)
                dqn = dqn + _dot(ds, kuse[j][e])
                dk_e = _dot_tn(qms[e], ds)
                dv_e = _dot_tn(dog_m * invs[e], ps[e])
                if j != e:
                    dk_e = pltpu.roll(dk_e, ATT_HEAD_DIM, 0)
                    dv_e = pltpu.roll(dv_e, ATT_HEAD_DIM, 0)
                dkn_t = dkn_t + dk_e
                dvn_t = dvn_t + dv_e
            dqn = dqn * ATT_SCALE2
            dqg_ref[...] += jnp.sum(dqn * qxh, axis=0, keepdims=True)
            dqx = dqn * qg_ref[...]
            dq = qr * (dqx - qxh * _half_sum(dqx * qxh, lo) * (1.0 / ATT_HEAD_DIM))
            du_ref[:, sl] = dq.astype(MXU_DTYPE)
            du_ref[:, 512 + gi * 128:512 + (gi + 1) * 128] = dgt.astype(MXU_DTYPE)

        dkn, dvn = dkn_t.T, dvn_t.T
        macc_ref[...] += jnp.concatenate([dkn[0:CHUNK], dvn[0:CHUNK]], axis=1)
        own0 = N_KEYS - tm
        tot = jnp.concatenate([dkn[own0:], dvn[own0:]], axis=1) + carry_ref[...]
        carry_ref[0:tm - PREV_ROWS, :] = jnp.zeros((tm - PREV_ROWS, 2 * D_KV), F32)
        carry_ref[tm - PREV_ROWS:tm, :] = jnp.concatenate([dkn[CHUNK:own0], dvn[CHUNK:own0]], axis=1)
        first = jnp.where((i == 0) & (_iota((tm, 1), 0) < CHUNK), 1.0, 0.0)
        tot = tot + first * jnp.concatenate([macc_ref[...], jnp.zeros((tm - CHUNK, 2 * D_KV), F32)], axis=0)
        dkn_own, dv_own = tot[:, :D_KV], tot[:, D_KV:]
        kx = k_raw[own0:] * krs[own0:]
        dkg_ref[...] += jnp.sum(dkn_own * kx, axis=0, keepdims=True)
        dkx = dkn_own * kg_ref[...]
        dk = krs[own0:] * (dkx - kx * _half_sum(dkx * kx, lo) * (1.0 / ATT_HEAD_DIM))
        du_ref[:, 1024:1152] = dk.astype(MXU_DTYPE)
        du_ref[:, 1152:1280] = dv_own.astype(MXU_DTYPE)
        du_ref[:, 1280:W_C] = jnp.zeros((tm, W_C - 1280), MXU_DTYPE)

    return pl.pallas_call(
        body, name="swa_bwd", grid=(nb,),
        in_specs=_swa_specs(tm, order) + [pl.BlockSpec((tm, D_ATT), lambda ii: (order(ii), 0)),
                                          pl.BlockSpec(memory_space=pl.ANY)],
        out_specs=[pl.BlockSpec((tm, W_C), lambda ii: (order(ii), OFF_C // W_C)), _full_spec((1, 128)),
                   _full_spec((1, 128)), _full_spec((ATT_Q_HEADS, 128))],
        out_shape=[jax.ShapeDtypeStruct(du.shape, du.dtype), jax.ShapeDtypeStruct((1, 128), F32),
                   jax.ShapeDtypeStruct((1, 128), F32), jax.ShapeDtypeStruct((ATT_Q_HEADS, 128), F32)],
        scratch_shapes=[pltpu.VMEM((tm, 2 * D_KV), F32), pltpu.VMEM((CHUNK, 2 * D_KV), F32)],
        input_output_aliases={7: 0},
        compiler_params=_cp(("arbitrary",)),
    )(u, u, u, qg, kg, sinks, dyc, du)


def _load_once(pairs, first):
    @pl.when(first)
    def _():
        for src, dst in pairs:
            pltpu.sync_copy(src, dst)


def _mix_fwd(h, u, ya, yb, yc, wa, wb, wc, wo):
    lp = h.shape[0]
    tm = TM_MIX

    def body(h_ref, g_ref, ya_ref, yb_ref, yc_ref, wa_hbm, wb_hbm, wc_hbm, wo_hbm, out_ref, wa_ref, wb_ref, wc_ref,
             wo_ref):
        _load_once(((wa_hbm, wa_ref), (wb_hbm, wb_ref), (wc_hbm, wc_ref), (wo_hbm, wo_ref)), pl.program_id(0) == 0)
        mixed = jnp.zeros((tm, D_MODEL), F32)
        for n, (y_ref, w_ref) in enumerate(((ya_ref, wa_ref), (yb_ref, wb_ref), (yc_ref, wc_ref))):
            z = jnp.dot(y_ref[...], w_ref[...], preferred_element_type=F32)
            mixed = mixed + _sig(g_ref[:, n * D_MODEL:(n + 1) * D_MODEL]) * z
        out_ref[...] = h_ref[...] + _dot(mixed, wo_ref[...])

    ybs = pl.BlockSpec((tm, 512), lambda i: (i, 0))
    anyspec = pl.BlockSpec(memory_space=pl.ANY)
    return pl.pallas_call(
        body, name="mix_fwd", grid=(lp // tm,),
        in_specs=[pl.BlockSpec((tm, D_MODEL), lambda i: (i, 0)), pl.BlockSpec((tm, W_G), lambda i: (i, OFF_G // W_G)),
                  ybs, ybs, ybs, anyspec, anyspec, anyspec, anyspec],
        out_specs=pl.BlockSpec((tm, D_MODEL), lambda i: (i, 0)),
        out_shape=jax.ShapeDtypeStruct((lp, D_MODEL), F32),
        scratch_shapes=[pltpu.VMEM((512, D_MODEL), MXU_DTYPE)] * 3 + [pltpu.VMEM((D_MODEL, D_MODEL), MXU_DTYPE)],
        compiler_params=_cp(("arbitrary",)),
    )(h, u, ya, yb, yc, wa, wb, wc, wo)


def _mix_bwd(dh, u, ya, yb, yc, wa, wb, wc, wo):
    lp = dh.shape[0]
    tm = TM_BR
    nb = lp // tm

    def body(dh_ref, g_ref, ya_ref, yb_ref, yc_ref, wa_hbm, wb_hbm, wc_hbm, wo_hbm,
             du_ref, dya_ref, dyb_ref, dyc_ref, dwa_hbm, dwb_hbm, dwc_hbm, dwo_hbm,
             wa_ref, wb_ref, wc_ref, wo_ref, dwa_ref, dwb_ref, dwc_ref, dwo_ref):
        i = pl.program_id(0)
        _load_once(((wa_hbm, wa_ref), (wb_hbm, wb_ref), (wc_hbm, wc_ref), (wo_hbm, wo_ref)), i == 0)

        @pl.when(i == 0)
        def _():
            for r in (dwa_ref, dwb_ref, dwc_ref, dwo_ref):
                r[...] = jnp.zeros_like(r)

        dh_b = dh_ref[...].astype(MXU_DTYPE)
        dmixed = _dot_nt(dh_b, wo_ref[...])
        mixed = jnp.zeros((tm, D_MODEL), F32)
        for n, (y_ref, w_ref, dy_ref, dw_ref) in enumerate(((ya_ref, wa_ref, dya_ref, dwa_ref),
                                                            (yb_ref, wb_ref, dyb_ref, dwb_ref),
                                                            (yc_ref, wc_ref, dyc_ref, dwc_ref))):
            y = y_ref[...]
            z = jnp.dot(y, w_ref[...], preferred_element_type=F32)
            gate = _sig(g_ref[:, n * D_MODEL:(n + 1) * D_MODEL])
            mixed = mixed + gate * z
            du_ref[:, n * D_MODEL:(n + 1) * D_MODEL] = (z * dmixed * gate * (1.0 - gate)).astype(MXU_DTYPE)
            dz = (gate * dmixed).astype(MXU_DTYPE)
            dy_ref[...] = _dot_nt(dz, w_ref[...]).astype(MXU_DTYPE)
            dw_ref[...] += _dot_tn(y, dz)
        dwo_ref[...] += _dot_tn(mixed, dh_b)

        @pl.when(i == nb - 1)
        def _():
            for src, dst in ((dwa_ref, dwa_hbm), (dwb_ref, dwb_hbm), (dwc_ref, dwc_hbm), (dwo_ref, dwo_hbm)):
                pltpu.sync_copy(src, dst)

    ybs = pl.BlockSpec((tm, 512), lambda i: (i, 0))
    anyspec = pl.BlockSpec(memory_space=pl.ANY)
    wsh = jax.ShapeDtypeStruct((512, D_MODEL), F32)
    return pl.pallas_call(
        body, name="mix_bwd", grid=(nb,),
        in_specs=[pl.BlockSpec((tm, D_MODEL), lambda i: (i, 0)), pl.BlockSpec((tm, W_G), lambda i: (i, OFF_G // W_G)),
                  ybs, ybs, ybs, anyspec, anyspec, anyspec, anyspec],
        out_specs=[pl.BlockSpec((tm, W_G), lambda i: (i, OFF_G // W_G)), ybs, ybs, ybs, anyspec, anyspec, anyspec, anyspec],
        out_shape=[jax.ShapeDtypeStruct((lp, NP), MXU_DTYPE)] + [jax.ShapeDtypeStruct((lp, 512), MXU_DTYPE)] * 3
        + [wsh, wsh, wsh, jax.ShapeDtypeStruct((D_MODEL, D_MODEL), F32)],
        scratch_shapes=[pltpu.VMEM((512, D_MODEL), MXU_DTYPE)] * 3 + [pltpu.VMEM((D_MODEL, D_MODEL), MXU_DTYPE)]
        + [pltpu.VMEM((512, D_MODEL), F32)] * 3 + [pltpu.VMEM((D_MODEL, D_MODEL), F32)],
        compiler_params=_cp(("arbitrary",)),
    )(dh, u, ya, yb, yc, wa, wb, wc, wo)


def _loss_head(h, target_p, seq):
    lp = h.shape[0]
    tm = TM_BR

    def body(h_ref, t_ref, dh_ref, loss_ref):
        i = pl.program_id(0)

        @pl.when(i == 0)
        def _():
            loss_ref[...] = jnp.zeros_like(loss_ref)

        rows = i * tm + _iota((tm, 1), 0)
        e = jnp.where((rows >= CHUNK) & (rows < CHUNK + seq), h_ref[...] - t_ref[...], 0.0)
        dh_ref[...] = e * (1.0 / D_MODEL)
        part = jnp.sum(jnp.mean(e * e, axis=-1, keepdims=True), axis=0, keepdims=True)
        loss_ref[...] += 0.5 * part

    return pl.pallas_call(
        body, name="loss_head", grid=(lp // tm,),
        in_specs=[pl.BlockSpec((tm, D_MODEL), lambda i: (i, 0))] * 2,
        out_specs=[pl.BlockSpec((tm, D_MODEL), lambda i: (i, 0)), _full_spec((1, 128))],
        out_shape=[jax.ShapeDtypeStruct((lp, D_MODEL), F32), jax.ShapeDtypeStruct((1, 128), F32)],
        compiler_params=_cp(("arbitrary",)),
    )(h, target_p)


def _lb_rows(p_ref):
    depth = p_ref.shape[0]
    rows = [p_ref[l:l + 1, :] for l in range(depth)]
    mx = functools.reduce(jnp.maximum, rows)
    ex = [jnp.exp(r - mx) for r in rows]
    tot = functools.reduce(jnp.add, ex)
    sm = [e / tot for e in ex]
    cs, run = [], jnp.zeros_like(sm[0])
    for l in range(depth):
        run = run + sm[l]
        cs.append(run)
    return sm, [c - sm[0] for c in cs]


def _lb_fwd(p):
    def body(p_ref, o_ref):
        _, xs = _lb_rows(p_ref)
        for l, xl in enumerate(xs):
            o_ref[l:l + 1, :] = jnp.clip(xl, 0.0, 1.0)

    return pl.pallas_call(body, name="lb_fwd", out_shape=jax.ShapeDtypeStruct(p.shape, F32))(p)


def _lb_bwd(p, dlb):
    def body(p_ref, d_ref, o_ref):
        sm, xs = _lb_rows(p_ref)
        depth = len(xs)
        dx = []
        for l in range(depth):
            x = xs[l]
            g0 = jnp.where(x > 0.0, 1.0, jnp.where(x == 0.0, 0.5, 0.0))
            y = jnp.maximum(x, 0.0)
            g1 = jnp.where(y < 1.0, 1.0, jnp.where(y == 1.0, 0.5, 0.0))
            dx.append(d_ref[l:l + 1, :] * g0 * g1)
        dsm = [functools.reduce(jnp.add, dx[jj:]) for jj in range(depth)]
        dsm[0] = dsm[0] - functools.reduce(jnp.add, dx)
        inner = functools.reduce(jnp.add, [a * b for a, b in zip(sm, dsm)])
        for l in range(depth):
            o_ref[l:l + 1, :] = sm[l] * (dsm[l] - inner)

    return pl.pallas_call(body, name="lb_bwd", out_shape=jax.ShapeDtypeStruct(p.shape, F32))(p, dlb)


def _exchange(gather, scatter, name):
    ng, ns = len(gather), len(scatter)
    n = ng + ns

    def body(*refs):
        x_refs, o_refs, sems = refs[:n], refs[n:2 * n], refs[2 * n:]
        exs = []
        if ng:
            exs.append(_Exchange(x_refs[:ng], o_refs[:ng], *sems[:3], scatter=False))
        if ns:
            exs.append(_Exchange(x_refs[ng:], o_refs[ng:], *sems[-3:], scatter=True))
        for ex in exs:
            ex.start()
        for ex in exs:
            ex.finish()

    out_shape = [jax.ShapeDtypeStruct((N_DEV,) + x.shape, x.dtype) for x in gather]
    out_shape += [jax.ShapeDtypeStruct(x.shape, x.dtype) for x in scatter]
    return pl.pallas_call(
        body, name=name, in_specs=[_ANY] * n, out_specs=[_ANY] * n, out_shape=out_shape,
        scratch_shapes=(_exchange_sems(ng) if ng else []) + (_exchange_sems(ns) if ns else []),
        compiler_params=pltpu.CompilerParams(has_side_effects=True),
    )(*gather, *scatter)


def _adamw(gp, w, m, v, name):
    r, cc = w.shape
    tr = 256 if r % 256 == 0 else r

    def body(g_ref, w_ref, m_ref, v_ref, go_ref, d_ref, mo_ref, vo_ref):
        g = g_ref[0].astype(F32)
        for s in range(1, N_DEV):
            g = g + g_ref[s].astype(F32)
        go_ref[...] = g
        mn = ADAM_B1 * m_ref[...] + (1.0 - ADAM_B1) * g
        vn = ADAM_B2 * v_ref[...] + (1.0 - ADAM_B2) * (g * g)
        m_hat = mn / (1.0 - ADAM_B1 ** ADAM_STEP)
        v_hat = vn / (1.0 - ADAM_B2 ** ADAM_STEP)
        d_ref[...] = -ADAM_LR * (m_hat / (jnp.sqrt(v_hat) + ADAM_EPS) + ADAM_WD * w_ref[...])
        mo_ref[...] = mn
        vo_ref[...] = vn

    bs = pl.BlockSpec((tr, cc), lambda i: (i, 0))
    sh = jax.ShapeDtypeStruct((r, cc), F32)
    return pl.pallas_call(
        body, name=name, grid=(r // tr,),
        in_specs=[pl.BlockSpec((N_DEV, tr, cc), lambda i: (0, i, 0)), bs, bs, bs],
        out_specs=[bs, bs, bs, bs], out_shape=[sh, sh, sh, sh],
        compiler_params=_cp(("parallel",)),
    )(gp, w, m, v)


def _pack_cols(w):
    parts, pos = [], 0
    for pstart, ostart, width in _PACK:
        if pstart != pos:
            parts.append(jnp.zeros(w.shape[:-1] + (pstart - pos,), w.dtype))
        parts.append(w[..., ostart:ostart + width])
        pos = pstart + width
    return jnp.concatenate(parts, axis=-1)


def _unpack_cols(wp):
    by_orig = sorted(_PACK, key=lambda t: t[1])
    return jnp.concatenate([wp[..., p:p + wd] for p, _, wd in by_orig], axis=-1)


_LAYER_SHARDED = ("w_in", "conv_w", "w_conv_out", "w_hg_out", "w_att_out", "w_out")
_NARROW = ("w_in", "w_conv_out", "w_hg_out", "w_att_out", "w_out")
_REPLICATED = ("norm_g", "conv_b", "conv_ln_g", "conv_ln_b", "hg_lower_bounds", "hg_norm_g", "q_norm_g", "k_norm_g",
               "attn_sinks")
_WEIGHTS = ("meta_tokens", "norm_g", "w_in", "conv_w", "conv_b", "conv_ln_g", "conv_ln_b", "w_conv_out",
            "hg_lower_bounds", "hg_norm_g", "w_hg_out", "q_norm_g", "k_norm_g", "attn_sinks", "w_att_out", "w_out")
_ROW_SHARDED = ("w_out",)


def _assemble(name, g):
    if name in _ROW_SHARDED:
        return g.reshape((N_DEV * g.shape[1],) + g.shape[2:])
    full = jnp.moveaxis(g, 0, -2)
    return full.reshape(full.shape[:-2] + (N_DEV * full.shape[-1],))


def _split(name, full):
    if name in _ROW_SHARDED:
        return full.reshape((N_DEV, full.shape[0] // N_DEV) + full.shape[1:])
    c = full.shape[-1] // N_DEV
    return jnp.moveaxis(full.reshape(full.shape[:-1] + (N_DEV, c)), -2, 0)


def _layer_weights(gathered):
    full = {k: _assemble(k, g) for k, g in zip(_LAYER_SHARDED, gathered)}
    wp = _pack_cols(full["w_in"])
    return dict(wp=wp, wpt=wp.T, cw=full["conv_w"], wa=full["w_conv_out"], wb=full["w_hg_out"],
                wc=full["w_att_out"], wo=full["w_out"])


def _layer_fwd(h, lw, sp, gather):
    u, hn, gathered = _inproj_fwd(h, sp["norm_g"], lw["wp"], gather)
    ya, y_conv, yb, states = _conv_hgrn_fwd(u, lw["cw"], sp["conv_b"], sp["conv_ln_g"], sp["conv_ln_b"], sp["lb"],
                                            sp["hg_norm_g"])
    yc = _swa_fwd(u, sp["qg"], sp["kg"], sp["sinks"])
    h_next = _mix_fwd(h, u, ya, yb, yc, lw["wa"], lw["wb"], lw["wc"], lw["wo"])
    return h_next, (h, u, hn, ya, yb, yc, states, y_conv), gathered


def _layer_bwd(dh, saved, lw, sp, stacked, layer, depth):
    h_l, u, hn, ya, yb, yc, states, y_conv = saved
    du, dya, dyb, dyc, dwa, dwb, dwc, dwo = _mix_bwd(dh, u, ya, yb, yc, lw["wa"], lw["wb"], lw["wc"], lw["wo"])
    du, dy, dlg, dlb_ = _conv_bwd1(u, y_conv, dya, du, sp["conv_ln_g"], sp["conv_ln_b"])
    du, dcw, dcb = _conv_bwd2(u, dy, du, lw["cw"])
    du, dlbl, dgg = _hgrn_bwd(u, dyb, states, du, sp["lb"], sp["hg_norm_g"])
    du, dqg, dkg, dsk = _swa_bwd(u, dyc, du, sp["qg"], sp["kg"], sp["sinks"])
    dwp = _inproj_bwd_dw(hn, du)
    full = dict(w_in=_unpack_cols(dwp), conv_w=dcw, w_conv_out=dwa, w_hg_out=dwb, w_att_out=dwc, w_out=dwo)
    pieces = [_split(k, full[k]).astype(WIRE_DTYPE) for k in _LAYER_SHARDED]
    dh, dng, stacked = _inproj_bwd_dh(du, lw["wpt"], h_l, sp["norm_g"], dh, pieces, stacked, layer, depth)
    fold = lambda a: a[0, :ATT_HEAD_DIM] + a[0, ATT_HEAD_DIM:]
    small = dict(norm_g=dng[0], conv_b=dcb[0], conv_ln_g=dlg[0], conv_ln_b=dlb_[0], hg_lower_bounds=dlbl[0],
                 hg_norm_g=dgg[0], q_norm_g=fold(dqg), k_norm_g=fold(dkg), attn_sinks=dsk[:, 0])
    return dh, small, stacked


def _as2d(a):
    return a.reshape((-1, a.shape[-1]))


def kernel(x, meta_tokens, norm_g, w_in, conv_w, conv_b, conv_ln_g, conv_ln_b, w_conv_out, hg_lower_bounds, hg_norm_g, w_hg_out, q_norm_g, k_norm_g, attn_sinks, w_att_out, w_out, loss_target, m_meta_tokens, m_norm_g, m_w_in, m_conv_w, m_conv_b, m_conv_ln_g, m_conv_ln_b, m_w_conv_out, m_hg_lower_bounds, m_hg_norm_g, m_w_hg_out, m_q_norm_g, m_k_norm_g, m_attn_sinks, m_w_att_out, m_w_out, v_meta_tokens, v_norm_g, v_w_in, v_conv_w, v_conv_b, v_conv_ln_g, v_conv_ln_b, v_w_conv_out, v_hg_lower_bounds, v_hg_norm_g, v_w_hg_out, v_q_norm_g, v_k_norm_g, v_attn_sinks, v_w_att_out, v_w_out):
    w = dict(meta_tokens=meta_tokens, norm_g=norm_g, w_in=w_in, conv_w=conv_w, conv_b=conv_b, conv_ln_g=conv_ln_g,
             conv_ln_b=conv_ln_b, w_conv_out=w_conv_out, hg_lower_bounds=hg_lower_bounds, hg_norm_g=hg_norm_g,
             w_hg_out=w_hg_out, q_norm_g=q_norm_g, k_norm_g=k_norm_g, attn_sinks=attn_sinks, w_att_out=w_att_out,
             w_out=w_out)
    m = dict(meta_tokens=m_meta_tokens, norm_g=m_norm_g, w_in=m_w_in, conv_w=m_conv_w, conv_b=m_conv_b,
             conv_ln_g=m_conv_ln_g, conv_ln_b=m_conv_ln_b, w_conv_out=m_w_conv_out, hg_lower_bounds=m_hg_lower_bounds,
             hg_norm_g=m_hg_norm_g, w_hg_out=m_w_hg_out, q_norm_g=m_q_norm_g, k_norm_g=m_k_norm_g,
             attn_sinks=m_attn_sinks, w_att_out=m_w_att_out, w_out=m_w_out)
    v = dict(meta_tokens=v_meta_tokens, norm_g=v_norm_g, w_in=v_w_in, conv_w=v_conv_w, conv_b=v_conv_b,
             conv_ln_g=v_conv_ln_g, conv_ln_b=v_conv_ln_b, w_conv_out=v_w_conv_out, hg_lower_bounds=v_hg_lower_bounds,
             hg_norm_g=v_hg_norm_g, w_hg_out=v_w_hg_out, q_norm_g=v_q_norm_g, k_norm_g=v_k_norm_g,
             attn_sinks=v_attn_sinks, w_att_out=v_w_att_out, w_out=v_w_out)

    depth = norm_g.shape[0]
    seq = x.shape[1]
    lp = -(-(seq + CHUNK) // TM_MM) * TM_MM
    tail = lp - seq - CHUNK
    zeros = lambda n: jnp.zeros((n, D_MODEL), F32)

    def shards(l):
        return [w[k][l].astype(MXU_DTYPE) if k in _NARROW else w[k][l] for k in _LAYER_SHARDED]

    first = _exchange(shards(0) + [meta_tokens], [], "gather_first")
    gathered, meta_full = first[:-1], _assemble("meta_tokens", first[-1])
    h = jnp.concatenate([zeros(META_PAD), meta_full, x[0], zeros(tail)], axis=0)
    target_p = jnp.concatenate([zeros(CHUNK), loss_target[0], zeros(tail)], axis=0)

    lb_all = _lb_fwd(hg_lower_bounds)
    tile2 = lambda a: jnp.concatenate([a, a], axis=-1)
    row = lambda a, l: a[l][None, :]

    def small_rows(l):
        sp = {k: row(w[k], l) for k in ("norm_g", "conv_b", "conv_ln_g", "conv_ln_b", "hg_norm_g")}
        sp.update(lb=row(lb_all, l), qg=tile2(row(q_norm_g, l)), kg=tile2(row(k_norm_g, l)), sinks=attn_sinks[l])
        return sp

    layer_w, saved = [], []
    for l in range(depth):
        layer_w.append(_layer_weights(gathered))
        h, sv, gathered = _layer_fwd(h, layer_w[l], small_rows(l), shards(l + 1) if l + 1 < depth else [])
        saved.append(sv)

    dh, loss_row = _loss_head(h, target_p, seq)
    loss = lax.psum(loss_row[0, 0], ("x", "y", "c"))

    stacked, small_grads = None, [None] * depth
    for l in reversed(range(depth)):
        dh, small_grads[l], stacked = _layer_bwd(dh, saved[l], layer_w[l], small_rows(l), stacked, l, depth)
    grad_x = dh[CHUNK:CHUNK + seq]
    grads = {k: jnp.stack([small_grads[l][k] for l in range(depth)]) for k in _REPLICATED}
    grads["hg_lower_bounds"] = _lb_bwd(hg_lower_bounds, grads["hg_lower_bounds"])

    small = jnp.concatenate([grads[k].reshape(-1) for k in _REPLICATED])
    small = jnp.concatenate([small, jnp.zeros((-small.shape[0] % 128,), F32)]).reshape(-1, 128)
    small_all, meta_pieces = _exchange([small], [_split("meta_tokens", dh[META_PAD:CHUNK])], "exchange_small_grads")
    small_all = small_all.reshape(N_DEV, -1)

    out_g, out_d, out_m, out_v = {}, {}, {}, {}
    for k, gp in zip(("meta_tokens",) + _LAYER_SHARDED, [meta_pieces] + stacked):
        shp = w[k].shape
        res = _adamw(gp.reshape((N_DEV,) + _as2d(w[k]).shape), _as2d(w[k]), _as2d(m[k]), _as2d(v[k]), "adamw_" + k)
        out_g[k], out_d[k], out_m[k], out_v[k] = (r.reshape(shp) for r in res)
    off = 0
    for k in _REPLICATED:
        shp = w[k].shape
        n = w[k].size
        gp = small_all[:, off:off + n].reshape((N_DEV,) + shp)
        off += n
        res = _adamw(gp, w[k], m[k], v[k], "adamw_" + k)
        out_g[k], out_d[k], out_m[k], out_v[k] = res

    return (loss, grad_x[None], *[out_g[k] for k in _WEIGHTS], *[out_d[k] for k in _WEIGHTS],
            *[out_m[k] for k in _WEIGHTS], *[out_v[k] for k in _WEIGHTS])
```

```python
import functools

import jax
import jax.numpy as jnp
from jax import lax
from jax.experimental import pallas as pl
from jax.experimental.pallas import tpu as pltpu

F32 = jnp.float32
MXU_DTYPE = jnp.bfloat16
WIRE_DTYPE = jnp.bfloat16

D_MODEL = 1024
CHUNK = 64
N_META = 16
META_PAD = CHUNK - N_META
D_CONV = 512
CONV_WIDTH = 31
HG_HEADS = 4
HG_D = 128
D_HG = HG_HEADS * HG_D
F_FLOOR = 1e-30
ATT_Q_HEADS = 8
ATT_HEAD_DIM = 64
D_ATT = 512
D_KV = 128
WINDOW_CHUNKS = 2
EPS = 1e-6
D_IN = 7936
N_DEV = 8

ADAM_LR = 0.001
ADAM_B1 = 0.9
ADAM_B2 = 0.999
ADAM_EPS = 1e-08
ADAM_WD = 0.01
ADAM_STEP = 10

NP = 8192
OFF_B, W_B = 0, 2048
OFF_A, W_A = 2048, 1024
OFF_G, W_G = 3072, 3072
OFF_C, W_C = 6144, 1536
OFF_AG, W_AG = 7680, 512
_PACK = ((0, 1536, 2048), (2048, 0, 1024), (3072, 4864, 3072), (6144, 3584, 512), (6656, 4352, 512),
         (7168, 4096, 256), (7680, 1024, 512))
_PAD_AT, _PAD_W = 7424, 256

TM_MM = 1280
TM_BR = 256
TM_MIX = 640
HALO = 32
EXP_CLAMP = 80.0
VMEM_LIMIT = 56 * 1024 * 1024

_HI = lax.Precision.HIGHEST


def _cp(sem):
    return pltpu.CompilerParams(dimension_semantics=sem, vmem_limit_bytes=VMEM_LIMIT)


def _sig(x):
    return 1.0 / (1.0 + jnp.exp(-x))


def _dot(a, b):
    return jnp.dot(a.astype(MXU_DTYPE), b.astype(MXU_DTYPE), preferred_element_type=F32)


def _dot_nt(a, b):
    return lax.dot_general(a.astype(MXU_DTYPE), b.astype(MXU_DTYPE), (((1,), (1,)), ((), ())),
                           preferred_element_type=F32)


def _dot_tn(a, b):
    return lax.dot_general(a.astype(MXU_DTYPE), b.astype(MXU_DTYPE), (((0,), (0,)), ((), ())),
                           preferred_element_type=F32)


def _rnd(x):
    return x.astype(MXU_DTYPE).astype(F32)


def _iota(shape, dim):
    return lax.broadcasted_iota(jnp.int32, shape, dim)


def _full_spec(shape):
    nd = len(shape)
    return pl.BlockSpec(shape, lambda *_: (0,) * nd)


def _my_index():
    return 4 * lax.axis_index("x") + 2 * lax.axis_index("y") + lax.axis_index("c")


def _mesh_id(p):
    return (p >> 2, (p >> 1) & 1, p & 1)


class _Exchange:
    def __init__(self, x_refs, o_refs, send_sems, recv_sems, loc_sems, scatter, dst=lambda o, s: o.at[s]):
        me = _my_index()
        self.local, self.sends, self.recvs = [], [], []
        for a, (x, o) in enumerate(zip(x_refs, o_refs)):
            mine = x.at[me] if scatter else x
            self.local.append(pltpu.make_async_copy(mine, dst(o, me), loc_sems.at[a]))
            for k in range(1, N_DEV):
                to, frm = (me + k) % N_DEV, (me + N_DEV - k) % N_DEV
                sems = dict(send_sem=send_sems.at[a, k - 1], recv_sem=recv_sems.at[a, k - 1],
                            device_id_type=pl.DeviceIdType.MESH)
                self.sends.append(pltpu.make_async_remote_copy(
                    src_ref=x.at[to] if scatter else x, dst_ref=dst(o, me), device_id=_mesh_id(to), **sems))
                self.recvs.append(pltpu.make_async_remote_copy(
                    src_ref=mine, dst_ref=dst(o, frm), device_id=_mesh_id(frm), **sems))

    def start(self):
        for cp in self.local + self.sends:
            cp.start()

    def finish(self):
        for cp in self.recvs:
            cp.wait_recv()
        for cp in self.sends:
            cp.wait_send()
        for cp in self.local:
            cp.wait()


def _exchange_sems(n):
    return [pltpu.SemaphoreType.DMA((n, N_DEV - 1)), pltpu.SemaphoreType.DMA((n, N_DEV - 1)),
            pltpu.SemaphoreType.DMA((n,))]


_ANY = pl.BlockSpec(memory_space=pl.ANY)


def _inproj_fwd(h, g, wp, gather=()):
    lp = h.shape[0]
    tm, tn = TM_MM, 1024
    ni, nj = lp // tm, NP // tn
    n = len(gather)

    def body(h_ref, g_ref, w_ref, *rest):
        x_refs, (u_ref, hn_ref), o_refs = rest[:n], rest[n:n + 2], rest[n + 2:2 * n + 2]
        hs_ref, sems = rest[2 * n + 2], rest[2 * n + 3:]
        i, j = pl.program_id(0), pl.program_id(1)
        if n:
            @pl.when((i == 0) & (j == 0))
            def _():
                _Exchange(x_refs, o_refs, *sems, scatter=False).start()

        @pl.when(j == 0)
        def _():
            x = h_ref[...]
            r = lax.rsqrt(jnp.mean(x * x, axis=-1, keepdims=True) + EPS)
            hn = (x * r * g_ref[...]).astype(MXU_DTYPE)
            hs_ref[...] = hn
            hn_ref[...] = hn
        u_ref[...] = jnp.dot(hs_ref[...], w_ref[...], preferred_element_type=F32)
        if n:
            @pl.when((i == ni - 1) & (j == nj - 1))
            def _():
                _Exchange(x_refs, o_refs, *sems, scatter=False).finish()

    res = pl.pallas_call(
        body, name="inproj_fwd_gather" if n else "inproj_fwd", grid=(ni, nj),
        in_specs=[pl.BlockSpec((tm, D_MODEL), lambda i, j: (i, 0)), pl.BlockSpec((1, D_MODEL), lambda i, j: (0, 0)),
                  pl.BlockSpec((D_MODEL, tn), lambda i, j: (0, j))] + [_ANY] * n,
        out_specs=[pl.BlockSpec((tm, tn), lambda i, j: (i, j)), pl.BlockSpec((tm, D_MODEL), lambda i, j: (i, 0))]
        + [_ANY] * n,
        out_shape=[jax.ShapeDtypeStruct((lp, NP), F32), jax.ShapeDtypeStruct((lp, D_MODEL), MXU_DTYPE)]
        + [jax.ShapeDtypeStruct((N_DEV,) + x.shape, x.dtype) for x in gather],
        scratch_shapes=[pltpu.VMEM((tm, D_MODEL), MXU_DTYPE)] + (_exchange_sems(n) if n else []),
        compiler_params=_cp(("arbitrary", "arbitrary")),
    )(h, g, wp, *gather)
    return res[0], res[1], list(res[2:])


def _inproj_bwd_dh(du, wpt, h, g, dh_next, pieces, stacked, layer, depth):
    lp = h.shape[0]
    tm, tk = TM_MM, 1024
    ni, nk = lp // tm, NP // tk
    n = len(pieces)
    n_acc = 0 if stacked is None else n

    def body(du_ref, w_ref, h_ref, g_ref, dhn_ref, *rest):
        x_refs, (dh_ref, dg_ref), o_refs = rest[:n], rest[n + n_acc:n + n_acc + 2], rest[n + n_acc + 2:2 * n + n_acc + 2]
        acc_ref, sems = rest[2 * n + n_acc + 2], rest[2 * n + n_acc + 3:]
        i, k = pl.program_id(0), pl.program_id(1)
        slot = lambda o, s: o.at[s, layer]

        @pl.when((i == 0) & (k == 0))
        def _():
            _Exchange(x_refs, o_refs, *sems, scatter=True, dst=slot).start()
            dg_ref[...] = jnp.zeros_like(dg_ref)

        @pl.when(k == 0)
        def _():
            acc_ref[...] = jnp.zeros_like(acc_ref)

        acc_ref[...] += jnp.dot(du_ref[...], w_ref[...], preferred_element_type=F32)

        @pl.when(k == nk - 1)
        def _():
            dhn = acc_ref[...]
            x = h_ref[...]
            r = lax.rsqrt(jnp.mean(x * x, axis=-1, keepdims=True) + EPS)
            xh = x * r
            dg_ref[...] += jnp.sum(dhn * xh, axis=0, keepdims=True)
            dxh = dhn * g_ref[...]
            dx = r * (dxh - xh * jnp.mean(dxh * xh, axis=-1, keepdims=True))
            dh_ref[...] = dhn_ref[...] + dx

        @pl.when((i == ni - 1) & (k == nk - 1))
        def _():
            _Exchange(x_refs, o_refs, *sems, scatter=True, dst=slot).finish()

    acc_in = [] if stacked is None else list(stacked)
    res = pl.pallas_call(
        body, name="inproj_bwd_dh_scatter", grid=(ni, nk),
        in_specs=[pl.BlockSpec((tm, tk), lambda i, k: (i, k)), pl.BlockSpec((tk, D_MODEL), lambda i, k: (k, 0)),
                  pl.BlockSpec((tm, D_MODEL), lambda i, k: (i, 0)), pl.BlockSpec((1, D_MODEL), lambda i, k: (0, 0)),
                  pl.BlockSpec((tm, D_MODEL), lambda i, k: (i, 0))] + [_ANY] * (n + n_acc),
        out_specs=[pl.BlockSpec((tm, D_MODEL), lambda i, k: (i, 0)), pl.BlockSpec((1, D_MODEL), lambda i, k: (0, 0))]
        + [_ANY] * n,
        out_shape=[jax.ShapeDtypeStruct((lp, D_MODEL), F32), jax.ShapeDtypeStruct((1, D_MODEL), F32)]
        + [jax.ShapeDtypeStruct((N_DEV, depth) + p.shape[1:], p.dtype) for p in pieces],
        scratch_shapes=[pltpu.VMEM((tm, D_MODEL), F32)] + _exchange_sems(n),
        input_output_aliases={5 + n + a: 2 + a for a in range(n_acc)},
        compiler_params=_cp(("arbitrary", "arbitrary")),
    )(du, wpt, h, g, dh_next, *pieces, *acc_in)
    return res[0], res[1], list(res[2:])


def _inproj_bwd_dw(hn, du):
    lp = hn.shape[0]
    tm, tn = TM_MM, 1024

    def body(hn_ref, du_ref, dw_ref):
        @pl.when(pl.program_id(1) == 0)
        def _():
            dw_ref[...] = jnp.zeros_like(dw_ref)
        dw_ref[...] += _dot_tn(hn_ref[...], du_ref[...])

    return pl.pallas_call(
        body, name="inproj_bwd_dw", grid=(NP // tn, lp // tm),
        in_specs=[pl.BlockSpec((tm, D_MODEL), lambda j, m: (m, 0)), pl.BlockSpec((tm, tn), lambda j, m: (m, j))],
        out_specs=pl.BlockSpec((D_MODEL, tn), lambda j, m: (0, j)),
        out_shape=jax.ShapeDtypeStruct((D_MODEL, NP), F32),
        compiler_params=_cp(("parallel", "arbitrary")),
    )(hn, du)


N_SHIFT = 8
CONV_SUB = 32


def _shift_copies(src_ref, sh_ref):
    n = sh_ref.shape[1]
    for b in range(1, N_SHIFT):
        sh_ref[b - 1, :, :] = src_ref[pl.ds(b, n), :]


def _window(src_ref, sh_ref, off, r0, n):
    a, b = divmod(off, N_SHIFT)
    start = pl.multiple_of(r0 + a * N_SHIFT, N_SHIFT)
    if b == 0:
        return src_ref[pl.ds(start, n), :]
    return sh_ref[b - 1, pl.ds(start, n), :]


def _shift_scratch(tm):
    return pltpu.VMEM((N_SHIFT - 1, tm + HALO - N_SHIFT, D_CONV), F32)


def _glu_ext(a_ref, ah_ref, ext_ref, sh_ref, i, tm):
    rows = i * tm + _iota((tm, 1), 0)
    a = a_ref[...]
    p, sq = a[:, :D_CONV], _sig(a[:, D_CONV:])
    valid = rows >= META_PAD
    ah = ah_ref[...]
    ext_ref[0:HALO, :] = jnp.where(i > 0, ah[:, :D_CONV] * _sig(ah[:, D_CONV:]), 0.0)
    ext_ref[HALO:HALO + tm, :] = jnp.where(valid, p * sq, 0.0)
    _shift_copies(ext_ref, sh_ref)
    return p, sq, valid


def _layernorm_stats(y):
    mu = jnp.mean(y, axis=-1, keepdims=True)
    yc = y - mu
    rstd = lax.rsqrt(jnp.mean(yc * yc, axis=-1, keepdims=True) + EPS)
    return yc * rstd, rstd


def _conv_specs(tm):
    hb = tm // HALO
    return [pl.BlockSpec((tm, W_A), lambda i: (i, OFF_A // W_A)),
            pl.BlockSpec((HALO, W_A), lambda i: (jnp.maximum(i * hb - 1, 0), OFF_A // W_A)),
            pl.BlockSpec((tm, W_AG), lambda i: (i, OFF_AG // W_AG))]


def _conv_fwd_body(tm):
    def body(a_ref, ah_ref, ag_ref, w_ref, b_ref, lg_ref, lb_ref, ya_ref, y_ref, ext_ref, sh_ref):
        i = pl.program_id(0)
        _glu_ext(a_ref, ah_ref, ext_ref, sh_ref, i, tm)
        base = HALO - (CONV_WIDTH - 1)

        y = jnp.zeros((tm, D_CONV), F32) + b_ref[...]
        for k in range(CONV_WIDTH):
            y = y + w_ref[k:k + 1, :] * _window(ext_ref, sh_ref, base + k, 0, tm)
        y_ref[...] = y
        xh, _ = _layernorm_stats(y)
        yn = xh * lg_ref[...] + lb_ref[...]
        gt = ag_ref[...]
        ya_ref[...] = (yn * _sig(yn) * gt * _sig(gt)).astype(MXU_DTYPE)

    return body


def _dsilu(x, s):
    return s * (1.0 + x * (1.0 - s))


def _conv_bwd1(u, y, dya, du, lg, lb_):
    lp = u.shape[0]
    tm = TM_MIX
    assert lp % tm == 0

    def body(ag_ref, y_ref, dya_ref, lg_ref, lb_ref, du_in, du_ref, dy_ref, dlg_ref, dlb_ref):
        del du_in
        i = pl.program_id(0)

        @pl.when(i == 0)
        def _():
            dlg_ref[...] = jnp.zeros_like(dlg_ref)
            dlb_ref[...] = jnp.zeros_like(dlb_ref)

        xh, rstd = _layernorm_stats(y_ref[...])
        yn = xh * lg_ref[...] + lb_ref[...]
        s1 = _sig(yn)
        gt = ag_ref[...]
        s2 = _sig(gt)
        do = dya_ref[...].astype(F32)
        du_ref[...] = (do * (yn * s1) * _dsilu(gt, s2)).astype(MXU_DTYPE)
        dyn = do * (gt * s2) * _dsilu(yn, s1)
        dlg_ref[...] += jnp.sum(dyn * xh, axis=0, keepdims=True)
        dlb_ref[...] += jnp.sum(dyn, axis=0, keepdims=True)
        dxh = dyn * lg_ref[...]
        dy_ref[...] = rstd * (dxh - jnp.mean(dxh, axis=-1, keepdims=True)
                              - xh * jnp.mean(dxh * xh, axis=-1, keepdims=True))

    rowspec = pl.BlockSpec((tm, D_CONV), lambda i: (i, 0))
    return pl.pallas_call(
        body, name="conv_bwd1", grid=(lp // tm,),
        in_specs=[_conv_specs(tm)[2], rowspec, rowspec, _full_spec((1, D_CONV)), _full_spec((1, D_CONV)),
                  pl.BlockSpec(memory_space=pl.ANY)],
        out_specs=[pl.BlockSpec((tm, W_AG), lambda i: (i, OFF_AG // W_AG)), rowspec,
                   _full_spec((1, D_CONV)), _full_spec((1, D_CONV))],
        out_shape=[jax.ShapeDtypeStruct(du.shape, du.dtype), jax.ShapeDtypeStruct((lp, D_CONV), F32),
                   jax.ShapeDtypeStruct((1, D_CONV), F32), jax.ShapeDtypeStruct((1, D_CONV), F32)],
        input_output_aliases={5: 0},
        compiler_params=_cp(("arbitrary",)),
    )(u, y, dya, lg, lb_, du)


def _conv_bwd2(u, dy, du, cw):
    lp = u.shape[0]
    tm = TM_BR
    nb = lp // tm
    hb = tm // HALO

    def body(a_ref, ah_ref, dy_ref, dyn_ref, w_ref, du_in, du_ref, dw_ref, db_ref, ext_ref, sh_ref, edy_ref, shd_ref,
             dwp_ref):
        del du_in
        i = pl.program_id(0)

        @pl.when(i == 0)
        def _():
            dwp_ref[...] = jnp.zeros_like(dwp_ref)
            db_ref[...] = jnp.zeros_like(db_ref)

        _glu_ext(a_ref, ah_ref, ext_ref, sh_ref, i, tm)
        dy_all = dy_ref[...]
        edy_ref[0:tm, :] = dy_all
        edy_ref[tm:tm + HALO, :] = jnp.where(i < nb - 1, dyn_ref[...], 0.0)
        _shift_copies(edy_ref, shd_ref)
        db_ref[...] += jnp.sum(dy_all, axis=0, keepdims=True)
        base = HALO - (CONV_WIDTH - 1)

        def fold8(x):
            parts = [x[s:s + N_SHIFT] for s in range(0, CONV_SUB, N_SHIFT)]
            return functools.reduce(jnp.add, parts)

        def sub(r, carry):
            r0 = pl.multiple_of(r * CONV_SUB, CONV_SUB)
            dy = dy_ref[pl.ds(r0, CONV_SUB), :]
            du0 = jnp.zeros((CONV_SUB, D_CONV), F32)
            for k in range(CONV_WIDTH):
                du0 = du0 + w_ref[k:k + 1, :] * _window(edy_ref, shd_ref, CONV_WIDTH - 1 - k, r0, CONV_SUB)
                dwp_ref[k] += fold8(dy * _window(ext_ref, sh_ref, base + k, r0, CONV_SUB))
            a = a_ref[pl.ds(r0, CONV_SUB), :]
            p, sq = a[:, :D_CONV], _sig(a[:, D_CONV:])
            valid = (i * tm + r0 + _iota((CONV_SUB, 1), 0)) >= META_PAD
            du0 = jnp.where(valid, du0, 0.0)
            du_ref[pl.ds(r0, CONV_SUB), :] = jnp.concatenate([du0 * sq, du0 * p * sq * (1.0 - sq)],
                                                             axis=1).astype(MXU_DTYPE)
            return carry

        lax.fori_loop(0, tm // CONV_SUB, sub, 0)

        @pl.when(i == nb - 1)
        def _():
            dw_ref[...] = jnp.sum(dwp_ref[...], axis=1)

    return pl.pallas_call(
        body, name="conv_bwd2", grid=(nb,),
        in_specs=_conv_specs(tm)[:2] + [pl.BlockSpec((tm, D_CONV), lambda i: (i, 0)),
                                        pl.BlockSpec((HALO, D_CONV), lambda i: (jnp.minimum((i + 1) * hb, nb * hb - 1), 0)),
                                        _full_spec((CONV_WIDTH, D_CONV)), pl.BlockSpec(memory_space=pl.ANY)],
        out_specs=[pl.BlockSpec((tm, W_A), lambda i: (i, OFF_A // W_A)), _full_spec((CONV_WIDTH, D_CONV)),
                   _full_spec((1, D_CONV))],
        out_shape=[jax.ShapeDtypeStruct(du.shape, du.dtype), jax.ShapeDtypeStruct((CONV_WIDTH, D_CONV), F32),
                   jax.ShapeDtypeStruct((1, D_CONV), F32)],
        scratch_shapes=[pltpu.VMEM((HALO + tm, D_CONV), F32), _shift_scratch(tm),
                        pltpu.VMEM((tm + HALO, D_CONV), F32), _shift_scratch(tm),
                        pltpu.VMEM((CONV_WIDTH, N_SHIFT, D_CONV), F32)],
        input_output_aliases={5: 0},
        compiler_params=_cp(("arbitrary",)),
    )(u, u, dy, dy, cw, du)


HG_T = 128
TM_HG = 256
HG_HALF = HG_T // 2


def _hg_chunk_fwd(blk, lb, valid, tri):
    bq, bf, v = blk[:, 0:512], blk[:, 512:1024], blk[:, 1024:1536]
    sgq = _sig(bq)
    qt = bq * sgq
    sz = _sig(bf)
    f = lb + (1.0 - lb) * sz
    g = jnp.where(valid, jnp.log(jnp.maximum(f, F_FLOOR)), 0.0)
    k = jnp.where(valid, (1.0 - lb) * (1.0 - sz), 0.0)
    b = jnp.dot(tri, g, precision=_HI, preferred_element_type=F32)
    ridx = _iota((HG_T, 1), 0)
    pick = lambda r: jnp.sum(jnp.where(ridx == r, b, 0.0), axis=0, keepdims=True)
    return dict(bq=bq, sgq=sgq, qt=qt, sz=sz, f=f, k=k, v=v, b=b, top=ridx < HG_HALF, rx=pick(HG_HALF - 1),
                rdt=pick(HG_HALF // 2 - 1), rdb=pick(HG_HALF + HG_HALF // 2 - 1), bl=pick(HG_T - 1))


def _hg_head(p, sl):
    top, b, qt, k = p["top"], p["b"][:, sl], p["qt"][:, sl], p["k"][:, sl]
    rx, bl = p["rx"][:, sl], p["bl"][:, sl]
    rd = jnp.where(top, p["rdt"][:, sl], p["rdb"][:, sl])
    eqx = jnp.where(top, 0.0, jnp.exp(jnp.minimum(b - rx, 0.0)))
    ekx = jnp.where(top, jnp.exp(jnp.minimum(rx - b, 0.0)), 0.0)
    eqd = jnp.exp(jnp.minimum(b - rd, EXP_CLAMP))
    ekd = jnp.exp(jnp.minimum(rd - b, EXP_CLAMP))
    e = jnp.exp(b)
    ekl = jnp.exp(bl - b)
    qx, kx, qd, kd = _rnd(qt * eqx), _rnd(k * ekx), _rnd(qt * eqd), _rnd(k * ekd)
    qcat = jnp.concatenate([qx, jnp.where(top, qd, 0.0), jnp.where(top, 0.0, qd)], axis=1)
    kcat = jnp.concatenate([kx, jnp.where(top, kd, 0.0), jnp.where(top, 0.0, kd)], axis=1)
    return dict(v=p["v"][:, sl], eqx=eqx, ekx=ekx, eqd=eqd, ekd=ekd, e=e, ekl=ekl, el=jnp.exp(bl), qx=qx, kx=kx,
                qd=qd, kd=kd, qe=qt * e, kl=k * ekl, qcat=qcat, kcat=kcat)


def _hgrn_fwd_body(tm):
    cpb = tm // HG_T

    def body(u_ref, lb_ref, gg_ref, y_ref, st_ref, s_ref):
        i = pl.program_id(0)

        @pl.when(i == 0)
        def _():
            s_ref[...] = jnp.zeros_like(s_ref)

        lbv = lb_ref[...]
        ggv = gg_ref[...]
        tri = (_iota((HG_T, HG_T), 0) >= _iota((HG_T, HG_T), 1)).astype(F32)

        def chunk(c, carry):
            r0 = pl.multiple_of(c * HG_T, HG_T)
            blk = u_ref[pl.ds(r0, HG_T), :]
            valid = (i * tm + r0 + _iota((HG_T, 1), 0)) >= META_PAD
            q = _hg_chunk_fwd(blk, lbv, valid, tri)
            outs = []
            for hh in range(HG_HEADS):
                h = _hg_head(q, slice(hh * HG_D, (hh + 1) * HG_D))
                a = jnp.where(tri > 0, _dot_nt(h["qcat"], h["kcat"]), 0.0)
                st = s_ref[hh]
                st_ref[c, hh] = st
                o = _dot(a, h["v"]) + _dot_nt(h["qe"], st)
                s_ref[hh] = st * h["el"] + _dot_tn(h["v"], h["kl"])
                rs = lax.rsqrt(jnp.mean(o * o, axis=-1, keepdims=True) + EPS)
                outs.append(o * rs * ggv)
            on = jnp.concatenate(outs, axis=1)
            bg = blk[:, 1536:2048]
            y_ref[pl.ds(r0, HG_T), :] = (on * bg * _sig(bg)).astype(MXU_DTYPE)
            return carry

        lax.fori_loop(0, cpb, chunk, 0, unroll=True)

    return body


def _conv_hgrn_fwd(u, cw, cb, lg, lb_, hlb, gg):
    lp = u.shape[0]
    tm = TM_BR
    cpb = tm // HG_T
    conv_body, hgrn_body = _conv_fwd_body(tm), _hgrn_fwd_body(tm)

    def body(a_ref, ah_ref, ag_ref, w_ref, b_ref, lg_ref, lb_ref, ub_ref, hlb_ref, gg_ref,
             ya_ref, y_ref, yb_ref, st_ref, ext_ref, sh_ref, s_ref):
        hgrn_body(ub_ref, hlb_ref, gg_ref, yb_ref, st_ref, s_ref)
        conv_body(a_ref, ah_ref, ag_ref, w_ref, b_ref, lg_ref, lb_ref, ya_ref, y_ref, ext_ref, sh_ref)

    rowspec = pl.BlockSpec((tm, D_CONV), lambda i: (i, 0))
    return pl.pallas_call(
        body, name="conv_hgrn_fwd", grid=(lp // tm,),
        in_specs=_conv_specs(tm) + [_full_spec((CONV_WIDTH, D_CONV))] + [_full_spec((1, D_CONV))] * 3
        + [pl.BlockSpec((tm, W_B), lambda i: (i, 0)), _full_spec((1, D_HG)), _full_spec((1, HG_D))],
        out_specs=[rowspec, rowspec, pl.BlockSpec((tm, D_HG), lambda i: (i, 0)),
                   pl.BlockSpec((cpb, HG_HEADS, HG_D, HG_D), lambda i: (i, 0, 0, 0))],
        out_shape=[jax.ShapeDtypeStruct((lp, D_CONV), MXU_DTYPE), jax.ShapeDtypeStruct((lp, D_CONV), F32),
                   jax.ShapeDtypeStruct((lp, D_HG), MXU_DTYPE),
                   jax.ShapeDtypeStruct((lp // HG_T, HG_HEADS, HG_D, HG_D), F32)],
        scratch_shapes=[pltpu.VMEM((HALO + tm, D_CONV), F32), _shift_scratch(tm),
                        pltpu.VMEM((HG_HEADS, HG_D, HG_D), F32)],
        compiler_params=_cp(("arbitrary",)),
    )(u, u, u, cw, cb, lg, lb_, u, hlb, gg)


def _hgrn_bwd(u, dyb, states, du, lb, gg):
    lp = u.shape[0]
    tm = TM_HG
    cpb = tm // HG_T
    nb = lp // tm

    def body(u_ref, dy_ref, st_ref, lb_ref, gg_ref, du_in, du_ref, dlb_ref, dgg_ref, ds_ref):
        del du_in
        ii = pl.program_id(0)
        i = nb - 1 - ii

        @pl.when(ii == 0)
        def _():
            ds_ref[...] = jnp.zeros_like(ds_ref)
            dlb_ref[...] = jnp.zeros_like(dlb_ref)
            dgg_ref[...] = jnp.zeros_like(dgg_ref)

        lbv = lb_ref[...]
        ggv = gg_ref[...]
        lower = _iota((HG_T, HG_T), 0) >= _iota((HG_T, HG_T), 1)
        tri = lower.astype(F32)
        triu = (_iota((HG_T, HG_T), 0) <= _iota((HG_T, HG_T), 1)).astype(F32)
        ridx = _iota((HG_T, 1), 0)

        def chunk(cc, carry):
            c = cpb - 1 - cc
            r0 = pl.multiple_of(c * HG_T, HG_T)
            blk = u_ref[pl.ds(r0, HG_T), :]
            valid = (i * tm + r0 + _iota((HG_T, 1), 0)) >= META_PAD
            q = _hg_chunk_fwd(blk, lbv, valid, tri)
            top = q["top"]
            bg = blk[:, 1536:2048]
            sg = _sig(bg)
            dy = dy_ref[pl.ds(r0, HG_T), :].astype(F32)
            don_all = dy * bg * sg
            dqt_l, dk_l, dv_l, db_l, dbl_l, on_l = [], [], [], [], [], []
            dgg = jnp.zeros((1, HG_D), F32)
            for hh in range(HG_HEADS):
                sl = slice(hh * HG_D, (hh + 1) * HG_D)
                h = _hg_head(q, sl)
                qe, kl, v, el, qcat, kcat = h["qe"], h["kl"], h["v"], h["el"], h["qcat"], h["kcat"]
                a = jnp.where(lower, _dot_nt(qcat, kcat), 0.0)
                st = st_ref[c, hh]
                o = _dot(a, v) + _dot_nt(qe, st)
                rs = lax.rsqrt(jnp.mean(o * o, axis=-1, keepdims=True) + EPS)
                xh = o * rs
                on_l.append(xh * ggv)
                don = don_all[:, sl]
                dgg = dgg + jnp.sum(don * xh, axis=0, keepdims=True)
                dxh = don * ggv
                do = rs * (dxh - xh * jnp.mean(dxh * xh, axis=-1, keepdims=True))
                dst = ds_ref[hh]
                dv = _dot_tn(a, do) + _dot_nt(kl, dst)
                da = jnp.where(lower, _dot_nt(do, v), 0.0)
                dqe = _dot(do, st)
                dkl = _dot(v, dst)
                d_el = jnp.sum(st * dst, axis=0, keepdims=True)
                ds_ref[hh] = _dot_tn(do, qe) + dst * el
                dqc = _dot(da, kcat)
                dkc = _dot_tn(da, qcat)
                dqx, dqd = dqc[:, :HG_D], jnp.where(top, dqc[:, HG_D:2 * HG_D], dqc[:, 2 * HG_D:])
                dkx, dkd = dkc[:, :HG_D], jnp.where(top, dkc[:, HG_D:2 * HG_D], dkc[:, 2 * HG_D:])
                dqt_l.append(dqx * h["eqx"] + dqd * h["eqd"] + dqe * h["e"])
                dk_l.append(dkx * h["ekx"] + dkd * h["ekd"] + dkl * h["ekl"])
                dv_l.append(dv)
                db_l.append(dqx * h["qx"] - dkx * h["kx"] + dqd * h["qd"] - dkd * h["kd"] + dqe * qe - dkl * kl)
                dbl_l.append(jnp.sum(dkl * kl, axis=0, keepdims=True) + d_el * el)
            dqt = jnp.concatenate(dqt_l, axis=1)
            dk = jnp.concatenate(dk_l, axis=1)
            dv = jnp.concatenate(dv_l, axis=1)
            db = jnp.concatenate(db_l, axis=1) + jnp.where(ridx == HG_T - 1, jnp.concatenate(dbl_l, axis=1), 0.0)
            on = jnp.concatenate(on_l, axis=1)
            dg = jnp.dot(triu, db, precision=_HI, preferred_element_type=F32)
            sz, f = q["sz"], q["f"]
            df = jnp.where(valid & (f > F_FLOOR), dg / f, 0.0)
            dkv = jnp.where(valid, dk, 0.0)
            t = (1.0 - sz) * (df - dkv)
            dlb_ref[...] += jnp.sum(t, axis=0, keepdims=True)
            dz = (1.0 - lbv) * (df - dkv) * sz * (1.0 - sz)
            dbq = dqt * _dsilu(q["bq"], q["sgq"])
            dbg = dy * on * _dsilu(bg, sg)
            dgg_ref[...] += dgg
            du_ref[pl.ds(r0, HG_T), :] = jnp.concatenate([dbq, dz, dv, dbg], axis=1).astype(MXU_DTYPE)
            return carry

        lax.fori_loop(0, cpb, chunk, 0, unroll=True)

    return pl.pallas_call(
        body, name="hgrn_bwd", grid=(nb,),
        in_specs=[pl.BlockSpec((tm, W_B), lambda ii: (nb - 1 - ii, 0)), pl.BlockSpec((tm, D_HG), lambda ii: (nb - 1 - ii, 0)),
                  pl.BlockSpec((cpb, HG_HEADS, HG_D, HG_D), lambda ii: (nb - 1 - ii, 0, 0, 0)),
                  _full_spec((1, D_HG)), _full_spec((1, HG_D)), pl.BlockSpec(memory_space=pl.ANY)],
        out_specs=[pl.BlockSpec((tm, W_B), lambda ii: (nb - 1 - ii, 0)), _full_spec((1, D_HG)), _full_spec((1, HG_D))],
        out_shape=[jax.ShapeDtypeStruct(du.shape, du.dtype), jax.ShapeDtypeStruct((1, D_HG), F32),
                   jax.ShapeDtypeStruct((1, HG_D), F32)],
        scratch_shapes=[pltpu.VMEM((HG_HEADS, HG_D, HG_D), F32)],
        input_output_aliases={5: 0},
        compiler_params=_cp(("arbitrary",)),
    )(u, dyb, states, lb, gg, du)


TM_SWA = 256
PREV_ROWS = WINDOW_CHUNKS * CHUNK
DEAD_ROWS = -(CHUNK + PREV_ROWS + TM_SWA) % 128
N_KEYS = CHUNK + DEAD_ROWS + PREV_ROWS + TM_SWA
LOG2E = 1.4426950408889634
ATT_SCALE2 = ATT_HEAD_DIM ** -0.5 * LOG2E
NEG = -1e30


def _half_sum(x, lo):
    a = jnp.sum(jnp.where(lo, x, 0.0), axis=1, keepdims=True)
    b = jnp.sum(jnp.where(lo, 0.0, x), axis=1, keepdims=True)
    return jnp.where(lo, a, b)


def _half_rms(x, lo):
    return lax.rsqrt(_half_sum(x * x, lo) * (1.0 / ATT_HEAD_DIM) + EPS)


def _swa_mask(i, tm):
    tq = i * tm + _iota((tm, N_KEYS), 0)
    s = _iota((tm, N_KEYS), 1)
    nq = tq >> 6
    kr = i * tm + s - (N_KEYS - tm)
    kc = kr >> 6
    band = (kr >= META_PAD) & (kc >= nq - WINDOW_CHUNKS) & (kc <= nq)
    meta = (nq > WINDOW_CHUNKS) & (s >= META_PAD)
    return ((s < CHUNK) & meta) | ((s >= CHUNK) & band)


def _swa_keys(own_kv, prev_ref, meta_ref, kg, tm):
    kv = jnp.concatenate([meta_ref[...], jnp.zeros((DEAD_ROWS, 2 * D_KV), F32), prev_ref[tm - PREV_ROWS:tm, :],
                          own_kv], axis=0)
    k_raw, v = kv[:, :D_KV], kv[:, D_KV:]
    lo = _iota((1, D_KV), 1) < ATT_HEAD_DIM
    kr = _half_rms(k_raw, lo)
    kn = k_raw * kr * kg
    return k_raw, kr, kn, v, lo


def _placed(x, lo):
    xr = pltpu.roll(x, ATT_HEAD_DIM, 1)
    z = jnp.zeros_like(x)
    return [[jnp.where(lo, x, z).astype(MXU_DTYPE), jnp.where(lo, z, xr).astype(MXU_DTYPE)],
            [jnp.where(lo, xr, z).astype(MXU_DTYPE), jnp.where(lo, z, x).astype(MXU_DTYPE)]]


def _swa_specs(tm, order):
    kvb = (OFF_C + 1024) // 256
    return [pl.BlockSpec((tm, W_C), lambda i: (order(i), OFF_C // W_C)),
            pl.BlockSpec((tm, 256), lambda i: (jnp.maximum(order(i) - 1, 0), kvb)),
            pl.BlockSpec((CHUNK, 256), lambda i: (0, kvb)),
            _full_spec((1, D_KV)), _full_spec((1, D_KV)), pl.BlockSpec(memory_space=pltpu.SMEM)]


def _swa_fwd(u, qg, kg, sinks):
    lp = u.shape[0]
    tm = TM_SWA

    def body(own_ref, prev_ref, meta_ref, qg_ref, kg_ref, sink_ref, y_ref):
        i = pl.program_id(0)
        own = own_ref[...]
        _, _, kn, v, lo = _swa_keys(own[:, 1024:1280], prev_ref, meta_ref, kg_ref[...], tm)
        kuse, vuse = _placed(kn, lo), _placed(v, lo)
        bias = jnp.where(_swa_mask(i, tm), 0.0, NEG)
        for gi in range(ATT_Q_HEADS // 2):
            j = gi // 2
            sl = slice(gi * 128, (gi + 1) * 128)
            qraw = own[:, sl]
            qs = qraw * _half_rms(qraw, lo) * (qg_ref[...] * ATT_SCALE2)
            og = jnp.zeros((tm, 128), F32)
            for e in range(2):
                qm = jnp.where(lo if e == 0 else ~lo, qs, 0.0)
                s = _dot_nt(qm, kuse[j][e]) + bias
                sk = sink_ref[2 * gi + e] * LOG2E
                m = jnp.maximum(jnp.max(s, axis=-1, keepdims=True), sk)
                p = jnp.exp2(s - m)
                inv = 1.0 / (jnp.sum(p, axis=-1, keepdims=True) + jnp.exp2(sk - m))
                og = og + _dot(p, vuse[j][e]) * inv
            gt = own[:, 512 + gi * 128:512 + (gi + 1) * 128]
            y_ref[:, sl] = (og * gt * _sig(gt)).astype(MXU_DTYPE)

    return pl.pallas_call(
        body, name="swa_fwd", grid=(lp // tm,),
        in_specs=_swa_specs(tm, lambda i: i),
        out_specs=pl.BlockSpec((tm, D_ATT), lambda i: (i, 0)),
        out_shape=jax.ShapeDtypeStruct((lp, D_ATT), MXU_DTYPE),
        compiler_params=_cp(("arbitrary",)),
    )(u, u, u, qg, kg, sinks)


def _swa_bwd(u, dyc, du, qg, kg, sinks):
    lp = u.shape[0]
    tm = TM_SWA
    nb = lp // tm
    order = lambda ii: nb - 1 - ii

    def body(own_ref, prev_ref, meta_ref, qg_ref, kg_ref, sink_ref, dy_ref, du_in, du_ref, dqg_ref, dkg_ref, dsk_ref,
             carry_ref, macc_ref):
        del du_in
        ii = pl.program_id(0)
        i = nb - 1 - ii

        @pl.when(ii == 0)
        def _():
            carry_ref[...] = jnp.zeros_like(carry_ref)
            macc_ref[...] = jnp.zeros_like(macc_ref)
            dqg_ref[...] = jnp.zeros_like(dqg_ref)
            dkg_ref[...] = jnp.zeros_like(dkg_ref)
            dsk_ref[...] = jnp.zeros_like(dsk_ref)

        own = own_ref[...]
        k_raw, krs, kn, v, lo = _swa_keys(own[:, 1024:1280], prev_ref, meta_ref, kg_ref[...], tm)
        kuse, vuse = _placed(kn, lo), _placed(v, lo)
        bias = jnp.where(_swa_mask(i, tm), 0.0, NEG)
        dkn_t = jnp.zeros((D_KV, N_KEYS), F32)
        dvn_t = jnp.zeros((D_KV, N_KEYS), F32)
        for gi in range(ATT_Q_HEADS // 2):
            j = gi // 2
            sl = slice(gi * 128, (gi + 1) * 128)
            qraw = own[:, sl]
            qr = _half_rms(qraw, lo)
            qxh = qraw * qr
            qs = qxh * (qg_ref[...] * ATT_SCALE2)
            ps, invs, pk, qms = [], [], [], []
            og = jnp.zeros((tm, 128), F32)
            for e in range(2):
                qm = jnp.where(lo if e == 0 else ~lo, qs, 0.0)
                s = _dot_nt(qm, kuse[j][e]) + bias
                sk = sink_ref[2 * gi + e] * LOG2E
                m = jnp.maximum(jnp.max(s, axis=-1, keepdims=True), sk)
                p = jnp.exp2(s - m)
                inv = 1.0 / (jnp.sum(p, axis=-1, keepdims=True) + jnp.exp2(sk - m))
                ps.append(p)
                invs.append(inv)
                pk.append(jnp.exp2(sk - m) * inv)
                qms.append(qm)
                og = og + _dot(p, vuse[j][e]) * inv
            gt = own[:, 512 + gi * 128:512 + (gi + 1) * 128]
            sg = _sig(gt)
            dy = dy_ref[:, sl].astype(F32)
            dgt = dy * og * _dsilu(gt, sg)
            dog = dy * gt * sg
            dqn = jnp.zeros((tm, 128), F32)
            for e in range(2):
                half = lo if e == 0 else ~lo
                dog_m = jnp.where(half, dog, 0.0)
                dl = jnp.sum(dog_m * og, axis=1, keepdims=True)
                dp = _dot_nt(dog_m, vuse[j][e])
                ds = ps[e] * ((dp - dl) * (invs[e] * (1.0 / LOG2E)))
                hsk = 2 * gi + e
                dsk_ref[hsk:hsk + 1, :] += jnp.zeros((1, 128), F32) - jnp.sum(pk[e] * dl, axis=0, keepdims=True)
                dqn = dqn + _dot(ds, kuse[j][e])
                dk_e = _dot_tn(qms[e], ds)
                dv_e = _dot_tn(dog_m * invs[e], ps[e])
                if j != e:
                    dk_e = pltpu.roll(dk_e, ATT_HEAD_DIM, 0)
                    dv_e = pltpu.roll(dv_e, ATT_HEAD_DIM, 0)
                dkn_t = dkn_t + dk_e
                dvn_t = dvn_t + dv_e
            dqn = dqn * ATT_SCALE2
            dqg_ref[...] += jnp.sum(dqn * qxh, axis=0, keepdims=True)
            dqx = dqn * qg_ref[...]
            dq = qr * (dqx - qxh * _half_sum(dqx * qxh, lo) * (1.0 / ATT_HEAD_DIM))
            du_ref[:, sl] = dq.astype(MXU_DTYPE)
            du_ref[:, 512 + gi * 128:512 + (gi + 1) * 128] = dgt.astype(MXU_DTYPE)

        dkn, dvn = dkn_t.T, dvn_t.T
        macc_ref[...] += jnp.concatenate([dkn[0:CHUNK], dvn[0:CHUNK]], axis=1)
        own0 = N_KEYS - tm
        tot = jnp.concatenate([dkn[own0:], dvn[own0:]], axis=1) + carry_ref[...]
        if tm > PREV_ROWS:
            carry_ref[0:tm - PREV_ROWS, :] = jnp.zeros((tm - PREV_ROWS, 2 * D_KV), F32)
        prev0 = own0 - PREV_ROWS
        carry_ref[tm - PREV_ROWS:tm, :] = jnp.concatenate([dkn[prev0:own0], dvn[prev0:own0]], axis=1)
        first = jnp.where((i == 0) & (_iota((tm, 1), 0) < CHUNK), 1.0, 0.0)
        tot = tot + first * jnp.concatenate([macc_ref[...], jnp.zeros((tm - CHUNK, 2 * D_KV), F32)], axis=0)
        dkn_own, dv_own = tot[:, :D_KV], tot[:, D_KV:]
        kx = k_raw[own0:] * krs[own0:]
        dkg_ref[...] += jnp.sum(dkn_own * kx, axis=0, keepdims=True)
        dkx = dkn_own * kg_ref[...]
        dk = krs[own0:] * (dkx - kx * _half_sum(dkx * kx, lo) * (1.0 / ATT_HEAD_DIM))
        du_ref[:, 1024:1152] = dk.astype(MXU_DTYPE)
        du_ref[:, 1152:1280] = dv_own.astype(MXU_DTYPE)
        du_ref[:, 1280:W_C] = jnp.zeros((tm, W_C - 1280), MXU_DTYPE)

    return pl.pallas_call(
        body, name="swa_bwd", grid=(nb,),
        in_specs=_swa_specs(tm, order) + [pl.BlockSpec((tm, D_ATT), lambda ii: (order(ii), 0)),
                                          pl.BlockSpec(memory_space=pl.ANY)],
        out_specs=[pl.BlockSpec((tm, W_C), lambda ii: (order(ii), OFF_C // W_C)), _full_spec((1, 128)),
                   _full_spec((1, 128)), _full_spec((ATT_Q_HEADS, 128))],
        out_shape=[jax.ShapeDtypeStruct(du.shape, du.dtype), jax.ShapeDtypeStruct((1, 128), F32),
                   jax.ShapeDtypeStruct((1, 128), F32), jax.ShapeDtypeStruct((ATT_Q_HEADS, 128), F32)],
        scratch_shapes=[pltpu.VMEM((tm, 2 * D_KV), F32), pltpu.VMEM((CHUNK, 2 * D_KV), F32)],
        input_output_aliases={7: 0},
        compiler_params=_cp(("arbitrary",)),
    )(u, u, u, qg, kg, sinks, dyc, du)


def _load_once(pairs, first):
    @pl.when(first)
    def _():
        for src, dst in pairs:
            pltpu.sync_copy(src, dst)


def _mix_fwd(h, u, ya, yb, yc, wa, wb, wc, wo):
    lp = h.shape[0]
    tm = TM_MIX
    assert lp % tm == 0

    def body(h_ref, g_ref, ya_ref, yb_ref, yc_ref, wa_hbm, wb_hbm, wc_hbm, wo_hbm, out_ref, wa_ref, wb_ref, wc_ref,
             wo_ref):
        _load_once(((wa_hbm, wa_ref), (wb_hbm, wb_ref), (wc_hbm, wc_ref), (wo_hbm, wo_ref)), pl.program_id(0) == 0)
        mixed = jnp.zeros((tm, D_MODEL), F32)
        for n, (y_ref, w_ref) in enumerate(((ya_ref, wa_ref), (yb_ref, wb_ref), (yc_ref, wc_ref))):
            z = jnp.dot(y_ref[...], w_ref[...], preferred_element_type=F32)
            mixed = mixed + _sig(g_ref[:, n * D_MODEL:(n + 1) * D_MODEL]) * z
        out_ref[...] = h_ref[...] + _dot(mixed, wo_ref[...])

    ybs = pl.BlockSpec((tm, 512), lambda i: (i, 0))
    anyspec = pl.BlockSpec(memory_space=pl.ANY)
    return pl.pallas_call(
        body, name="mix_fwd", grid=(lp // tm,),
        in_specs=[pl.BlockSpec((tm, D_MODEL), lambda i: (i, 0)), pl.BlockSpec((tm, W_G), lambda i: (i, OFF_G // W_G)),
                  ybs, ybs, ybs, anyspec, anyspec, anyspec, anyspec],
        out_specs=pl.BlockSpec((tm, D_MODEL), lambda i: (i, 0)),
        out_shape=jax.ShapeDtypeStruct((lp, D_MODEL), F32),
        scratch_shapes=[pltpu.VMEM((512, D_MODEL), MXU_DTYPE)] * 3 + [pltpu.VMEM((D_MODEL, D_MODEL), MXU_DTYPE)],
        compiler_params=_cp(("arbitrary",)),
    )(h, u, ya, yb, yc, wa, wb, wc, wo)


def _mix_bwd(dh, u, ya, yb, yc, wa, wb, wc, wo):
    lp = dh.shape[0]
    tm = TM_BR
    nb = lp // tm

    def body(dh_ref, g_ref, ya_ref, yb_ref, yc_ref, wa_hbm, wb_hbm, wc_hbm, wo_hbm,
             du_ref, dya_ref, dyb_ref, dyc_ref, dwa_hbm, dwb_hbm, dwc_hbm, dwo_hbm,
             wa_ref, wb_ref, wc_ref, wo_ref, dwa_ref, dwb_ref, dwc_ref, dwo_ref):
        i = pl.program_id(0)
        _load_once(((wa_hbm, wa_ref), (wb_hbm, wb_ref), (wc_hbm, wc_ref), (wo_hbm, wo_ref)), i == 0)

        @pl.when(i == 0)
        def _():
            for r in (dwa_ref, dwb_ref, dwc_ref, dwo_ref):
                r[...] = jnp.zeros_like(r)

        dh_b = dh_ref[...].astype(MXU_DTYPE)
        dmixed = _dot_nt(dh_b, wo_ref[...])
        mixed = jnp.zeros((tm, D_MODEL), F32)
        for n, (y_ref, w_ref, dy_ref, dw_ref) in enumerate(((ya_ref, wa_ref, dya_ref, dwa_ref),
                                                            (yb_ref, wb_ref, dyb_ref, dwb_ref),
                                                            (yc_ref, wc_ref, dyc_ref, dwc_ref))):
            y = y_ref[...]
            z = jnp.dot(y, w_ref[...], preferred_element_type=F32)
            gate = _sig(g_ref[:, n * D_MODEL:(n + 1) * D_MODEL])
            mixed = mixed + gate * z
            du_ref[:, n * D_MODEL:(n + 1) * D_MODEL] = (z * dmixed * gate * (1.0 - gate)).astype(MXU_DTYPE)
            dz = (gate * dmixed).astype(MXU_DTYPE)
            dy_ref[...] = _dot_nt(dz, w_ref[...]).astype(MXU_DTYPE)
            dw_ref[...] += _dot_tn(y, dz)
        dwo_ref[...] += _dot_tn(mixed, dh_b)

        @pl.when(i == nb - 1)
        def _():
            for src, dst in ((dwa_ref, dwa_hbm), (dwb_ref, dwb_hbm), (dwc_ref, dwc_hbm), (dwo_ref, dwo_hbm)):
                pltpu.sync_copy(src, dst)

    ybs = pl.BlockSpec((tm, 512), lambda i: (i, 0))
    anyspec = pl.BlockSpec(memory_space=pl.ANY)
    wsh = jax.ShapeDtypeStruct((512, D_MODEL), F32)
    return pl.pallas_call(
        body, name="mix_bwd", grid=(nb,),
        in_specs=[pl.BlockSpec((tm, D_MODEL), lambda i: (i, 0)), pl.BlockSpec((tm, W_G), lambda i: (i, OFF_G // W_G)),
                  ybs, ybs, ybs, anyspec, anyspec, anyspec, anyspec],
        out_specs=[pl.BlockSpec((tm, W_G), lambda i: (i, OFF_G // W_G)), ybs, ybs, ybs, anyspec, anyspec, anyspec, anyspec],
        out_shape=[jax.ShapeDtypeStruct((lp, NP), MXU_DTYPE)] + [jax.ShapeDtypeStruct((lp, 512), MXU_DTYPE)] * 3
        + [wsh, wsh, wsh, jax.ShapeDtypeStruct((D_MODEL, D_MODEL), F32)],
        scratch_shapes=[pltpu.VMEM((512, D_MODEL), MXU_DTYPE)] * 3 + [pltpu.VMEM((D_MODEL, D_MODEL), MXU_DTYPE)]
        + [pltpu.VMEM((512, D_MODEL), F32)] * 3 + [pltpu.VMEM((D_MODEL, D_MODEL), F32)],
        compiler_params=_cp(("arbitrary",)),
    )(dh, u, ya, yb, yc, wa, wb, wc, wo)


def _loss_head(h, target_p, seq):
    lp = h.shape[0]
    tm = TM_MIX
    assert lp % tm == 0

    def body(h_ref, t_ref, dh_ref, loss_ref):
        i = pl.program_id(0)

        @pl.when(i == 0)
        def _():
            loss_ref[...] = jnp.zeros_like(loss_ref)

        rows = i * tm + _iota((tm, 1), 0)
        e = jnp.where((rows >= CHUNK) & (rows < CHUNK + seq), h_ref[...] - t_ref[...], 0.0)
        dh_ref[...] = e * (1.0 / D_MODEL)
        part = jnp.sum(jnp.mean(e * e, axis=-1, keepdims=True), axis=0, keepdims=True)
        loss_ref[...] += 0.5 * part

    return pl.pallas_call(
        body, name="loss_head", grid=(lp // tm,),
        in_specs=[pl.BlockSpec((tm, D_MODEL), lambda i: (i, 0))] * 2,
        out_specs=[pl.BlockSpec((tm, D_MODEL), lambda i: (i, 0)), _full_spec((1, 128))],
        out_shape=[jax.ShapeDtypeStruct((lp, D_MODEL), F32), jax.ShapeDtypeStruct((1, 128), F32)],
        compiler_params=_cp(("arbitrary",)),
    )(h, target_p)


def _lb_rows(p_ref):
    depth = p_ref.shape[0]
    rows = [p_ref[l:l + 1, :] for l in range(depth)]
    mx = functools.reduce(jnp.maximum, rows)
    ex = [jnp.exp(r - mx) for r in rows]
    tot = functools.reduce(jnp.add, ex)
    sm = [e / tot for e in ex]
    cs, run = [], jnp.zeros_like(sm[0])
    for l in range(depth):
        run = run + sm[l]
        cs.append(run)
    return sm, [c - sm[0] for c in cs]


def _lb_fwd(p):
    def body(p_ref, o_ref):
        _, xs = _lb_rows(p_ref)
        for l, xl in enumerate(xs):
            o_ref[l:l + 1, :] = jnp.clip(xl, 0.0, 1.0)

    return pl.pallas_call(body, name="lb_fwd", out_shape=jax.ShapeDtypeStruct(p.shape, F32))(p)


def _lb_bwd(p, dlb):
    def body(p_ref, d_ref, o_ref):
        sm, xs = _lb_rows(p_ref)
        depth = len(xs)
        dx = []
        for l in range(depth):
            x = xs[l]
            g0 = jnp.where(x > 0.0, 1.0, jnp.where(x == 0.0, 0.5, 0.0))
            y = jnp.maximum(x, 0.0)
            g1 = jnp.where(y < 1.0, 1.0, jnp.where(y == 1.0, 0.5, 0.0))
            dx.append(d_ref[l:l + 1, :] * g0 * g1)
        dsm = [functools.reduce(jnp.add, dx[jj:]) for jj in range(depth)]
        dsm[0] = dsm[0] - functools.reduce(jnp.add, dx)
        inner = functools.reduce(jnp.add, [a * b for a, b in zip(sm, dsm)])
        for l in range(depth):
            o_ref[l:l + 1, :] = sm[l] * (dsm[l] - inner)

    return pl.pallas_call(body, name="lb_bwd", out_shape=jax.ShapeDtypeStruct(p.shape, F32))(p, dlb)


def _exchange(gather, scatter, name):
    ng, ns = len(gather), len(scatter)
    n = ng + ns

    def body(*refs):
        x_refs, o_refs, sems = refs[:n], refs[n:2 * n], refs[2 * n:]
        exs = []
        if ng:
            exs.append(_Exchange(x_refs[:ng], o_refs[:ng], *sems[:3], scatter=False))
        if ns:
            exs.append(_Exchange(x_refs[ng:], o_refs[ng:], *sems[-3:], scatter=True))
        for ex in exs:
            ex.start()
        for ex in exs:
            ex.finish()

    out_shape = [jax.ShapeDtypeStruct((N_DEV,) + x.shape, x.dtype) for x in gather]
    out_shape += [jax.ShapeDtypeStruct(x.shape, x.dtype) for x in scatter]
    return pl.pallas_call(
        body, name=name, in_specs=[_ANY] * n, out_specs=[_ANY] * n, out_shape=out_shape,
        scratch_shapes=(_exchange_sems(ng) if ng else []) + (_exchange_sems(ns) if ns else []),
        compiler_params=pltpu.CompilerParams(has_side_effects=True),
    )(*gather, *scatter)


def _gather_two_level(xs, name):
    n = len(xs)

    def body(*refs):
        x_refs, o_refs = refs[:n], refs[n:2 * n]
        send_sems, recv_sems, loc_sems = refs[2 * n:]
        x, y, c = lax.axis_index("x"), lax.axis_index("y"), lax.axis_index("c")
        chips = [(1 - x, y), (x, 1 - y), (1 - x, 1 - y)]
        idx = lambda px, py, pc: 4 * px + 2 * py + pc

        def copy(a, k, block, to, src=None):
            slot = o_refs[a].at[idx(*block)]
            return pltpu.make_async_remote_copy(src_ref=slot if src is None else src, dst_ref=slot,
                                                send_sem=send_sems.at[a, k], recv_sem=recv_sems.at[a, k],
                                                device_id=to, device_id_type=pl.DeviceIdType.MESH)

        me, sib = (x, y, c), (x, y, 1 - c)
        local = [pltpu.make_async_copy(x_refs[a], o_refs[a].at[idx(*me)], loc_sems.at[a]) for a in range(n)]
        first = [copy(a, 0, me, sib, src=x_refs[a]) for a in range(n)]
        first += [copy(a, 1 + j, me, (*chip, c), src=x_refs[a]) for j, chip in enumerate(chips) for a in range(n)]
        for cp in local + first:
            cp.start()
        passed = []
        for j, chip in enumerate(chips):
            for a in range(n):
                copy(a, 1 + j, (*chip, c), me).wait_recv()
                cp = copy(a, 4 + j, (*chip, c), sib)
                cp.start()
                passed.append(cp)
        for a in range(n):
            copy(a, 0, sib, me).wait_recv()
            for j, chip in enumerate(chips):
                copy(a, 4 + j, (*chip, 1 - c), me).wait_recv()
        for cp in first + passed:
            cp.wait_send()
        for cp in local:
            cp.wait()

    return pl.pallas_call(
        body, name=name, in_specs=[_ANY] * n, out_specs=[_ANY] * n,
        out_shape=[jax.ShapeDtypeStruct((N_DEV,) + x.shape, x.dtype) for x in xs],
        scratch_shapes=_exchange_sems(n), compiler_params=pltpu.CompilerParams(has_side_effects=True),
    )(*xs)


def _adamw(gp, w, m, v, name):
    r, cc = w.shape
    tr = 256 if r % 256 == 0 else r

    def body(g_ref, w_ref, m_ref, v_ref, go_ref, d_ref, mo_ref, vo_ref):
        g = g_ref[0].astype(F32)
        for s in range(1, N_DEV):
            g = g + g_ref[s].astype(F32)
        go_ref[...] = g
        mn = ADAM_B1 * m_ref[...] + (1.0 - ADAM_B1) * g
        vn = ADAM_B2 * v_ref[...] + (1.0 - ADAM_B2) * (g * g)
        m_hat = mn / (1.0 - ADAM_B1 ** ADAM_STEP)
        v_hat = vn / (1.0 - ADAM_B2 ** ADAM_STEP)
        d_ref[...] = -ADAM_LR * (m_hat / (jnp.sqrt(v_hat) + ADAM_EPS) + ADAM_WD * w_ref[...])
        mo_ref[...] = mn
        vo_ref[...] = vn

    bs = pl.BlockSpec((tr, cc), lambda i: (i, 0))
    sh = jax.ShapeDtypeStruct((r, cc), F32)
    return pl.pallas_call(
        body, name=name, grid=(r // tr,),
        in_specs=[pl.BlockSpec((N_DEV, tr, cc), lambda i: (0, i, 0)), bs, bs, bs],
        out_specs=[bs, bs, bs, bs], out_shape=[sh, sh, sh, sh],
        compiler_params=_cp(("parallel",)),
    )(gp, w, m, v)


def _pack_cols(w):
    parts, pos = [], 0
    for pstart, ostart, width in _PACK:
        if pstart != pos:
            parts.append(jnp.zeros(w.shape[:-1] + (pstart - pos,), w.dtype))
        parts.append(w[..., ostart:ostart + width])
        pos = pstart + width
    return jnp.concatenate(parts, axis=-1)


def _unpack_cols(wp):
    by_orig = sorted(_PACK, key=lambda t: t[1])
    return jnp.concatenate([wp[..., p:p + wd] for p, _, wd in by_orig], axis=-1)


_LAYER_SHARDED = ("w_in", "conv_w", "w_conv_out", "w_hg_out", "w_att_out", "w_out")
_NARROW = ("w_in", "w_conv_out", "w_hg_out", "w_att_out", "w_out")
_REPLICATED = ("norm_g", "conv_b", "conv_ln_g", "conv_ln_b", "hg_lower_bounds", "hg_norm_g", "q_norm_g", "k_norm_g",
               "attn_sinks")
_WEIGHTS = ("meta_tokens", "norm_g", "w_in", "conv_w", "conv_b", "conv_ln_g", "conv_ln_b", "w_conv_out",
            "hg_lower_bounds", "hg_norm_g", "w_hg_out", "q_norm_g", "k_norm_g", "attn_sinks", "w_att_out", "w_out")
_ROW_SHARDED = ("w_out",)


def _assemble(name, g):
    if name in _ROW_SHARDED:
        return g.reshape((N_DEV * g.shape[1],) + g.shape[2:])
    full = jnp.moveaxis(g, 0, -2)
    return full.reshape(full.shape[:-2] + (N_DEV * full.shape[-1],))


def _split(name, full):
    if name in _ROW_SHARDED:
        return full.reshape((N_DEV, full.shape[0] // N_DEV) + full.shape[1:])
    c = full.shape[-1] // N_DEV
    return jnp.moveaxis(full.reshape(full.shape[:-1] + (N_DEV, c)), -2, 0)


def _layer_weights(gathered):
    full = {k: _assemble(k, g) for k, g in zip(_LAYER_SHARDED, gathered)}
    wp = _pack_cols(full["w_in"])
    return dict(wp=wp, wpt=wp.T, cw=full["conv_w"], wa=full["w_conv_out"], wb=full["w_hg_out"],
                wc=full["w_att_out"], wo=full["w_out"])


def _layer_fwd(h, lw, sp, gather):
    u, hn, gathered = _inproj_fwd(h, sp["norm_g"], lw["wp"], gather)
    ya, y_conv, yb, states = _conv_hgrn_fwd(u, lw["cw"], sp["conv_b"], sp["conv_ln_g"], sp["conv_ln_b"], sp["lb"],
                                            sp["hg_norm_g"])
    yc = _swa_fwd(u, sp["qg"], sp["kg"], sp["sinks"])
    h_next = _mix_fwd(h, u, ya, yb, yc, lw["wa"], lw["wb"], lw["wc"], lw["wo"])
    return h_next, (h, u, hn, ya, yb, yc, states, y_conv), gathered


def _layer_bwd(dh, saved, lw, sp, stacked, layer, depth):
    h_l, u, hn, ya, yb, yc, states, y_conv = saved
    du, dya, dyb, dyc, dwa, dwb, dwc, dwo = _mix_bwd(dh, u, ya, yb, yc, lw["wa"], lw["wb"], lw["wc"], lw["wo"])
    du, dy, dlg, dlb_ = _conv_bwd1(u, y_conv, dya, du, sp["conv_ln_g"], sp["conv_ln_b"])
    du, dcw, dcb = _conv_bwd2(u, dy, du, lw["cw"])
    du, dlbl, dgg = _hgrn_bwd(u, dyb, states, du, sp["lb"], sp["hg_norm_g"])
    du, dqg, dkg, dsk = _swa_bwd(u, dyc, du, sp["qg"], sp["kg"], sp["sinks"])
    dwp = _inproj_bwd_dw(hn, du)
    full = dict(w_in=_unpack_cols(dwp), conv_w=dcw, w_conv_out=dwa, w_hg_out=dwb, w_att_out=dwc, w_out=dwo)
    pieces = [_split(k, full[k]).astype(WIRE_DTYPE) for k in _LAYER_SHARDED]
    dh, dng, stacked = _inproj_bwd_dh(du, lw["wpt"], h_l, sp["norm_g"], dh, pieces, stacked, layer, depth)
    fold = lambda a: a[0, :ATT_HEAD_DIM] + a[0, ATT_HEAD_DIM:]
    small = dict(norm_g=dng[0], conv_b=dcb[0], conv_ln_g=dlg[0], conv_ln_b=dlb_[0], hg_lower_bounds=dlbl[0],
                 hg_norm_g=dgg[0], q_norm_g=fold(dqg), k_norm_g=fold(dkg), attn_sinks=dsk[:, 0])
    return dh, small, stacked


def _as2d(a):
    return a.reshape((-1, a.shape[-1]))


def kernel(x, meta_tokens, norm_g, w_in, conv_w, conv_b, conv_ln_g, conv_ln_b, w_conv_out, hg_lower_bounds, hg_norm_g, w_hg_out, q_norm_g, k_norm_g, attn_sinks, w_att_out, w_out, loss_target, m_meta_tokens, m_norm_g, m_w_in, m_conv_w, m_conv_b, m_conv_ln_g, m_conv_ln_b, m_w_conv_out, m_hg_lower_bounds, m_hg_norm_g, m_w_hg_out, m_q_norm_g, m_k_norm_g, m_attn_sinks, m_w_att_out, m_w_out, v_meta_tokens, v_norm_g, v_w_in, v_conv_w, v_conv_b, v_conv_ln_g, v_conv_ln_b, v_w_conv_out, v_hg_lower_bounds, v_hg_norm_g, v_w_hg_out, v_q_norm_g, v_k_norm_g, v_attn_sinks, v_w_att_out, v_w_out):
    w = dict(meta_tokens=meta_tokens, norm_g=norm_g, w_in=w_in, conv_w=conv_w, conv_b=conv_b, conv_ln_g=conv_ln_g,
             conv_ln_b=conv_ln_b, w_conv_out=w_conv_out, hg_lower_bounds=hg_lower_bounds, hg_norm_g=hg_norm_g,
             w_hg_out=w_hg_out, q_norm_g=q_norm_g, k_norm_g=k_norm_g, attn_sinks=attn_sinks, w_att_out=w_att_out,
             w_out=w_out)
    m = dict(meta_tokens=m_meta_tokens, norm_g=m_norm_g, w_in=m_w_in, conv_w=m_conv_w, conv_b=m_conv_b,
             conv_ln_g=m_conv_ln_g, conv_ln_b=m_conv_ln_b, w_conv_out=m_w_conv_out, hg_lower_bounds=m_hg_lower_bounds,
             hg_norm_g=m_hg_norm_g, w_hg_out=m_w_hg_out, q_norm_g=m_q_norm_g, k_norm_g=m_k_norm_g,
             attn_sinks=m_attn_sinks, w_att_out=m_w_att_out, w_out=m_w_out)
    v = dict(meta_tokens=v_meta_tokens, norm_g=v_norm_g, w_in=v_w_in, conv_w=v_conv_w, conv_b=v_conv_b,
             conv_ln_g=v_conv_ln_g, conv_ln_b=v_conv_ln_b, w_conv_out=v_w_conv_out, hg_lower_bounds=v_hg_lower_bounds,
             hg_norm_g=v_hg_norm_g, w_hg_out=v_w_hg_out, q_norm_g=v_q_norm_g, k_norm_g=v_k_norm_g,
             attn_sinks=v_attn_sinks, w_att_out=v_w_att_out, w_out=v_w_out)

    depth = norm_g.shape[0]
    seq = x.shape[1]
    lp = -(-(seq + CHUNK) // TM_MM) * TM_MM
    tail = lp - seq - CHUNK
    zeros = lambda n: jnp.zeros((n, D_MODEL), F32)

    def shards(l):
        return [w[k][l].astype(MXU_DTYPE) if k in _NARROW else w[k][l] for k in _LAYER_SHARDED]

    first = _gather_two_level(shards(0) + [meta_tokens], "gather_first")
    gathered, meta_full = first[:-1], _assemble("meta_tokens", first[-1])
    h = jnp.concatenate([zeros(META_PAD), meta_full, x[0], zeros(tail)], axis=0)
    target_p = jnp.concatenate([zeros(CHUNK), loss_target[0], zeros(tail)], axis=0)

    lb_all = _lb_fwd(hg_lower_bounds)
    tile2 = lambda a: jnp.concatenate([a, a], axis=-1)
    row = lambda a, l: a[l][None, :]

    def small_rows(l):
        sp = {k: row(w[k], l) for k in ("norm_g", "conv_b", "conv_ln_g", "conv_ln_b", "hg_norm_g")}
        sp.update(lb=row(lb_all, l), qg=tile2(row(q_norm_g, l)), kg=tile2(row(k_norm_g, l)), sinks=attn_sinks[l])
        return sp

    layer_w, saved = [], []
    for l in range(depth):
        layer_w.append(_layer_weights(gathered))
        h, sv, gathered = _layer_fwd(h, layer_w[l], small_rows(l), shards(l + 1) if l + 1 < depth else [])
        saved.append(sv)

    dh, loss_row = _loss_head(h, target_p, seq)
    loss = lax.psum(loss_row[0, 0], ("x", "y", "c"))

    stacked, small_grads = None, [None] * depth
    for l in reversed(range(depth)):
        dh, small_grads[l], stacked = _layer_bwd(dh, saved[l], layer_w[l], small_rows(l), stacked, l, depth)
    grad_x = dh[CHUNK:CHUNK + seq]
    grads = {k: jnp.stack([small_grads[l][k] for l in range(depth)]) for k in _REPLICATED}
    grads["hg_lower_bounds"] = _lb_bwd(hg_lower_bounds, grads["hg_lower_bounds"])

    small = jnp.concatenate([grads[k].reshape(-1) for k in _REPLICATED])
    small = jnp.concatenate([small, jnp.zeros((-small.shape[0] % 128,), F32)]).reshape(-1, 128)
    small_all, meta_pieces = _exchange([small], [_split("meta_tokens", dh[META_PAD:CHUNK])], "exchange_small_grads")
    small_all = small_all.reshape(N_DEV, -1)

    out_g, out_d, out_m, out_v = {}, {}, {}, {}
    for k, gp in zip(("meta_tokens",) + _LAYER_SHARDED, [meta_pieces] + stacked):
        shp = w[k].shape
        res = _adamw(gp.reshape((N_DEV,) + _as2d(w[k]).shape), _as2d(w[k]), _as2d(m[k]), _as2d(v[k]), "adamw_" + k)
        out_g[k], out_d[k], out_m[k], out_v[k] = (r.reshape(shp) for r in res)
    off = 0
    for k in _REPLICATED:
        shp = w[k].shape
        n = w[k].size
        gp = small_all[:, off:off + n].reshape((N_DEV,) + shp)
        off += n
        res = _adamw(gp, w[k], m[k], v[k], "adamw_" + k)
        out_g[k], out_d[k], out_m[k], out_v[k] = res

    return (loss, grad_x[None], *[out_g[k] for k in _WEIGHTS], *[out_d[k] for k in _WEIGHTS],
            *[out_m[k] for k in _WEIGHTS], *[out_v[k] for k in _WEIGHTS])
```

```python
import functools

import jax
import jax.numpy as jnp
from jax import lax
from jax.experimental import pallas as pl
from jax.experimental.pallas import tpu as pltpu

F32 = jnp.float32
MXU_DTYPE = jnp.bfloat16
WIRE_DTYPE = jnp.bfloat16

D_MODEL = 1024
CHUNK = 64
N_META = 16
META_PAD = CHUNK - N_META
D_CONV = 512
CONV_WIDTH = 31
HG_HEADS = 4
HG_D = 128
D_HG = HG_HEADS * HG_D
F_FLOOR = 1e-30
ATT_Q_HEADS = 8
ATT_HEAD_DIM = 64
D_ATT = 512
D_KV = 128
WINDOW_CHUNKS = 2
EPS = 1e-6
D_IN = 7936
N_DEV = 8

ADAM_LR = 0.001
ADAM_B1 = 0.9
ADAM_B2 = 0.999
ADAM_EPS = 1e-08
ADAM_WD = 0.01
ADAM_STEP = 10

NP = 8192
OFF_B, W_B = 0, 2048
OFF_A, W_A = 2048, 1024
OFF_G, W_G = 3072, 3072
OFF_C, W_C = 6144, 1536
OFF_AG, W_AG = 7680, 512
_PACK = ((0, 1536, 2048), (2048, 0, 1024), (3072, 4864, 3072), (6144, 3584, 512), (6656, 4352, 512),
         (7168, 4096, 256), (7680, 1024, 512))
_PAD_AT, _PAD_W = 7424, 256

TM_MM = 1280
TM_BR = 256
TM_WIDE = 640
TM_MIX = 640
HALO = 32
EXP_CLAMP = 80.0
VMEM_LIMIT = 56 * 1024 * 1024

_HI = lax.Precision.HIGHEST


def _cp(sem):
    return pltpu.CompilerParams(dimension_semantics=sem, vmem_limit_bytes=VMEM_LIMIT)


def _sig(x):
    return 1.0 / (1.0 + jnp.exp(-x))


def _dot(a, b):
    return jnp.dot(a.astype(MXU_DTYPE), b.astype(MXU_DTYPE), preferred_element_type=F32)


def _dot_nt(a, b):
    return lax.dot_general(a.astype(MXU_DTYPE), b.astype(MXU_DTYPE), (((1,), (1,)), ((), ())),
                           preferred_element_type=F32)


def _dot_tn(a, b):
    return lax.dot_general(a.astype(MXU_DTYPE), b.astype(MXU_DTYPE), (((0,), (0,)), ((), ())),
                           preferred_element_type=F32)


def _rnd(x):
    return x.astype(MXU_DTYPE).astype(F32)


def _iota(shape, dim):
    return lax.broadcasted_iota(jnp.int32, shape, dim)


def _full_spec(shape):
    nd = len(shape)
    return pl.BlockSpec(shape, lambda *_: (0,) * nd)


def _my_index():
    return 4 * lax.axis_index("x") + 2 * lax.axis_index("y") + lax.axis_index("c")


def _mesh_id(p):
    return (p >> 2, (p >> 1) & 1, p & 1)


class _Exchange:
    def __init__(self, x_refs, o_refs, send_sems, recv_sems, loc_sems, scatter, dst=lambda o, s: o.at[s]):
        me = _my_index()
        self.local, self.sends, self.recvs = [], [], []
        for a, (x, o) in enumerate(zip(x_refs, o_refs)):
            mine = x.at[me] if scatter else x
            self.local.append(pltpu.make_async_copy(mine, dst(o, me), loc_sems.at[a]))
            for k in range(1, N_DEV):
                to, frm = (me + k) % N_DEV, (me + N_DEV - k) % N_DEV
                sems = dict(send_sem=send_sems.at[a, k - 1], recv_sem=recv_sems.at[a, k - 1],
                            device_id_type=pl.DeviceIdType.MESH)
                self.sends.append(pltpu.make_async_remote_copy(
                    src_ref=x.at[to] if scatter else x, dst_ref=dst(o, me), device_id=_mesh_id(to), **sems))
                self.recvs.append(pltpu.make_async_remote_copy(
                    src_ref=mine, dst_ref=dst(o, frm), device_id=_mesh_id(frm), **sems))

    def start(self):
        for cp in self.local + self.sends:
            cp.start()

    def finish(self):
        for cp in self.recvs:
            cp.wait_recv()
        for cp in self.sends:
            cp.wait_send()
        for cp in self.local:
            cp.wait()


def _exchange_sems(n):
    return [pltpu.SemaphoreType.DMA((n, N_DEV - 1)), pltpu.SemaphoreType.DMA((n, N_DEV - 1)),
            pltpu.SemaphoreType.DMA((n,))]


_ANY = pl.BlockSpec(memory_space=pl.ANY)


def _inproj_fwd(h, g, wp, gather=()):
    lp = h.shape[0]
    tm, tn = TM_MM, 1024
    ni, nj = lp // tm, NP // tn
    n = len(gather)

    def body(h_ref, g_ref, w_ref, *rest):
        x_refs, (u_ref, hn_ref), o_refs = rest[:n], rest[n:n + 2], rest[n + 2:2 * n + 2]
        hs_ref, sems = rest[2 * n + 2], rest[2 * n + 3:]
        i, j = pl.program_id(0), pl.program_id(1)
        if n:
            @pl.when((i == 0) & (j == 0))
            def _():
                _Exchange(x_refs, o_refs, *sems, scatter=False).start()

        @pl.when(j == 0)
        def _():
            x = h_ref[...]
            r = lax.rsqrt(jnp.mean(x * x, axis=-1, keepdims=True) + EPS)
            hn = (x * r * g_ref[...]).astype(MXU_DTYPE)
            hs_ref[...] = hn
            hn_ref[...] = hn
        u_ref[...] = jnp.dot(hs_ref[...], w_ref[...], preferred_element_type=F32)
        if n:
            @pl.when((i == ni - 1) & (j == nj - 1))
            def _():
                _Exchange(x_refs, o_refs, *sems, scatter=False).finish()

    res = pl.pallas_call(
        body, name="inproj_fwd_gather" if n else "inproj_fwd", grid=(ni, nj),
        in_specs=[pl.BlockSpec((tm, D_MODEL), lambda i, j: (i, 0)), pl.BlockSpec((1, D_MODEL), lambda i, j: (0, 0)),
                  pl.BlockSpec((D_MODEL, tn), lambda i, j: (0, j))] + [_ANY] * n,
        out_specs=[pl.BlockSpec((tm, tn), lambda i, j: (i, j)), pl.BlockSpec((tm, D_MODEL), lambda i, j: (i, 0))]
        + [_ANY] * n,
        out_shape=[jax.ShapeDtypeStruct((lp, NP), F32), jax.ShapeDtypeStruct((lp, D_MODEL), MXU_DTYPE)]
        + [jax.ShapeDtypeStruct((N_DEV,) + x.shape, x.dtype) for x in gather],
        scratch_shapes=[pltpu.VMEM((tm, D_MODEL), MXU_DTYPE)] + (_exchange_sems(n) if n else []),
        compiler_params=_cp(("arbitrary", "arbitrary")),
    )(h, g, wp, *gather)
    return res[0], res[1], list(res[2:])


def _inproj_bwd_dh(du, wpt, h, g, dh_next, pieces, stacked, layer, depth):
    lp = h.shape[0]
    tm, tk = TM_MM, 1024
    ni, nk = lp // tm, NP // tk
    n = len(pieces)
    n_acc = 0 if stacked is None else n

    def body(du_ref, w_ref, h_ref, g_ref, dhn_ref, *rest):
        x_refs, (dh_ref, dg_ref), o_refs = rest[:n], rest[n + n_acc:n + n_acc + 2], rest[n + n_acc + 2:2 * n + n_acc + 2]
        acc_ref, sems = rest[2 * n + n_acc + 2], rest[2 * n + n_acc + 3:]
        i, k = pl.program_id(0), pl.program_id(1)
        slot = lambda o, s: o.at[s, layer]

        @pl.when((i == 0) & (k == 0))
        def _():
            _Exchange(x_refs, o_refs, *sems, scatter=True, dst=slot).start()
            dg_ref[...] = jnp.zeros_like(dg_ref)

        @pl.when(k == 0)
        def _():
            acc_ref[...] = jnp.zeros_like(acc_ref)

        acc_ref[...] += jnp.dot(du_ref[...], w_ref[...], preferred_element_type=F32)

        @pl.when(k == nk - 1)
        def _():
            dhn = acc_ref[...]
            x = h_ref[...]
            r = lax.rsqrt(jnp.mean(x * x, axis=-1, keepdims=True) + EPS)
            xh = x * r
            dg_ref[...] += jnp.sum(dhn * xh, axis=0, keepdims=True)
            dxh = dhn * g_ref[...]
            dx = r * (dxh - xh * jnp.mean(dxh * xh, axis=-1, keepdims=True))
            dh_ref[...] = dhn_ref[...] + dx

        @pl.when((i == ni - 1) & (k == nk - 1))
        def _():
            _Exchange(x_refs, o_refs, *sems, scatter=True, dst=slot).finish()

    acc_in = [] if stacked is None else list(stacked)
    res = pl.pallas_call(
        body, name="inproj_bwd_dh_scatter", grid=(ni, nk),
        in_specs=[pl.BlockSpec((tm, tk), lambda i, k: (i, k)), pl.BlockSpec((tk, D_MODEL), lambda i, k: (k, 0)),
                  pl.BlockSpec((tm, D_MODEL), lambda i, k: (i, 0)), pl.BlockSpec((1, D_MODEL), lambda i, k: (0, 0)),
                  pl.BlockSpec((tm, D_MODEL), lambda i, k: (i, 0))] + [_ANY] * (n + n_acc),
        out_specs=[pl.BlockSpec((tm, D_MODEL), lambda i, k: (i, 0)), pl.BlockSpec((1, D_MODEL), lambda i, k: (0, 0))]
        + [_ANY] * n,
        out_shape=[jax.ShapeDtypeStruct((lp, D_MODEL), F32), jax.ShapeDtypeStruct((1, D_MODEL), F32)]
        + [jax.ShapeDtypeStruct((N_DEV, depth) + p.shape[1:], p.dtype) for p in pieces],
        scratch_shapes=[pltpu.VMEM((tm, D_MODEL), F32)] + _exchange_sems(n),
        input_output_aliases={5 + n + a: 2 + a for a in range(n_acc)},
        compiler_params=_cp(("arbitrary", "arbitrary")),
    )(du, wpt, h, g, dh_next, *pieces, *acc_in)
    return res[0], res[1], list(res[2:])


def _inproj_bwd_dw(hn, du):
    lp = hn.shape[0]
    tm, tn = TM_MM, 1024

    def body(hn_ref, du_ref, dw_ref):
        @pl.when(pl.program_id(1) == 0)
        def _():
            dw_ref[...] = jnp.zeros_like(dw_ref)
        dw_ref[...] += _dot_tn(hn_ref[...], du_ref[...])

    return pl.pallas_call(
        body, name="inproj_bwd_dw", grid=(NP // tn, lp // tm),
        in_specs=[pl.BlockSpec((tm, D_MODEL), lambda j, m: (m, 0)), pl.BlockSpec((tm, tn), lambda j, m: (m, j))],
        out_specs=pl.BlockSpec((D_MODEL, tn), lambda j, m: (0, j)),
        out_shape=jax.ShapeDtypeStruct((D_MODEL, NP), F32),
        compiler_params=_cp(("parallel", "arbitrary")),
    )(hn, du)


N_SHIFT = 8
CONV_SUB = 32


def _shift_copies(src_ref, sh_ref):
    n = sh_ref.shape[1]
    for b in range(1, N_SHIFT):
        sh_ref[b - 1, :, :] = src_ref[pl.ds(b, n), :]


def _window(src_ref, sh_ref, off, r0, n):
    a, b = divmod(off, N_SHIFT)
    start = pl.multiple_of(r0 + a * N_SHIFT, N_SHIFT)
    if b == 0:
        return src_ref[pl.ds(start, n), :]
    return sh_ref[b - 1, pl.ds(start, n), :]


def _shift_scratch(tm):
    return pltpu.VMEM((N_SHIFT - 1, tm + HALO - N_SHIFT, D_CONV), F32)


def _glu_ext(a_ref, ah_ref, ext_ref, sh_ref, i, tm):
    rows = i * tm + _iota((tm, 1), 0)
    a = a_ref[...]
    p, sq = a[:, :D_CONV], _sig(a[:, D_CONV:])
    valid = rows >= META_PAD
    ah = ah_ref[...]
    ext_ref[0:HALO, :] = jnp.where(i > 0, ah[:, :D_CONV] * _sig(ah[:, D_CONV:]), 0.0)
    ext_ref[HALO:HALO + tm, :] = jnp.where(valid, p * sq, 0.0)
    _shift_copies(ext_ref, sh_ref)
    return p, sq, valid


def _layernorm_stats(y):
    mu = jnp.mean(y, axis=-1, keepdims=True)
    yc = y - mu
    rstd = lax.rsqrt(jnp.mean(yc * yc, axis=-1, keepdims=True) + EPS)
    return yc * rstd, rstd


def _conv_specs(tm):
    hb = tm // HALO
    return [pl.BlockSpec((tm, W_A), lambda i: (i, OFF_A // W_A)),
            pl.BlockSpec((HALO, W_A), lambda i: (jnp.maximum(i * hb - 1, 0), OFF_A // W_A)),
            pl.BlockSpec((tm, W_AG), lambda i: (i, OFF_AG // W_AG))]


def _conv_fwd_body(tm):
    def body(a_ref, ah_ref, ag_ref, w_ref, b_ref, lg_ref, lb_ref, ya_ref, y_ref, ext_ref, sh_ref):
        i = pl.program_id(0)
        _glu_ext(a_ref, ah_ref, ext_ref, sh_ref, i, tm)
        base = HALO - (CONV_WIDTH - 1)

        y = jnp.zeros((tm, D_CONV), F32) + b_ref[...]
        for k in range(CONV_WIDTH):
            y = y + w_ref[k:k + 1, :] * _window(ext_ref, sh_ref, base + k, 0, tm)
        y_ref[...] = y
        xh, _ = _layernorm_stats(y)
        yn = xh * lg_ref[...] + lb_ref[...]
        gt = ag_ref[...]
        ya_ref[...] = (yn * _sig(yn) * gt * _sig(gt)).astype(MXU_DTYPE)

    return body


def _dsilu(x, s):
    return s * (1.0 + x * (1.0 - s))


def _conv_bwd1(u, y, dya, du, lg, lb_):
    lp = u.shape[0]
    tm = TM_MIX
    assert lp % tm == 0

    def body(ag_ref, y_ref, dya_ref, lg_ref, lb_ref, du_in, du_ref, dy_ref, dlg_ref, dlb_ref):
        del du_in
        i = pl.program_id(0)

        @pl.when(i == 0)
        def _():
            dlg_ref[...] = jnp.zeros_like(dlg_ref)
            dlb_ref[...] = jnp.zeros_like(dlb_ref)

        xh, rstd = _layernorm_stats(y_ref[...])
        yn = xh * lg_ref[...] + lb_ref[...]
        s1 = _sig(yn)
        gt = ag_ref[...]
        s2 = _sig(gt)
        do = dya_ref[...].astype(F32)
        du_ref[...] = (do * (yn * s1) * _dsilu(gt, s2)).astype(MXU_DTYPE)
        dyn = do * (gt * s2) * _dsilu(yn, s1)
        dlg_ref[...] += jnp.sum(dyn * xh, axis=0, keepdims=True)
        dlb_ref[...] += jnp.sum(dyn, axis=0, keepdims=True)
        dxh = dyn * lg_ref[...]
        dy_ref[...] = rstd * (dxh - jnp.mean(dxh, axis=-1, keepdims=True)
                              - xh * jnp.mean(dxh * xh, axis=-1, keepdims=True))

    rowspec = pl.BlockSpec((tm, D_CONV), lambda i: (i, 0))
    return pl.pallas_call(
        body, name="conv_bwd1", grid=(lp // tm,),
        in_specs=[_conv_specs(tm)[2], rowspec, rowspec, _full_spec((1, D_CONV)), _full_spec((1, D_CONV)),
                  pl.BlockSpec(memory_space=pl.ANY)],
        out_specs=[pl.BlockSpec((tm, W_AG), lambda i: (i, OFF_AG // W_AG)), rowspec,
                   _full_spec((1, D_CONV)), _full_spec((1, D_CONV))],
        out_shape=[jax.ShapeDtypeStruct(du.shape, du.dtype), jax.ShapeDtypeStruct((lp, D_CONV), F32),
                   jax.ShapeDtypeStruct((1, D_CONV), F32), jax.ShapeDtypeStruct((1, D_CONV), F32)],
        input_output_aliases={5: 0},
        compiler_params=_cp(("arbitrary",)),
    )(u, y, dya, lg, lb_, du)


def _conv_bwd2(u, dy, du, cw):
    lp = u.shape[0]
    tm = TM_WIDE
    assert lp % tm == 0
    nb = lp // tm
    hb = tm // HALO

    def body(a_ref, ah_ref, dy_ref, dyn_ref, w_ref, du_in, du_ref, dw_ref, db_ref, ext_ref, sh_ref, edy_ref, shd_ref,
             dwp_ref):
        del du_in
        i = pl.program_id(0)

        @pl.when(i == 0)
        def _():
            dwp_ref[...] = jnp.zeros_like(dwp_ref)
            db_ref[...] = jnp.zeros_like(db_ref)

        _glu_ext(a_ref, ah_ref, ext_ref, sh_ref, i, tm)
        dy_all = dy_ref[...]
        edy_ref[0:tm, :] = dy_all
        edy_ref[tm:tm + HALO, :] = jnp.where(i < nb - 1, dyn_ref[...], 0.0)
        _shift_copies(edy_ref, shd_ref)
        db_ref[...] += jnp.sum(dy_all, axis=0, keepdims=True)
        base = HALO - (CONV_WIDTH - 1)

        def fold8(x):
            parts = [x[s:s + N_SHIFT] for s in range(0, CONV_SUB, N_SHIFT)]
            return functools.reduce(jnp.add, parts)

        def sub(r, carry):
            r0 = pl.multiple_of(r * CONV_SUB, CONV_SUB)
            dy = dy_ref[pl.ds(r0, CONV_SUB), :]
            du0 = jnp.zeros((CONV_SUB, D_CONV), F32)
            for k in range(CONV_WIDTH):
                du0 = du0 + w_ref[k:k + 1, :] * _window(edy_ref, shd_ref, CONV_WIDTH - 1 - k, r0, CONV_SUB)
                dwp_ref[k] += fold8(dy * _window(ext_ref, sh_ref, base + k, r0, CONV_SUB))
            a = a_ref[pl.ds(r0, CONV_SUB), :]
            p, sq = a[:, :D_CONV], _sig(a[:, D_CONV:])
            valid = (i * tm + r0 + _iota((CONV_SUB, 1), 0)) >= META_PAD
            du0 = jnp.where(valid, du0, 0.0)
            du_ref[pl.ds(r0, CONV_SUB), :] = jnp.concatenate([du0 * sq, du0 * p * sq * (1.0 - sq)],
                                                             axis=1).astype(MXU_DTYPE)
            return carry

        lax.fori_loop(0, tm // CONV_SUB, sub, 0)

        @pl.when(i == nb - 1)
        def _():
            dw_ref[...] = jnp.sum(dwp_ref[...], axis=1)

    return pl.pallas_call(
        body, name="conv_bwd2", grid=(nb,),
        in_specs=_conv_specs(tm)[:2] + [pl.BlockSpec((tm, D_CONV), lambda i: (i, 0)),
                                        pl.BlockSpec((HALO, D_CONV), lambda i: (jnp.minimum((i + 1) * hb, nb * hb - 1), 0)),
                                        _full_spec((CONV_WIDTH, D_CONV)), pl.BlockSpec(memory_space=pl.ANY)],
        out_specs=[pl.BlockSpec((tm, W_A), lambda i: (i, OFF_A // W_A)), _full_spec((CONV_WIDTH, D_CONV)),
                   _full_spec((1, D_CONV))],
        out_shape=[jax.ShapeDtypeStruct(du.shape, du.dtype), jax.ShapeDtypeStruct((CONV_WIDTH, D_CONV), F32),
                   jax.ShapeDtypeStruct((1, D_CONV), F32)],
        scratch_shapes=[pltpu.VMEM((HALO + tm, D_CONV), F32), _shift_scratch(tm),
                        pltpu.VMEM((tm + HALO, D_CONV), F32), _shift_scratch(tm),
                        pltpu.VMEM((CONV_WIDTH, N_SHIFT, D_CONV), F32)],
        input_output_aliases={5: 0},
        compiler_params=_cp(("arbitrary",)),
    )(u, u, dy, dy, cw, du)


HG_T = 128
TM_HG = TM_WIDE
HG_HALF = HG_T // 2


def _hg_chunk_fwd(blk, lb, valid, tri):
    bq, bf, v = blk[:, 0:512], blk[:, 512:1024], blk[:, 1024:1536]
    sgq = _sig(bq)
    qt = bq * sgq
    sz = _sig(bf)
    f = lb + (1.0 - lb) * sz
    g = jnp.where(valid, jnp.log(jnp.maximum(f, F_FLOOR)), 0.0)
    k = jnp.where(valid, (1.0 - lb) * (1.0 - sz), 0.0)
    b = jnp.dot(tri, g, precision=_HI, preferred_element_type=F32)
    ridx = _iota((HG_T, 1), 0)
    pick = lambda r: jnp.sum(jnp.where(ridx == r, b, 0.0), axis=0, keepdims=True)
    return dict(bq=bq, sgq=sgq, qt=qt, sz=sz, f=f, k=k, v=v, b=b, top=ridx < HG_HALF, rx=pick(HG_HALF - 1),
                rdt=pick(HG_HALF // 2 - 1), rdb=pick(HG_HALF + HG_HALF // 2 - 1), bl=pick(HG_T - 1))


def _hg_head(p, sl):
    top, b, qt, k = p["top"], p["b"][:, sl], p["qt"][:, sl], p["k"][:, sl]
    rx, bl = p["rx"][:, sl], p["bl"][:, sl]
    rd = jnp.where(top, p["rdt"][:, sl], p["rdb"][:, sl])
    eqx = jnp.where(top, 0.0, jnp.exp(jnp.minimum(b - rx, 0.0)))
    ekx = jnp.where(top, jnp.exp(jnp.minimum(rx - b, 0.0)), 0.0)
    eqd = jnp.exp(jnp.minimum(b - rd, EXP_CLAMP))
    ekd = jnp.exp(jnp.minimum(rd - b, EXP_CLAMP))
    e = jnp.exp(b)
    ekl = jnp.exp(bl - b)
    qx, kx, qd, kd = _rnd(qt * eqx), _rnd(k * ekx), _rnd(qt * eqd), _rnd(k * ekd)
    qcat = jnp.concatenate([qx, jnp.where(top, qd, 0.0), jnp.where(top, 0.0, qd)], axis=1)
    kcat = jnp.concatenate([kx, jnp.where(top, kd, 0.0), jnp.where(top, 0.0, kd)], axis=1)
    return dict(v=p["v"][:, sl], eqx=eqx, ekx=ekx, eqd=eqd, ekd=ekd, e=e, ekl=ekl, el=jnp.exp(bl), qx=qx, kx=kx,
                qd=qd, kd=kd, qe=qt * e, kl=k * ekl, qcat=qcat, kcat=kcat)


def _hgrn_fwd_body(tm):
    cpb = tm // HG_T

    def body(u_ref, lb_ref, gg_ref, y_ref, st_ref, s_ref):
        i = pl.program_id(0)

        @pl.when(i == 0)
        def _():
            s_ref[...] = jnp.zeros_like(s_ref)

        lbv = lb_ref[...]
        ggv = gg_ref[...]
        tri = (_iota((HG_T, HG_T), 0) >= _iota((HG_T, HG_T), 1)).astype(F32)

        def chunk(c, carry):
            r0 = pl.multiple_of(c * HG_T, HG_T)
            blk = u_ref[pl.ds(r0, HG_T), :]
            valid = (i * tm + r0 + _iota((HG_T, 1), 0)) >= META_PAD
            q = _hg_chunk_fwd(blk, lbv, valid, tri)
            outs = []
            for hh in range(HG_HEADS):
                h = _hg_head(q, slice(hh * HG_D, (hh + 1) * HG_D))
                a = jnp.where(tri > 0, _dot_nt(h["qcat"], h["kcat"]), 0.0)
                st = s_ref[hh]
                st_ref[c, hh] = st
                o = _dot(a, h["v"]) + _dot_nt(h["qe"], st)
                s_ref[hh] = st * h["el"] + _dot_tn(h["v"], h["kl"])
                rs = lax.rsqrt(jnp.mean(o * o, axis=-1, keepdims=True) + EPS)
                outs.append(o * rs * ggv)
            on = jnp.concatenate(outs, axis=1)
            bg = blk[:, 1536:2048]
            y_ref[pl.ds(r0, HG_T), :] = (on * bg * _sig(bg)).astype(MXU_DTYPE)
            return carry

        lax.fori_loop(0, cpb, chunk, 0, unroll=True)

    return body


def _conv_hgrn_fwd(u, cw, cb, lg, lb_, hlb, gg):
    lp = u.shape[0]
    tm = TM_WIDE
    assert lp % tm == 0
    cpb = tm // HG_T
    conv_body, hgrn_body = _conv_fwd_body(tm), _hgrn_fwd_body(tm)

    def body(a_ref, ah_ref, ag_ref, w_ref, b_ref, lg_ref, lb_ref, ub_ref, hlb_ref, gg_ref,
             ya_ref, y_ref, yb_ref, st_ref, ext_ref, sh_ref, s_ref):
        hgrn_body(ub_ref, hlb_ref, gg_ref, yb_ref, st_ref, s_ref)
        conv_body(a_ref, ah_ref, ag_ref, w_ref, b_ref, lg_ref, lb_ref, ya_ref, y_ref, ext_ref, sh_ref)

    rowspec = pl.BlockSpec((tm, D_CONV), lambda i: (i, 0))
    return pl.pallas_call(
        body, name="conv_hgrn_fwd", grid=(lp // tm,),
        in_specs=_conv_specs(tm) + [_full_spec((CONV_WIDTH, D_CONV))] + [_full_spec((1, D_CONV))] * 3
        + [pl.BlockSpec((tm, W_B), lambda i: (i, 0)), _full_spec((1, D_HG)), _full_spec((1, HG_D))],
        out_specs=[rowspec, rowspec, pl.BlockSpec((tm, D_HG), lambda i: (i, 0)),
                   pl.BlockSpec((cpb, HG_HEADS, HG_D, HG_D), lambda i: (i, 0, 0, 0))],
        out_shape=[jax.ShapeDtypeStruct((lp, D_CONV), MXU_DTYPE), jax.ShapeDtypeStruct((lp, D_CONV), F32),
                   jax.ShapeDtypeStruct((lp, D_HG), MXU_DTYPE),
                   jax.ShapeDtypeStruct((lp // HG_T, HG_HEADS, HG_D, HG_D), F32)],
        scratch_shapes=[pltpu.VMEM((HALO + tm, D_CONV), F32), _shift_scratch(tm),
                        pltpu.VMEM((HG_HEADS, HG_D, HG_D), F32)],
        compiler_params=_cp(("arbitrary",)),
    )(u, u, u, cw, cb, lg, lb_, u, hlb, gg)


def _hgrn_bwd(u, dyb, states, du, lb, gg):
    lp = u.shape[0]
    tm = TM_HG
    cpb = tm // HG_T
    nb = lp // tm

    def body(u_ref, dy_ref, st_ref, lb_ref, gg_ref, du_in, du_ref, dlb_ref, dgg_ref, ds_ref):
        del du_in
        ii = pl.program_id(0)
        i = nb - 1 - ii

        @pl.when(ii == 0)
        def _():
            ds_ref[...] = jnp.zeros_like(ds_ref)
            dlb_ref[...] = jnp.zeros_like(dlb_ref)
            dgg_ref[...] = jnp.zeros_like(dgg_ref)

        lbv = lb_ref[...]
        ggv = gg_ref[...]
        lower = _iota((HG_T, HG_T), 0) >= _iota((HG_T, HG_T), 1)
        tri = lower.astype(F32)
        triu = (_iota((HG_T, HG_T), 0) <= _iota((HG_T, HG_T), 1)).astype(F32)
        ridx = _iota((HG_T, 1), 0)

        def chunk(cc, carry):
            c = cpb - 1 - cc
            r0 = pl.multiple_of(c * HG_T, HG_T)
            blk = u_ref[pl.ds(r0, HG_T), :]
            valid = (i * tm + r0 + _iota((HG_T, 1), 0)) >= META_PAD
            q = _hg_chunk_fwd(blk, lbv, valid, tri)
            top = q["top"]
            bg = blk[:, 1536:2048]
            sg = _sig(bg)
            dy = dy_ref[pl.ds(r0, HG_T), :].astype(F32)
            don_all = dy * bg * sg
            dqt_l, dk_l, dv_l, db_l, dbl_l, on_l = [], [], [], [], [], []
            dgg = jnp.zeros((1, HG_D), F32)
            for hh in range(HG_HEADS):
                sl = slice(hh * HG_D, (hh + 1) * HG_D)
                h = _hg_head(q, sl)
                qe, kl, v, el, qcat, kcat = h["qe"], h["kl"], h["v"], h["el"], h["qcat"], h["kcat"]
                a = jnp.where(lower, _dot_nt(qcat, kcat), 0.0)
                st = st_ref[c, hh]
                o = _dot(a, v) + _dot_nt(qe, st)
                rs = lax.rsqrt(jnp.mean(o * o, axis=-1, keepdims=True) + EPS)
                xh = o * rs
                on_l.append(xh * ggv)
                don = don_all[:, sl]
                dgg = dgg + jnp.sum(don * xh, axis=0, keepdims=True)
                dxh = don * ggv
                do = rs * (dxh - xh * jnp.mean(dxh * xh, axis=-1, keepdims=True))
                dst = ds_ref[hh]
                dv = _dot_tn(a, do) + _dot_nt(kl, dst)
                da = jnp.where(lower, _dot_nt(do, v), 0.0)
                dqe = _dot(do, st)
                dkl = _dot(v, dst)
                d_el = jnp.sum(st * dst, axis=0, keepdims=True)
                ds_ref[hh] = _dot_tn(do, qe) + dst * el
                dqc = _dot(da, kcat)
                dkc = _dot_tn(da, qcat)
                dqx, dqd = dqc[:, :HG_D], jnp.where(top, dqc[:, HG_D:2 * HG_D], dqc[:, 2 * HG_D:])
                dkx, dkd = dkc[:, :HG_D], jnp.where(top, dkc[:, HG_D:2 * HG_D], dkc[:, 2 * HG_D:])
                dqt_l.append(dqx * h["eqx"] + dqd * h["eqd"] + dqe * h["e"])
                dk_l.append(dkx * h["ekx"] + dkd * h["ekd"] + dkl * h["ekl"])
                dv_l.append(dv)
                db_l.append(dqx * h["qx"] - dkx * h["kx"] + dqd * h["qd"] - dkd * h["kd"] + dqe * qe - dkl * kl)
                dbl_l.append(jnp.sum(dkl * kl, axis=0, keepdims=True) + d_el * el)
            dqt = jnp.concatenate(dqt_l, axis=1)
            dk = jnp.concatenate(dk_l, axis=1)
            dv = jnp.concatenate(dv_l, axis=1)
            db = jnp.concatenate(db_l, axis=1) + jnp.where(ridx == HG_T - 1, jnp.concatenate(dbl_l, axis=1), 0.0)
            on = jnp.concatenate(on_l, axis=1)
            dg = jnp.dot(triu, db, precision=_HI, preferred_element_type=F32)
            sz, f = q["sz"], q["f"]
            df = jnp.where(valid & (f > F_FLOOR), dg / f, 0.0)
            dkv = jnp.where(valid, dk, 0.0)
            t = (1.0 - sz) * (df - dkv)
            dlb_ref[...] += jnp.sum(t, axis=0, keepdims=True)
            dz = (1.0 - lbv) * (df - dkv) * sz * (1.0 - sz)
            dbq = dqt * _dsilu(q["bq"], q["sgq"])
            dbg = dy * on * _dsilu(bg, sg)
            dgg_ref[...] += dgg
            du_ref[pl.ds(r0, HG_T), :] = jnp.concatenate([dbq, dz, dv, dbg], axis=1).astype(MXU_DTYPE)
            return carry

        lax.fori_loop(0, cpb, chunk, 0, unroll=True)

    return pl.pallas_call(
        body, name="hgrn_bwd", grid=(nb,),
        in_specs=[pl.BlockSpec((tm, W_B), lambda ii: (nb - 1 - ii, 0)), pl.BlockSpec((tm, D_HG), lambda ii: (nb - 1 - ii, 0)),
                  pl.BlockSpec((cpb, HG_HEADS, HG_D, HG_D), lambda ii: (nb - 1 - ii, 0, 0, 0)),
                  _full_spec((1, D_HG)), _full_spec((1, HG_D)), pl.BlockSpec(memory_space=pl.ANY)],
        out_specs=[pl.BlockSpec((tm, W_B), lambda ii: (nb - 1 - ii, 0)), _full_spec((1, D_HG)), _full_spec((1, HG_D))],
        out_shape=[jax.ShapeDtypeStruct(du.shape, du.dtype), jax.ShapeDtypeStruct((1, D_HG), F32),
                   jax.ShapeDtypeStruct((1, HG_D), F32)],
        scratch_shapes=[pltpu.VMEM((HG_HEADS, HG_D, HG_D), F32)],
        input_output_aliases={5: 0},
        compiler_params=_cp(("arbitrary",)),
    )(u, dyb, states, lb, gg, du)


TM_SWA = 256
PREV_ROWS = WINDOW_CHUNKS * CHUNK
DEAD_ROWS = -(CHUNK + PREV_ROWS + TM_SWA) % 128
N_KEYS = CHUNK + DEAD_ROWS + PREV_ROWS + TM_SWA
LOG2E = 1.4426950408889634
ATT_SCALE2 = ATT_HEAD_DIM ** -0.5 * LOG2E
NEG = -1e30


def _half_sum(x, lo):
    a = jnp.sum(jnp.where(lo, x, 0.0), axis=1, keepdims=True)
    b = jnp.sum(jnp.where(lo, 0.0, x), axis=1, keepdims=True)
    return jnp.where(lo, a, b)


def _half_rms(x, lo):
    return lax.rsqrt(_half_sum(x * x, lo) * (1.0 / ATT_HEAD_DIM) + EPS)


def _swa_mask(i, tm):
    tq = i * tm + _iota((tm, N_KEYS), 0)
    s = _iota((tm, N_KEYS), 1)
    nq = tq >> 6
    kr = i * tm + s - (N_KEYS - tm)
    kc = kr >> 6
    band = (kr >= META_PAD) & (kc >= nq - WINDOW_CHUNKS) & (kc <= nq)
    meta = (nq > WINDOW_CHUNKS) & (s >= META_PAD)
    return ((s < CHUNK) & meta) | ((s >= CHUNK) & band)


def _swa_keys(own_kv, prev_ref, meta_ref, kg, tm):
    kv = jnp.concatenate([meta_ref[...], jnp.zeros((DEAD_ROWS, 2 * D_KV), F32), prev_ref[tm - PREV_ROWS:tm, :],
                          own_kv], axis=0)
    k_raw, v = kv[:, :D_KV], kv[:, D_KV:]
    lo = _iota((1, D_KV), 1) < ATT_HEAD_DIM
    kr = _half_rms(k_raw, lo)
    kn = k_raw * kr * kg
    return k_raw, kr, kn, v, lo


def _placed(x, lo):
    xr = pltpu.roll(x, ATT_HEAD_DIM, 1)
    z = jnp.zeros_like(x)
    return [[jnp.where(lo, x, z).astype(MXU_DTYPE), jnp.where(lo, z, xr).astype(MXU_DTYPE)],
            [jnp.where(lo, xr, z).astype(MXU_DTYPE), jnp.where(lo, z, x).astype(MXU_DTYPE)]]


def _swa_specs(tm, order):
    kvb = (OFF_C + 1024) // 256
    return [pl.BlockSpec((tm, W_C), lambda i: (order(i), OFF_C // W_C)),
            pl.BlockSpec((tm, 256), lambda i: (jnp.maximum(order(i) - 1, 0), kvb)),
            pl.BlockSpec((CHUNK, 256), lambda i: (0, kvb)),
            _full_spec((1, D_KV)), _full_spec((1, D_KV)), pl.BlockSpec(memory_space=pltpu.SMEM)]


def _swa_fwd(u, qg, kg, sinks):
    lp = u.shape[0]
    tm = TM_SWA

    def body(own_ref, prev_ref, meta_ref, qg_ref, kg_ref, sink_ref, y_ref):
        i = pl.program_id(0)
        own = own_ref[...]
        _, _, kn, v, lo = _swa_keys(own[:, 1024:1280], prev_ref, meta_ref, kg_ref[...], tm)
        kuse, vuse = _placed(kn, lo), _placed(v, lo)
        bias = jnp.where(_swa_mask(i, tm), 0.0, NEG)
        for gi in range(ATT_Q_HEADS // 2):
            j = gi // 2
            sl = slice(gi * 128, (gi + 1) * 128)
            qraw = own[:, sl]
            qs = qraw * _half_rms(qraw, lo) * (qg_ref[...] * ATT_SCALE2)
            og = jnp.zeros((tm, 128), F32)
            for e in range(2):
                qm = jnp.where(lo if e == 0 else ~lo, qs, 0.0)
                s = _dot_nt(qm, kuse[j][e]) + bias
                sk = sink_ref[2 * gi + e] * LOG2E
                m = jnp.maximum(jnp.max(s, axis=-1, keepdims=True), sk)
                p = jnp.exp2(s - m)
                inv = 1.0 / (jnp.sum(p, axis=-1, keepdims=True) + jnp.exp2(sk - m))
                og = og + _dot(p, vuse[j][e]) * inv
            gt = own[:, 512 + gi * 128:512 + (gi + 1) * 128]
            y_ref[:, sl] = (og * gt * _sig(gt)).astype(MXU_DTYPE)

    return pl.pallas_call(
        body, name="swa_fwd", grid=(lp // tm,),
        in_specs=_swa_specs(tm, lambda i: i),
        out_specs=pl.BlockSpec((tm, D_ATT), lambda i: (i, 0)),
        out_shape=jax.ShapeDtypeStruct((lp, D_ATT), MXU_DTYPE),
        compiler_params=_cp(("arbitrary",)),
    )(u, u, u, qg, kg, sinks)


def _swa_bwd(u, dyc, du, qg, kg, sinks):
    lp = u.shape[0]
    tm = TM_SWA
    nb = lp // tm
    order = lambda ii: nb - 1 - ii

    def body(own_ref, prev_ref, meta_ref, qg_ref, kg_ref, sink_ref, dy_ref, du_in, du_ref, dqg_ref, dkg_ref, dsk_ref,
             carry_ref, macc_ref):
        del du_in
        ii = pl.program_id(0)
        i = nb - 1 - ii

        @pl.when(ii == 0)
        def _():
            carry_ref[...] = jnp.zeros_like(carry_ref)
            macc_ref[...] = jnp.zeros_like(macc_ref)
            dqg_ref[...] = jnp.zeros_like(dqg_ref)
            dkg_ref[...] = jnp.zeros_like(dkg_ref)
            dsk_ref[...] = jnp.zeros_like(dsk_ref)

        own = own_ref[...]
        k_raw, krs, kn, v, lo = _swa_keys(own[:, 1024:1280], prev_ref, meta_ref, kg_ref[...], tm)
        kuse, vuse = _placed(kn, lo), _placed(v, lo)
        bias = jnp.where(_swa_mask(i, tm), 0.0, NEG)
        dkn_t = jnp.zeros((D_KV, N_KEYS), F32)
        dvn_t = jnp.zeros((D_KV, N_KEYS), F32)
        for gi in range(ATT_Q_HEADS // 2):
            j = gi // 2
            sl = slice(gi * 128, (gi + 1) * 128)
            qraw = own[:, sl]
            qr = _half_rms(qraw, lo)
            qxh = qraw * qr
            qs = qxh * (qg_ref[...] * ATT_SCALE2)
            ps, invs, pk, qms = [], [], [], []
            og = jnp.zeros((tm, 128), F32)
            for e in range(2):
                qm = jnp.where(lo if e == 0 else ~lo, qs, 0.0)
                s = _dot_nt(qm, kuse[j][e]) + bias
                sk = sink_ref[2 * gi + e] * LOG2E
                m = jnp.maximum(jnp.max(s, axis=-1, keepdims=True), sk)
                p = jnp.exp2(s - m)
                inv = 1.0 / (jnp.sum(p, axis=-1, keepdims=True) + jnp.exp2(sk - m))
                ps.append(p)
                invs.append(inv)
                pk.append(jnp.exp2(sk - m) * inv)
                qms.append(qm)
                og = og + _dot(p, vuse[j][e]) * inv
            gt = own[:, 512 + gi * 128:512 + (gi + 1) * 128]
            sg = _sig(gt)
            dy = dy_ref[:, sl].astype(F32)
            dgt = dy * og * _dsilu(gt, sg)
            dog = dy * gt * sg
            dqn = jnp.zeros((tm, 128), F32)
            for e in range(2):
                half = lo if e == 0 else ~lo
                dog_m = jnp.where(half, dog, 0.0)
                dl = jnp.sum(dog_m * og, axis=1, keepdims=True)
                dp = _dot_nt(dog_m, vuse[j][e])
                ds = ps[e] * ((dp - dl) * (invs[e] * (1.0 / LOG2E)))
                hsk = 2 * gi + e
                dsk_ref[hsk:hsk + 1, :] += jnp.zeros((1, 128), F32) - jnp.sum(pk[e] * dl, axis=0, keepdims=True)
                dqn = dqn + _dot(ds, kuse[j][e])
                dk_e = _dot_tn(qms[e], ds)
                dv_e = _dot_tn(dog_m * invs[e], ps[e])
                if j != e:
                    dk_e = pltpu.roll(dk_e, ATT_HEAD_DIM, 0)
                    dv_e = pltpu.roll(dv_e, ATT_HEAD_DIM, 0)
                dkn_t = dkn_t + dk_e
                dvn_t = dvn_t + dv_e
            dqn = dqn * ATT_SCALE2
            dqg_ref[...] += jnp.sum(dqn * qxh, axis=0, keepdims=True)
            dqx = dqn * qg_ref[...]
            dq = qr * (dqx - qxh * _half_sum(dqx * qxh, lo) * (1.0 / ATT_HEAD_DIM))
            du_ref[:, sl] = dq.astype(MXU_DTYPE)
            du_ref[:, 512 + gi * 128:512 + (gi + 1) * 128] = dgt.astype(MXU_DTYPE)

        dkn, dvn = dkn_t.T, dvn_t.T
        macc_ref[...] += jnp.concatenate([dkn[0:CHUNK], dvn[0:CHUNK]], axis=1)
        own0 = N_KEYS - tm
        tot = jnp.concatenate([dkn[own0:], dvn[own0:]], axis=1) + carry_ref[...]
        if tm > PREV_ROWS:
            carry_ref[0:tm - PREV_ROWS, :] = jnp.zeros((tm - PREV_ROWS, 2 * D_KV), F32)
        prev0 = own0 - PREV_ROWS
        carry_ref[tm - PREV_ROWS:tm, :] = jnp.concatenate([dkn[prev0:own0], dvn[prev0:own0]], axis=1)
        first = jnp.where((i == 0) & (_iota((tm, 1), 0) < CHUNK), 1.0, 0.0)
        tot = tot + first * jnp.concatenate([macc_ref[...], jnp.zeros((tm - CHUNK, 2 * D_KV), F32)], axis=0)
        dkn_own, dv_own = tot[:, :D_KV], tot[:, D_KV:]
        kx = k_raw[own0:] * krs[own0:]
        dkg_ref[...] += jnp.sum(dkn_own * kx, axis=0, keepdims=True)
        dkx = dkn_own * kg_ref[...]
        dk = krs[own0:] * (dkx - kx * _half_sum(dkx * kx, lo) * (1.0 / ATT_HEAD_DIM))
        du_ref[:, 1024:1152] = dk.astype(MXU_DTYPE)
        du_ref[:, 1152:1280] = dv_own.astype(MXU_DTYPE)
        du_ref[:, 1280:W_C] = jnp.zeros((tm, W_C - 1280), MXU_DTYPE)

    return pl.pallas_call(
        body, name="swa_bwd", grid=(nb,),
        in_specs=_swa_specs(tm, order) + [pl.BlockSpec((tm, D_ATT), lambda ii: (order(ii), 0)),
                                          pl.BlockSpec(memory_space=pl.ANY)],
        out_specs=[pl.BlockSpec((tm, W_C), lambda ii: (order(ii), OFF_C // W_C)), _full_spec((1, 128)),
                   _full_spec((1, 128)), _full_spec((ATT_Q_HEADS, 128))],
        out_shape=[jax.ShapeDtypeStruct(du.shape, du.dtype), jax.ShapeDtypeStruct((1, 128), F32),
                   jax.ShapeDtypeStruct((1, 128), F32), jax.ShapeDtypeStruct((ATT_Q_HEADS, 128), F32)],
        scratch_shapes=[pltpu.VMEM((tm, 2 * D_KV), F32), pltpu.VMEM((CHUNK, 2 * D_KV), F32)],
        input_output_aliases={7: 0},
        compiler_params=_cp(("arbitrary",)),
    )(u, u, u, qg, kg, sinks, dyc, du)


def _load_once(pairs, first):
    @pl.when(first)
    def _():
        for src, dst in pairs:
            pltpu.sync_copy(src, dst)


def _mix_fwd(h, u, ya, yb, yc, wa, wb, wc, wo):
    lp = h.shape[0]
    tm = TM_MIX
    assert lp % tm == 0

    def body(h_ref, g_ref, ya_ref, yb_ref, yc_ref, wa_hbm, wb_hbm, wc_hbm, wo_hbm, out_ref, wa_ref, wb_ref, wc_ref,
             wo_ref):
        _load_once(((wa_hbm, wa_ref), (wb_hbm, wb_ref), (wc_hbm, wc_ref), (wo_hbm, wo_ref)), pl.program_id(0) == 0)
        mixed = jnp.zeros((tm, D_MODEL), F32)
        for n, (y_ref, w_ref) in enumerate(((ya_ref, wa_ref), (yb_ref, wb_ref), (yc_ref, wc_ref))):
            z = jnp.dot(y_ref[...], w_ref[...], preferred_element_type=F32)
            mixed = mixed + _sig(g_ref[:, n * D_MODEL:(n + 1) * D_MODEL]) * z
        out_ref[...] = h_ref[...] + _dot(mixed, wo_ref[...])

    ybs = pl.BlockSpec((tm, 512), lambda i: (i, 0))
    anyspec = pl.BlockSpec(memory_space=pl.ANY)
    return pl.pallas_call(
        body, name="mix_fwd", grid=(lp // tm,),
        in_specs=[pl.BlockSpec((tm, D_MODEL), lambda i: (i, 0)), pl.BlockSpec((tm, W_G), lambda i: (i, OFF_G // W_G)),
                  ybs, ybs, ybs, anyspec, anyspec, anyspec, anyspec],
        out_specs=pl.BlockSpec((tm, D_MODEL), lambda i: (i, 0)),
        out_shape=jax.ShapeDtypeStruct((lp, D_MODEL), F32),
        scratch_shapes=[pltpu.VMEM((512, D_MODEL), MXU_DTYPE)] * 3 + [pltpu.VMEM((D_MODEL, D_MODEL), MXU_DTYPE)],
        compiler_params=_cp(("arbitrary",)),
    )(h, u, ya, yb, yc, wa, wb, wc, wo)


def _mix_bwd(dh, u, ya, yb, yc, wa, wb, wc, wo):
    lp = dh.shape[0]
    tm = TM_BR
    nb = lp // tm

    def body(dh_ref, g_ref, ya_ref, yb_ref, yc_ref, wa_hbm, wb_hbm, wc_hbm, wo_hbm,
             du_ref, dya_ref, dyb_ref, dyc_ref, dwa_hbm, dwb_hbm, dwc_hbm, dwo_hbm,
             wa_ref, wb_ref, wc_ref, wo_ref, dwa_ref, dwb_ref, dwc_ref, dwo_ref):
        i = pl.program_id(0)
        _load_once(((wa_hbm, wa_ref), (wb_hbm, wb_ref), (wc_hbm, wc_ref), (wo_hbm, wo_ref)), i == 0)

        @pl.when(i == 0)
        def _():
            for r in (dwa_ref, dwb_ref, dwc_ref, dwo_ref):
                r[...] = jnp.zeros_like(r)

        dh_b = dh_ref[...].astype(MXU_DTYPE)
        dmixed = _dot_nt(dh_b, wo_ref[...])
        mixed = jnp.zeros((tm, D_MODEL), F32)
        for n, (y_ref, w_ref, dy_ref, dw_ref) in enumerate(((ya_ref, wa_ref, dya_ref, dwa_ref),
                                                            (yb_ref, wb_ref, dyb_ref, dwb_ref),
                                                            (yc_ref, wc_ref, dyc_ref, dwc_ref))):
            y = y_ref[...]
            z = jnp.dot(y, w_ref[...], preferred_element_type=F32)
            gate = _sig(g_ref[:, n * D_MODEL:(n + 1) * D_MODEL])
            mixed = mixed + gate * z
            du_ref[:, n * D_MODEL:(n + 1) * D_MODEL] = (z * dmixed * gate * (1.0 - gate)).astype(MXU_DTYPE)
            dz = (gate * dmixed).astype(MXU_DTYPE)
            dy_ref[...] = _dot_nt(dz, w_ref[...]).astype(MXU_DTYPE)
            dw_ref[...] += _dot_tn(y, dz)
        dwo_ref[...] += _dot_tn(mixed, dh_b)

        @pl.when(i == nb - 1)
        def _():
            for src, dst in ((dwa_ref, dwa_hbm), (dwb_ref, dwb_hbm), (dwc_ref, dwc_hbm), (dwo_ref, dwo_hbm)):
                pltpu.sync_copy(src, dst)

    ybs = pl.BlockSpec((tm, 512), lambda i: (i, 0))
    anyspec = pl.BlockSpec(memory_space=pl.ANY)
    wsh = jax.ShapeDtypeStruct((512, D_MODEL), F32)
    return pl.pallas_call(
        body, name="mix_bwd", grid=(nb,),
        in_specs=[pl.BlockSpec((tm, D_MODEL), lambda i: (i, 0)), pl.BlockSpec((tm, W_G), lambda i: (i, OFF_G // W_G)),
                  ybs, ybs, ybs, anyspec, anyspec, anyspec, anyspec],
        out_specs=[pl.BlockSpec((tm, W_G), lambda i: (i, OFF_G // W_G)), ybs, ybs, ybs, anyspec, anyspec, anyspec, anyspec],
        out_shape=[jax.ShapeDtypeStruct((lp, NP), MXU_DTYPE)] + [jax.ShapeDtypeStruct((lp, 512), MXU_DTYPE)] * 3
        + [wsh, wsh, wsh, jax.ShapeDtypeStruct((D_MODEL, D_MODEL), F32)],
        scratch_shapes=[pltpu.VMEM((512, D_MODEL), MXU_DTYPE)] * 3 + [pltpu.VMEM((D_MODEL, D_MODEL), MXU_DTYPE)]
        + [pltpu.VMEM((512, D_MODEL), F32)] * 3 + [pltpu.VMEM((D_MODEL, D_MODEL), F32)],
        compiler_params=_cp(("arbitrary",)),
    )(dh, u, ya, yb, yc, wa, wb, wc, wo)


def _loss_head(h, target_p, seq):
    lp = h.shape[0]
    tm = TM_MIX
    assert lp % tm == 0

    def body(h_ref, t_ref, dh_ref, loss_ref):
        i = pl.program_id(0)

        @pl.when(i == 0)
        def _():
            loss_ref[...] = jnp.zeros_like(loss_ref)

        rows = i * tm + _iota((tm, 1), 0)
        e = jnp.where((rows >= CHUNK) & (rows < CHUNK + seq), h_ref[...] - t_ref[...], 0.0)
        dh_ref[...] = e * (1.0 / D_MODEL)
        part = jnp.sum(jnp.mean(e * e, axis=-1, keepdims=True), axis=0, keepdims=True)
        loss_ref[...] += 0.5 * part

    return pl.pallas_call(
        body, name="loss_head", grid=(lp // tm,),
        in_specs=[pl.BlockSpec((tm, D_MODEL), lambda i: (i, 0))] * 2,
        out_specs=[pl.BlockSpec((tm, D_MODEL), lambda i: (i, 0)), _full_spec((1, 128))],
        out_shape=[jax.ShapeDtypeStruct((lp, D_MODEL), F32), jax.ShapeDtypeStruct((1, 128), F32)],
        compiler_params=_cp(("arbitrary",)),
    )(h, target_p)


def _lb_rows(p_ref):
    depth = p_ref.shape[0]
    rows = [p_ref[l:l + 1, :] for l in range(depth)]
    mx = functools.reduce(jnp.maximum, rows)
    ex = [jnp.exp(r - mx) for r in rows]
    tot = functools.reduce(jnp.add, ex)
    sm = [e / tot for e in ex]
    cs, run = [], jnp.zeros_like(sm[0])
    for l in range(depth):
        run = run + sm[l]
        cs.append(run)
    return sm, [c - sm[0] for c in cs]


def _lb_fwd(p):
    def body(p_ref, o_ref):
        _, xs = _lb_rows(p_ref)
        for l, xl in enumerate(xs):
            o_ref[l:l + 1, :] = jnp.clip(xl, 0.0, 1.0)

    return pl.pallas_call(body, name="lb_fwd", out_shape=jax.ShapeDtypeStruct(p.shape, F32))(p)


def _lb_bwd(p, dlb):
    def body(p_ref, d_ref, o_ref):
        sm, xs = _lb_rows(p_ref)
        depth = len(xs)
        dx = []
        for l in range(depth):
            x = xs[l]
            g0 = jnp.where(x > 0.0, 1.0, jnp.where(x == 0.0, 0.5, 0.0))
            y = jnp.maximum(x, 0.0)
            g1 = jnp.where(y < 1.0, 1.0, jnp.where(y == 1.0, 0.5, 0.0))
            dx.append(d_ref[l:l + 1, :] * g0 * g1)
        dsm = [functools.reduce(jnp.add, dx[jj:]) for jj in range(depth)]
        dsm[0] = dsm[0] - functools.reduce(jnp.add, dx)
        inner = functools.reduce(jnp.add, [a * b for a, b in zip(sm, dsm)])
        for l in range(depth):
            o_ref[l:l + 1, :] = sm[l] * (dsm[l] - inner)

    return pl.pallas_call(body, name="lb_bwd", out_shape=jax.ShapeDtypeStruct(p.shape, F32))(p, dlb)


def _exchange(gather, scatter, name):
    ng, ns = len(gather), len(scatter)
    n = ng + ns

    def body(*refs):
        x_refs, o_refs, sems = refs[:n], refs[n:2 * n], refs[2 * n:]
        exs = []
        if ng:
            exs.append(_Exchange(x_refs[:ng], o_refs[:ng], *sems[:3], scatter=False))
        if ns:
            exs.append(_Exchange(x_refs[ng:], o_refs[ng:], *sems[-3:], scatter=True))
        for ex in exs:
            ex.start()
        for ex in exs:
            ex.finish()

    out_shape = [jax.ShapeDtypeStruct((N_DEV,) + x.shape, x.dtype) for x in gather]
    out_shape += [jax.ShapeDtypeStruct(x.shape, x.dtype) for x in scatter]
    return pl.pallas_call(
        body, name=name, in_specs=[_ANY] * n, out_specs=[_ANY] * n, out_shape=out_shape,
        scratch_shapes=(_exchange_sems(ng) if ng else []) + (_exchange_sems(ns) if ns else []),
        compiler_params=pltpu.CompilerParams(has_side_effects=True),
    )(*gather, *scatter)


def _gather_two_level(xs, name):
    n = len(xs)

    def body(*refs):
        x_refs, o_refs = refs[:n], refs[n:2 * n]
        send_sems, recv_sems, loc_sems = refs[2 * n:]
        x, y, c = lax.axis_index("x"), lax.axis_index("y"), lax.axis_index("c")
        chips = [(1 - x, y), (x, 1 - y), (1 - x, 1 - y)]
        idx = lambda px, py, pc: 4 * px + 2 * py + pc

        def copy(a, k, block, to, src=None):
            slot = o_refs[a].at[idx(*block)]
            return pltpu.make_async_remote_copy(src_ref=slot if src is None else src, dst_ref=slot,
                                                send_sem=send_sems.at[a, k], recv_sem=recv_sems.at[a, k],
                                                device_id=to, device_id_type=pl.DeviceIdType.MESH)

        me, sib = (x, y, c), (x, y, 1 - c)
        local = [pltpu.make_async_copy(x_refs[a], o_refs[a].at[idx(*me)], loc_sems.at[a]) for a in range(n)]
        first = [copy(a, 0, me, sib, src=x_refs[a]) for a in range(n)]
        first += [copy(a, 1 + j, me, (*chip, c), src=x_refs[a]) for j, chip in enumerate(chips) for a in range(n)]
        for cp in local + first:
            cp.start()
        passed = []
        for j, chip in enumerate(chips):
            for a in range(n):
                copy(a, 1 + j, (*chip, c), me).wait_recv()
                cp = copy(a, 4 + j, (*chip, c), sib)
                cp.start()
                passed.append(cp)
        for a in range(n):
            copy(a, 0, sib, me).wait_recv()
            for j, chip in enumerate(chips):
                copy(a, 4 + j, (*chip, 1 - c), me).wait_recv()
        for cp in first + passed:
            cp.wait_send()
        for cp in local:
            cp.wait()

    return pl.pallas_call(
        body, name=name, in_specs=[_ANY] * n, out_specs=[_ANY] * n,
        out_shape=[jax.ShapeDtypeStruct((N_DEV,) + x.shape, x.dtype) for x in xs],
        scratch_shapes=_exchange_sems(n), compiler_params=pltpu.CompilerParams(has_side_effects=True),
    )(*xs)


def _adamw(gp, w, m, v, name):
    r, cc = w.shape
    tr = 256 if r % 256 == 0 else r

    def body(g_ref, w_ref, m_ref, v_ref, go_ref, d_ref, mo_ref, vo_ref):
        g = g_ref[0].astype(F32)
        for s in range(1, N_DEV):
            g = g + g_ref[s].astype(F32)
        go_ref[...] = g
        mn = ADAM_B1 * m_ref[...] + (1.0 - ADAM_B1) * g
        vn = ADAM_B2 * v_ref[...] + (1.0 - ADAM_B2) * (g * g)
        m_hat = mn / (1.0 - ADAM_B1 ** ADAM_STEP)
        v_hat = vn / (1.0 - ADAM_B2 ** ADAM_STEP)
        d_ref[...] = -ADAM_LR * (m_hat / (jnp.sqrt(v_hat) + ADAM_EPS) + ADAM_WD * w_ref[...])
        mo_ref[...] = mn
        vo_ref[...] = vn

    bs = pl.BlockSpec((tr, cc), lambda i: (i, 0))
    sh = jax.ShapeDtypeStruct((r, cc), F32)
    return pl.pallas_call(
        body, name=name, grid=(r // tr,),
        in_specs=[pl.BlockSpec((N_DEV, tr, cc), lambda i: (0, i, 0)), bs, bs, bs],
        out_specs=[bs, bs, bs, bs], out_shape=[sh, sh, sh, sh],
        compiler_params=_cp(("parallel",)),
    )(gp, w, m, v)


def _pack_cols(w):
    parts, pos = [], 0
    for pstart, ostart, width in _PACK:
        if pstart != pos:
            parts.append(jnp.zeros(w.shape[:-1] + (pstart - pos,), w.dtype))
        parts.append(w[..., ostart:ostart + width])
        pos = pstart + width
    return jnp.concatenate(parts, axis=-1)


def _unpack_cols(wp):
    by_orig = sorted(_PACK, key=lambda t: t[1])
    return jnp.concatenate([wp[..., p:p + wd] for p, _, wd in by_orig], axis=-1)


_LAYER_SHARDED = ("w_in", "conv_w", "w_conv_out", "w_hg_out", "w_att_out", "w_out")
_NARROW = ("w_in", "w_conv_out", "w_hg_out", "w_att_out", "w_out")
_REPLICATED = ("norm_g", "conv_b", "conv_ln_g", "conv_ln_b", "hg_lower_bounds", "hg_norm_g", "q_norm_g", "k_norm_g",
               "attn_sinks")
_WEIGHTS = ("meta_tokens", "norm_g", "w_in", "conv_w", "conv_b", "conv_ln_g", "conv_ln_b", "w_conv_out",
            "hg_lower_bounds", "hg_norm_g", "w_hg_out", "q_norm_g", "k_norm_g", "attn_sinks", "w_att_out", "w_out")
_ROW_SHARDED = ("w_out",)


def _assemble(name, g):
    if name in _ROW_SHARDED:
        return g.reshape((N_DEV * g.shape[1],) + g.shape[2:])
    full = jnp.moveaxis(g, 0, -2)
    return full.reshape(full.shape[:-2] + (N_DEV * full.shape[-1],))


def _split(name, full):
    if name in _ROW_SHARDED:
        return full.reshape((N_DEV, full.shape[0] // N_DEV) + full.shape[1:])
    c = full.shape[-1] // N_DEV
    return jnp.moveaxis(full.reshape(full.shape[:-1] + (N_DEV, c)), -2, 0)


def _layer_weights(gathered):
    full = {k: _assemble(k, g) for k, g in zip(_LAYER_SHARDED, gathered)}
    wp = _pack_cols(full["w_in"])
    return dict(wp=wp, wpt=wp.T, cw=full["conv_w"], wa=full["w_conv_out"], wb=full["w_hg_out"],
                wc=full["w_att_out"], wo=full["w_out"])


def _layer_fwd(h, lw, sp, gather):
    u, hn, gathered = _inproj_fwd(h, sp["norm_g"], lw["wp"], gather)
    ya, y_conv, yb, states = _conv_hgrn_fwd(u, lw["cw"], sp["conv_b"], sp["conv_ln_g"], sp["conv_ln_b"], sp["lb"],
                                            sp["hg_norm_g"])
    yc = _swa_fwd(u, sp["qg"], sp["kg"], sp["sinks"])
    h_next = _mix_fwd(h, u, ya, yb, yc, lw["wa"], lw["wb"], lw["wc"], lw["wo"])
    return h_next, (h, u, hn, ya, yb, yc, states, y_conv), gathered


def _layer_bwd(dh, saved, lw, sp, stacked, layer, depth):
    h_l, u, hn, ya, yb, yc, states, y_conv = saved
    du, dya, dyb, dyc, dwa, dwb, dwc, dwo = _mix_bwd(dh, u, ya, yb, yc, lw["wa"], lw["wb"], lw["wc"], lw["wo"])
    du, dy, dlg, dlb_ = _conv_bwd1(u, y_conv, dya, du, sp["conv_ln_g"], sp["conv_ln_b"])
    du, dcw, dcb = _conv_bwd2(u, dy, du, lw["cw"])
    du, dlbl, dgg = _hgrn_bwd(u, dyb, states, du, sp["lb"], sp["hg_norm_g"])
    du, dqg, dkg, dsk = _swa_bwd(u, dyc, du, sp["qg"], sp["kg"], sp["sinks"])
    dwp = _inproj_bwd_dw(hn, du)
    full = dict(w_in=_unpack_cols(dwp), conv_w=dcw, w_conv_out=dwa, w_hg_out=dwb, w_att_out=dwc, w_out=dwo)
    pieces = [_split(k, full[k]).astype(WIRE_DTYPE) for k in _LAYER_SHARDED]
    dh, dng, stacked = _inproj_bwd_dh(du, lw["wpt"], h_l, sp["norm_g"], dh, pieces, stacked, layer, depth)
    fold = lambda a: a[0, :ATT_HEAD_DIM] + a[0, ATT_HEAD_DIM:]
    small = dict(norm_g=dng[0], conv_b=dcb[0], conv_ln_g=dlg[0], conv_ln_b=dlb_[0], hg_lower_bounds=dlbl[0],
                 hg_norm_g=dgg[0], q_norm_g=fold(dqg), k_norm_g=fold(dkg), attn_sinks=dsk[:, 0])
    return dh, small, stacked


def _as2d(a):
    return a.reshape((-1, a.shape[-1]))


def kernel(x, meta_tokens, norm_g, w_in, conv_w, conv_b, conv_ln_g, conv_ln_b, w_conv_out, hg_lower_bounds, hg_norm_g, w_hg_out, q_norm_g, k_norm_g, attn_sinks, w_att_out, w_out, loss_target, m_meta_tokens, m_norm_g, m_w_in, m_conv_w, m_conv_b, m_conv_ln_g, m_conv_ln_b, m_w_conv_out, m_hg_lower_bounds, m_hg_norm_g, m_w_hg_out, m_q_norm_g, m_k_norm_g, m_attn_sinks, m_w_att_out, m_w_out, v_meta_tokens, v_norm_g, v_w_in, v_conv_w, v_conv_b, v_conv_ln_g, v_conv_ln_b, v_w_conv_out, v_hg_lower_bounds, v_hg_norm_g, v_w_hg_out, v_q_norm_g, v_k_norm_g, v_attn_sinks, v_w_att_out, v_w_out):
    w = dict(meta_tokens=meta_tokens, norm_g=norm_g, w_in=w_in, conv_w=conv_w, conv_b=conv_b, conv_ln_g=conv_ln_g,
             conv_ln_b=conv_ln_b, w_conv_out=w_conv_out, hg_lower_bounds=hg_lower_bounds, hg_norm_g=hg_norm_g,
             w_hg_out=w_hg_out, q_norm_g=q_norm_g, k_norm_g=k_norm_g, attn_sinks=attn_sinks, w_att_out=w_att_out,
             w_out=w_out)
    m = dict(meta_tokens=m_meta_tokens, norm_g=m_norm_g, w_in=m_w_in, conv_w=m_conv_w, conv_b=m_conv_b,
             conv_ln_g=m_conv_ln_g, conv_ln_b=m_conv_ln_b, w_conv_out=m_w_conv_out, hg_lower_bounds=m_hg_lower_bounds,
             hg_norm_g=m_hg_norm_g, w_hg_out=m_w_hg_out, q_norm_g=m_q_norm_g, k_norm_g=m_k_norm_g,
             attn_sinks=m_attn_sinks, w_att_out=m_w_att_out, w_out=m_w_out)
    v = dict(meta_tokens=v_meta_tokens, norm_g=v_norm_g, w_in=v_w_in, conv_w=v_conv_w, conv_b=v_conv_b,
             conv_ln_g=v_conv_ln_g, conv_ln_b=v_conv_ln_b, w_conv_out=v_w_conv_out, hg_lower_bounds=v_hg_lower_bounds,
             hg_norm_g=v_hg_norm_g, w_hg_out=v_w_hg_out, q_norm_g=v_q_norm_g, k_norm_g=v_k_norm_g,
             attn_sinks=v_attn_sinks, w_att_out=v_w_att_out, w_out=v_w_out)

    depth = norm_g.shape[0]
    seq = x.shape[1]
    lp = -(-(seq + CHUNK) // TM_MM) * TM_MM
    tail = lp - seq - CHUNK
    zeros = lambda n: jnp.zeros((n, D_MODEL), F32)

    def shards(l):
        return [w[k][l].astype(MXU_DTYPE) if k in _NARROW else w[k][l] for k in _LAYER_SHARDED]

    first = _gather_two_level(shards(0) + [meta_tokens], "gather_first")
    gathered, meta_full = first[:-1], _assemble("meta_tokens", first[-1])
    h = jnp.concatenate([zeros(META_PAD), meta_full, x[0], zeros(tail)], axis=0)
    target_p = jnp.concatenate([zeros(CHUNK), loss_target[0], zeros(tail)], axis=0)

    lb_all = _lb_fwd(hg_lower_bounds)
    tile2 = lambda a: jnp.concatenate([a, a], axis=-1)
    row = lambda a, l: a[l][None, :]

    def small_rows(l):
        sp = {k: row(w[k], l) for k in ("norm_g", "conv_b", "conv_ln_g", "conv_ln_b", "hg_norm_g")}
        sp.update(lb=row(lb_all, l), qg=tile2(row(q_norm_g, l)), kg=tile2(row(k_norm_g, l)), sinks=attn_sinks[l])
        return sp

    layer_w, saved = [], []
    for l in range(depth):
        layer_w.append(_layer_weights(gathered))
        h, sv, gathered = _layer_fwd(h, layer_w[l], small_rows(l), shards(l + 1) if l + 1 < depth else [])
        saved.append(sv)

    dh, loss_row = _loss_head(h, target_p, seq)
    loss = lax.psum(loss_row[0, 0], ("x", "y", "c"))

    stacked, small_grads = None, [None] * depth
    for l in reversed(range(depth)):
        dh, small_grads[l], stacked = _layer_bwd(dh, saved[l], layer_w[l], small_rows(l), stacked, l, depth)
    grad_x = dh[CHUNK:CHUNK + seq]
    grads = {k: jnp.stack([small_grads[l][k] for l in range(depth)]) for k in _REPLICATED}
    grads["hg_lower_bounds"] = _lb_bwd(hg_lower_bounds, grads["hg_lower_bounds"])

    small = jnp.concatenate([grads[k].reshape(-1) for k in _REPLICATED])
    small = jnp.concatenate([small, jnp.zeros((-small.shape[0] % 128,), F32)]).reshape(-1, 128)
    small_all, meta_pieces = _exchange([small], [_split("meta_tokens", dh[META_PAD:CHUNK])], "exchange_small_grads")
    small_all = small_all.reshape(N_DEV, -1)

    out_g, out_d, out_m, out_v = {}, {}, {}, {}
    for k, gp in zip(("meta_tokens",) + _LAYER_SHARDED, [meta_pieces] + stacked):
        shp = w[k].shape
        res = _adamw(gp.reshape((N_DEV,) + _as2d(w[k]).shape), _as2d(w[k]), _as2d(m[k]), _as2d(v[k]), "adamw_" + k)
        out_g[k], out_d[k], out_m[k], out_v[k] = (r.reshape(shp) for r in res)
    off = 0
    for k in _REPLICATED:
        shp = w[k].shape
        n = w[k].size
        gp = small_all[:, off:off + n].reshape((N_DEV,) + shp)
        off += n
        res = _adamw(gp, w[k], m[k], v[k], "adamw_" + k)
        out_g[k], out_d[k], out_m[k], out_v[k] = res

    return (loss, grad_x[None], *[out_g[k] for k in _WEIGHTS], *[out_d[k] for k in _WEIGHTS],
            *[out_m[k] for k in _WEIGHTS], *[out_v[k] for k in _WEIGHTS])
```

```python
import functools

import jax
import jax.numpy as jnp
from jax import lax
from jax.experimental import pallas as pl
from jax.experimental.pallas import tpu as pltpu

F32 = jnp.float32
MXU_DTYPE = jnp.bfloat16
WIRE_DTYPE = jnp.bfloat16
U_DTYPE = jnp.bfloat16

D_MODEL = 1024
CHUNK = 64
N_META = 16
META_PAD = CHUNK - N_META
D_CONV = 512
CONV_WIDTH = 31
HG_HEADS = 4
HG_D = 128
D_HG = HG_HEADS * HG_D
F_FLOOR = 1e-30
ATT_Q_HEADS = 8
ATT_HEAD_DIM = 64
D_ATT = 512
D_KV = 128
WINDOW_CHUNKS = 2
EPS = 1e-6
D_IN = 7936
N_DEV = 8

ADAM_LR = 0.001
ADAM_B1 = 0.9
ADAM_B2 = 0.999
ADAM_EPS = 1e-08
ADAM_WD = 0.01
ADAM_STEP = 10

NP = 8192
OFF_B, W_B = 0, 2048
OFF_A, W_A = 2048, 1024
OFF_G, W_G = 3072, 3072
OFF_C, W_C = 6144, 1536
OFF_AG, W_AG = 7680, 512
_PACK = ((0, 1536, 2048), (2048, 0, 1024), (3072, 4864, 3072), (6144, 3584, 512), (6656, 4352, 512),
         (7168, 4096, 256), (7680, 1024, 512))
_PAD_AT, _PAD_W = 7424, 256

TM_MM = 1280
TM_BR = 256
TM_WIDE = 640
TM_MIX = 640
HALO = 32
EXP_CLAMP = 80.0
VMEM_LIMIT = 56 * 1024 * 1024

_HI = lax.Precision.HIGHEST


def _cp(sem):
    return pltpu.CompilerParams(dimension_semantics=sem, vmem_limit_bytes=VMEM_LIMIT)


def _sig(x):
    return 1.0 / (1.0 + jnp.exp(-x))


def _dot(a, b):
    return jnp.dot(a.astype(MXU_DTYPE), b.astype(MXU_DTYPE), preferred_element_type=F32)


def _dot_nt(a, b):
    return lax.dot_general(a.astype(MXU_DTYPE), b.astype(MXU_DTYPE), (((1,), (1,)), ((), ())),
                           preferred_element_type=F32)


def _dot_tn(a, b):
    return lax.dot_general(a.astype(MXU_DTYPE), b.astype(MXU_DTYPE), (((0,), (0,)), ((), ())),
                           preferred_element_type=F32)


def _rnd(x):
    return x.astype(MXU_DTYPE).astype(F32)


def _iota(shape, dim):
    return lax.broadcasted_iota(jnp.int32, shape, dim)


def _full_spec(shape):
    nd = len(shape)
    return pl.BlockSpec(shape, lambda *_: (0,) * nd)


def _my_index():
    return 4 * lax.axis_index("x") + 2 * lax.axis_index("y") + lax.axis_index("c")


def _mesh_id(p):
    return (p >> 2, (p >> 1) & 1, p & 1)


class _Exchange:
    def __init__(self, x_refs, o_refs, send_sems, recv_sems, loc_sems, scatter, dst=lambda o, s: o.at[s]):
        me = _my_index()
        self.local, self.sends, self.recvs = [], [], []
        for a, (x, o) in enumerate(zip(x_refs, o_refs)):
            mine = x.at[me] if scatter else x
            self.local.append(pltpu.make_async_copy(mine, dst(o, me), loc_sems.at[a]))
            for k in range(1, N_DEV):
                to, frm = (me + k) % N_DEV, (me + N_DEV - k) % N_DEV
                sems = dict(send_sem=send_sems.at[a, k - 1], recv_sem=recv_sems.at[a, k - 1],
                            device_id_type=pl.DeviceIdType.MESH)
                self.sends.append(pltpu.make_async_remote_copy(
                    src_ref=x.at[to] if scatter else x, dst_ref=dst(o, me), device_id=_mesh_id(to), **sems))
                self.recvs.append(pltpu.make_async_remote_copy(
                    src_ref=mine, dst_ref=dst(o, frm), device_id=_mesh_id(frm), **sems))

    def start(self):
        for cp in self.local + self.sends:
            cp.start()

    def finish(self):
        for cp in self.recvs:
            cp.wait_recv()
        for cp in self.sends:
            cp.wait_send()
        for cp in self.local:
            cp.wait()


def _exchange_sems(n):
    return [pltpu.SemaphoreType.DMA((n, N_DEV - 1)), pltpu.SemaphoreType.DMA((n, N_DEV - 1)),
            pltpu.SemaphoreType.DMA((n,))]


_ANY = pl.BlockSpec(memory_space=pl.ANY)


def _inproj_fwd(h, g, wp, gather=()):
    lp = h.shape[0]
    tm, tn = TM_MM, 2048
    ni, nj = lp // tm, NP // tn
    n = len(gather)

    def body(h_ref, g_ref, w_ref, *rest):
        x_refs, (u_ref, hn_ref), o_refs = rest[:n], rest[n:n + 2], rest[n + 2:2 * n + 2]
        hs_ref, sems = rest[2 * n + 2], rest[2 * n + 3:]
        i, j = pl.program_id(0), pl.program_id(1)
        if n:
            @pl.when((i == 0) & (j == 0))
            def _():
                _Exchange(x_refs, o_refs, *sems, scatter=False).start()

        @pl.when(j == 0)
        def _():
            x = h_ref[...]
            r = lax.rsqrt(jnp.mean(x * x, axis=-1, keepdims=True) + EPS)
            hn = (x * r * g_ref[...]).astype(MXU_DTYPE)
            hs_ref[...] = hn
            hn_ref[...] = hn
        u_ref[...] = jnp.dot(hs_ref[...], w_ref[...], preferred_element_type=F32).astype(U_DTYPE)
        if n:
            @pl.when((i == ni - 1) & (j == nj - 1))
            def _():
                _Exchange(x_refs, o_refs, *sems, scatter=False).finish()

    res = pl.pallas_call(
        body, name="inproj_fwd_gather" if n else "inproj_fwd", grid=(ni, nj),
        in_specs=[pl.BlockSpec((tm, D_MODEL), lambda i, j: (i, 0)), pl.BlockSpec((1, D_MODEL), lambda i, j: (0, 0)),
                  pl.BlockSpec((D_MODEL, tn), lambda i, j: (0, j))] + [_ANY] * n,
        out_specs=[pl.BlockSpec((tm, tn), lambda i, j: (i, j)), pl.BlockSpec((tm, D_MODEL), lambda i, j: (i, 0))]
        + [_ANY] * n,
        out_shape=[jax.ShapeDtypeStruct((lp, NP), U_DTYPE), jax.ShapeDtypeStruct((lp, D_MODEL), MXU_DTYPE)]
        + [jax.ShapeDtypeStruct((N_DEV,) + x.shape, x.dtype) for x in gather],
        scratch_shapes=[pltpu.VMEM((tm, D_MODEL), MXU_DTYPE)] + (_exchange_sems(n) if n else []),
        compiler_params=_cp(("arbitrary", "arbitrary")),
    )(h, g, wp, *gather)
    return res[0], res[1], list(res[2:])


def _inproj_bwd_dh(du, wpt, h, g, dh_next, pieces, stacked, layer, depth):
    lp = h.shape[0]
    tm, tk = TM_MM, 1024
    ni, nk = lp // tm, NP // tk
    n = len(pieces)
    n_acc = 0 if stacked is None else n

    def body(du_ref, w_ref, h_ref, g_ref, dhn_ref, *rest):
        x_refs, (dh_ref, dg_ref), o_refs = rest[:n], rest[n + n_acc:n + n_acc + 2], rest[n + n_acc + 2:2 * n + n_acc + 2]
        acc_ref, sems = rest[2 * n + n_acc + 2], rest[2 * n + n_acc + 3:]
        i, k = pl.program_id(0), pl.program_id(1)
        slot = lambda o, s: o.at[s, layer]

        @pl.when((i == 0) & (k == 0))
        def _():
            _Exchange(x_refs, o_refs, *sems, scatter=True, dst=slot).start()
            dg_ref[...] = jnp.zeros_like(dg_ref)

        @pl.when(k == 0)
        def _():
            acc_ref[...] = jnp.zeros_like(acc_ref)

        acc_ref[...] += jnp.dot(du_ref[...], w_ref[...], preferred_element_type=F32)

        @pl.when(k == nk - 1)
        def _():
            dhn = acc_ref[...]
            x = h_ref[...]
            r = lax.rsqrt(jnp.mean(x * x, axis=-1, keepdims=True) + EPS)
            xh = x * r
            dg_ref[...] += jnp.sum(dhn * xh, axis=0, keepdims=True)
            dxh = dhn * g_ref[...]
            dx = r * (dxh - xh * jnp.mean(dxh * xh, axis=-1, keepdims=True))
            dh_ref[...] = dhn_ref[...] + dx

        @pl.when((i == ni - 1) & (k == nk - 1))
        def _():
            _Exchange(x_refs, o_refs, *sems, scatter=True, dst=slot).finish()

    acc_in = [] if stacked is None else list(stacked)
    res = pl.pallas_call(
        body, name="inproj_bwd_dh_scatter", grid=(ni, nk),
        in_specs=[pl.BlockSpec((tm, tk), lambda i, k: (i, k)), pl.BlockSpec((tk, D_MODEL), lambda i, k: (k, 0)),
                  pl.BlockSpec((tm, D_MODEL), lambda i, k: (i, 0)), pl.BlockSpec((1, D_MODEL), lambda i, k: (0, 0)),
                  pl.BlockSpec((tm, D_MODEL), lambda i, k: (i, 0))] + [_ANY] * (n + n_acc),
        out_specs=[pl.BlockSpec((tm, D_MODEL), lambda i, k: (i, 0)), pl.BlockSpec((1, D_MODEL), lambda i, k: (0, 0))]
        + [_ANY] * n,
        out_shape=[jax.ShapeDtypeStruct((lp, D_MODEL), F32), jax.ShapeDtypeStruct((1, D_MODEL), F32)]
        + [jax.ShapeDtypeStruct((N_DEV, depth) + p.shape[1:], p.dtype) for p in pieces],
        scratch_shapes=[pltpu.VMEM((tm, D_MODEL), F32)] + _exchange_sems(n),
        input_output_aliases={5 + n + a: 2 + a for a in range(n_acc)},
        compiler_params=_cp(("arbitrary", "arbitrary")),
    )(du, wpt, h, g, dh_next, *pieces, *acc_in)
    return res[0], res[1], list(res[2:])


def _inproj_bwd_dw(hn, du):
    lp = hn.shape[0]
    tm, tn = TM_MM, 2048

    def body(hn_ref, du_ref, dw_ref):
        @pl.when(pl.program_id(1) == 0)
        def _():
            dw_ref[...] = jnp.zeros_like(dw_ref)
        dw_ref[...] += _dot_tn(hn_ref[...], du_ref[...])

    return pl.pallas_call(
        body, name="inproj_bwd_dw", grid=(NP // tn, lp // tm),
        in_specs=[pl.BlockSpec((tm, D_MODEL), lambda j, m: (m, 0)), pl.BlockSpec((tm, tn), lambda j, m: (m, j))],
        out_specs=pl.BlockSpec((D_MODEL, tn), lambda j, m: (0, j)),
        out_shape=jax.ShapeDtypeStruct((D_MODEL, NP), F32),
        compiler_params=_cp(("parallel", "arbitrary")),
    )(hn, du)


N_SHIFT = 8
CONV_SUB = 32


def _shift_copies(src_ref, sh_ref):
    n = sh_ref.shape[1]
    for b in range(1, N_SHIFT):
        sh_ref[b - 1, :, :] = src_ref[pl.ds(b, n), :]


def _window(src_ref, sh_ref, off, r0, n):
    a, b = divmod(off, N_SHIFT)
    start = pl.multiple_of(r0 + a * N_SHIFT, N_SHIFT)
    if b == 0:
        return src_ref[pl.ds(start, n), :]
    return sh_ref[b - 1, pl.ds(start, n), :]


def _shift_scratch(tm):
    return pltpu.VMEM((N_SHIFT - 1, tm + HALO - N_SHIFT, D_CONV), F32)


def _glu_ext(a_ref, ah_ref, ext_ref, sh_ref, i, tm):
    rows = i * tm + _iota((tm, 1), 0)
    a = a_ref[...].astype(F32)
    p, sq = a[:, :D_CONV], _sig(a[:, D_CONV:])
    valid = rows >= META_PAD
    ah = ah_ref[...].astype(F32)
    ext_ref[0:HALO, :] = jnp.where(i > 0, ah[:, :D_CONV] * _sig(ah[:, D_CONV:]), 0.0)
    ext_ref[HALO:HALO + tm, :] = jnp.where(valid, p * sq, 0.0)
    _shift_copies(ext_ref, sh_ref)
    return p, sq, valid


def _layernorm_stats(y):
    mu = jnp.mean(y, axis=-1, keepdims=True)
    yc = y - mu
    rstd = lax.rsqrt(jnp.mean(yc * yc, axis=-1, keepdims=True) + EPS)
    return yc * rstd, rstd


def _conv_specs(tm):
    hb = tm // HALO
    return [pl.BlockSpec((tm, W_A), lambda i: (i, OFF_A // W_A)),
            pl.BlockSpec((HALO, W_A), lambda i: (jnp.maximum(i * hb - 1, 0), OFF_A // W_A)),
            pl.BlockSpec((tm, W_AG), lambda i: (i, OFF_AG // W_AG))]


def _conv_fwd_body(tm):
    def body(a_ref, ah_ref, ag_ref, w_ref, b_ref, lg_ref, lb_ref, ya_ref, y_ref, ext_ref, sh_ref):
        i = pl.program_id(0)
        _glu_ext(a_ref, ah_ref, ext_ref, sh_ref, i, tm)
        base = HALO - (CONV_WIDTH - 1)

        y = jnp.zeros((tm, D_CONV), F32) + b_ref[...]
        for k in range(CONV_WIDTH):
            y = y + w_ref[k:k + 1, :] * _window(ext_ref, sh_ref, base + k, 0, tm)
        y_ref[...] = y
        xh, _ = _layernorm_stats(y)
        yn = xh * lg_ref[...] + lb_ref[...]
        gt = ag_ref[...].astype(F32)
        ya_ref[...] = (yn * _sig(yn) * gt * _sig(gt)).astype(MXU_DTYPE)

    return body


def _dsilu(x, s):
    return s * (1.0 + x * (1.0 - s))


def _conv_bwd1(u, y, dya, du, lg, lb_):
    lp = u.shape[0]
    tm = TM_MIX
    assert lp % tm == 0

    def body(ag_ref, y_ref, dya_ref, lg_ref, lb_ref, du_in, du_ref, dy_ref, dlg_ref, dlb_ref):
        del du_in
        i = pl.program_id(0)

        @pl.when(i == 0)
        def _():
            dlg_ref[...] = jnp.zeros_like(dlg_ref)
            dlb_ref[...] = jnp.zeros_like(dlb_ref)

        xh, rstd = _layernorm_stats(y_ref[...])
        yn = xh * lg_ref[...] + lb_ref[...]
        s1 = _sig(yn)
        gt = ag_ref[...].astype(F32)
        s2 = _sig(gt)
        do = dya_ref[...].astype(F32)
        du_ref[...] = (do * (yn * s1) * _dsilu(gt, s2)).astype(MXU_DTYPE)
        dyn = do * (gt * s2) * _dsilu(yn, s1)
        dlg_ref[...] += jnp.sum(dyn * xh, axis=0, keepdims=True)
        dlb_ref[...] += jnp.sum(dyn, axis=0, keepdims=True)
        dxh = dyn * lg_ref[...]
        dy_ref[...] = rstd * (dxh - jnp.mean(dxh, axis=-1, keepdims=True)
                              - xh * jnp.mean(dxh * xh, axis=-1, keepdims=True))

    rowspec = pl.BlockSpec((tm, D_CONV), lambda i: (i, 0))
    return pl.pallas_call(
        body, name="conv_bwd1", grid=(lp // tm,),
        in_specs=[_conv_specs(tm)[2], rowspec, rowspec, _full_spec((1, D_CONV)), _full_spec((1, D_CONV)),
                  pl.BlockSpec(memory_space=pl.ANY)],
        out_specs=[pl.BlockSpec((tm, W_AG), lambda i: (i, OFF_AG // W_AG)), rowspec,
                   _full_spec((1, D_CONV)), _full_spec((1, D_CONV))],
        out_shape=[jax.ShapeDtypeStruct(du.shape, du.dtype), jax.ShapeDtypeStruct((lp, D_CONV), F32),
                   jax.ShapeDtypeStruct((1, D_CONV), F32), jax.ShapeDtypeStruct((1, D_CONV), F32)],
        input_output_aliases={5: 0},
        compiler_params=_cp(("arbitrary",)),
    )(u, y, dya, lg, lb_, du)


def _conv_bwd2(u, dy, du, cw):
    lp = u.shape[0]
    tm = TM_WIDE
    assert lp % tm == 0
    nb = lp // tm
    hb = tm // HALO

    def body(a_ref, ah_ref, dy_ref, dyn_ref, w_ref, du_in, du_ref, dw_ref, db_ref, ext_ref, sh_ref, edy_ref, shd_ref,
             dwp_ref):
        del du_in
        i = pl.program_id(0)

        @pl.when(i == 0)
        def _():
            dwp_ref[...] = jnp.zeros_like(dwp_ref)
            db_ref[...] = jnp.zeros_like(db_ref)

        _glu_ext(a_ref, ah_ref, ext_ref, sh_ref, i, tm)
        dy_all = dy_ref[...]
        edy_ref[0:tm, :] = dy_all
        edy_ref[tm:tm + HALO, :] = jnp.where(i < nb - 1, dyn_ref[...], 0.0)
        _shift_copies(edy_ref, shd_ref)
        db_ref[...] += jnp.sum(dy_all, axis=0, keepdims=True)
        base = HALO - (CONV_WIDTH - 1)

        def fold8(x):
            parts = [x[s:s + N_SHIFT] for s in range(0, CONV_SUB, N_SHIFT)]
            return functools.reduce(jnp.add, parts)

        def sub(r, carry):
            r0 = pl.multiple_of(r * CONV_SUB, CONV_SUB)
            dy = dy_ref[pl.ds(r0, CONV_SUB), :]
            du0 = jnp.zeros((CONV_SUB, D_CONV), F32)
            for k in range(CONV_WIDTH):
                du0 = du0 + w_ref[k:k + 1, :] * _window(edy_ref, shd_ref, CONV_WIDTH - 1 - k, r0, CONV_SUB)
                dwp_ref[k] += fold8(dy * _window(ext_ref, sh_ref, base + k, r0, CONV_SUB))
            a = a_ref[pl.ds(r0, CONV_SUB), :].astype(F32)
            p, sq = a[:, :D_CONV], _sig(a[:, D_CONV:])
            valid = (i * tm + r0 + _iota((CONV_SUB, 1), 0)) >= META_PAD
            du0 = jnp.where(valid, du0, 0.0)
            du_ref[pl.ds(r0, CONV_SUB), :] = jnp.concatenate([du0 * sq, du0 * p * sq * (1.0 - sq)],
                                                             axis=1).astype(MXU_DTYPE)
            return carry

        lax.fori_loop(0, tm // CONV_SUB, sub, 0)

        @pl.when(i == nb - 1)
        def _():
            dw_ref[...] = jnp.sum(dwp_ref[...], axis=1)

    return pl.pallas_call(
        body, name="conv_bwd2", grid=(nb,),
        in_specs=_conv_specs(tm)[:2] + [pl.BlockSpec((tm, D_CONV), lambda i: (i, 0)),
                                        pl.BlockSpec((HALO, D_CONV), lambda i: (jnp.minimum((i + 1) * hb, nb * hb - 1), 0)),
                                        _full_spec((CONV_WIDTH, D_CONV)), pl.BlockSpec(memory_space=pl.ANY)],
        out_specs=[pl.BlockSpec((tm, W_A), lambda i: (i, OFF_A // W_A)), _full_spec((CONV_WIDTH, D_CONV)),
                   _full_spec((1, D_CONV))],
        out_shape=[jax.ShapeDtypeStruct(du.shape, du.dtype), jax.ShapeDtypeStruct((CONV_WIDTH, D_CONV), F32),
                   jax.ShapeDtypeStruct((1, D_CONV), F32)],
        scratch_shapes=[pltpu.VMEM((HALO + tm, D_CONV), F32), _shift_scratch(tm),
                        pltpu.VMEM((tm + HALO, D_CONV), F32), _shift_scratch(tm),
                        pltpu.VMEM((CONV_WIDTH, N_SHIFT, D_CONV), F32)],
        input_output_aliases={5: 0},
        compiler_params=_cp(("arbitrary",)),
    )(u, u, dy, dy, cw, du)


HG_T = 128
TM_HG = TM_WIDE
HG_HALF = HG_T // 2


def _hg_chunk_fwd(blk, lb, valid, tri):
    bq, bf, v = blk[:, 0:512], blk[:, 512:1024], blk[:, 1024:1536]
    sgq = _sig(bq)
    qt = bq * sgq
    sz = _sig(bf)
    f = lb + (1.0 - lb) * sz
    g = jnp.where(valid, jnp.log(jnp.maximum(f, F_FLOOR)), 0.0)
    k = jnp.where(valid, (1.0 - lb) * (1.0 - sz), 0.0)
    b = jnp.dot(tri, g, precision=_HI, preferred_element_type=F32)
    ridx = _iota((HG_T, 1), 0)
    pick = lambda r: jnp.sum(jnp.where(ridx == r, b, 0.0), axis=0, keepdims=True)
    return dict(bq=bq, sgq=sgq, qt=qt, sz=sz, f=f, k=k, v=v, b=b, top=ridx < HG_HALF, rx=pick(HG_HALF - 1),
                rdt=pick(HG_HALF // 2 - 1), rdb=pick(HG_HALF + HG_HALF // 2 - 1), bl=pick(HG_T - 1))


def _hg_head(p, sl):
    top, b, qt, k = p["top"], p["b"][:, sl], p["qt"][:, sl], p["k"][:, sl]
    rx, bl = p["rx"][:, sl], p["bl"][:, sl]
    rd = jnp.where(top, p["rdt"][:, sl], p["rdb"][:, sl])
    eqx = jnp.where(top, 0.0, jnp.exp(jnp.minimum(b - rx, 0.0)))
    ekx = jnp.where(top, jnp.exp(jnp.minimum(rx - b, 0.0)), 0.0)
    eqd = jnp.exp(jnp.minimum(b - rd, EXP_CLAMP))
    ekd = jnp.exp(jnp.minimum(rd - b, EXP_CLAMP))
    e = jnp.exp(b)
    ekl = jnp.exp(bl - b)
    qx, kx, qd, kd = _rnd(qt * eqx), _rnd(k * ekx), _rnd(qt * eqd), _rnd(k * ekd)
    qcat = jnp.concatenate([qx, jnp.where(top, qd, 0.0), jnp.where(top, 0.0, qd)], axis=1)
    kcat = jnp.concatenate([kx, jnp.where(top, kd, 0.0), jnp.where(top, 0.0, kd)], axis=1)
    return dict(v=p["v"][:, sl], eqx=eqx, ekx=ekx, eqd=eqd, ekd=ekd, e=e, ekl=ekl, el=jnp.exp(bl), qx=qx, kx=kx,
                qd=qd, kd=kd, qe=qt * e, kl=k * ekl, qcat=qcat, kcat=kcat)


def _hgrn_fwd_body(tm):
    cpb = tm // HG_T

    def body(u_ref, lb_ref, gg_ref, y_ref, st_ref, s_ref):
        i = pl.program_id(0)

        @pl.when(i == 0)
        def _():
            s_ref[...] = jnp.zeros_like(s_ref)

        lbv = lb_ref[...]
        ggv = gg_ref[...]
        tri = (_iota((HG_T, HG_T), 0) >= _iota((HG_T, HG_T), 1)).astype(F32)

        def chunk(c, carry):
            r0 = pl.multiple_of(c * HG_T, HG_T)
            blk = u_ref[pl.ds(r0, HG_T), :].astype(F32)
            valid = (i * tm + r0 + _iota((HG_T, 1), 0)) >= META_PAD
            q = _hg_chunk_fwd(blk, lbv, valid, tri)
            outs = []
            for hh in range(HG_HEADS):
                h = _hg_head(q, slice(hh * HG_D, (hh + 1) * HG_D))
                a = jnp.where(tri > 0, _dot_nt(h["qcat"], h["kcat"]), 0.0)
                st = s_ref[hh]
                st_ref[c, hh] = st
                o = _dot(a, h["v"]) + _dot_nt(h["qe"], st)
                s_ref[hh] = st * h["el"] + _dot_tn(h["v"], h["kl"])
                rs = lax.rsqrt(jnp.mean(o * o, axis=-1, keepdims=True) + EPS)
                outs.append(o * rs * ggv)
            on = jnp.concatenate(outs, axis=1)
            bg = blk[:, 1536:2048]
            y_ref[pl.ds(r0, HG_T), :] = (on * bg * _sig(bg)).astype(MXU_DTYPE)
            return carry

        lax.fori_loop(0, cpb, chunk, 0, unroll=True)

    return body


def _conv_hgrn_fwd(u, cw, cb, lg, lb_, hlb, gg):
    lp = u.shape[0]
    tm = TM_WIDE
    assert lp % tm == 0
    cpb = tm // HG_T
    conv_body, hgrn_body = _conv_fwd_body(tm), _hgrn_fwd_body(tm)

    def body(a_ref, ah_ref, ag_ref, w_ref, b_ref, lg_ref, lb_ref, ub_ref, hlb_ref, gg_ref,
             ya_ref, y_ref, yb_ref, st_ref, ext_ref, sh_ref, s_ref):
        hgrn_body(ub_ref, hlb_ref, gg_ref, yb_ref, st_ref, s_ref)
        conv_body(a_ref, ah_ref, ag_ref, w_ref, b_ref, lg_ref, lb_ref, ya_ref, y_ref, ext_ref, sh_ref)

    rowspec = pl.BlockSpec((tm, D_CONV), lambda i: (i, 0))
    return pl.pallas_call(
        body, name="conv_hgrn_fwd", grid=(lp // tm,),
        in_specs=_conv_specs(tm) + [_full_spec((CONV_WIDTH, D_CONV))] + [_full_spec((1, D_CONV))] * 3
        + [pl.BlockSpec((tm, W_B), lambda i: (i, 0)), _full_spec((1, D_HG)), _full_spec((1, HG_D))],
        out_specs=[rowspec, rowspec, pl.BlockSpec((tm, D_HG), lambda i: (i, 0)),
                   pl.BlockSpec((cpb, HG_HEADS, HG_D, HG_D), lambda i: (i, 0, 0, 0))],
        out_shape=[jax.ShapeDtypeStruct((lp, D_CONV), MXU_DTYPE), jax.ShapeDtypeStruct((lp, D_CONV), F32),
                   jax.ShapeDtypeStruct((lp, D_HG), MXU_DTYPE),
                   jax.ShapeDtypeStruct((lp // HG_T, HG_HEADS, HG_D, HG_D), F32)],
        scratch_shapes=[pltpu.VMEM((HALO + tm, D_CONV), F32), _shift_scratch(tm),
                        pltpu.VMEM((HG_HEADS, HG_D, HG_D), F32)],
        compiler_params=_cp(("arbitrary",)),
    )(u, u, u, cw, cb, lg, lb_, u, hlb, gg)


def _hgrn_bwd(u, dyb, states, du, lb, gg):
    lp = u.shape[0]
    tm = TM_HG
    cpb = tm // HG_T
    nb = lp // tm

    def body(u_ref, dy_ref, st_ref, lb_ref, gg_ref, du_in, du_ref, dlb_ref, dgg_ref, ds_ref):
        del du_in
        ii = pl.program_id(0)
        i = nb - 1 - ii

        @pl.when(ii == 0)
        def _():
            ds_ref[...] = jnp.zeros_like(ds_ref)
            dlb_ref[...] = jnp.zeros_like(dlb_ref)
            dgg_ref[...] = jnp.zeros_like(dgg_ref)

        lbv = lb_ref[...]
        ggv = gg_ref[...]
        lower = _iota((HG_T, HG_T), 0) >= _iota((HG_T, HG_T), 1)
        tri = lower.astype(F32)
        triu = (_iota((HG_T, HG_T), 0) <= _iota((HG_T, HG_T), 1)).astype(F32)
        ridx = _iota((HG_T, 1), 0)

        def chunk(cc, carry):
            c = cpb - 1 - cc
            r0 = pl.multiple_of(c * HG_T, HG_T)
            blk = u_ref[pl.ds(r0, HG_T), :].astype(F32)
            valid = (i * tm + r0 + _iota((HG_T, 1), 0)) >= META_PAD
            q = _hg_chunk_fwd(blk, lbv, valid, tri)
            top = q["top"]
            bg = blk[:, 1536:2048]
            sg = _sig(bg)
            dy = dy_ref[pl.ds(r0, HG_T), :].astype(F32)
            don_all = dy * bg * sg
            dqt_l, dk_l, dv_l, db_l, dbl_l, on_l = [], [], [], [], [], []
            dgg = jnp.zeros((1, HG_D), F32)
            for hh in range(HG_HEADS):
                sl = slice(hh * HG_D, (hh + 1) * HG_D)
                h = _hg_head(q, sl)
                qe, kl, v, el, qcat, kcat = h["qe"], h["kl"], h["v"], h["el"], h["qcat"], h["kcat"]
                a = jnp.where(lower, _dot_nt(qcat, kcat), 0.0)
                st = st_ref[c, hh]
                o = _dot(a, v) + _dot_nt(qe, st)
                rs = lax.rsqrt(jnp.mean(o * o, axis=-1, keepdims=True) + EPS)
                xh = o * rs
                on_l.append(xh * ggv)
                don = don_all[:, sl]
                dgg = dgg + jnp.sum(don * xh, axis=0, keepdims=True)
                dxh = don * ggv
                do = rs * (dxh - xh * jnp.mean(dxh * xh, axis=-1, keepdims=True))
                dst = ds_ref[hh]
                dv = _dot_tn(a, do) + _dot_nt(kl, dst)
                da = jnp.where(lower, _dot_nt(do, v), 0.0)
                dqe = _dot(do, st)
                dkl = _dot(v, dst)
                d_el = jnp.sum(st * dst, axis=0, keepdims=True)
                ds_ref[hh] = _dot_tn(do, qe) + dst * el
                dqc = _dot(da, kcat)
                dkc = _dot_tn(da, qcat)
                dqx, dqd = dqc[:, :HG_D], jnp.where(top, dqc[:, HG_D:2 * HG_D], dqc[:, 2 * HG_D:])
                dkx, dkd = dkc[:, :HG_D], jnp.where(top, dkc[:, HG_D:2 * HG_D], dkc[:, 2 * HG_D:])
                dqt_l.append(dqx * h["eqx"] + dqd * h["eqd"] + dqe * h["e"])
                dk_l.append(dkx * h["ekx"] + dkd * h["ekd"] + dkl * h["ekl"])
                dv_l.append(dv)
                db_l.append(dqx * h["qx"] - dkx * h["kx"] + dqd * h["qd"] - dkd * h["kd"] + dqe * qe - dkl * kl)
                dbl_l.append(jnp.sum(dkl * kl, axis=0, keepdims=True) + d_el * el)
            dqt = jnp.concatenate(dqt_l, axis=1)
            dk = jnp.concatenate(dk_l, axis=1)
            dv = jnp.concatenate(dv_l, axis=1)
            db = jnp.concatenate(db_l, axis=1) + jnp.where(ridx == HG_T - 1, jnp.concatenate(dbl_l, axis=1), 0.0)
            on = jnp.concatenate(on_l, axis=1)
            dg = jnp.dot(triu, db, precision=_HI, preferred_element_type=F32)
            sz, f = q["sz"], q["f"]
            df = jnp.where(valid & (f > F_FLOOR), dg / f, 0.0)
            dkv = jnp.where(valid, dk, 0.0)
            t = (1.0 - sz) * (df - dkv)
            dlb_ref[...] += jnp.sum(t, axis=0, keepdims=True)
            dz = (1.0 - lbv) * (df - dkv) * sz * (1.0 - sz)
            dbq = dqt * _dsilu(q["bq"], q["sgq"])
            dbg = dy * on * _dsilu(bg, sg)
            dgg_ref[...] += dgg
            du_ref[pl.ds(r0, HG_T), :] = jnp.concatenate([dbq, dz, dv, dbg], axis=1).astype(MXU_DTYPE)
            return carry

        lax.fori_loop(0, cpb, chunk, 0, unroll=True)

    return pl.pallas_call(
        body, name="hgrn_bwd", grid=(nb,),
        in_specs=[pl.BlockSpec((tm, W_B), lambda ii: (nb - 1 - ii, 0)), pl.BlockSpec((tm, D_HG), lambda ii: (nb - 1 - ii, 0)),
                  pl.BlockSpec((cpb, HG_HEADS, HG_D, HG_D), lambda ii: (nb - 1 - ii, 0, 0, 0)),
                  _full_spec((1, D_HG)), _full_spec((1, HG_D)), pl.BlockSpec(memory_space=pl.ANY)],
        out_specs=[pl.BlockSpec((tm, W_B), lambda ii: (nb - 1 - ii, 0)), _full_spec((1, D_HG)), _full_spec((1, HG_D))],
        out_shape=[jax.ShapeDtypeStruct(du.shape, du.dtype), jax.ShapeDtypeStruct((1, D_HG), F32),
                   jax.ShapeDtypeStruct((1, HG_D), F32)],
        scratch_shapes=[pltpu.VMEM((HG_HEADS, HG_D, HG_D), F32)],
        input_output_aliases={5: 0},
        compiler_params=_cp(("arbitrary",)),
    )(u, dyb, states, lb, gg, du)


TM_SWA = 256
PREV_ROWS = WINDOW_CHUNKS * CHUNK
DEAD_ROWS = -(CHUNK + PREV_ROWS + TM_SWA) % 128
N_KEYS = CHUNK + DEAD_ROWS + PREV_ROWS + TM_SWA
LOG2E = 1.4426950408889634
ATT_SCALE2 = ATT_HEAD_DIM ** -0.5 * LOG2E
NEG = -1e30


def _half_sum(x, lo):
    a = jnp.sum(jnp.where(lo, x, 0.0), axis=1, keepdims=True)
    b = jnp.sum(jnp.where(lo, 0.0, x), axis=1, keepdims=True)
    return jnp.where(lo, a, b)


def _half_rms(x, lo):
    return lax.rsqrt(_half_sum(x * x, lo) * (1.0 / ATT_HEAD_DIM) + EPS)


def _swa_mask(i, tm):
    tq = i * tm + _iota((tm, N_KEYS), 0)
    s = _iota((tm, N_KEYS), 1)
    nq = tq >> 6
    kr = i * tm + s - (N_KEYS - tm)
    kc = kr >> 6
    band = (kr >= META_PAD) & (kc >= nq - WINDOW_CHUNKS) & (kc <= nq)
    meta = (nq > WINDOW_CHUNKS) & (s >= META_PAD)
    return ((s < CHUNK) & meta) | ((s >= CHUNK) & band)


def _swa_keys(own_kv, prev_ref, meta_ref, kg, tm):
    kv = jnp.concatenate([meta_ref[...].astype(F32), jnp.zeros((DEAD_ROWS, 2 * D_KV), F32),
                          prev_ref[tm - PREV_ROWS:tm, :].astype(F32), own_kv], axis=0)
    k_raw, v = kv[:, :D_KV], kv[:, D_KV:]
    lo = _iota((1, D_KV), 1) < ATT_HEAD_DIM
    kr = _half_rms(k_raw, lo)
    kn = k_raw * kr * kg
    return k_raw, kr, kn, v, lo


def _placed(x, lo):
    xr = pltpu.roll(x, ATT_HEAD_DIM, 1)
    z = jnp.zeros_like(x)
    return [[jnp.where(lo, x, z).astype(MXU_DTYPE), jnp.where(lo, z, xr).astype(MXU_DTYPE)],
            [jnp.where(lo, xr, z).astype(MXU_DTYPE), jnp.where(lo, z, x).astype(MXU_DTYPE)]]


def _swa_specs(tm, order):
    kvb = (OFF_C + 1024) // 256
    return [pl.BlockSpec((tm, W_C), lambda i: (order(i), OFF_C // W_C)),
            pl.BlockSpec((tm, 256), lambda i: (jnp.maximum(order(i) - 1, 0), kvb)),
            pl.BlockSpec((CHUNK, 256), lambda i: (0, kvb)),
            _full_spec((1, D_KV)), _full_spec((1, D_KV)), pl.BlockSpec(memory_space=pltpu.SMEM)]


def _swa_fwd(u, qg, kg, sinks):
    lp = u.shape[0]
    tm = TM_SWA

    def body(own_ref, prev_ref, meta_ref, qg_ref, kg_ref, sink_ref, y_ref):
        i = pl.program_id(0)
        own = own_ref[...].astype(F32)
        _, _, kn, v, lo = _swa_keys(own[:, 1024:1280], prev_ref, meta_ref, kg_ref[...], tm)
        kuse, vuse = _placed(kn, lo), _placed(v, lo)
        bias = jnp.where(_swa_mask(i, tm), 0.0, NEG)
        for gi in range(ATT_Q_HEADS // 2):
            j = gi // 2
            sl = slice(gi * 128, (gi + 1) * 128)
            qraw = own[:, sl]
            qs = qraw * _half_rms(qraw, lo) * (qg_ref[...] * ATT_SCALE2)
            og = jnp.zeros((tm, 128), F32)
            for e in range(2):
                qm = jnp.where(lo if e == 0 else ~lo, qs, 0.0)
                s = _dot_nt(qm, kuse[j][e]) + bias
                sk = sink_ref[2 * gi + e] * LOG2E
                m = jnp.maximum(jnp.max(s, axis=-1, keepdims=True), sk)
                p = jnp.exp2(s - m)
                inv = 1.0 / (jnp.sum(p, axis=-1, keepdims=True) + jnp.exp2(sk - m))
                og = og + _dot(p, vuse[j][e]) * inv
            gt = own[:, 512 + gi * 128:512 + (gi + 1) * 128]
            y_ref[:, sl] = (og * gt * _sig(gt)).astype(MXU_DTYPE)

    return pl.pallas_call(
        body, name="swa_fwd", grid=(lp // tm,),
        in_specs=_swa_specs(tm, lambda i: i),
        out_specs=pl.BlockSpec((tm, D_ATT), lambda i: (i, 0)),
        out_shape=jax.ShapeDtypeStruct((lp, D_ATT), MXU_DTYPE),
        compiler_params=_cp(("arbitrary",)),
    )(u, u, u, qg, kg, sinks)


def _swa_bwd(u, dyc, du, qg, kg, sinks):
    lp = u.shape[0]
    tm = TM_SWA
    nb = lp // tm
    order = lambda ii: nb - 1 - ii

    def body(own_ref, prev_ref, meta_ref, qg_ref, kg_ref, sink_ref, dy_ref, du_in, du_ref, dqg_ref, dkg_ref, dsk_ref,
             carry_ref, macc_ref):
        del du_in
        ii = pl.program_id(0)
        i = nb - 1 - ii

        @pl.when(ii == 0)
        def _():
            carry_ref[...] = jnp.zeros_like(carry_ref)
            macc_ref[...] = jnp.zeros_like(macc_ref)
            dqg_ref[...] = jnp.zeros_like(dqg_ref)
            dkg_ref[...] = jnp.zeros_like(dkg_ref)
            dsk_ref[...] = jnp.zeros_like(dsk_ref)

        own = own_ref[...].astype(F32)
        k_raw, krs, kn, v, lo = _swa_keys(own[:, 1024:1280], prev_ref, meta_ref, kg_ref[...], tm)
        kuse, vuse = _placed(kn, lo), _placed(v, lo)
        bias = jnp.where(_swa_mask(i, tm), 0.0, NEG)
        dkn_t = jnp.zeros((D_KV, N_KEYS), F32)
        dvn_t = jnp.zeros((D_KV, N_KEYS), F32)
        for gi in range(ATT_Q_HEADS // 2):
            j = gi // 2
            sl = slice(gi * 128, (gi + 1) * 128)
            qraw = own[:, sl]
            qr = _half_rms(qraw, lo)
            qxh = qraw * qr
            qs = qxh * (qg_ref[...] * ATT_SCALE2)
            ps, invs, pk, qms = [], [], [], []
            og = jnp.zeros((tm, 128), F32)
            for e in range(2):
                qm = jnp.where(lo if e == 0 else ~lo, qs, 0.0)
                s = _dot_nt(qm, kuse[j][e]) + bias
                sk = sink_ref[2 * gi + e] * LOG2E
                m = jnp.maximum(jnp.max(s, axis=-1, keepdims=True), sk)
                p = jnp.exp2(s - m)
                inv = 1.0 / (jnp.sum(p, axis=-1, keepdims=True) + jnp.exp2(sk - m))
                ps.append(p)
                invs.append(inv)
                pk.append(jnp.exp2(sk - m) * inv)
                qms.append(qm)
                og = og + _dot(p, vuse[j][e]) * inv
            gt = own[:, 512 + gi * 128:512 + (gi + 1) * 128]
            sg = _sig(gt)
            dy = dy_ref[:, sl].astype(F32)
            dgt = dy * og * _dsilu(gt, sg)
            dog = dy * gt * sg
            dqn = jnp.zeros((tm, 128), F32)
            for e in range(2):
                half = lo if e == 0 else ~lo
                dog_m = jnp.where(half, dog, 0.0)
                dl = jnp.sum(dog_m * og, axis=1, keepdims=True)
                dp = _dot_nt(dog_m, vuse[j][e])
                ds = ps[e] * ((dp - dl) * (invs[e] * (1.0 / LOG2E)))
                hsk = 2 * gi + e
                dsk_ref[hsk:hsk + 1, :] += jnp.zeros((1, 128), F32) - jnp.sum(pk[e] * dl, axis=0, keepdims=True)
                dqn = dqn + _dot(ds, kuse[j][e])
                dk_e = _dot_tn(qms[e], ds)
                dv_e = _dot_tn(dog_m * invs[e], ps[e])
                if j != e:
                    dk_e = pltpu.roll(dk_e, ATT_HEAD_DIM, 0)
                    dv_e = pltpu.roll(dv_e, ATT_HEAD_DIM, 0)
                dkn_t = dkn_t + dk_e
                dvn_t = dvn_t + dv_e
            dqn = dqn * ATT_SCALE2
            dqg_ref[...] += jnp.sum(dqn * qxh, axis=0, keepdims=True)
            dqx = dqn * qg_ref[...]
            dq = qr * (dqx - qxh * _half_sum(dqx * qxh, lo) * (1.0 / ATT_HEAD_DIM))
            du_ref[:, sl] = dq.astype(MXU_DTYPE)
            du_ref[:, 512 + gi * 128:512 + (gi + 1) * 128] = dgt.astype(MXU_DTYPE)

        dkn, dvn = dkn_t.T, dvn_t.T
        macc_ref[...] += jnp.concatenate([dkn[0:CHUNK], dvn[0:CHUNK]], axis=1)
        own0 = N_KEYS - tm
        tot = jnp.concatenate([dkn[own0:], dvn[own0:]], axis=1) + carry_ref[...]
        if tm > PREV_ROWS:
            carry_ref[0:tm - PREV_ROWS, :] = jnp.zeros((tm - PREV_ROWS, 2 * D_KV), F32)
        prev0 = own0 - PREV_ROWS
        carry_ref[tm - PREV_ROWS:tm, :] = jnp.concatenate([dkn[prev0:own0], dvn[prev0:own0]], axis=1)
        first = jnp.where((i == 0) & (_iota((tm, 1), 0) < CHUNK), 1.0, 0.0)
        tot = tot + first * jnp.concatenate([macc_ref[...], jnp.zeros((tm - CHUNK, 2 * D_KV), F32)], axis=0)
        dkn_own, dv_own = tot[:, :D_KV], tot[:, D_KV:]
        kx = k_raw[own0:] * krs[own0:]
        dkg_ref[...] += jnp.sum(dkn_own * kx, axis=0, keepdims=True)
        dkx = dkn_own * kg_ref[...]
        dk = krs[own0:] * (dkx - kx * _half_sum(dkx * kx, lo) * (1.0 / ATT_HEAD_DIM))
        du_ref[:, 1024:1152] = dk.astype(MXU_DTYPE)
        du_ref[:, 1152:1280] = dv_own.astype(MXU_DTYPE)
        du_ref[:, 1280:W_C] = jnp.zeros((tm, W_C - 1280), MXU_DTYPE)

    return pl.pallas_call(
        body, name="swa_bwd", grid=(nb,),
        in_specs=_swa_specs(tm, order) + [pl.BlockSpec((tm, D_ATT), lambda ii: (order(ii), 0)),
                                          pl.BlockSpec(memory_space=pl.ANY)],
        out_specs=[pl.BlockSpec((tm, W_C), lambda ii: (order(ii), OFF_C // W_C)), _full_spec((1, 128)),
                   _full_spec((1, 128)), _full_spec((ATT_Q_HEADS, 128))],
        out_shape=[jax.ShapeDtypeStruct(du.shape, du.dtype), jax.ShapeDtypeStruct((1, 128), F32),
                   jax.ShapeDtypeStruct((1, 128), F32), jax.ShapeDtypeStruct((ATT_Q_HEADS, 128), F32)],
        scratch_shapes=[pltpu.VMEM((tm, 2 * D_KV), F32), pltpu.VMEM((CHUNK, 2 * D_KV), F32)],
        input_output_aliases={7: 0},
        compiler_params=_cp(("arbitrary",)),
    )(u, u, u, qg, kg, sinks, dyc, du)


def _load_once(pairs, first):
    @pl.when(first)
    def _():
        for src, dst in pairs:
            pltpu.sync_copy(src, dst)


def _mix_fwd(h, u, ya, yb, yc, wa, wb, wc, wo):
    lp = h.shape[0]
    tm = TM_MIX
    assert lp % tm == 0

    def body(h_ref, g_ref, ya_ref, yb_ref, yc_ref, wa_hbm, wb_hbm, wc_hbm, wo_hbm, out_ref, wa_ref, wb_ref, wc_ref,
             wo_ref):
        _load_once(((wa_hbm, wa_ref), (wb_hbm, wb_ref), (wc_hbm, wc_ref), (wo_hbm, wo_ref)), pl.program_id(0) == 0)
        mixed = jnp.zeros((tm, D_MODEL), F32)
        for n, (y_ref, w_ref) in enumerate(((ya_ref, wa_ref), (yb_ref, wb_ref), (yc_ref, wc_ref))):
            z = jnp.dot(y_ref[...], w_ref[...], preferred_element_type=F32)
            mixed = mixed + _sig(g_ref[:, n * D_MODEL:(n + 1) * D_MODEL].astype(F32)) * z
        out_ref[...] = h_ref[...] + _dot(mixed, wo_ref[...])

    ybs = pl.BlockSpec((tm, 512), lambda i: (i, 0))
    anyspec = pl.BlockSpec(memory_space=pl.ANY)
    return pl.pallas_call(
        body, name="mix_fwd", grid=(lp // tm,),
        in_specs=[pl.BlockSpec((tm, D_MODEL), lambda i: (i, 0)), pl.BlockSpec((tm, W_G), lambda i: (i, OFF_G // W_G)),
                  ybs, ybs, ybs, anyspec, anyspec, anyspec, anyspec],
        out_specs=pl.BlockSpec((tm, D_MODEL), lambda i: (i, 0)),
        out_shape=jax.ShapeDtypeStruct((lp, D_MODEL), F32),
        scratch_shapes=[pltpu.VMEM((512, D_MODEL), MXU_DTYPE)] * 3 + [pltpu.VMEM((D_MODEL, D_MODEL), MXU_DTYPE)],
        compiler_params=_cp(("arbitrary",)),
    )(h, u, ya, yb, yc, wa, wb, wc, wo)


def _mix_bwd(dh, u, ya, yb, yc, wa, wb, wc, wo):
    lp = dh.shape[0]
    tm = TM_BR
    nb = lp // tm

    def body(dh_ref, g_ref, ya_ref, yb_ref, yc_ref, wa_hbm, wb_hbm, wc_hbm, wo_hbm,
             du_ref, dya_ref, dyb_ref, dyc_ref, dwa_hbm, dwb_hbm, dwc_hbm, dwo_hbm,
             wa_ref, wb_ref, wc_ref, wo_ref, dwa_ref, dwb_ref, dwc_ref, dwo_ref):
        i = pl.program_id(0)
        _load_once(((wa_hbm, wa_ref), (wb_hbm, wb_ref), (wc_hbm, wc_ref), (wo_hbm, wo_ref)), i == 0)

        @pl.when(i == 0)
        def _():
            for r in (dwa_ref, dwb_ref, dwc_ref, dwo_ref):
                r[...] = jnp.zeros_like(r)

        dh_b = dh_ref[...].astype(MXU_DTYPE)
        dmixed = _dot_nt(dh_b, wo_ref[...])
        mixed = jnp.zeros((tm, D_MODEL), F32)
        for n, (y_ref, w_ref, dy_ref, dw_ref) in enumerate(((ya_ref, wa_ref, dya_ref, dwa_ref),
                                                            (yb_ref, wb_ref, dyb_ref, dwb_ref),
                                                            (yc_ref, wc_ref, dyc_ref, dwc_ref))):
            y = y_ref[...]
            z = jnp.dot(y, w_ref[...], preferred_element_type=F32)
            gate = _sig(g_ref[:, n * D_MODEL:(n + 1) * D_MODEL].astype(F32))
            mixed = mixed + gate * z
            du_ref[:, n * D_MODEL:(n + 1) * D_MODEL] = (z * dmixed * gate * (1.0 - gate)).astype(MXU_DTYPE)
            dz = (gate * dmixed).astype(MXU_DTYPE)
            dy_ref[...] = _dot_nt(dz, w_ref[...]).astype(MXU_DTYPE)
            dw_ref[...] += _dot_tn(y, dz)
        dwo_ref[...] += _dot_tn(mixed, dh_b)

        @pl.when(i == nb - 1)
        def _():
            for src, dst in ((dwa_ref, dwa_hbm), (dwb_ref, dwb_hbm), (dwc_ref, dwc_hbm), (dwo_ref, dwo_hbm)):
                pltpu.sync_copy(src, dst)

    ybs = pl.BlockSpec((tm, 512), lambda i: (i, 0))
    anyspec = pl.BlockSpec(memory_space=pl.ANY)
    wsh = jax.ShapeDtypeStruct((512, D_MODEL), F32)
    return pl.pallas_call(
        body, name="mix_bwd", grid=(nb,),
        in_specs=[pl.BlockSpec((tm, D_MODEL), lambda i: (i, 0)), pl.BlockSpec((tm, W_G), lambda i: (i, OFF_G // W_G)),
                  ybs, ybs, ybs, anyspec, anyspec, anyspec, anyspec],
        out_specs=[pl.BlockSpec((tm, W_G), lambda i: (i, OFF_G // W_G)), ybs, ybs, ybs, anyspec, anyspec, anyspec, anyspec],
        out_shape=[jax.ShapeDtypeStruct((lp, NP), MXU_DTYPE)] + [jax.ShapeDtypeStruct((lp, 512), MXU_DTYPE)] * 3
        + [wsh, wsh, wsh, jax.ShapeDtypeStruct((D_MODEL, D_MODEL), F32)],
        scratch_shapes=[pltpu.VMEM((512, D_MODEL), MXU_DTYPE)] * 3 + [pltpu.VMEM((D_MODEL, D_MODEL), MXU_DTYPE)]
        + [pltpu.VMEM((512, D_MODEL), F32)] * 3 + [pltpu.VMEM((D_MODEL, D_MODEL), F32)],
        compiler_params=_cp(("arbitrary",)),
    )(dh, u, ya, yb, yc, wa, wb, wc, wo)


def _loss_head(h, target_p, seq):
    lp = h.shape[0]
    tm = TM_MIX
    assert lp % tm == 0

    def body(h_ref, t_ref, dh_ref, loss_ref):
        i = pl.program_id(0)

        @pl.when(i == 0)
        def _():
            loss_ref[...] = jnp.zeros_like(loss_ref)

        rows = i * tm + _iota((tm, 1), 0)
        e = jnp.where((rows >= CHUNK) & (rows < CHUNK + seq), h_ref[...] - t_ref[...], 0.0)
        dh_ref[...] = e * (1.0 / D_MODEL)
        part = jnp.sum(jnp.mean(e * e, axis=-1, keepdims=True), axis=0, keepdims=True)
        loss_ref[...] += 0.5 * part

    return pl.pallas_call(
        body, name="loss_head", grid=(lp // tm,),
        in_specs=[pl.BlockSpec((tm, D_MODEL), lambda i: (i, 0))] * 2,
        out_specs=[pl.BlockSpec((tm, D_MODEL), lambda i: (i, 0)), _full_spec((1, 128))],
        out_shape=[jax.ShapeDtypeStruct((lp, D_MODEL), F32), jax.ShapeDtypeStruct((1, 128), F32)],
        compiler_params=_cp(("arbitrary",)),
    )(h, target_p)


def _lb_rows(p_ref):
    depth = p_ref.shape[0]
    rows = [p_ref[l:l + 1, :] for l in range(depth)]
    mx = functools.reduce(jnp.maximum, rows)
    ex = [jnp.exp(r - mx) for r in rows]
    tot = functools.reduce(jnp.add, ex)
    sm = [e / tot for e in ex]
    cs, run = [], jnp.zeros_like(sm[0])
    for l in range(depth):
        run = run + sm[l]
        cs.append(run)
    return sm, [c - sm[0] for c in cs]


def _lb_fwd(p):
    def body(p_ref, o_ref):
        _, xs = _lb_rows(p_ref)
        for l, xl in enumerate(xs):
            o_ref[l:l + 1, :] = jnp.clip(xl, 0.0, 1.0)

    return pl.pallas_call(body, name="lb_fwd", out_shape=jax.ShapeDtypeStruct(p.shape, F32))(p)


def _lb_bwd(p, dlb):
    def body(p_ref, d_ref, o_ref):
        sm, xs = _lb_rows(p_ref)
        depth = len(xs)
        dx = []
        for l in range(depth):
            x = xs[l]
            g0 = jnp.where(x > 0.0, 1.0, jnp.where(x == 0.0, 0.5, 0.0))
            y = jnp.maximum(x, 0.0)
            g1 = jnp.where(y < 1.0, 1.0, jnp.where(y == 1.0, 0.5, 0.0))
            dx.append(d_ref[l:l + 1, :] * g0 * g1)
        dsm = [functools.reduce(jnp.add, dx[jj:]) for jj in range(depth)]
        dsm[0] = dsm[0] - functools.reduce(jnp.add, dx)
        inner = functools.reduce(jnp.add, [a * b for a, b in zip(sm, dsm)])
        for l in range(depth):
            o_ref[l:l + 1, :] = sm[l] * (dsm[l] - inner)

    return pl.pallas_call(body, name="lb_bwd", out_shape=jax.ShapeDtypeStruct(p.shape, F32))(p, dlb)


def _exchange(gather, scatter, name):
    ng, ns = len(gather), len(scatter)
    n = ng + ns

    def body(*refs):
        x_refs, o_refs, sems = refs[:n], refs[n:2 * n], refs[2 * n:]
        exs = []
        if ng:
            exs.append(_Exchange(x_refs[:ng], o_refs[:ng], *sems[:3], scatter=False))
        if ns:
            exs.append(_Exchange(x_refs[ng:], o_refs[ng:], *sems[-3:], scatter=True))
        for ex in exs:
            ex.start()
        for ex in exs:
            ex.finish()

    out_shape = [jax.ShapeDtypeStruct((N_DEV,) + x.shape, x.dtype) for x in gather]
    out_shape += [jax.ShapeDtypeStruct(x.shape, x.dtype) for x in scatter]
    return pl.pallas_call(
        body, name=name, in_specs=[_ANY] * n, out_specs=[_ANY] * n, out_shape=out_shape,
        scratch_shapes=(_exchange_sems(ng) if ng else []) + (_exchange_sems(ns) if ns else []),
        compiler_params=pltpu.CompilerParams(has_side_effects=True),
    )(*gather, *scatter)


def _gather_two_level(xs, name):
    n = len(xs)

    def body(*refs):
        x_refs, o_refs = refs[:n], refs[n:2 * n]
        send_sems, recv_sems, loc_sems = refs[2 * n:]
        x, y, c = lax.axis_index("x"), lax.axis_index("y"), lax.axis_index("c")
        chips = [(1 - x, y), (x, 1 - y), (1 - x, 1 - y)]
        idx = lambda px, py, pc: 4 * px + 2 * py + pc

        def copy(a, k, block, to, src=None):
            slot = o_refs[a].at[idx(*block)]
            return pltpu.make_async_remote_copy(src_ref=slot if src is None else src, dst_ref=slot,
                                                send_sem=send_sems.at[a, k], recv_sem=recv_sems.at[a, k],
                                                device_id=to, device_id_type=pl.DeviceIdType.MESH)

        me, sib = (x, y, c), (x, y, 1 - c)
        local = [pltpu.make_async_copy(x_refs[a], o_refs[a].at[idx(*me)], loc_sems.at[a]) for a in range(n)]
        first = [copy(a, 0, me, sib, src=x_refs[a]) for a in range(n)]
        first += [copy(a, 1 + j, me, (*chip, c), src=x_refs[a]) for j, chip in enumerate(chips) for a in range(n)]
        for cp in local + first:
            cp.start()
        passed = []
        for j, chip in enumerate(chips):
            for a in range(n):
                copy(a, 1 + j, (*chip, c), me).wait_recv()
                cp = copy(a, 4 + j, (*chip, c), sib)
                cp.start()
                passed.append(cp)
        for a in range(n):
            copy(a, 0, sib, me).wait_recv()
            for j, chip in enumerate(chips):
                copy(a, 4 + j, (*chip, 1 - c), me).wait_recv()
        for cp in first + passed:
            cp.wait_send()
        for cp in local:
            cp.wait()

    return pl.pallas_call(
        body, name=name, in_specs=[_ANY] * n, out_specs=[_ANY] * n,
        out_shape=[jax.ShapeDtypeStruct((N_DEV,) + x.shape, x.dtype) for x in xs],
        scratch_shapes=_exchange_sems(n), compiler_params=pltpu.CompilerParams(has_side_effects=True),
    )(*xs)


def _adamw(gp, w, m, v, name):
    r, cc = w.shape
    tr = 256 if r % 256 == 0 else r

    def body(g_ref, w_ref, m_ref, v_ref, go_ref, d_ref, mo_ref, vo_ref):
        g = g_ref[0].astype(F32)
        for s in range(1, N_DEV):
            g = g + g_ref[s].astype(F32)
        go_ref[...] = g
        mn = ADAM_B1 * m_ref[...] + (1.0 - ADAM_B1) * g
        vn = ADAM_B2 * v_ref[...] + (1.0 - ADAM_B2) * (g * g)
        m_hat = mn / (1.0 - ADAM_B1 ** ADAM_STEP)
        v_hat = vn / (1.0 - ADAM_B2 ** ADAM_STEP)
        d_ref[...] = -ADAM_LR * (m_hat / (jnp.sqrt(v_hat) + ADAM_EPS) + ADAM_WD * w_ref[...])
        mo_ref[...] = mn
        vo_ref[...] = vn

    bs = pl.BlockSpec((tr, cc), lambda i: (i, 0))
    sh = jax.ShapeDtypeStruct((r, cc), F32)
    return pl.pallas_call(
        body, name=name, grid=(r // tr,),
        in_specs=[pl.BlockSpec((N_DEV, tr, cc), lambda i: (0, i, 0)), bs, bs, bs],
        out_specs=[bs, bs, bs, bs], out_shape=[sh, sh, sh, sh],
        compiler_params=_cp(("parallel",)),
    )(gp, w, m, v)


def _pack_cols(w):
    parts, pos = [], 0
    for pstart, ostart, width in _PACK:
        if pstart != pos:
            parts.append(jnp.zeros(w.shape[:-1] + (pstart - pos,), w.dtype))
        parts.append(w[..., ostart:ostart + width])
        pos = pstart + width
    return jnp.concatenate(parts, axis=-1)


def _unpack_cols(wp):
    by_orig = sorted(_PACK, key=lambda t: t[1])
    return jnp.concatenate([wp[..., p:p + wd] for p, _, wd in by_orig], axis=-1)


_LAYER_SHARDED = ("w_in", "conv_w", "w_conv_out", "w_hg_out", "w_att_out", "w_out")
_NARROW = ("w_in", "w_conv_out", "w_hg_out", "w_att_out", "w_out")
_REPLICATED = ("norm_g", "conv_b", "conv_ln_g", "conv_ln_b", "hg_lower_bounds", "hg_norm_g", "q_norm_g", "k_norm_g",
               "attn_sinks")
_WEIGHTS = ("meta_tokens", "norm_g", "w_in", "conv_w", "conv_b", "conv_ln_g", "conv_ln_b", "w_conv_out",
            "hg_lower_bounds", "hg_norm_g", "w_hg_out", "q_norm_g", "k_norm_g", "attn_sinks", "w_att_out", "w_out")
_ROW_SHARDED = ("w_out",)


def _assemble(name, g):
    if name in _ROW_SHARDED:
        return g.reshape((N_DEV * g.shape[1],) + g.shape[2:])
    full = jnp.moveaxis(g, 0, -2)
    return full.reshape(full.shape[:-2] + (N_DEV * full.shape[-1],))


def _split(name, full):
    if name in _ROW_SHARDED:
        return full.reshape((N_DEV, full.shape[0] // N_DEV) + full.shape[1:])
    c = full.shape[-1] // N_DEV
    return jnp.moveaxis(full.reshape(full.shape[:-1] + (N_DEV, c)), -2, 0)


def _layer_weights(gathered):
    full = {k: _assemble(k, g) for k, g in zip(_LAYER_SHARDED, gathered)}
    wp = _pack_cols(full["w_in"])
    return dict(wp=wp, wpt=wp.T, cw=full["conv_w"], wa=full["w_conv_out"], wb=full["w_hg_out"],
                wc=full["w_att_out"], wo=full["w_out"])


def _layer_fwd(h, lw, sp, gather):
    u, hn, gathered = _inproj_fwd(h, sp["norm_g"], lw["wp"], gather)
    ya, y_conv, yb, states = _conv_hgrn_fwd(u, lw["cw"], sp["conv_b"], sp["conv_ln_g"], sp["conv_ln_b"], sp["lb"],
                                            sp["hg_norm_g"])
    yc = _swa_fwd(u, sp["qg"], sp["kg"], sp["sinks"])
    h_next = _mix_fwd(h, u, ya, yb, yc, lw["wa"], lw["wb"], lw["wc"], lw["wo"])
    return h_next, (h, u, hn, ya, yb, yc, states, y_conv), gathered


def _layer_bwd(dh, saved, lw, sp, stacked, layer, depth):
    h_l, u, hn, ya, yb, yc, states, y_conv = saved
    du, dya, dyb, dyc, dwa, dwb, dwc, dwo = _mix_bwd(dh, u, ya, yb, yc, lw["wa"], lw["wb"], lw["wc"], lw["wo"])
    du, dy, dlg, dlb_ = _conv_bwd1(u, y_conv, dya, du, sp["conv_ln_g"], sp["conv_ln_b"])
    du, dcw, dcb = _conv_bwd2(u, dy, du, lw["cw"])
    du, dlbl, dgg = _hgrn_bwd(u, dyb, states, du, sp["lb"], sp["hg_norm_g"])
    du, dqg, dkg, dsk = _swa_bwd(u, dyc, du, sp["qg"], sp["kg"], sp["sinks"])
    dwp = _inproj_bwd_dw(hn, du)
    full = dict(w_in=_unpack_cols(dwp), conv_w=dcw, w_conv_out=dwa, w_hg_out=dwb, w_att_out=dwc, w_out=dwo)
    pieces = [_split(k, full[k]).astype(WIRE_DTYPE) for k in _LAYER_SHARDED]
    dh, dng, stacked = _inproj_bwd_dh(du, lw["wpt"], h_l, sp["norm_g"], dh, pieces, stacked, layer, depth)
    fold = lambda a: a[0, :ATT_HEAD_DIM] + a[0, ATT_HEAD_DIM:]
    small = dict(norm_g=dng[0], conv_b=dcb[0], conv_ln_g=dlg[0], conv_ln_b=dlb_[0], hg_lower_bounds=dlbl[0],
                 hg_norm_g=dgg[0], q_norm_g=fold(dqg), k_norm_g=fold(dkg), attn_sinks=dsk[:, 0])
    return dh, small, stacked


def _as2d(a):
    return a.reshape((-1, a.shape[-1]))


def kernel(x, meta_tokens, norm_g, w_in, conv_w, conv_b, conv_ln_g, conv_ln_b, w_conv_out, hg_lower_bounds, hg_norm_g, w_hg_out, q_norm_g, k_norm_g, attn_sinks, w_att_out, w_out, loss_target, m_meta_tokens, m_norm_g, m_w_in, m_conv_w, m_conv_b, m_conv_ln_g, m_conv_ln_b, m_w_conv_out, m_hg_lower_bounds, m_hg_norm_g, m_w_hg_out, m_q_norm_g, m_k_norm_g, m_attn_sinks, m_w_att_out, m_w_out, v_meta_tokens, v_norm_g, v_w_in, v_conv_w, v_conv_b, v_conv_ln_g, v_conv_ln_b, v_w_conv_out, v_hg_lower_bounds, v_hg_norm_g, v_w_hg_out, v_q_norm_g, v_k_norm_g, v_attn_sinks, v_w_att_out, v_w_out):
    w = dict(meta_tokens=meta_tokens, norm_g=norm_g, w_in=w_in, conv_w=conv_w, conv_b=conv_b, conv_ln_g=conv_ln_g,
             conv_ln_b=conv_ln_b, w_conv_out=w_conv_out, hg_lower_bounds=hg_lower_bounds, hg_norm_g=hg_norm_g,
             w_hg_out=w_hg_out, q_norm_g=q_norm_g, k_norm_g=k_norm_g, attn_sinks=attn_sinks, w_att_out=w_att_out,
             w_out=w_out)
    m = dict(meta_tokens=m_meta_tokens, norm_g=m_norm_g, w_in=m_w_in, conv_w=m_conv_w, conv_b=m_conv_b,
             conv_ln_g=m_conv_ln_g, conv_ln_b=m_conv_ln_b, w_conv_out=m_w_conv_out, hg_lower_bounds=m_hg_lower_bounds,
             hg_norm_g=m_hg_norm_g, w_hg_out=m_w_hg_out, q_norm_g=m_q_norm_g, k_norm_g=m_k_norm_g,
             attn_sinks=m_attn_sinks, w_att_out=m_w_att_out, w_out=m_w_out)
    v = dict(meta_tokens=v_meta_tokens, norm_g=v_norm_g, w_in=v_w_in, conv_w=v_conv_w, conv_b=v_conv_b,
             conv_ln_g=v_conv_ln_g, conv_ln_b=v_conv_ln_b, w_conv_out=v_w_conv_out, hg_lower_bounds=v_hg_lower_bounds,
             hg_norm_g=v_hg_norm_g, w_hg_out=v_w_hg_out, q_norm_g=v_q_norm_g, k_norm_g=v_k_norm_g,
             attn_sinks=v_attn_sinks, w_att_out=v_w_att_out, w_out=v_w_out)

    depth = norm_g.shape[0]
    seq = x.shape[1]
    lp = -(-(seq + CHUNK) // TM_MM) * TM_MM
    tail = lp - seq - CHUNK
    zeros = lambda n: jnp.zeros((n, D_MODEL), F32)

    def shards(l):
        return [w[k][l].astype(MXU_DTYPE) if k in _NARROW else w[k][l] for k in _LAYER_SHARDED]

    first = _gather_two_level(shards(0) + [meta_tokens], "gather_first")
    gathered, meta_full = first[:-1], _assemble("meta_tokens", first[-1])
    h = jnp.concatenate([zeros(META_PAD), meta_full, x[0], zeros(tail)], axis=0)
    target_p = jnp.concatenate([zeros(CHUNK), loss_target[0], zeros(tail)], axis=0)

    lb_all = _lb_fwd(hg_lower_bounds)
    tile2 = lambda a: jnp.concatenate([a, a], axis=-1)
    row = lambda a, l: a[l][None, :]

    def small_rows(l):
        sp = {k: row(w[k], l) for k in ("norm_g", "conv_b", "conv_ln_g", "conv_ln_b", "hg_norm_g")}
        sp.update(lb=row(lb_all, l), qg=tile2(row(q_norm_g, l)), kg=tile2(row(k_norm_g, l)), sinks=attn_sinks[l])
        return sp

    layer_w, saved = [], []
    for l in range(depth):
        layer_w.append(_layer_weights(gathered))
        h, sv, gathered = _layer_fwd(h, layer_w[l], small_rows(l), shards(l + 1) if l + 1 < depth else [])
        saved.append(sv)

    dh, loss_row = _loss_head(h, target_p, seq)
    loss = lax.psum(loss_row[0, 0], ("x", "y", "c"))

    stacked, small_grads = None, [None] * depth
    for l in reversed(range(depth)):
        dh, small_grads[l], stacked = _layer_bwd(dh, saved[l], layer_w[l], small_rows(l), stacked, l, depth)
    grad_x = dh[CHUNK:CHUNK + seq]
    grads = {k: jnp.stack([small_grads[l][k] for l in range(depth)]) for k in _REPLICATED}
    grads["hg_lower_bounds"] = _lb_bwd(hg_lower_bounds, grads["hg_lower_bounds"])

    small = jnp.concatenate([grads[k].reshape(-1) for k in _REPLICATED])
    small = jnp.concatenate([small, jnp.zeros((-small.shape[0] % 128,), F32)]).reshape(-1, 128)
    small_all, meta_pieces = _exchange([small], [_split("meta_tokens", dh[META_PAD:CHUNK])], "exchange_small_grads")
    small_all = small_all.reshape(N_DEV, -1)

    out_g, out_d, out_m, out_v = {}, {}, {}, {}
    for k, gp in zip(("meta_tokens",) + _LAYER_SHARDED, [meta_pieces] + stacked):
        shp = w[k].shape
        res = _adamw(gp.reshape((N_DEV,) + _as2d(w[k]).shape), _as2d(w[k]), _as2d(m[k]), _as2d(v[k]), "adamw_" + k)
        out_g[k], out_d[k], out_m[k], out_v[k] = (r.reshape(shp) for r in res)
    off = 0
    for k in _REPLICATED:
        shp = w[k].shape
        n = w[k].size
        gp = small_all[:, off:off + n].reshape((N_DEV,) + shp)
        off += n
        res = _adamw(gp, w[k], m[k], v[k], "adamw_" + k)
        out_g[k], out_d[k], out_m[k], out_v[k] = res

    return (loss, grad_x[None], *[out_g[k] for k in _WEIGHTS], *[out_d[k] for k in _WEIGHTS],
            *[out_m[k] for k in _WEIGHTS], *[out_v[k] for k in _WEIGHTS])
```

```python
import functools

import jax
import jax.numpy as jnp
from jax import lax
from jax.experimental import pallas as pl
from jax.experimental.pallas import tpu as pltpu

F32 = jnp.float32
MXU_DTYPE = jnp.bfloat16
WIRE_DTYPE = jnp.bfloat16
U_DTYPE = jnp.bfloat16

D_MODEL = 1024
CHUNK = 64
N_META = 16
META_PAD = CHUNK - N_META
D_CONV = 512
CONV_WIDTH = 31
HG_HEADS = 4
HG_D = 128
D_HG = HG_HEADS * HG_D
F_FLOOR = 1e-30
ATT_Q_HEADS = 8
ATT_HEAD_DIM = 64
D_ATT = 512
D_KV = 128
WINDOW_CHUNKS = 2
EPS = 1e-6
D_IN = 7936
N_DEV = 8

ADAM_LR = 0.001
ADAM_B1 = 0.9
ADAM_B2 = 0.999
ADAM_EPS = 1e-08
ADAM_WD = 0.01
ADAM_STEP = 10

NP = 8192
OFF_B, W_B = 0, 2048
OFF_A, W_A = 2048, 1024
OFF_G, W_G = 3072, 3072
OFF_C, W_C = 6144, 1536
OFF_AG, W_AG = 7680, 512
_PACK = ((0, 1536, 2048), (2048, 0, 1024), (3072, 4864, 3072), (6144, 3584, 512), (6656, 4352, 512),
         (7168, 4096, 256), (7680, 1024, 512))
_PAD_AT, _PAD_W = 7424, 256

TM_MM = 1280
TM_BR = 256
TM_WIDE = 640
TM_MIX = 640
HALO = 32
EXP_CLAMP = 80.0
VMEM_LIMIT = 56 * 1024 * 1024

_HI = lax.Precision.HIGHEST


def _cp(sem):
    return pltpu.CompilerParams(dimension_semantics=sem, vmem_limit_bytes=VMEM_LIMIT)


def _sig(x):
    return 1.0 / (1.0 + jnp.exp(-x))


def _dot(a, b):
    return jnp.dot(a.astype(MXU_DTYPE), b.astype(MXU_DTYPE), preferred_element_type=F32)


def _dot_nt(a, b):
    return lax.dot_general(a.astype(MXU_DTYPE), b.astype(MXU_DTYPE), (((1,), (1,)), ((), ())),
                           preferred_element_type=F32)


def _dot_tn(a, b):
    return lax.dot_general(a.astype(MXU_DTYPE), b.astype(MXU_DTYPE), (((0,), (0,)), ((), ())),
                           preferred_element_type=F32)


def _rnd(x):
    return x.astype(MXU_DTYPE).astype(F32)


def _iota(shape, dim):
    return lax.broadcasted_iota(jnp.int32, shape, dim)


def _full_spec(shape):
    nd = len(shape)
    return pl.BlockSpec(shape, lambda *_: (0,) * nd)


def _my_index():
    return 4 * lax.axis_index("x") + 2 * lax.axis_index("y") + lax.axis_index("c")


def _mesh_id(p):
    return (p >> 2, (p >> 1) & 1, p & 1)


class _Exchange:
    def __init__(self, x_refs, o_refs, send_sems, recv_sems, loc_sems, scatter, dst=lambda o, s: o.at[s]):
        me = _my_index()
        self.local, self.sends, self.recvs = [], [], []
        for a, (x, o) in enumerate(zip(x_refs, o_refs)):
            mine = x.at[me] if scatter else x
            self.local.append(pltpu.make_async_copy(mine, dst(o, me), loc_sems.at[a]))
            for k in range(1, N_DEV):
                to, frm = (me + k) % N_DEV, (me + N_DEV - k) % N_DEV
                sems = dict(send_sem=send_sems.at[a, k - 1], recv_sem=recv_sems.at[a, k - 1],
                            device_id_type=pl.DeviceIdType.MESH)
                self.sends.append(pltpu.make_async_remote_copy(
                    src_ref=x.at[to] if scatter else x, dst_ref=dst(o, me), device_id=_mesh_id(to), **sems))
                self.recvs.append(pltpu.make_async_remote_copy(
                    src_ref=mine, dst_ref=dst(o, frm), device_id=_mesh_id(frm), **sems))

    def start(self):
        for cp in self.local + self.sends:
            cp.start()

    def finish(self):
        for cp in self.recvs:
            cp.wait_recv()
        for cp in self.sends:
            cp.wait_send()
        for cp in self.local:
            cp.wait()


def _exchange_sems(n):
    return [pltpu.SemaphoreType.DMA((n, N_DEV - 1)), pltpu.SemaphoreType.DMA((n, N_DEV - 1)),
            pltpu.SemaphoreType.DMA((n,))]


_ANY = pl.BlockSpec(memory_space=pl.ANY)


def _inproj_fwd(h, g, wp, gather=()):
    lp = h.shape[0]
    tm, tn = TM_MM, 2048
    ni, nj = lp // tm, NP // tn
    n = len(gather)

    def body(h_ref, g_ref, w_ref, *rest):
        x_refs, (u_ref, hn_ref), o_refs = rest[:n], rest[n:n + 2], rest[n + 2:2 * n + 2]
        hs_ref, sems = rest[2 * n + 2], rest[2 * n + 3:]
        i, j = pl.program_id(0), pl.program_id(1)
        if n:
            @pl.when((i == 0) & (j == 0))
            def _():
                _Exchange(x_refs, o_refs, *sems, scatter=False).start()

        @pl.when(j == 0)
        def _():
            x = h_ref[...]
            r = lax.rsqrt(jnp.mean(x * x, axis=-1, keepdims=True) + EPS)
            hn = (x * r * g_ref[...]).astype(MXU_DTYPE)
            hs_ref[...] = hn
            hn_ref[...] = hn
        u_ref[...] = jnp.dot(hs_ref[...], w_ref[...], preferred_element_type=F32).astype(U_DTYPE)
        if n:
            @pl.when((i == ni - 1) & (j == nj - 1))
            def _():
                _Exchange(x_refs, o_refs, *sems, scatter=False).finish()

    res = pl.pallas_call(
        body, name="inproj_fwd_gather" if n else "inproj_fwd", grid=(ni, nj),
        in_specs=[pl.BlockSpec((tm, D_MODEL), lambda i, j: (i, 0)), pl.BlockSpec((1, D_MODEL), lambda i, j: (0, 0)),
                  pl.BlockSpec((D_MODEL, tn), lambda i, j: (0, j))] + [_ANY] * n,
        out_specs=[pl.BlockSpec((tm, tn), lambda i, j: (i, j)), pl.BlockSpec((tm, D_MODEL), lambda i, j: (i, 0))]
        + [_ANY] * n,
        out_shape=[jax.ShapeDtypeStruct((lp, NP), U_DTYPE), jax.ShapeDtypeStruct((lp, D_MODEL), MXU_DTYPE)]
        + [jax.ShapeDtypeStruct((N_DEV,) + x.shape, x.dtype) for x in gather],
        scratch_shapes=[pltpu.VMEM((tm, D_MODEL), MXU_DTYPE)] + (_exchange_sems(n) if n else []),
        compiler_params=_cp(("arbitrary", "arbitrary")),
    )(h, g, wp, *gather)
    return res[0], res[1], list(res[2:])


def _inproj_bwd_dh(du, wpt, h, g, dh_next, pieces, stacked, layer, depth):
    lp = h.shape[0]
    tm, tk = TM_MM, 1024
    ni, nk = lp // tm, NP // tk
    n = len(pieces)
    n_acc = 0 if stacked is None else n

    def body(du_ref, w_ref, h_ref, g_ref, dhn_ref, *rest):
        x_refs, (dh_ref, dg_ref), o_refs = rest[:n], rest[n + n_acc:n + n_acc + 2], rest[n + n_acc + 2:2 * n + n_acc + 2]
        acc_ref, sems = rest[2 * n + n_acc + 2], rest[2 * n + n_acc + 3:]
        i, k = pl.program_id(0), pl.program_id(1)
        slot = lambda o, s: o.at[s, layer]

        @pl.when((i == 0) & (k == 0))
        def _():
            _Exchange(x_refs, o_refs, *sems, scatter=True, dst=slot).start()
            dg_ref[...] = jnp.zeros_like(dg_ref)

        @pl.when(k == 0)
        def _():
            acc_ref[...] = jnp.zeros_like(acc_ref)

        acc_ref[...] += jnp.dot(du_ref[...], w_ref[...], preferred_element_type=F32)

        @pl.when(k == nk - 1)
        def _():
            dhn = acc_ref[...]
            x = h_ref[...]
            r = lax.rsqrt(jnp.mean(x * x, axis=-1, keepdims=True) + EPS)
            xh = x * r
            dg_ref[...] += jnp.sum(dhn * xh, axis=0, keepdims=True)
            dxh = dhn * g_ref[...]
            dx = r * (dxh - xh * jnp.mean(dxh * xh, axis=-1, keepdims=True))
            dh_ref[...] = dhn_ref[...] + dx

        @pl.when((i == ni - 1) & (k == nk - 1))
        def _():
            _Exchange(x_refs, o_refs, *sems, scatter=True, dst=slot).finish()

    acc_in = [] if stacked is None else list(stacked)
    res = pl.pallas_call(
        body, name="inproj_bwd_dh_scatter", grid=(ni, nk),
        in_specs=[pl.BlockSpec((tm, tk), lambda i, k: (i, k)), pl.BlockSpec((tk, D_MODEL), lambda i, k: (k, 0)),
                  pl.BlockSpec((tm, D_MODEL), lambda i, k: (i, 0)), pl.BlockSpec((1, D_MODEL), lambda i, k: (0, 0)),
                  pl.BlockSpec((tm, D_MODEL), lambda i, k: (i, 0))] + [_ANY] * (n + n_acc),
        out_specs=[pl.BlockSpec((tm, D_MODEL), lambda i, k: (i, 0)), pl.BlockSpec((1, D_MODEL), lambda i, k: (0, 0))]
        + [_ANY] * n,
        out_shape=[jax.ShapeDtypeStruct((lp, D_MODEL), F32), jax.ShapeDtypeStruct((1, D_MODEL), F32)]
        + [jax.ShapeDtypeStruct((N_DEV, depth) + p.shape[1:], p.dtype) for p in pieces],
        scratch_shapes=[pltpu.VMEM((tm, D_MODEL), F32)] + _exchange_sems(n),
        input_output_aliases={5 + n + a: 2 + a for a in range(n_acc)},
        compiler_params=_cp(("arbitrary", "arbitrary")),
    )(du, wpt, h, g, dh_next, *pieces, *acc_in)
    return res[0], res[1], list(res[2:])


def _inproj_bwd_dw(hn, du):
    lp = hn.shape[0]
    tm, tn = TM_MM, 2048

    def body(hn_ref, du_ref, dw_ref):
        @pl.when(pl.program_id(1) == 0)
        def _():
            dw_ref[...] = jnp.zeros_like(dw_ref)
        dw_ref[...] += _dot_tn(hn_ref[...], du_ref[...])

    return pl.pallas_call(
        body, name="inproj_bwd_dw", grid=(NP // tn, lp // tm),
        in_specs=[pl.BlockSpec((tm, D_MODEL), lambda j, m: (m, 0)), pl.BlockSpec((tm, tn), lambda j, m: (m, j))],
        out_specs=pl.BlockSpec((D_MODEL, tn), lambda j, m: (0, j)),
        out_shape=jax.ShapeDtypeStruct((D_MODEL, NP), F32),
        compiler_params=_cp(("parallel", "arbitrary")),
    )(hn, du)


N_SHIFT = 8
CONV_SUB = 32


def _shift_copies(src_ref, sh_ref):
    n = sh_ref.shape[1]
    for b in range(1, N_SHIFT):
        sh_ref[b - 1, :, :] = src_ref[pl.ds(b, n), :]


def _window(src_ref, sh_ref, off, r0, n):
    a, b = divmod(off, N_SHIFT)
    start = pl.multiple_of(r0 + a * N_SHIFT, N_SHIFT)
    if b == 0:
        return src_ref[pl.ds(start, n), :]
    return sh_ref[b - 1, pl.ds(start, n), :]


def _shift_scratch(tm):
    return pltpu.VMEM((N_SHIFT - 1, tm + HALO - N_SHIFT, D_CONV), F32)


def _glu_ext(a_ref, ah_ref, ext_ref, sh_ref, i, tm):
    rows = i * tm + _iota((tm, 1), 0)
    a = a_ref[...].astype(F32)
    p, sq = a[:, :D_CONV], _sig(a[:, D_CONV:])
    valid = rows >= META_PAD
    ah = ah_ref[...].astype(F32)
    ext_ref[0:HALO, :] = jnp.where(i > 0, ah[:, :D_CONV] * _sig(ah[:, D_CONV:]), 0.0)
    ext_ref[HALO:HALO + tm, :] = jnp.where(valid, p * sq, 0.0)
    _shift_copies(ext_ref, sh_ref)
    return p, sq, valid


def _layernorm_stats(y):
    mu = jnp.mean(y, axis=-1, keepdims=True)
    yc = y - mu
    rstd = lax.rsqrt(jnp.mean(yc * yc, axis=-1, keepdims=True) + EPS)
    return yc * rstd, rstd


def _conv_specs(tm):
    hb = tm // HALO
    return [pl.BlockSpec((tm, W_A), lambda i: (i, OFF_A // W_A)),
            pl.BlockSpec((HALO, W_A), lambda i: (jnp.maximum(i * hb - 1, 0), OFF_A // W_A)),
            pl.BlockSpec((tm, W_AG), lambda i: (i, OFF_AG // W_AG))]


def _conv_fwd_body(tm):
    def body(a_ref, ah_ref, ag_ref, w_ref, b_ref, lg_ref, lb_ref, ya_ref, y_ref, ext_ref, sh_ref):
        i = pl.program_id(0)
        _glu_ext(a_ref, ah_ref, ext_ref, sh_ref, i, tm)
        base = HALO - (CONV_WIDTH - 1)

        y = jnp.zeros((tm, D_CONV), F32) + b_ref[...]
        for k in range(CONV_WIDTH):
            y = y + w_ref[k:k + 1, :] * _window(ext_ref, sh_ref, base + k, 0, tm)
        y_ref[...] = y
        xh, _ = _layernorm_stats(y)
        yn = xh * lg_ref[...] + lb_ref[...]
        gt = ag_ref[...].astype(F32)
        ya_ref[...] = (yn * _sig(yn) * gt * _sig(gt)).astype(MXU_DTYPE)

    return body


def _dsilu(x, s):
    return s * (1.0 + x * (1.0 - s))


def _conv_bwd1(u, y, dya, du, lg, lb_):
    lp = u.shape[0]
    tm = TM_MIX
    assert lp % tm == 0

    def body(ag_ref, y_ref, dya_ref, lg_ref, lb_ref, du_in, du_ref, dy_ref, dlg_ref, dlb_ref):
        del du_in
        i = pl.program_id(0)

        @pl.when(i == 0)
        def _():
            dlg_ref[...] = jnp.zeros_like(dlg_ref)
            dlb_ref[...] = jnp.zeros_like(dlb_ref)

        xh, rstd = _layernorm_stats(y_ref[...])
        yn = xh * lg_ref[...] + lb_ref[...]
        s1 = _sig(yn)
        gt = ag_ref[...].astype(F32)
        s2 = _sig(gt)
        do = dya_ref[...].astype(F32)
        du_ref[...] = (do * (yn * s1) * _dsilu(gt, s2)).astype(MXU_DTYPE)
        dyn = do * (gt * s2) * _dsilu(yn, s1)
        dlg_ref[...] += jnp.sum(dyn * xh, axis=0, keepdims=True)
        dlb_ref[...] += jnp.sum(dyn, axis=0, keepdims=True)
        dxh = dyn * lg_ref[...]
        dy_ref[...] = rstd * (dxh - jnp.mean(dxh, axis=-1, keepdims=True)
                              - xh * jnp.mean(dxh * xh, axis=-1, keepdims=True))

    rowspec = pl.BlockSpec((tm, D_CONV), lambda i: (i, 0))
    return pl.pallas_call(
        body, name="conv_bwd1", grid=(lp // tm,),
        in_specs=[_conv_specs(tm)[2], rowspec, rowspec, _full_spec((1, D_CONV)), _full_spec((1, D_CONV)),
                  pl.BlockSpec(memory_space=pl.ANY)],
        out_specs=[pl.BlockSpec((tm, W_AG), lambda i: (i, OFF_AG // W_AG)), rowspec,
                   _full_spec((1, D_CONV)), _full_spec((1, D_CONV))],
        out_shape=[jax.ShapeDtypeStruct(du.shape, du.dtype), jax.ShapeDtypeStruct((lp, D_CONV), F32),
                   jax.ShapeDtypeStruct((1, D_CONV), F32), jax.ShapeDtypeStruct((1, D_CONV), F32)],
        input_output_aliases={5: 0},
        compiler_params=_cp(("arbitrary",)),
    )(u, y, dya, lg, lb_, du)


def _conv_bwd2(u, dy, du, cw):
    lp = u.shape[0]
    tm = TM_WIDE
    assert lp % tm == 0
    nb = lp // tm
    hb = tm // HALO

    def body(a_ref, ah_ref, dy_ref, dyn_ref, w_ref, du_in, du_ref, dw_ref, db_ref, ext_ref, sh_ref, edy_ref, shd_ref,
             dwp_ref):
        del du_in
        i = pl.program_id(0)

        @pl.when(i == 0)
        def _():
            dwp_ref[...] = jnp.zeros_like(dwp_ref)
            db_ref[...] = jnp.zeros_like(db_ref)

        _glu_ext(a_ref, ah_ref, ext_ref, sh_ref, i, tm)
        dy_all = dy_ref[...]
        edy_ref[0:tm, :] = dy_all
        edy_ref[tm:tm + HALO, :] = jnp.where(i < nb - 1, dyn_ref[...], 0.0)
        _shift_copies(edy_ref, shd_ref)
        db_ref[...] += jnp.sum(dy_all, axis=0, keepdims=True)
        base = HALO - (CONV_WIDTH - 1)

        def fold8(x):
            parts = [x[s:s + N_SHIFT] for s in range(0, CONV_SUB, N_SHIFT)]
            return functools.reduce(jnp.add, parts)

        def sub(r, carry):
            r0 = pl.multiple_of(r * CONV_SUB, CONV_SUB)
            dy = dy_ref[pl.ds(r0, CONV_SUB), :]
            du0 = jnp.zeros((CONV_SUB, D_CONV), F32)
            for k in range(CONV_WIDTH):
                du0 = du0 + w_ref[k:k + 1, :] * _window(edy_ref, shd_ref, CONV_WIDTH - 1 - k, r0, CONV_SUB)
                dwp_ref[k] += fold8(dy * _window(ext_ref, sh_ref, base + k, r0, CONV_SUB))
            a = a_ref[pl.ds(r0, CONV_SUB), :].astype(F32)
            p, sq = a[:, :D_CONV], _sig(a[:, D_CONV:])
            valid = (i * tm + r0 + _iota((CONV_SUB, 1), 0)) >= META_PAD
            du0 = jnp.where(valid, du0, 0.0)
            du_ref[pl.ds(r0, CONV_SUB), :] = jnp.concatenate([du0 * sq, du0 * p * sq * (1.0 - sq)],
                                                             axis=1).astype(MXU_DTYPE)
            return carry

        lax.fori_loop(0, tm // CONV_SUB, sub, 0)

        @pl.when(i == nb - 1)
        def _():
            dw_ref[...] = jnp.sum(dwp_ref[...], axis=1)

    return pl.pallas_call(
        body, name="conv_bwd2", grid=(nb,),
        in_specs=_conv_specs(tm)[:2] + [pl.BlockSpec((tm, D_CONV), lambda i: (i, 0)),
                                        pl.BlockSpec((HALO, D_CONV), lambda i: (jnp.minimum((i + 1) * hb, nb * hb - 1), 0)),
                                        _full_spec((CONV_WIDTH, D_CONV)), pl.BlockSpec(memory_space=pl.ANY)],
        out_specs=[pl.BlockSpec((tm, W_A), lambda i: (i, OFF_A // W_A)), _full_spec((CONV_WIDTH, D_CONV)),
                   _full_spec((1, D_CONV))],
        out_shape=[jax.ShapeDtypeStruct(du.shape, du.dtype), jax.ShapeDtypeStruct((CONV_WIDTH, D_CONV), F32),
                   jax.ShapeDtypeStruct((1, D_CONV), F32)],
        scratch_shapes=[pltpu.VMEM((HALO + tm, D_CONV), F32), _shift_scratch(tm),
                        pltpu.VMEM((tm + HALO, D_CONV), F32), _shift_scratch(tm),
                        pltpu.VMEM((CONV_WIDTH, N_SHIFT, D_CONV), F32)],
        input_output_aliases={5: 0},
        compiler_params=_cp(("arbitrary",)),
    )(u, u, dy, dy, cw, du)


HG_T = 128
TM_HG = TM_WIDE
HG_HALF = HG_T // 2


def _hg_chunk_fwd(blk, lb, valid, tri):
    bq, bf, v = blk[:, 0:512], blk[:, 512:1024], blk[:, 1024:1536]
    sgq = _sig(bq)
    qt = bq * sgq
    sz = _sig(bf)
    f = lb + (1.0 - lb) * sz
    g = jnp.where(valid, jnp.log(jnp.maximum(f, F_FLOOR)), 0.0)
    k = jnp.where(valid, (1.0 - lb) * (1.0 - sz), 0.0)
    b = jnp.dot(tri, g, precision=_HI, preferred_element_type=F32)
    ridx = _iota((HG_T, 1), 0)
    pick = lambda r: jnp.sum(jnp.where(ridx == r, b, 0.0), axis=0, keepdims=True)
    return dict(bq=bq, sgq=sgq, qt=qt, sz=sz, f=f, k=k, v=v, b=b, top=ridx < HG_HALF, rx=pick(HG_HALF - 1),
                rdt=pick(HG_HALF // 2 - 1), rdb=pick(HG_HALF + HG_HALF // 2 - 1), bl=pick(HG_T - 1))


def _hg_head(p, sl):
    top, b, qt, k = p["top"], p["b"][:, sl], p["qt"][:, sl], p["k"][:, sl]
    rx, bl = p["rx"][:, sl], p["bl"][:, sl]
    rd = jnp.where(top, p["rdt"][:, sl], p["rdb"][:, sl])
    eqx = jnp.where(top, 0.0, jnp.exp(jnp.minimum(b - rx, 0.0)))
    ekx = jnp.where(top, jnp.exp(jnp.minimum(rx - b, 0.0)), 0.0)
    eqd = jnp.exp(jnp.minimum(b - rd, EXP_CLAMP))
    ekd = jnp.exp(jnp.minimum(rd - b, EXP_CLAMP))
    e = jnp.exp(b)
    ekl = jnp.exp(bl - b)
    qx, kx, qd, kd = _rnd(qt * eqx), _rnd(k * ekx), _rnd(qt * eqd), _rnd(k * ekd)
    qcat = jnp.concatenate([qx, jnp.where(top, qd, 0.0), jnp.where(top, 0.0, qd)], axis=1)
    kcat = jnp.concatenate([kx, jnp.where(top, kd, 0.0), jnp.where(top, 0.0, kd)], axis=1)
    return dict(v=p["v"][:, sl], eqx=eqx, ekx=ekx, eqd=eqd, ekd=ekd, e=e, ekl=ekl, el=jnp.exp(bl), qx=qx, kx=kx,
                qd=qd, kd=kd, qe=qt * e, kl=k * ekl, qcat=qcat, kcat=kcat)


def _hgrn_fwd_body(tm):
    cpb = tm // HG_T

    def body(u_ref, lb_ref, gg_ref, y_ref, st_ref, s_ref):
        i = pl.program_id(0)

        @pl.when(i == 0)
        def _():
            s_ref[...] = jnp.zeros_like(s_ref)

        lbv = lb_ref[...]
        ggv = gg_ref[...]
        tri = (_iota((HG_T, HG_T), 0) >= _iota((HG_T, HG_T), 1)).astype(F32)

        def chunk(c, carry):
            r0 = pl.multiple_of(c * HG_T, HG_T)
            blk = u_ref[pl.ds(r0, HG_T), :].astype(F32)
            valid = (i * tm + r0 + _iota((HG_T, 1), 0)) >= META_PAD
            q = _hg_chunk_fwd(blk, lbv, valid, tri)
            outs = []
            for hh in range(HG_HEADS):
                h = _hg_head(q, slice(hh * HG_D, (hh + 1) * HG_D))
                a = jnp.where(tri > 0, _dot_nt(h["qcat"], h["kcat"]), 0.0)
                st = s_ref[hh]
                st_ref[c, hh] = st
                o = _dot(a, h["v"]) + _dot_nt(h["qe"], st)
                s_ref[hh] = st * h["el"] + _dot_tn(h["v"], h["kl"])
                rs = lax.rsqrt(jnp.mean(o * o, axis=-1, keepdims=True) + EPS)
                outs.append(o * rs * ggv)
            on = jnp.concatenate(outs, axis=1)
            bg = blk[:, 1536:2048]
            y_ref[pl.ds(r0, HG_T), :] = (on * bg * _sig(bg)).astype(MXU_DTYPE)
            return carry

        lax.fori_loop(0, cpb, chunk, 0, unroll=True)

    return body


def _conv_hgrn_fwd(u, cw, cb, lg, lb_, hlb, gg):
    lp = u.shape[0]
    tm = TM_WIDE
    assert lp % tm == 0
    cpb = tm // HG_T
    conv_body, hgrn_body = _conv_fwd_body(tm), _hgrn_fwd_body(tm)

    def body(a_ref, ah_ref, ag_ref, w_ref, b_ref, lg_ref, lb_ref, ub_ref, hlb_ref, gg_ref,
             ya_ref, y_ref, yb_ref, st_ref, ext_ref, sh_ref, s_ref):
        hgrn_body(ub_ref, hlb_ref, gg_ref, yb_ref, st_ref, s_ref)
        conv_body(a_ref, ah_ref, ag_ref, w_ref, b_ref, lg_ref, lb_ref, ya_ref, y_ref, ext_ref, sh_ref)

    rowspec = pl.BlockSpec((tm, D_CONV), lambda i: (i, 0))
    return pl.pallas_call(
        body, name="conv_hgrn_fwd", grid=(lp // tm,),
        in_specs=_conv_specs(tm) + [_full_spec((CONV_WIDTH, D_CONV))] + [_full_spec((1, D_CONV))] * 3
        + [pl.BlockSpec((tm, W_B), lambda i: (i, 0)), _full_spec((1, D_HG)), _full_spec((1, HG_D))],
        out_specs=[rowspec, rowspec, pl.BlockSpec((tm, D_HG), lambda i: (i, 0)),
                   pl.BlockSpec((cpb, HG_HEADS, HG_D, HG_D), lambda i: (i, 0, 0, 0))],
        out_shape=[jax.ShapeDtypeStruct((lp, D_CONV), MXU_DTYPE), jax.ShapeDtypeStruct((lp, D_CONV), F32),
                   jax.ShapeDtypeStruct((lp, D_HG), MXU_DTYPE),
                   jax.ShapeDtypeStruct((lp // HG_T, HG_HEADS, HG_D, HG_D), F32)],
        scratch_shapes=[pltpu.VMEM((HALO + tm, D_CONV), F32), _shift_scratch(tm),
                        pltpu.VMEM((HG_HEADS, HG_D, HG_D), F32)],
        compiler_params=_cp(("arbitrary",)),
    )(u, u, u, cw, cb, lg, lb_, u, hlb, gg)


def _hgrn_bwd(u, dyb, states, du, lb, gg):
    lp = u.shape[0]
    tm = TM_HG
    cpb = tm // HG_T
    nb = lp // tm

    def body(u_ref, dy_ref, st_ref, lb_ref, gg_ref, du_in, du_ref, dlb_ref, dgg_ref, ds_ref):
        del du_in
        ii = pl.program_id(0)
        i = nb - 1 - ii

        @pl.when(ii == 0)
        def _():
            ds_ref[...] = jnp.zeros_like(ds_ref)
            dlb_ref[...] = jnp.zeros_like(dlb_ref)
            dgg_ref[...] = jnp.zeros_like(dgg_ref)

        lbv = lb_ref[...]
        ggv = gg_ref[...]
        lower = _iota((HG_T, HG_T), 0) >= _iota((HG_T, HG_T), 1)
        tri = lower.astype(F32)
        triu = (_iota((HG_T, HG_T), 0) <= _iota((HG_T, HG_T), 1)).astype(F32)
        ridx = _iota((HG_T, 1), 0)

        def chunk(cc, carry):
            c = cpb - 1 - cc
            r0 = pl.multiple_of(c * HG_T, HG_T)
            blk = u_ref[pl.ds(r0, HG_T), :].astype(F32)
            valid = (i * tm + r0 + _iota((HG_T, 1), 0)) >= META_PAD
            q = _hg_chunk_fwd(blk, lbv, valid, tri)
            top = q["top"]
            bg = blk[:, 1536:2048]
            sg = _sig(bg)
            dy = dy_ref[pl.ds(r0, HG_T), :].astype(F32)
            don_all = dy * bg * sg
            dqt_l, dk_l, dv_l, db_l, dbl_l, on_l = [], [], [], [], [], []
            dgg = jnp.zeros((1, HG_D), F32)
            for hh in range(HG_HEADS):
                sl = slice(hh * HG_D, (hh + 1) * HG_D)
                h = _hg_head(q, sl)
                qe, kl, v, el, qcat, kcat = h["qe"], h["kl"], h["v"], h["el"], h["qcat"], h["kcat"]
                a = jnp.where(lower, _dot_nt(qcat, kcat), 0.0)
                st = st_ref[c, hh]
                o = _dot(a, v) + _dot_nt(qe, st)
                rs = lax.rsqrt(jnp.mean(o * o, axis=-1, keepdims=True) + EPS)
                xh = o * rs
                on_l.append(xh * ggv)
                don = don_all[:, sl]
                dgg = dgg + jnp.sum(don * xh, axis=0, keepdims=True)
                dxh = don * ggv
                do = rs * (dxh - xh * jnp.mean(dxh * xh, axis=-1, keepdims=True))
                dst = ds_ref[hh]
                dv = _dot_tn(a, do) + _dot_nt(kl, dst)
                da = jnp.where(lower, _dot_nt(do, v), 0.0)
                dqe = _dot(do, st)
                dkl = _dot(v, dst)
                d_el = jnp.sum(st * dst, axis=0, keepdims=True)
                ds_ref[hh] = _dot_tn(do, qe) + dst * el
                dqc = _dot(da, kcat)
                dkc = _dot_tn(da, qcat)
                dqx, dqd = dqc[:, :HG_D], jnp.where(top, dqc[:, HG_D:2 * HG_D], dqc[:, 2 * HG_D:])
                dkx, dkd = dkc[:, :HG_D], jnp.where(top, dkc[:, HG_D:2 * HG_D], dkc[:, 2 * HG_D:])
                dqt_l.append(dqx * h["eqx"] + dqd * h["eqd"] + dqe * h["e"])
                dk_l.append(dkx * h["ekx"] + dkd * h["ekd"] + dkl * h["ekl"])
                dv_l.append(dv)
                db_l.append(dqx * h["qx"] - dkx * h["kx"] + dqd * h["qd"] - dkd * h["kd"] + dqe * qe - dkl * kl)
                dbl_l.append(jnp.sum(dkl * kl, axis=0, keepdims=True) + d_el * el)
            dqt = jnp.concatenate(dqt_l, axis=1)
            dk = jnp.concatenate(dk_l, axis=1)
            dv = jnp.concatenate(dv_l, axis=1)
            db = jnp.concatenate(db_l, axis=1) + jnp.where(ridx == HG_T - 1, jnp.concatenate(dbl_l, axis=1), 0.0)
            on = jnp.concatenate(on_l, axis=1)
            dg = jnp.dot(triu, db, precision=_HI, preferred_element_type=F32)
            sz, f = q["sz"], q["f"]
            df = jnp.where(valid & (f > F_FLOOR), dg / f, 0.0)
            dkv = jnp.where(valid, dk, 0.0)
            t = (1.0 - sz) * (df - dkv)
            dlb_ref[...] += jnp.sum(t, axis=0, keepdims=True)
            dz = (1.0 - lbv) * (df - dkv) * sz * (1.0 - sz)
            dbq = dqt * _dsilu(q["bq"], q["sgq"])
            dbg = dy * on * _dsilu(bg, sg)
            dgg_ref[...] += dgg
            du_ref[pl.ds(r0, HG_T), :] = jnp.concatenate([dbq, dz, dv, dbg], axis=1).astype(MXU_DTYPE)
            return carry

        lax.fori_loop(0, cpb, chunk, 0, unroll=True)

    return pl.pallas_call(
        body, name="hgrn_bwd", grid=(nb,),
        in_specs=[pl.BlockSpec((tm, W_B), lambda ii: (nb - 1 - ii, 0)), pl.BlockSpec((tm, D_HG), lambda ii: (nb - 1 - ii, 0)),
                  pl.BlockSpec((cpb, HG_HEADS, HG_D, HG_D), lambda ii: (nb - 1 - ii, 0, 0, 0)),
                  _full_spec((1, D_HG)), _full_spec((1, HG_D)), pl.BlockSpec(memory_space=pl.ANY)],
        out_specs=[pl.BlockSpec((tm, W_B), lambda ii: (nb - 1 - ii, 0)), _full_spec((1, D_HG)), _full_spec((1, HG_D))],
        out_shape=[jax.ShapeDtypeStruct(du.shape, du.dtype), jax.ShapeDtypeStruct((1, D_HG), F32),
                   jax.ShapeDtypeStruct((1, HG_D), F32)],
        scratch_shapes=[pltpu.VMEM((HG_HEADS, HG_D, HG_D), F32)],
        input_output_aliases={5: 0},
        compiler_params=_cp(("arbitrary",)),
    )(u, dyb, states, lb, gg, du)


TM_SWA = 256
PREV_ROWS = WINDOW_CHUNKS * CHUNK
DEAD_ROWS = -(CHUNK + PREV_ROWS + TM_SWA) % 128
N_KEYS = CHUNK + DEAD_ROWS + PREV_ROWS + TM_SWA
LOG2E = 1.4426950408889634
ATT_SCALE2 = ATT_HEAD_DIM ** -0.5 * LOG2E
NEG = -1e30


def _half_sum(x, lo):
    a = jnp.sum(jnp.where(lo, x, 0.0), axis=1, keepdims=True)
    b = jnp.sum(jnp.where(lo, 0.0, x), axis=1, keepdims=True)
    return jnp.where(lo, a, b)


def _half_rms(x, lo):
    return lax.rsqrt(_half_sum(x * x, lo) * (1.0 / ATT_HEAD_DIM) + EPS)


def _swa_mask(i, tm):
    tq = i * tm + _iota((tm, N_KEYS), 0)
    s = _iota((tm, N_KEYS), 1)
    nq = tq >> 6
    kr = i * tm + s - (N_KEYS - tm)
    kc = kr >> 6
    band = (kr >= META_PAD) & (kc >= nq - WINDOW_CHUNKS) & (kc <= nq)
    meta = (nq > WINDOW_CHUNKS) & (s >= META_PAD)
    return ((s < CHUNK) & meta) | ((s >= CHUNK) & band)


def _swa_keys(own_kv, prev_ref, meta_ref, kg, tm):
    kv = jnp.concatenate([meta_ref[...].astype(F32), jnp.zeros((DEAD_ROWS, 2 * D_KV), F32),
                          prev_ref[tm - PREV_ROWS:tm, :].astype(F32), own_kv], axis=0)
    k_raw, v = kv[:, :D_KV], kv[:, D_KV:]
    lo = _iota((1, D_KV), 1) < ATT_HEAD_DIM
    kr = _half_rms(k_raw, lo)
    kn = k_raw * kr * kg
    return k_raw, kr, kn, v, lo


def _placed(x, lo):
    xr = pltpu.roll(x, ATT_HEAD_DIM, 1)
    z = jnp.zeros_like(x)
    return [[jnp.where(lo, x, z).astype(MXU_DTYPE), jnp.where(lo, z, xr).astype(MXU_DTYPE)],
            [jnp.where(lo, xr, z).astype(MXU_DTYPE), jnp.where(lo, z, x).astype(MXU_DTYPE)]]


def _swa_specs(tm, order):
    kvb = (OFF_C + 1024) // 256
    return [pl.BlockSpec((tm, W_C), lambda i: (order(i), OFF_C // W_C)),
            pl.BlockSpec((tm, 256), lambda i: (jnp.maximum(order(i) - 1, 0), kvb)),
            pl.BlockSpec((CHUNK, 256), lambda i: (0, kvb)),
            _full_spec((1, D_KV)), _full_spec((1, D_KV)), pl.BlockSpec(memory_space=pltpu.SMEM)]


def _swa_fwd(u, qg, kg, sinks):
    lp = u.shape[0]
    tm = TM_SWA

    def body(own_ref, prev_ref, meta_ref, qg_ref, kg_ref, sink_ref, y_ref):
        i = pl.program_id(0)
        own = own_ref[...].astype(F32)
        _, _, kn, v, lo = _swa_keys(own[:, 1024:1280], prev_ref, meta_ref, kg_ref[...], tm)
        kuse, vuse = _placed(kn, lo), _placed(v, lo)
        bias = jnp.where(_swa_mask(i, tm), 0.0, NEG)
        ones = jnp.ones((N_KEYS, 128), MXU_DTYPE)
        vones = [[jnp.concatenate([vuse[j][e], ones], axis=1) for e in range(2)] for j in range(2)]
        for gi in range(ATT_Q_HEADS // 2):
            j = gi // 2
            sl = slice(gi * 128, (gi + 1) * 128)
            qraw = own[:, sl]
            qs = qraw * _half_rms(qraw, lo) * (qg_ref[...] * ATT_SCALE2)
            og = jnp.zeros((tm, 128), F32)
            for e in range(2):
                qm = jnp.where(lo if e == 0 else ~lo, qs, 0.0)
                s = _dot_nt(qm, kuse[j][e]) + bias
                sk = sink_ref[2 * gi + e] * LOG2E
                m = jnp.maximum(jnp.max(s, axis=-1, keepdims=True), sk)
                p = jnp.exp2((s - m).astype(MXU_DTYPE))
                pv = _dot(p, vones[j][e])
                og = og + pv[:, :128] / (pv[:, 128:] + jnp.exp2(sk - m))
            gt = own[:, 512 + gi * 128:512 + (gi + 1) * 128]
            y_ref[:, sl] = (og * gt * _sig(gt)).astype(MXU_DTYPE)

    return pl.pallas_call(
        body, name="swa_fwd", grid=(lp // tm,),
        in_specs=_swa_specs(tm, lambda i: i),
        out_specs=pl.BlockSpec((tm, D_ATT), lambda i: (i, 0)),
        out_shape=jax.ShapeDtypeStruct((lp, D_ATT), MXU_DTYPE),
        compiler_params=_cp(("arbitrary",)),
    )(u, u, u, qg, kg, sinks)


def _swa_bwd(u, dyc, du, qg, kg, sinks):
    lp = u.shape[0]
    tm = TM_SWA
    nb = lp // tm
    order = lambda ii: nb - 1 - ii

    def body(own_ref, prev_ref, meta_ref, qg_ref, kg_ref, sink_ref, dy_ref, du_in, du_ref, dqg_ref, dkg_ref, dsk_ref,
             carry_ref, macc_ref):
        del du_in
        ii = pl.program_id(0)
        i = nb - 1 - ii

        @pl.when(ii == 0)
        def _():
            carry_ref[...] = jnp.zeros_like(carry_ref)
            macc_ref[...] = jnp.zeros_like(macc_ref)
            dqg_ref[...] = jnp.zeros_like(dqg_ref)
            dkg_ref[...] = jnp.zeros_like(dkg_ref)
            dsk_ref[...] = jnp.zeros_like(dsk_ref)

        own = own_ref[...].astype(F32)
        k_raw, krs, kn, v, lo = _swa_keys(own[:, 1024:1280], prev_ref, meta_ref, kg_ref[...], tm)
        kuse, vuse = _placed(kn, lo), _placed(v, lo)
        bias = jnp.where(_swa_mask(i, tm), 0.0, NEG)
        dkn_t = jnp.zeros((D_KV, N_KEYS), F32)
        dvn_t = jnp.zeros((D_KV, N_KEYS), F32)
        for gi in range(ATT_Q_HEADS // 2):
            j = gi // 2
            sl = slice(gi * 128, (gi + 1) * 128)
            qraw = own[:, sl]
            qr = _half_rms(qraw, lo)
            qxh = qraw * qr
            qs = qxh * (qg_ref[...] * ATT_SCALE2)
            ps, invs, pk, qms = [], [], [], []
            og = jnp.zeros((tm, 128), F32)
            for e in range(2):
                qm = jnp.where(lo if e == 0 else ~lo, qs, 0.0)
                s = _dot_nt(qm, kuse[j][e]) + bias
                sk = sink_ref[2 * gi + e] * LOG2E
                m = jnp.maximum(jnp.max(s, axis=-1, keepdims=True), sk)
                p = jnp.exp2(s - m)
                inv = 1.0 / (jnp.sum(p, axis=-1, keepdims=True) + jnp.exp2(sk - m))
                ps.append(p)
                invs.append(inv)
                pk.append(jnp.exp2(sk - m) * inv)
                qms.append(qm)
                og = og + _dot(p, vuse[j][e]) * inv
            gt = own[:, 512 + gi * 128:512 + (gi + 1) * 128]
            sg = _sig(gt)
            dy = dy_ref[:, sl].astype(F32)
            dgt = dy * og * _dsilu(gt, sg)
            dog = dy * gt * sg
            dqn = jnp.zeros((tm, 128), F32)
            for e in range(2):
                half = lo if e == 0 else ~lo
                dog_m = jnp.where(half, dog, 0.0)
                dl = jnp.sum(dog_m * og, axis=1, keepdims=True)
                dp = _dot_nt(dog_m, vuse[j][e])
                ds = ps[e] * ((dp - dl) * (invs[e] * (1.0 / LOG2E)))
                hsk = 2 * gi + e
                dsk_ref[hsk:hsk + 1, :] += jnp.zeros((1, 128), F32) - jnp.sum(pk[e] * dl, axis=0, keepdims=True)
                dqn = dqn + _dot(ds, kuse[j][e])
                dk_e = _dot_tn(qms[e], ds)
                dv_e = _dot_tn(dog_m * invs[e], ps[e])
                if j != e:
                    dk_e = pltpu.roll(dk_e, ATT_HEAD_DIM, 0)
                    dv_e = pltpu.roll(dv_e, ATT_HEAD_DIM, 0)
                dkn_t = dkn_t + dk_e
                dvn_t = dvn_t + dv_e
            dqn = dqn * ATT_SCALE2
            dqg_ref[...] += jnp.sum(dqn * qxh, axis=0, keepdims=True)
            dqx = dqn * qg_ref[...]
            dq = qr * (dqx - qxh * _half_sum(dqx * qxh, lo) * (1.0 / ATT_HEAD_DIM))
            du_ref[:, sl] = dq.astype(MXU_DTYPE)
            du_ref[:, 512 + gi * 128:512 + (gi + 1) * 128] = dgt.astype(MXU_DTYPE)

        dkn, dvn = dkn_t.T, dvn_t.T
        macc_ref[...] += jnp.concatenate([dkn[0:CHUNK], dvn[0:CHUNK]], axis=1)
        own0 = N_KEYS - tm
        tot = jnp.concatenate([dkn[own0:], dvn[own0:]], axis=1) + carry_ref[...]
        if tm > PREV_ROWS:
            carry_ref[0:tm - PREV_ROWS, :] = jnp.zeros((tm - PREV_ROWS, 2 * D_KV), F32)
        prev0 = own0 - PREV_ROWS
        carry_ref[tm - PREV_ROWS:tm, :] = jnp.concatenate([dkn[prev0:own0], dvn[prev0:own0]], axis=1)
        first = jnp.where((i == 0) & (_iota((tm, 1), 0) < CHUNK), 1.0, 0.0)
        tot = tot + first * jnp.concatenate([macc_ref[...], jnp.zeros((tm - CHUNK, 2 * D_KV), F32)], axis=0)
        dkn_own, dv_own = tot[:, :D_KV], tot[:, D_KV:]
        kx = k_raw[own0:] * krs[own0:]
        dkg_ref[...] += jnp.sum(dkn_own * kx, axis=0, keepdims=True)
        dkx = dkn_own * kg_ref[...]
        dk = krs[own0:] * (dkx - kx * _half_sum(dkx * kx, lo) * (1.0 / ATT_HEAD_DIM))
        du_ref[:, 1024:1152] = dk.astype(MXU_DTYPE)
        du_ref[:, 1152:1280] = dv_own.astype(MXU_DTYPE)
        du_ref[:, 1280:W_C] = jnp.zeros((tm, W_C - 1280), MXU_DTYPE)

    return pl.pallas_call(
        body, name="swa_bwd", grid=(nb,),
        in_specs=_swa_specs(tm, order) + [pl.BlockSpec((tm, D_ATT), lambda ii: (order(ii), 0)),
                                          pl.BlockSpec(memory_space=pl.ANY)],
        out_specs=[pl.BlockSpec((tm, W_C), lambda ii: (order(ii), OFF_C // W_C)), _full_spec((1, 128)),
                   _full_spec((1, 128)), _full_spec((ATT_Q_HEADS, 128))],
        out_shape=[jax.ShapeDtypeStruct(du.shape, du.dtype), jax.ShapeDtypeStruct((1, 128), F32),
                   jax.ShapeDtypeStruct((1, 128), F32), jax.ShapeDtypeStruct((ATT_Q_HEADS, 128), F32)],
        scratch_shapes=[pltpu.VMEM((tm, 2 * D_KV), F32), pltpu.VMEM((CHUNK, 2 * D_KV), F32)],
        input_output_aliases={7: 0},
        compiler_params=_cp(("arbitrary",)),
    )(u, u, u, qg, kg, sinks, dyc, du)


def _load_once(pairs, first):
    @pl.when(first)
    def _():
        for src, dst in pairs:
            pltpu.sync_copy(src, dst)


def _mix_fwd(h, u, ya, yb, yc, wa, wb, wc, wo):
    lp = h.shape[0]
    tm = TM_MIX
    assert lp % tm == 0

    def body(h_ref, g_ref, ya_ref, yb_ref, yc_ref, wa_hbm, wb_hbm, wc_hbm, wo_hbm, out_ref, wa_ref, wb_ref, wc_ref,
             wo_ref):
        _load_once(((wa_hbm, wa_ref), (wb_hbm, wb_ref), (wc_hbm, wc_ref), (wo_hbm, wo_ref)), pl.program_id(0) == 0)
        mixed = jnp.zeros((tm, D_MODEL), F32)
        for n, (y_ref, w_ref) in enumerate(((ya_ref, wa_ref), (yb_ref, wb_ref), (yc_ref, wc_ref))):
            z = jnp.dot(y_ref[...], w_ref[...], preferred_element_type=F32)
            mixed = mixed + _sig(g_ref[:, n * D_MODEL:(n + 1) * D_MODEL].astype(F32)) * z
        out_ref[...] = h_ref[...] + _dot(mixed, wo_ref[...])

    ybs = pl.BlockSpec((tm, 512), lambda i: (i, 0))
    anyspec = pl.BlockSpec(memory_space=pl.ANY)
    return pl.pallas_call(
        body, name="mix_fwd", grid=(lp // tm,),
        in_specs=[pl.BlockSpec((tm, D_MODEL), lambda i: (i, 0)), pl.BlockSpec((tm, W_G), lambda i: (i, OFF_G // W_G)),
                  ybs, ybs, ybs, anyspec, anyspec, anyspec, anyspec],
        out_specs=pl.BlockSpec((tm, D_MODEL), lambda i: (i, 0)),
        out_shape=jax.ShapeDtypeStruct((lp, D_MODEL), F32),
        scratch_shapes=[pltpu.VMEM((512, D_MODEL), MXU_DTYPE)] * 3 + [pltpu.VMEM((D_MODEL, D_MODEL), MXU_DTYPE)],
        compiler_params=_cp(("arbitrary",)),
    )(h, u, ya, yb, yc, wa, wb, wc, wo)


def _mix_bwd(dh, u, ya, yb, yc, wa, wb, wc, wo):
    lp = dh.shape[0]
    tm = TM_BR
    nb = lp // tm

    def body(dh_ref, g_ref, ya_ref, yb_ref, yc_ref, wa_hbm, wb_hbm, wc_hbm, wo_hbm,
             du_ref, dya_ref, dyb_ref, dyc_ref, dwa_hbm, dwb_hbm, dwc_hbm, dwo_hbm,
             wa_ref, wb_ref, wc_ref, wo_ref, dwa_ref, dwb_ref, dwc_ref, dwo_ref):
        i = pl.program_id(0)
        _load_once(((wa_hbm, wa_ref), (wb_hbm, wb_ref), (wc_hbm, wc_ref), (wo_hbm, wo_ref)), i == 0)

        @pl.when(i == 0)
        def _():
            for r in (dwa_ref, dwb_ref, dwc_ref, dwo_ref):
                r[...] = jnp.zeros_like(r)

        dh_b = dh_ref[...].astype(MXU_DTYPE)
        dmixed = _dot_nt(dh_b, wo_ref[...])
        mixed = jnp.zeros((tm, D_MODEL), F32)
        for n, (y_ref, w_ref, dy_ref, dw_ref) in enumerate(((ya_ref, wa_ref, dya_ref, dwa_ref),
                                                            (yb_ref, wb_ref, dyb_ref, dwb_ref),
                                                            (yc_ref, wc_ref, dyc_ref, dwc_ref))):
            y = y_ref[...]
            z = jnp.dot(y, w_ref[...], preferred_element_type=F32)
            gate = _sig(g_ref[:, n * D_MODEL:(n + 1) * D_MODEL].astype(F32))
            mixed = mixed + gate * z
            du_ref[:, n * D_MODEL:(n + 1) * D_MODEL] = (z * dmixed * gate * (1.0 - gate)).astype(MXU_DTYPE)
            dz = (gate * dmixed).astype(MXU_DTYPE)
            dy_ref[...] = _dot_nt(dz, w_ref[...]).astype(MXU_DTYPE)
            dw_ref[...] += _dot_tn(y, dz)
        dwo_ref[...] += _dot_tn(mixed, dh_b)

        @pl.when(i == nb - 1)
        def _():
            for src, dst in ((dwa_ref, dwa_hbm), (dwb_ref, dwb_hbm), (dwc_ref, dwc_hbm), (dwo_ref, dwo_hbm)):
                pltpu.sync_copy(src, dst)

    ybs = pl.BlockSpec((tm, 512), lambda i: (i, 0))
    anyspec = pl.BlockSpec(memory_space=pl.ANY)
    wsh = jax.ShapeDtypeStruct((512, D_MODEL), F32)
    return pl.pallas_call(
        body, name="mix_bwd", grid=(nb,),
        in_specs=[pl.BlockSpec((tm, D_MODEL), lambda i: (i, 0)), pl.BlockSpec((tm, W_G), lambda i: (i, OFF_G // W_G)),
                  ybs, ybs, ybs, anyspec, anyspec, anyspec, anyspec],
        out_specs=[pl.BlockSpec((tm, W_G), lambda i: (i, OFF_G // W_G)), ybs, ybs, ybs, anyspec, anyspec, anyspec, anyspec],
        out_shape=[jax.ShapeDtypeStruct((lp, NP), MXU_DTYPE)] + [jax.ShapeDtypeStruct((lp, 512), MXU_DTYPE)] * 3
        + [wsh, wsh, wsh, jax.ShapeDtypeStruct((D_MODEL, D_MODEL), F32)],
        scratch_shapes=[pltpu.VMEM((512, D_MODEL), MXU_DTYPE)] * 3 + [pltpu.VMEM((D_MODEL, D_MODEL), MXU_DTYPE)]
        + [pltpu.VMEM((512, D_MODEL), F32)] * 3 + [pltpu.VMEM((D_MODEL, D_MODEL), F32)],
        compiler_params=_cp(("arbitrary",)),
    )(dh, u, ya, yb, yc, wa, wb, wc, wo)


def _loss_head(h, target_p, seq):
    lp = h.shape[0]
    tm = TM_MIX
    assert lp % tm == 0

    def body(h_ref, t_ref, dh_ref, loss_ref):
        i = pl.program_id(0)

        @pl.when(i == 0)
        def _():
            loss_ref[...] = jnp.zeros_like(loss_ref)

        rows = i * tm + _iota((tm, 1), 0)
        e = jnp.where((rows >= CHUNK) & (rows < CHUNK + seq), h_ref[...] - t_ref[...], 0.0)
        dh_ref[...] = e * (1.0 / D_MODEL)
        part = jnp.sum(jnp.mean(e * e, axis=-1, keepdims=True), axis=0, keepdims=True)
        loss_ref[...] += 0.5 * part

    return pl.pallas_call(
        body, name="loss_head", grid=(lp // tm,),
        in_specs=[pl.BlockSpec((tm, D_MODEL), lambda i: (i, 0))] * 2,
        out_specs=[pl.BlockSpec((tm, D_MODEL), lambda i: (i, 0)), _full_spec((1, 128))],
        out_shape=[jax.ShapeDtypeStruct((lp, D_MODEL), F32), jax.ShapeDtypeStruct((1, 128), F32)],
        compiler_params=_cp(("arbitrary",)),
    )(h, target_p)


def _lb_rows(p_ref):
    depth = p_ref.shape[0]
    rows = [p_ref[l:l + 1, :] for l in range(depth)]
    mx = functools.reduce(jnp.maximum, rows)
    ex = [jnp.exp(r - mx) for r in rows]
    tot = functools.reduce(jnp.add, ex)
    sm = [e / tot for e in ex]
    cs, run = [], jnp.zeros_like(sm[0])
    for l in range(depth):
        run = run + sm[l]
        cs.append(run)
    return sm, [c - sm[0] for c in cs]


def _lb_fwd(p):
    def body(p_ref, o_ref):
        _, xs = _lb_rows(p_ref)
        for l, xl in enumerate(xs):
            o_ref[l:l + 1, :] = jnp.clip(xl, 0.0, 1.0)

    return pl.pallas_call(body, name="lb_fwd", out_shape=jax.ShapeDtypeStruct(p.shape, F32))(p)


def _lb_bwd(p, dlb):
    def body(p_ref, d_ref, o_ref):
        sm, xs = _lb_rows(p_ref)
        depth = len(xs)
        dx = []
        for l in range(depth):
            x = xs[l]
            g0 = jnp.where(x > 0.0, 1.0, jnp.where(x == 0.0, 0.5, 0.0))
            y = jnp.maximum(x, 0.0)
            g1 = jnp.where(y < 1.0, 1.0, jnp.where(y == 1.0, 0.5, 0.0))
            dx.append(d_ref[l:l + 1, :] * g0 * g1)
        dsm = [functools.reduce(jnp.add, dx[jj:]) for jj in range(depth)]
        dsm[0] = dsm[0] - functools.reduce(jnp.add, dx)
        inner = functools.reduce(jnp.add, [a * b for a, b in zip(sm, dsm)])
        for l in range(depth):
            o_ref[l:l + 1, :] = sm[l] * (dsm[l] - inner)

    return pl.pallas_call(body, name="lb_bwd", out_shape=jax.ShapeDtypeStruct(p.shape, F32))(p, dlb)


def _exchange(gather, scatter, name):
    ng, ns = len(gather), len(scatter)
    n = ng + ns

    def body(*refs):
        x_refs, o_refs, sems = refs[:n], refs[n:2 * n], refs[2 * n:]
        exs = []
        if ng:
            exs.append(_Exchange(x_refs[:ng], o_refs[:ng], *sems[:3], scatter=False))
        if ns:
            exs.append(_Exchange(x_refs[ng:], o_refs[ng:], *sems[-3:], scatter=True))
        for ex in exs:
            ex.start()
        for ex in exs:
            ex.finish()

    out_shape = [jax.ShapeDtypeStruct((N_DEV,) + x.shape, x.dtype) for x in gather]
    out_shape += [jax.ShapeDtypeStruct(x.shape, x.dtype) for x in scatter]
    return pl.pallas_call(
        body, name=name, in_specs=[_ANY] * n, out_specs=[_ANY] * n, out_shape=out_shape,
        scratch_shapes=(_exchange_sems(ng) if ng else []) + (_exchange_sems(ns) if ns else []),
        compiler_params=pltpu.CompilerParams(has_side_effects=True),
    )(*gather, *scatter)


def _gather_two_level(xs, name):
    n = len(xs)

    def body(*refs):
        x_refs, o_refs = refs[:n], refs[n:2 * n]
        send_sems, recv_sems, loc_sems = refs[2 * n:]
        x, y, c = lax.axis_index("x"), lax.axis_index("y"), lax.axis_index("c")
        chips = [(1 - x, y), (x, 1 - y), (1 - x, 1 - y)]
        idx = lambda px, py, pc: 4 * px + 2 * py + pc

        def copy(a, k, block, to, src=None):
            slot = o_refs[a].at[idx(*block)]
            return pltpu.make_async_remote_copy(src_ref=slot if src is None else src, dst_ref=slot,
                                                send_sem=send_sems.at[a, k], recv_sem=recv_sems.at[a, k],
                                                device_id=to, device_id_type=pl.DeviceIdType.MESH)

        me, sib = (x, y, c), (x, y, 1 - c)
        local = [pltpu.make_async_copy(x_refs[a], o_refs[a].at[idx(*me)], loc_sems.at[a]) for a in range(n)]
        first = [copy(a, 0, me, sib, src=x_refs[a]) for a in range(n)]
        first += [copy(a, 1 + j, me, (*chip, c), src=x_refs[a]) for j, chip in enumerate(chips) for a in range(n)]
        for cp in local + first:
            cp.start()
        passed = []
        for j, chip in enumerate(chips):
            for a in range(n):
                copy(a, 1 + j, (*chip, c), me).wait_recv()
                cp = copy(a, 4 + j, (*chip, c), sib)
                cp.start()
                passed.append(cp)
        for a in range(n):
            copy(a, 0, sib, me).wait_recv()
            for j, chip in enumerate(chips):
                copy(a, 4 + j, (*chip, 1 - c), me).wait_recv()
        for cp in first + passed:
            cp.wait_send()
        for cp in local:
            cp.wait()

    return pl.pallas_call(
        body, name=name, in_specs=[_ANY] * n, out_specs=[_ANY] * n,
        out_shape=[jax.ShapeDtypeStruct((N_DEV,) + x.shape, x.dtype) for x in xs],
        scratch_shapes=_exchange_sems(n), compiler_params=pltpu.CompilerParams(has_side_effects=True),
    )(*xs)


def _adamw(gp, w, m, v, name):
    r, cc = w.shape
    tr = 256 if r % 256 == 0 else r

    def body(g_ref, w_ref, m_ref, v_ref, go_ref, d_ref, mo_ref, vo_ref):
        g = g_ref[0].astype(F32)
        for s in range(1, N_DEV):
            g = g + g_ref[s].astype(F32)
        go_ref[...] = g
        mn = ADAM_B1 * m_ref[...] + (1.0 - ADAM_B1) * g
        vn = ADAM_B2 * v_ref[...] + (1.0 - ADAM_B2) * (g * g)
        m_hat = mn / (1.0 - ADAM_B1 ** ADAM_STEP)
        v_hat = vn / (1.0 - ADAM_B2 ** ADAM_STEP)
        d_ref[...] = -ADAM_LR * (m_hat / (jnp.sqrt(v_hat) + ADAM_EPS) + ADAM_WD * w_ref[...])
        mo_ref[...] = mn
        vo_ref[...] = vn

    bs = pl.BlockSpec((tr, cc), lambda i: (i, 0))
    sh = jax.ShapeDtypeStruct((r, cc), F32)
    return pl.pallas_call(
        body, name=name, grid=(r // tr,),
        in_specs=[pl.BlockSpec((N_DEV, tr, cc), lambda i: (0, i, 0)), bs, bs, bs],
        out_specs=[bs, bs, bs, bs], out_shape=[sh, sh, sh, sh],
        compiler_params=_cp(("parallel",)),
    )(gp, w, m, v)


def _pack_cols(w):
    parts, pos = [], 0
    for pstart, ostart, width in _PACK:
        if pstart != pos:
            parts.append(jnp.zeros(w.shape[:-1] + (pstart - pos,), w.dtype))
        parts.append(w[..., ostart:ostart + width])
        pos = pstart + width
    return jnp.concatenate(parts, axis=-1)


def _unpack_cols(wp):
    by_orig = sorted(_PACK, key=lambda t: t[1])
    return jnp.concatenate([wp[..., p:p + wd] for p, _, wd in by_orig], axis=-1)


_LAYER_SHARDED = ("w_in", "conv_w", "w_conv_out", "w_hg_out", "w_att_out", "w_out")
_NARROW = ("w_in", "w_conv_out", "w_hg_out", "w_att_out", "w_out")
_REPLICATED = ("norm_g", "conv_b", "conv_ln_g", "conv_ln_b", "hg_lower_bounds", "hg_norm_g", "q_norm_g", "k_norm_g",
               "attn_sinks")
_WEIGHTS = ("meta_tokens", "norm_g", "w_in", "conv_w", "conv_b", "conv_ln_g", "conv_ln_b", "w_conv_out",
            "hg_lower_bounds", "hg_norm_g", "w_hg_out", "q_norm_g", "k_norm_g", "attn_sinks", "w_att_out", "w_out")
_ROW_SHARDED = ("w_out",)


def _assemble(name, g):
    if name in _ROW_SHARDED:
        return g.reshape((N_DEV * g.shape[1],) + g.shape[2:])
    full = jnp.moveaxis(g, 0, -2)
    return full.reshape(full.shape[:-2] + (N_DEV * full.shape[-1],))


def _split(name, full):
    if name in _ROW_SHARDED:
        return full.reshape((N_DEV, full.shape[0] // N_DEV) + full.shape[1:])
    c = full.shape[-1] // N_DEV
    return jnp.moveaxis(full.reshape(full.shape[:-1] + (N_DEV, c)), -2, 0)


def _layer_weights(gathered):
    full = {k: _assemble(k, g) for k, g in zip(_LAYER_SHARDED, gathered)}
    wp = _pack_cols(full["w_in"])
    return dict(wp=wp, wpt=wp.T, cw=full["conv_w"], wa=full["w_conv_out"], wb=full["w_hg_out"],
                wc=full["w_att_out"], wo=full["w_out"])


def _layer_fwd(h, lw, sp, gather):
    u, hn, gathered = _inproj_fwd(h, sp["norm_g"], lw["wp"], gather)
    ya, y_conv, yb, states = _conv_hgrn_fwd(u, lw["cw"], sp["conv_b"], sp["conv_ln_g"], sp["conv_ln_b"], sp["lb"],
                                            sp["hg_norm_g"])
    yc = _swa_fwd(u, sp["qg"], sp["kg"], sp["sinks"])
    h_next = _mix_fwd(h, u, ya, yb, yc, lw["wa"], lw["wb"], lw["wc"], lw["wo"])
    return h_next, (h, u, hn, ya, yb, yc, states, y_conv), gathered


def _layer_bwd(dh, saved, lw, sp, stacked, layer, depth):
    h_l, u, hn, ya, yb, yc, states, y_conv = saved
    du, dya, dyb, dyc, dwa, dwb, dwc, dwo = _mix_bwd(dh, u, ya, yb, yc, lw["wa"], lw["wb"], lw["wc"], lw["wo"])
    du, dy, dlg, dlb_ = _conv_bwd1(u, y_conv, dya, du, sp["conv_ln_g"], sp["conv_ln_b"])
    du, dcw, dcb = _conv_bwd2(u, dy, du, lw["cw"])
    du, dlbl, dgg = _hgrn_bwd(u, dyb, states, du, sp["lb"], sp["hg_norm_g"])
    du, dqg, dkg, dsk = _swa_bwd(u, dyc, du, sp["qg"], sp["kg"], sp["sinks"])
    dwp = _inproj_bwd_dw(hn, du)
    full = dict(w_in=_unpack_cols(dwp), conv_w=dcw, w_conv_out=dwa, w_hg_out=dwb, w_att_out=dwc, w_out=dwo)
    pieces = [_split(k, full[k]).astype(WIRE_DTYPE) for k in _LAYER_SHARDED]
    dh, dng, stacked = _inproj_bwd_dh(du, lw["wpt"], h_l, sp["norm_g"], dh, pieces, stacked, layer, depth)
    fold = lambda a: a[0, :ATT_HEAD_DIM] + a[0, ATT_HEAD_DIM:]
    small = dict(norm_g=dng[0], conv_b=dcb[0], conv_ln_g=dlg[0], conv_ln_b=dlb_[0], hg_lower_bounds=dlbl[0],
                 hg_norm_g=dgg[0], q_norm_g=fold(dqg), k_norm_g=fold(dkg), attn_sinks=dsk[:, 0])
    return dh, small, stacked


def _as2d(a):
    return a.reshape((-1, a.shape[-1]))


def kernel(x, meta_tokens, norm_g, w_in, conv_w, conv_b, conv_ln_g, conv_ln_b, w_conv_out, hg_lower_bounds, hg_norm_g, w_hg_out, q_norm_g, k_norm_g, attn_sinks, w_att_out, w_out, loss_target, m_meta_tokens, m_norm_g, m_w_in, m_conv_w, m_conv_b, m_conv_ln_g, m_conv_ln_b, m_w_conv_out, m_hg_lower_bounds, m_hg_norm_g, m_w_hg_out, m_q_norm_g, m_k_norm_g, m_attn_sinks, m_w_att_out, m_w_out, v_meta_tokens, v_norm_g, v_w_in, v_conv_w, v_conv_b, v_conv_ln_g, v_conv_ln_b, v_w_conv_out, v_hg_lower_bounds, v_hg_norm_g, v_w_hg_out, v_q_norm_g, v_k_norm_g, v_attn_sinks, v_w_att_out, v_w_out):
    w = dict(meta_tokens=meta_tokens, norm_g=norm_g, w_in=w_in, conv_w=conv_w, conv_b=conv_b, conv_ln_g=conv_ln_g,
             conv_ln_b=conv_ln_b, w_conv_out=w_conv_out, hg_lower_bounds=hg_lower_bounds, hg_norm_g=hg_norm_g,
             w_hg_out=w_hg_out, q_norm_g=q_norm_g, k_norm_g=k_norm_g, attn_sinks=attn_sinks, w_att_out=w_att_out,
             w_out=w_out)
    m = dict(meta_tokens=m_meta_tokens, norm_g=m_norm_g, w_in=m_w_in, conv_w=m_conv_w, conv_b=m_conv_b,
             conv_ln_g=m_conv_ln_g, conv_ln_b=m_conv_ln_b, w_conv_out=m_w_conv_out, hg_lower_bounds=m_hg_lower_bounds,
             hg_norm_g=m_hg_norm_g, w_hg_out=m_w_hg_out, q_norm_g=m_q_norm_g, k_norm_g=m_k_norm_g,
             attn_sinks=m_attn_sinks, w_att_out=m_w_att_out, w_out=m_w_out)
    v = dict(meta_tokens=v_meta_tokens, norm_g=v_norm_g, w_in=v_w_in, conv_w=v_conv_w, conv_b=v_conv_b,
             conv_ln_g=v_conv_ln_g, conv_ln_b=v_conv_ln_b, w_conv_out=v_w_conv_out, hg_lower_bounds=v_hg_lower_bounds,
             hg_norm_g=v_hg_norm_g, w_hg_out=v_w_hg_out, q_norm_g=v_q_norm_g, k_norm_g=v_k_norm_g,
             attn_sinks=v_attn_sinks, w_att_out=v_w_att_out, w_out=v_w_out)

    depth = norm_g.shape[0]
    seq = x.shape[1]
    lp = -(-(seq + CHUNK) // TM_MM) * TM_MM
    tail = lp - seq - CHUNK
    zeros = lambda n: jnp.zeros((n, D_MODEL), F32)

    def shards(l):
        return [w[k][l].astype(MXU_DTYPE) if k in _NARROW else w[k][l] for k in _LAYER_SHARDED]

    first = _gather_two_level(shards(0) + [meta_tokens], "gather_first")
    gathered, meta_full = first[:-1], _assemble("meta_tokens", first[-1])
    h = jnp.concatenate([zeros(META_PAD), meta_full, x[0], zeros(tail)], axis=0)
    target_p = jnp.concatenate([zeros(CHUNK), loss_target[0], zeros(tail)], axis=0)

    lb_all = _lb_fwd(hg_lower_bounds)
    tile2 = lambda a: jnp.concatenate([a, a], axis=-1)
    row = lambda a, l: a[l][None, :]

    def small_rows(l):
        sp = {k: row(w[k], l) for k in ("norm_g", "conv_b", "conv_ln_g", "conv_ln_b", "hg_norm_g")}
        sp.update(lb=row(lb_all, l), qg=tile2(row(q_norm_g, l)), kg=tile2(row(k_norm_g, l)), sinks=attn_sinks[l])
        return sp

    layer_w, saved = [], []
    for l in range(depth):
        layer_w.append(_layer_weights(gathered))
        h, sv, gathered = _layer_fwd(h, layer_w[l], small_rows(l), shards(l + 1) if l + 1 < depth else [])
        saved.append(sv)

    dh, loss_row = _loss_head(h, target_p, seq)
    loss = lax.psum(loss_row[0, 0], ("x", "y", "c"))

    stacked, small_grads = None, [None] * depth
    for l in reversed(range(depth)):
        dh, small_grads[l], stacked = _layer_bwd(dh, saved[l], layer_w[l], small_rows(l), stacked, l, depth)
    grad_x = dh[CHUNK:CHUNK + seq]
    grads = {k: jnp.stack([small_grads[l][k] for l in range(depth)]) for k in _REPLICATED}
    grads["hg_lower_bounds"] = _lb_bwd(hg_lower_bounds, grads["hg_lower_bounds"])

    small = jnp.concatenate([grads[k].reshape(-1) for k in _REPLICATED])
    small = jnp.concatenate([small, jnp.zeros((-small.shape[0] % 128,), F32)]).reshape(-1, 128)
    small_all, meta_pieces = _exchange([small], [_split("meta_tokens", dh[META_PAD:CHUNK])], "exchange_small_grads")
    small_all = small_all.reshape(N_DEV, -1)

    out_g, out_d, out_m, out_v = {}, {}, {}, {}
    for k, gp in zip(("meta_tokens",) + _LAYER_SHARDED, [meta_pieces] + stacked):
        shp = w[k].shape
        res = _adamw(gp.reshape((N_DEV,) + _as2d(w[k]).shape), _as2d(w[k]), _as2d(m[k]), _as2d(v[k]), "adamw_" + k)
        out_g[k], out_d[k], out_m[k], out_v[k] = (r.reshape(shp) for r in res)
    off = 0
    for k in _REPLICATED:
        shp = w[k].shape
        n = w[k].size
        gp = small_all[:, off:off + n].reshape((N_DEV,) + shp)
        off += n
        res = _adamw(gp, w[k], m[k], v[k], "adamw_" + k)
        out_g[k], out_d[k], out_m[k], out_v[k] = res

    return (loss, grad_x[None], *[out_g[k] for k in _WEIGHTS], *[out_d[k] for k in _WEIGHTS],
            *[out_m[k] for k in _WEIGHTS], *[out_v[k] for k in _WEIGHTS])
```

```python
import functools

import jax
import jax.numpy as jnp
from jax import lax
from jax.experimental import pallas as pl
from jax.experimental.pallas import tpu as pltpu

F32 = jnp.float32
MXU_DTYPE = jnp.bfloat16
WIRE_DTYPE = jnp.bfloat16
U_DTYPE = jnp.bfloat16

D_MODEL = 1024
CHUNK = 64
N_META = 16
META_PAD = CHUNK - N_META
D_CONV = 512
CONV_WIDTH = 31
HG_HEADS = 4
HG_D = 128
D_HG = HG_HEADS * HG_D
F_FLOOR = 1e-30
ATT_Q_HEADS = 8
ATT_HEAD_DIM = 64
D_ATT = 512
D_KV = 128
WINDOW_CHUNKS = 2
EPS = 1e-6
D_IN = 7936
N_DEV = 8

ADAM_LR = 0.001
ADAM_B1 = 0.9
ADAM_B2 = 0.999
ADAM_EPS = 1e-08
ADAM_WD = 0.01
ADAM_STEP = 10

NP = 8192
OFF_B, W_B = 0, 2048
OFF_A, W_A = 2048, 1024
OFF_G, W_G = 3072, 3072
OFF_C, W_C = 6144, 1536
OFF_AG, W_AG = 7680, 512
_PACK = ((0, 1536, 2048), (2048, 0, 1024), (3072, 4864, 3072), (6144, 3584, 512), (6656, 4352, 512),
         (7168, 4096, 256), (7680, 1024, 512))
_PAD_AT, _PAD_W = 7424, 256

TM_MM = 1280
TM_BR = 256
TM_WIDE = 640
TM_MIX = 640
HALO = 32
EXP_CLAMP = 80.0
VMEM_LIMIT = 56 * 1024 * 1024

_HI = lax.Precision.HIGHEST


def _cp(sem):
    return pltpu.CompilerParams(dimension_semantics=sem, vmem_limit_bytes=VMEM_LIMIT)


def _sig(x):
    return 1.0 / (1.0 + jnp.exp(-x))


def _dot(a, b):
    return jnp.dot(a.astype(MXU_DTYPE), b.astype(MXU_DTYPE), preferred_element_type=F32)


def _dot_nt(a, b):
    return lax.dot_general(a.astype(MXU_DTYPE), b.astype(MXU_DTYPE), (((1,), (1,)), ((), ())),
                           preferred_element_type=F32)


def _dot_tn(a, b):
    return lax.dot_general(a.astype(MXU_DTYPE), b.astype(MXU_DTYPE), (((0,), (0,)), ((), ())),
                           preferred_element_type=F32)


def _rnd(x):
    return x.astype(MXU_DTYPE).astype(F32)


def _iota(shape, dim):
    return lax.broadcasted_iota(jnp.int32, shape, dim)


def _full_spec(shape):
    nd = len(shape)
    return pl.BlockSpec(shape, lambda *_: (0,) * nd)


def _my_index():
    return 4 * lax.axis_index("x") + 2 * lax.axis_index("y") + lax.axis_index("c")


def _mesh_id(p):
    return (p >> 2, (p >> 1) & 1, p & 1)


class _Exchange:
    def __init__(self, x_refs, o_refs, send_sems, recv_sems, loc_sems, scatter, dst=lambda o, s: o.at[s]):
        me = _my_index()
        self.local, self.sends, self.recvs = [], [], []
        for a, (x, o) in enumerate(zip(x_refs, o_refs)):
            mine = x.at[me] if scatter else x
            self.local.append(pltpu.make_async_copy(mine, dst(o, me), loc_sems.at[a]))
            for k in range(1, N_DEV):
                to, frm = (me + k) % N_DEV, (me + N_DEV - k) % N_DEV
                sems = dict(send_sem=send_sems.at[a, k - 1], recv_sem=recv_sems.at[a, k - 1],
                            device_id_type=pl.DeviceIdType.MESH)
                self.sends.append(pltpu.make_async_remote_copy(
                    src_ref=x.at[to] if scatter else x, dst_ref=dst(o, me), device_id=_mesh_id(to), **sems))
                self.recvs.append(pltpu.make_async_remote_copy(
                    src_ref=mine, dst_ref=dst(o, frm), device_id=_mesh_id(frm), **sems))

    def start(self):
        for cp in self.local + self.sends:
            cp.start()

    def finish(self):
        for cp in self.recvs:
            cp.wait_recv()
        for cp in self.sends:
            cp.wait_send()
        for cp in self.local:
            cp.wait()


def _exchange_sems(n):
    return [pltpu.SemaphoreType.DMA((n, N_DEV - 1)), pltpu.SemaphoreType.DMA((n, N_DEV - 1)),
            pltpu.SemaphoreType.DMA((n,))]


_ANY = pl.BlockSpec(memory_space=pl.ANY)


def _inproj_fwd(h, g, wp, gather=()):
    lp = h.shape[0]
    tm, tn = TM_MM, 2048
    ni, nj = lp // tm, NP // tn
    n = len(gather)

    def body(h_ref, g_ref, w_ref, *rest):
        x_refs, (u_ref, hn_ref), o_refs = rest[:n], rest[n:n + 2], rest[n + 2:2 * n + 2]
        hs_ref, sems = rest[2 * n + 2], rest[2 * n + 3:]
        i, j = pl.program_id(0), pl.program_id(1)
        if n:
            @pl.when((i == 0) & (j == 0))
            def _():
                _Exchange(x_refs, o_refs, *sems, scatter=False).start()

        @pl.when(j == 0)
        def _():
            x = h_ref[...]
            r = lax.rsqrt(jnp.mean(x * x, axis=-1, keepdims=True) + EPS)
            hn = (x * r * g_ref[...]).astype(MXU_DTYPE)
            hs_ref[...] = hn
            hn_ref[...] = hn
        u_ref[...] = jnp.dot(hs_ref[...], w_ref[...], preferred_element_type=F32).astype(U_DTYPE)
        if n:
            @pl.when((i == ni - 1) & (j == nj - 1))
            def _():
                _Exchange(x_refs, o_refs, *sems, scatter=False).finish()

    res = pl.pallas_call(
        body, name="inproj_fwd_gather" if n else "inproj_fwd", grid=(ni, nj),
        in_specs=[pl.BlockSpec((tm, D_MODEL), lambda i, j: (i, 0)), pl.BlockSpec((1, D_MODEL), lambda i, j: (0, 0)),
                  pl.BlockSpec((D_MODEL, tn), lambda i, j: (0, j))] + [_ANY] * n,
        out_specs=[pl.BlockSpec((tm, tn), lambda i, j: (i, j)), pl.BlockSpec((tm, D_MODEL), lambda i, j: (i, 0))]
        + [_ANY] * n,
        out_shape=[jax.ShapeDtypeStruct((lp, NP), U_DTYPE), jax.ShapeDtypeStruct((lp, D_MODEL), MXU_DTYPE)]
        + [jax.ShapeDtypeStruct((N_DEV,) + x.shape, x.dtype) for x in gather],
        scratch_shapes=[pltpu.VMEM((tm, D_MODEL), MXU_DTYPE)] + (_exchange_sems(n) if n else []),
        compiler_params=_cp(("arbitrary", "arbitrary")),
    )(h, g, wp, *gather)
    return res[0], res[1], list(res[2:])


def _inproj_bwd_dh(du, wpt, h, g, dh_next, pieces, stacked, layer, depth):
    lp = h.shape[0]
    tm, tk = TM_MM, 1024
    ni, nk = lp // tm, NP // tk
    n = len(pieces)
    n_acc = 0 if stacked is None else n

    def body(du_ref, w_ref, h_ref, g_ref, dhn_ref, *rest):
        x_refs, (dh_ref, dg_ref), o_refs = rest[:n], rest[n + n_acc:n + n_acc + 2], rest[n + n_acc + 2:2 * n + n_acc + 2]
        acc_ref, sems = rest[2 * n + n_acc + 2], rest[2 * n + n_acc + 3:]
        i, k = pl.program_id(0), pl.program_id(1)
        slot = lambda o, s: o.at[s, layer]

        @pl.when((i == 0) & (k == 0))
        def _():
            _Exchange(x_refs, o_refs, *sems, scatter=True, dst=slot).start()
            dg_ref[...] = jnp.zeros_like(dg_ref)

        @pl.when(k == 0)
        def _():
            acc_ref[...] = jnp.zeros_like(acc_ref)

        acc_ref[...] += jnp.dot(du_ref[...], w_ref[...], preferred_element_type=F32)

        @pl.when(k == nk - 1)
        def _():
            dhn = acc_ref[...]
            x = h_ref[...]
            r = lax.rsqrt(jnp.mean(x * x, axis=-1, keepdims=True) + EPS)
            xh = x * r
            dg_ref[...] += jnp.sum(dhn * xh, axis=0, keepdims=True)
            dxh = dhn * g_ref[...]
            dx = r * (dxh - xh * jnp.mean(dxh * xh, axis=-1, keepdims=True))
            dh_ref[...] = dhn_ref[...] + dx

        @pl.when((i == ni - 1) & (k == nk - 1))
        def _():
            _Exchange(x_refs, o_refs, *sems, scatter=True, dst=slot).finish()

    acc_in = [] if stacked is None else list(stacked)
    res = pl.pallas_call(
        body, name="inproj_bwd_dh_scatter", grid=(ni, nk),
        in_specs=[pl.BlockSpec((tm, tk), lambda i, k: (i, k)), pl.BlockSpec((tk, D_MODEL), lambda i, k: (k, 0)),
                  pl.BlockSpec((tm, D_MODEL), lambda i, k: (i, 0)), pl.BlockSpec((1, D_MODEL), lambda i, k: (0, 0)),
                  pl.BlockSpec((tm, D_MODEL), lambda i, k: (i, 0))] + [_ANY] * (n + n_acc),
        out_specs=[pl.BlockSpec((tm, D_MODEL), lambda i, k: (i, 0)), pl.BlockSpec((1, D_MODEL), lambda i, k: (0, 0))]
        + [_ANY] * n,
        out_shape=[jax.ShapeDtypeStruct((lp, D_MODEL), F32), jax.ShapeDtypeStruct((1, D_MODEL), F32)]
        + [jax.ShapeDtypeStruct((N_DEV, depth) + p.shape[1:], p.dtype) for p in pieces],
        scratch_shapes=[pltpu.VMEM((tm, D_MODEL), F32)] + _exchange_sems(n),
        input_output_aliases={5 + n + a: 2 + a for a in range(n_acc)},
        compiler_params=_cp(("arbitrary", "arbitrary")),
    )(du, wpt, h, g, dh_next, *pieces, *acc_in)
    return res[0], res[1], list(res[2:])


def _inproj_bwd_dw(hn, du):
    lp = hn.shape[0]
    tm, tn = TM_MM, 2048

    def body(hn_ref, du_ref, dw_ref):
        @pl.when(pl.program_id(1) == 0)
        def _():
            dw_ref[...] = jnp.zeros_like(dw_ref)
        dw_ref[...] += _dot_tn(hn_ref[...], du_ref[...])

    return pl.pallas_call(
        body, name="inproj_bwd_dw", grid=(NP // tn, lp // tm),
        in_specs=[pl.BlockSpec((tm, D_MODEL), lambda j, m: (m, 0)), pl.BlockSpec((tm, tn), lambda j, m: (m, j))],
        out_specs=pl.BlockSpec((D_MODEL, tn), lambda j, m: (0, j)),
        out_shape=jax.ShapeDtypeStruct((D_MODEL, NP), F32),
        compiler_params=_cp(("parallel", "arbitrary")),
    )(hn, du)


N_SHIFT = 8
CONV_SUB = 32


def _shift_copies(src_ref, sh_ref):
    n = sh_ref.shape[1]
    for b in range(1, N_SHIFT):
        sh_ref[b - 1, :, :] = src_ref[pl.ds(b, n), :]


def _window(src_ref, sh_ref, off, r0, n):
    a, b = divmod(off, N_SHIFT)
    start = pl.multiple_of(r0 + a * N_SHIFT, N_SHIFT)
    if b == 0:
        return src_ref[pl.ds(start, n), :]
    return sh_ref[b - 1, pl.ds(start, n), :]


def _shift_scratch(tm):
    return pltpu.VMEM((N_SHIFT - 1, tm + HALO - N_SHIFT, D_CONV), F32)


def _glu_ext(a_ref, ah_ref, ext_ref, sh_ref, i, tm):
    rows = i * tm + _iota((tm, 1), 0)
    a = a_ref[...].astype(F32)
    p, sq = a[:, :D_CONV], _sig(a[:, D_CONV:])
    valid = rows >= META_PAD
    ah = ah_ref[...].astype(F32)
    ext_ref[0:HALO, :] = jnp.where(i > 0, ah[:, :D_CONV] * _sig(ah[:, D_CONV:]), 0.0)
    ext_ref[HALO:HALO + tm, :] = jnp.where(valid, p * sq, 0.0)
    _shift_copies(ext_ref, sh_ref)
    return p, sq, valid


def _layernorm_stats(y):
    mu = jnp.mean(y, axis=-1, keepdims=True)
    yc = y - mu
    rstd = lax.rsqrt(jnp.mean(yc * yc, axis=-1, keepdims=True) + EPS)
    return yc * rstd, rstd


def _conv_specs(tm):
    hb = tm // HALO
    return [pl.BlockSpec((tm, W_A), lambda i: (i, OFF_A // W_A)),
            pl.BlockSpec((HALO, W_A), lambda i: (jnp.maximum(i * hb - 1, 0), OFF_A // W_A)),
            pl.BlockSpec((tm, W_AG), lambda i: (i, OFF_AG // W_AG))]


def _conv_fwd_body(tm):
    def body(a_ref, ah_ref, ag_ref, w_ref, b_ref, lg_ref, lb_ref, ya_ref, y_ref, ext_ref, sh_ref):
        i = pl.program_id(0)
        _glu_ext(a_ref, ah_ref, ext_ref, sh_ref, i, tm)
        base = HALO - (CONV_WIDTH - 1)

        y = jnp.zeros((tm, D_CONV), F32) + b_ref[...]
        for k in range(CONV_WIDTH):
            y = y + w_ref[k:k + 1, :] * _window(ext_ref, sh_ref, base + k, 0, tm)
        y_ref[...] = y
        xh, _ = _layernorm_stats(y)
        yn = xh * lg_ref[...] + lb_ref[...]
        gt = ag_ref[...].astype(F32)
        ya_ref[...] = (yn * _sig(yn) * gt * _sig(gt)).astype(MXU_DTYPE)

    return body


def _dsilu(x, s):
    return s * (1.0 + x * (1.0 - s))


def _conv_bwd1(u, y, dya, du, lg, lb_):
    lp = u.shape[0]
    tm = TM_MIX
    assert lp % tm == 0

    def body(ag_ref, y_ref, dya_ref, lg_ref, lb_ref, du_in, du_ref, dy_ref, dlg_ref, dlb_ref):
        del du_in
        i = pl.program_id(0)

        @pl.when(i == 0)
        def _():
            dlg_ref[...] = jnp.zeros_like(dlg_ref)
            dlb_ref[...] = jnp.zeros_like(dlb_ref)

        xh, rstd = _layernorm_stats(y_ref[...])
        yn = xh * lg_ref[...] + lb_ref[...]
        s1 = _sig(yn)
        gt = ag_ref[...].astype(F32)
        s2 = _sig(gt)
        do = dya_ref[...].astype(F32)
        du_ref[...] = (do * (yn * s1) * _dsilu(gt, s2)).astype(MXU_DTYPE)
        dyn = do * (gt * s2) * _dsilu(yn, s1)
        dlg_ref[...] += jnp.sum(dyn * xh, axis=0, keepdims=True)
        dlb_ref[...] += jnp.sum(dyn, axis=0, keepdims=True)
        dxh = dyn * lg_ref[...]
        dy_ref[...] = rstd * (dxh - jnp.mean(dxh, axis=-1, keepdims=True)
                              - xh * jnp.mean(dxh * xh, axis=-1, keepdims=True))

    rowspec = pl.BlockSpec((tm, D_CONV), lambda i: (i, 0))
    return pl.pallas_call(
        body, name="conv_bwd1", grid=(lp // tm,),
        in_specs=[_conv_specs(tm)[2], rowspec, rowspec, _full_spec((1, D_CONV)), _full_spec((1, D_CONV)),
                  pl.BlockSpec(memory_space=pl.ANY)],
        out_specs=[pl.BlockSpec((tm, W_AG), lambda i: (i, OFF_AG // W_AG)), rowspec,
                   _full_spec((1, D_CONV)), _full_spec((1, D_CONV))],
        out_shape=[jax.ShapeDtypeStruct(du.shape, du.dtype), jax.ShapeDtypeStruct((lp, D_CONV), F32),
                   jax.ShapeDtypeStruct((1, D_CONV), F32), jax.ShapeDtypeStruct((1, D_CONV), F32)],
        input_output_aliases={5: 0},
        compiler_params=_cp(("arbitrary",)),
    )(u, y, dya, lg, lb_, du)


def _conv_bwd2(u, dy, du, cw):
    lp = u.shape[0]
    tm = TM_WIDE
    assert lp % tm == 0
    nb = lp // tm
    hb = tm // HALO

    def body(a_ref, ah_ref, dy_ref, dyn_ref, w_ref, du_in, du_ref, dw_ref, db_ref, ext_ref, sh_ref, edy_ref, shd_ref,
             dwp_ref):
        del du_in
        i = pl.program_id(0)

        @pl.when(i == 0)
        def _():
            dwp_ref[...] = jnp.zeros_like(dwp_ref)
            db_ref[...] = jnp.zeros_like(db_ref)

        _glu_ext(a_ref, ah_ref, ext_ref, sh_ref, i, tm)
        dy_all = dy_ref[...]
        edy_ref[0:tm, :] = dy_all
        edy_ref[tm:tm + HALO, :] = jnp.where(i < nb - 1, dyn_ref[...], 0.0)
        _shift_copies(edy_ref, shd_ref)
        db_ref[...] += jnp.sum(dy_all, axis=0, keepdims=True)
        base = HALO - (CONV_WIDTH - 1)

        def fold8(x):
            parts = [x[s:s + N_SHIFT] for s in range(0, CONV_SUB, N_SHIFT)]
            return functools.reduce(jnp.add, parts)

        def sub(r, carry):
            r0 = pl.multiple_of(r * CONV_SUB, CONV_SUB)
            dy = dy_ref[pl.ds(r0, CONV_SUB), :]
            du0 = jnp.zeros((CONV_SUB, D_CONV), F32)
            for k in range(CONV_WIDTH):
                du0 = du0 + w_ref[k:k + 1, :] * _window(edy_ref, shd_ref, CONV_WIDTH - 1 - k, r0, CONV_SUB)
                dwp_ref[k] += fold8(dy * _window(ext_ref, sh_ref, base + k, r0, CONV_SUB))
            a = a_ref[pl.ds(r0, CONV_SUB), :].astype(F32)
            p, sq = a[:, :D_CONV], _sig(a[:, D_CONV:])
            valid = (i * tm + r0 + _iota((CONV_SUB, 1), 0)) >= META_PAD
            du0 = jnp.where(valid, du0, 0.0)
            du_ref[pl.ds(r0, CONV_SUB), :] = jnp.concatenate([du0 * sq, du0 * p * sq * (1.0 - sq)],
                                                             axis=1).astype(MXU_DTYPE)
            return carry

        lax.fori_loop(0, tm // CONV_SUB, sub, 0)

        @pl.when(i == nb - 1)
        def _():
            dw_ref[...] = jnp.sum(dwp_ref[...], axis=1)

    return pl.pallas_call(
        body, name="conv_bwd2", grid=(nb,),
        in_specs=_conv_specs(tm)[:2] + [pl.BlockSpec((tm, D_CONV), lambda i: (i, 0)),
                                        pl.BlockSpec((HALO, D_CONV), lambda i: (jnp.minimum((i + 1) * hb, nb * hb - 1), 0)),
                                        _full_spec((CONV_WIDTH, D_CONV)), pl.BlockSpec(memory_space=pl.ANY)],
        out_specs=[pl.BlockSpec((tm, W_A), lambda i: (i, OFF_A // W_A)), _full_spec((CONV_WIDTH, D_CONV)),
                   _full_spec((1, D_CONV))],
        out_shape=[jax.ShapeDtypeStruct(du.shape, du.dtype), jax.ShapeDtypeStruct((CONV_WIDTH, D_CONV), F32),
                   jax.ShapeDtypeStruct((1, D_CONV), F32)],
        scratch_shapes=[pltpu.VMEM((HALO + tm, D_CONV), F32), _shift_scratch(tm),
                        pltpu.VMEM((tm + HALO, D_CONV), F32), _shift_scratch(tm),
                        pltpu.VMEM((CONV_WIDTH, N_SHIFT, D_CONV), F32)],
        input_output_aliases={5: 0},
        compiler_params=_cp(("arbitrary",)),
    )(u, u, dy, dy, cw, du)


HG_T = 128
TM_HG = TM_WIDE
HG_HALF = HG_T // 2


def _hg_chunk_fwd(blk, lb, valid, tri):
    bq, bf, v = blk[:, 0:512], blk[:, 512:1024], blk[:, 1024:1536]
    sgq = _sig(bq)
    qt = bq * sgq
    sz = _sig(bf)
    f = lb + (1.0 - lb) * sz
    g = jnp.where(valid, jnp.log(jnp.maximum(f, F_FLOOR)), 0.0)
    k = jnp.where(valid, (1.0 - lb) * (1.0 - sz), 0.0)
    b = jnp.dot(tri, g, precision=_HI, preferred_element_type=F32)
    ridx = _iota((HG_T, 1), 0)
    pick = lambda r: jnp.sum(jnp.where(ridx == r, b, 0.0), axis=0, keepdims=True)
    return dict(bq=bq, sgq=sgq, qt=qt, sz=sz, f=f, k=k, v=v, b=b, top=ridx < HG_HALF, rx=pick(HG_HALF - 1),
                rdt=pick(HG_HALF // 2 - 1), rdb=pick(HG_HALF + HG_HALF // 2 - 1), bl=pick(HG_T - 1))


def _hg_head(p, sl):
    top, b, qt, k = p["top"], p["b"][:, sl], p["qt"][:, sl], p["k"][:, sl]
    rx, bl = p["rx"][:, sl], p["bl"][:, sl]
    rd = jnp.where(top, p["rdt"][:, sl], p["rdb"][:, sl])
    eqx = jnp.where(top, 0.0, jnp.exp(jnp.minimum(b - rx, 0.0)))
    ekx = jnp.where(top, jnp.exp(jnp.minimum(rx - b, 0.0)), 0.0)
    eqd = jnp.exp(jnp.minimum(b - rd, EXP_CLAMP))
    ekd = jnp.exp(jnp.minimum(rd - b, EXP_CLAMP))
    e = jnp.exp(b)
    ekl = jnp.exp(bl - b)
    qx, kx, qd, kd = _rnd(qt * eqx), _rnd(k * ekx), _rnd(qt * eqd), _rnd(k * ekd)
    qcat = jnp.concatenate([qx, jnp.where(top, qd, 0.0), jnp.where(top, 0.0, qd)], axis=1)
    kcat = jnp.concatenate([kx, jnp.where(top, kd, 0.0), jnp.where(top, 0.0, kd)], axis=1)
    return dict(v=p["v"][:, sl], eqx=eqx, ekx=ekx, eqd=eqd, ekd=ekd, e=e, ekl=ekl, el=jnp.exp(bl), qx=qx, kx=kx,
                qd=qd, kd=kd, qe=qt * e, kl=k * ekl, qcat=qcat, kcat=kcat)


def _hgrn_fwd_body(tm):
    cpb = tm // HG_T

    def body(u_ref, lb_ref, gg_ref, y_ref, st_ref, s_ref):
        i = pl.program_id(0)

        @pl.when(i == 0)
        def _():
            s_ref[...] = jnp.zeros_like(s_ref)

        lbv = lb_ref[...]
        ggv = gg_ref[...]
        tri = (_iota((HG_T, HG_T), 0) >= _iota((HG_T, HG_T), 1)).astype(F32)

        def chunk(c, carry):
            r0 = pl.multiple_of(c * HG_T, HG_T)
            blk = u_ref[pl.ds(r0, HG_T), :].astype(F32)
            valid = (i * tm + r0 + _iota((HG_T, 1), 0)) >= META_PAD
            q = _hg_chunk_fwd(blk, lbv, valid, tri)
            outs = []
            for hh in range(HG_HEADS):
                h = _hg_head(q, slice(hh * HG_D, (hh + 1) * HG_D))
                a = jnp.where(tri > 0, _dot_nt(h["qcat"], h["kcat"]), 0.0)
                st = s_ref[hh]
                st_ref[c, hh] = st
                o = _dot(a, h["v"]) + _dot_nt(h["qe"], st)
                s_ref[hh] = st * h["el"] + _dot_tn(h["v"], h["kl"])
                rs = lax.rsqrt(jnp.mean(o * o, axis=-1, keepdims=True) + EPS)
                outs.append(o * rs * ggv)
            on = jnp.concatenate(outs, axis=1)
            bg = blk[:, 1536:2048]
            y_ref[pl.ds(r0, HG_T), :] = (on * bg * _sig(bg)).astype(MXU_DTYPE)
            return carry

        lax.fori_loop(0, cpb, chunk, 0, unroll=True)

    return body


def _conv_hgrn_fwd(u, cw, cb, lg, lb_, hlb, gg):
    lp = u.shape[0]
    tm = TM_WIDE
    assert lp % tm == 0
    cpb = tm // HG_T
    conv_body, hgrn_body = _conv_fwd_body(tm), _hgrn_fwd_body(tm)

    def body(a_ref, ah_ref, ag_ref, w_ref, b_ref, lg_ref, lb_ref, ub_ref, hlb_ref, gg_ref,
             ya_ref, y_ref, yb_ref, st_ref, ext_ref, sh_ref, s_ref):
        hgrn_body(ub_ref, hlb_ref, gg_ref, yb_ref, st_ref, s_ref)
        conv_body(a_ref, ah_ref, ag_ref, w_ref, b_ref, lg_ref, lb_ref, ya_ref, y_ref, ext_ref, sh_ref)

    rowspec = pl.BlockSpec((tm, D_CONV), lambda i: (i, 0))
    return pl.pallas_call(
        body, name="conv_hgrn_fwd", grid=(lp // tm,),
        in_specs=_conv_specs(tm) + [_full_spec((CONV_WIDTH, D_CONV))] + [_full_spec((1, D_CONV))] * 3
        + [pl.BlockSpec((tm, W_B), lambda i: (i, 0)), _full_spec((1, D_HG)), _full_spec((1, HG_D))],
        out_specs=[rowspec, rowspec, pl.BlockSpec((tm, D_HG), lambda i: (i, 0)),
                   pl.BlockSpec((cpb, HG_HEADS, HG_D, HG_D), lambda i: (i, 0, 0, 0))],
        out_shape=[jax.ShapeDtypeStruct((lp, D_CONV), MXU_DTYPE), jax.ShapeDtypeStruct((lp, D_CONV), F32),
                   jax.ShapeDtypeStruct((lp, D_HG), MXU_DTYPE),
                   jax.ShapeDtypeStruct((lp // HG_T, HG_HEADS, HG_D, HG_D), F32)],
        scratch_shapes=[pltpu.VMEM((HALO + tm, D_CONV), F32), _shift_scratch(tm),
                        pltpu.VMEM((HG_HEADS, HG_D, HG_D), F32)],
        compiler_params=_cp(("arbitrary",)),
    )(u, u, u, cw, cb, lg, lb_, u, hlb, gg)


def _hgrn_bwd(u, dyb, states, du, lb, gg):
    lp = u.shape[0]
    tm = TM_HG
    cpb = tm // HG_T
    nb = lp // tm

    def body(u_ref, dy_ref, st_ref, lb_ref, gg_ref, du_in, du_ref, dlb_ref, dgg_ref, ds_ref):
        del du_in
        ii = pl.program_id(0)
        i = nb - 1 - ii

        @pl.when(ii == 0)
        def _():
            ds_ref[...] = jnp.zeros_like(ds_ref)
            dlb_ref[...] = jnp.zeros_like(dlb_ref)
            dgg_ref[...] = jnp.zeros_like(dgg_ref)

        lbv = lb_ref[...]
        ggv = gg_ref[...]
        lower = _iota((HG_T, HG_T), 0) >= _iota((HG_T, HG_T), 1)
        tri = lower.astype(F32)
        triu = (_iota((HG_T, HG_T), 0) <= _iota((HG_T, HG_T), 1)).astype(F32)
        ridx = _iota((HG_T, 1), 0)

        def chunk(cc, carry):
            c = cpb - 1 - cc
            r0 = pl.multiple_of(c * HG_T, HG_T)
            blk = u_ref[pl.ds(r0, HG_T), :].astype(F32)
            valid = (i * tm + r0 + _iota((HG_T, 1), 0)) >= META_PAD
            q = _hg_chunk_fwd(blk, lbv, valid, tri)
            top = q["top"]
            bg = blk[:, 1536:2048]
            sg = _sig(bg)
            dy = dy_ref[pl.ds(r0, HG_T), :].astype(F32)
            don_all = dy * bg * sg
            dqt_l, dk_l, dv_l, db_l, dbl_l, on_l = [], [], [], [], [], []
            dgg = jnp.zeros((1, HG_D), F32)
            for hh in range(HG_HEADS):
                sl = slice(hh * HG_D, (hh + 1) * HG_D)
                h = _hg_head(q, sl)
                qe, kl, v, el, qcat, kcat = h["qe"], h["kl"], h["v"], h["el"], h["qcat"], h["kcat"]
                a = jnp.where(lower, _dot_nt(qcat, kcat), 0.0)
                st = st_ref[c, hh]
                o = _dot(a, v) + _dot_nt(qe, st)
                rs = lax.rsqrt(jnp.mean(o * o, axis=-1, keepdims=True) + EPS)
                xh = o * rs
                on_l.append(xh * ggv)
                don = don_all[:, sl]
                dgg = dgg + jnp.sum(don * xh, axis=0, keepdims=True)
                dxh = don * ggv
                do = rs * (dxh - xh * jnp.mean(dxh * xh, axis=-1, keepdims=True))
                dst = ds_ref[hh]
                dv = _dot_tn(a, do) + _dot_nt(kl, dst)
                da = jnp.where(lower, _dot_nt(do, v), 0.0)
                dqe = _dot(do, st)
                dkl = _dot(v, dst)
                d_el = jnp.sum(st * dst, axis=0, keepdims=True)
                ds_ref[hh] = _dot_tn(do, qe) + dst * el
                dqc = _dot(da, kcat)
                dkc = _dot_tn(da, qcat)
                dqx, dqd = dqc[:, :HG_D], jnp.where(top, dqc[:, HG_D:2 * HG_D], dqc[:, 2 * HG_D:])
                dkx, dkd = dkc[:, :HG_D], jnp.where(top, dkc[:, HG_D:2 * HG_D], dkc[:, 2 * HG_D:])
                dqt_l.append(dqx * h["eqx"] + dqd * h["eqd"] + dqe * h["e"])
                dk_l.append(dkx * h["ekx"] + dkd * h["ekd"] + dkl * h["ekl"])
                dv_l.append(dv)
                db_l.append(dqx * h["qx"] - dkx * h["kx"] + dqd * h["qd"] - dkd * h["kd"] + dqe * qe - dkl * kl)
                dbl_l.append(jnp.sum(dkl * kl, axis=0, keepdims=True) + d_el * el)
            dqt = jnp.concatenate(dqt_l, axis=1)
            dk = jnp.concatenate(dk_l, axis=1)
            dv = jnp.concatenate(dv_l, axis=1)
            db = jnp.concatenate(db_l, axis=1) + jnp.where(ridx == HG_T - 1, jnp.concatenate(dbl_l, axis=1), 0.0)
            on = jnp.concatenate(on_l, axis=1)
            dg = jnp.dot(triu, db, precision=_HI, preferred_element_type=F32)
            sz, f = q["sz"], q["f"]
            df = jnp.where(valid & (f > F_FLOOR), dg / f, 0.0)
            dkv = jnp.where(valid, dk, 0.0)
            t = (1.0 - sz) * (df - dkv)
            dlb_ref[...] += jnp.sum(t, axis=0, keepdims=True)
            dz = (1.0 - lbv) * (df - dkv) * sz * (1.0 - sz)
            dbq = dqt * _dsilu(q["bq"], q["sgq"])
            dbg = dy * on * _dsilu(bg, sg)
            dgg_ref[...] += dgg
            du_ref[pl.ds(r0, HG_T), :] = jnp.concatenate([dbq, dz, dv, dbg], axis=1).astype(MXU_DTYPE)
            return carry

        lax.fori_loop(0, cpb, chunk, 0, unroll=True)

    return pl.pallas_call(
        body, name="hgrn_bwd", grid=(nb,),
        in_specs=[pl.BlockSpec((tm, W_B), lambda ii: (nb - 1 - ii, 0)), pl.BlockSpec((tm, D_HG), lambda ii: (nb - 1 - ii, 0)),
                  pl.BlockSpec((cpb, HG_HEADS, HG_D, HG_D), lambda ii: (nb - 1 - ii, 0, 0, 0)),
                  _full_spec((1, D_HG)), _full_spec((1, HG_D)), pl.BlockSpec(memory_space=pl.ANY)],
        out_specs=[pl.BlockSpec((tm, W_B), lambda ii: (nb - 1 - ii, 0)), _full_spec((1, D_HG)), _full_spec((1, HG_D))],
        out_shape=[jax.ShapeDtypeStruct(du.shape, du.dtype), jax.ShapeDtypeStruct((1, D_HG), F32),
                   jax.ShapeDtypeStruct((1, HG_D), F32)],
        scratch_shapes=[pltpu.VMEM((HG_HEADS, HG_D, HG_D), F32)],
        input_output_aliases={5: 0},
        compiler_params=_cp(("arbitrary",)),
    )(u, dyb, states, lb, gg, du)


TM_SWA = 256
PREV_ROWS = WINDOW_CHUNKS * CHUNK
DEAD_ROWS = -(CHUNK + PREV_ROWS + TM_SWA) % 128
N_KEYS = CHUNK + DEAD_ROWS + PREV_ROWS + TM_SWA
LOG2E = 1.4426950408889634
ATT_SCALE2 = ATT_HEAD_DIM ** -0.5 * LOG2E
NEG = -1e30


def _half_sum(x, lo):
    a = jnp.sum(jnp.where(lo, x, 0.0), axis=1, keepdims=True)
    b = jnp.sum(jnp.where(lo, 0.0, x), axis=1, keepdims=True)
    return jnp.where(lo, a, b)


def _half_rms(x, lo):
    return lax.rsqrt(_half_sum(x * x, lo) * (1.0 / ATT_HEAD_DIM) + EPS)


def _swa_mask(i, tm):
    tq = i * tm + _iota((tm, N_KEYS), 0)
    s = _iota((tm, N_KEYS), 1)
    nq = tq >> 6
    kr = i * tm + s - (N_KEYS - tm)
    kc = kr >> 6
    band = (kr >= META_PAD) & (kc >= nq - WINDOW_CHUNKS) & (kc <= nq)
    meta = (nq > WINDOW_CHUNKS) & (s >= META_PAD)
    return ((s < CHUNK) & meta) | ((s >= CHUNK) & band)


def _swa_keys(own_kv, prev_ref, meta_ref, kg, tm):
    kv = jnp.concatenate([meta_ref[...].astype(F32), jnp.zeros((DEAD_ROWS, 2 * D_KV), F32),
                          prev_ref[tm - PREV_ROWS:tm, :].astype(F32), own_kv], axis=0)
    k_raw, v = kv[:, :D_KV], kv[:, D_KV:]
    lo = _iota((1, D_KV), 1) < ATT_HEAD_DIM
    kr = _half_rms(k_raw, lo)
    kn = k_raw * kr * kg
    return k_raw, kr, kn, v, lo


def _placed(x, lo):
    xr = pltpu.roll(x, ATT_HEAD_DIM, 1)
    z = jnp.zeros_like(x)
    return [[jnp.where(lo, x, z).astype(MXU_DTYPE), jnp.where(lo, z, xr).astype(MXU_DTYPE)],
            [jnp.where(lo, xr, z).astype(MXU_DTYPE), jnp.where(lo, z, x).astype(MXU_DTYPE)]]


def _swa_specs(tm, order):
    kvb = (OFF_C + 1024) // 256
    return [pl.BlockSpec((tm, W_C), lambda i: (order(i), OFF_C // W_C)),
            pl.BlockSpec((tm, 256), lambda i: (jnp.maximum(order(i) - 1, 0), kvb)),
            pl.BlockSpec((CHUNK, 256), lambda i: (0, kvb)),
            _full_spec((1, D_KV)), _full_spec((1, D_KV)), pl.BlockSpec(memory_space=pltpu.SMEM)]


def _swa_fwd(u, qg, kg, sinks):
    lp = u.shape[0]
    tm = TM_SWA

    def body(own_ref, prev_ref, meta_ref, qg_ref, kg_ref, sink_ref, y_ref, og_ref, st_ref):
        i = pl.program_id(0)
        lane = _iota((1, 128), 1)
        stats = jnp.zeros((tm, 128), F32)
        own = own_ref[...].astype(F32)
        _, _, kn, v, lo = _swa_keys(own[:, 1024:1280], prev_ref, meta_ref, kg_ref[...], tm)
        kuse, vuse = _placed(kn, lo), _placed(v, lo)
        bias = jnp.where(_swa_mask(i, tm), 0.0, NEG)
        ones = jnp.ones((N_KEYS, 128), MXU_DTYPE)
        vones = [[jnp.concatenate([vuse[j][e], ones], axis=1) for e in range(2)] for j in range(2)]
        for gi in range(ATT_Q_HEADS // 2):
            j = gi // 2
            sl = slice(gi * 128, (gi + 1) * 128)
            qraw = own[:, sl]
            qs = qraw * _half_rms(qraw, lo) * (qg_ref[...] * ATT_SCALE2)
            og = jnp.zeros((tm, 128), F32)
            for e in range(2):
                qm = jnp.where(lo if e == 0 else ~lo, qs, 0.0)
                s = _dot_nt(qm, kuse[j][e]) + bias
                sk = sink_ref[2 * gi + e] * LOG2E
                m = jnp.maximum(jnp.max(s, axis=-1, keepdims=True), sk)
                p = jnp.exp2((s - m).astype(MXU_DTYPE))
                pv = _dot(p, vones[j][e])
                inv = 1.0 / (pv[:, 128:] + jnp.exp2(sk - m))
                og = og + pv[:, :128] * inv
                hd = 2 * gi + e
                stats = jnp.where(lane == 2 * hd, m, jnp.where(lane == 2 * hd + 1, inv, stats))
            gt = own[:, 512 + gi * 128:512 + (gi + 1) * 128]
            og_ref[:, sl] = og
            y_ref[:, sl] = (og * gt * _sig(gt)).astype(MXU_DTYPE)
        st_ref[...] = stats

    rowspec = pl.BlockSpec((tm, D_ATT), lambda i: (i, 0))
    return pl.pallas_call(
        body, name="swa_fwd", grid=(lp // tm,),
        in_specs=_swa_specs(tm, lambda i: i),
        out_specs=[rowspec, rowspec, pl.BlockSpec((tm, 128), lambda i: (i, 0))],
        out_shape=[jax.ShapeDtypeStruct((lp, D_ATT), MXU_DTYPE), jax.ShapeDtypeStruct((lp, D_ATT), F32),
                   jax.ShapeDtypeStruct((lp, 128), F32)],
        compiler_params=_cp(("arbitrary",)),
    )(u, u, u, qg, kg, sinks)


def _swa_bwd(u, dyc, du, qg, kg, sinks, og_att, stats):
    lp = u.shape[0]
    tm = TM_SWA
    nb = lp // tm
    order = lambda ii: nb - 1 - ii

    def body(own_ref, prev_ref, meta_ref, qg_ref, kg_ref, sink_ref, dy_ref, du_in, og_ref, st_ref, du_ref, dqg_ref,
             dkg_ref, dsk_ref, carry_ref, macc_ref):
        del du_in
        ii = pl.program_id(0)
        i = nb - 1 - ii

        @pl.when(ii == 0)
        def _():
            carry_ref[...] = jnp.zeros_like(carry_ref)
            macc_ref[...] = jnp.zeros_like(macc_ref)
            dqg_ref[...] = jnp.zeros_like(dqg_ref)
            dkg_ref[...] = jnp.zeros_like(dkg_ref)
            dsk_ref[...] = jnp.zeros_like(dsk_ref)

        own = own_ref[...].astype(F32)
        k_raw, krs, kn, v, lo = _swa_keys(own[:, 1024:1280], prev_ref, meta_ref, kg_ref[...], tm)
        kuse, vuse = _placed(kn, lo), _placed(v, lo)
        bias = jnp.where(_swa_mask(i, tm), 0.0, NEG)
        dkn_t = jnp.zeros((D_KV, N_KEYS), F32)
        dvn_t = jnp.zeros((D_KV, N_KEYS), F32)
        for gi in range(ATT_Q_HEADS // 2):
            j = gi // 2
            sl = slice(gi * 128, (gi + 1) * 128)
            qraw = own[:, sl]
            qr = _half_rms(qraw, lo)
            qxh = qraw * qr
            qs = qxh * (qg_ref[...] * ATT_SCALE2)
            ps, invs, pk, qms = [], [], [], []
            og = og_ref[:, sl]
            for e in range(2):
                qm = jnp.where(lo if e == 0 else ~lo, qs, 0.0)
                s = _dot_nt(qm, kuse[j][e]) + bias
                sk = sink_ref[2 * gi + e] * LOG2E
                hd = 2 * gi + e
                m, inv = st_ref[:, 2 * hd:2 * hd + 1], st_ref[:, 2 * hd + 1:2 * hd + 2]
                ps.append(jnp.exp2((s - m).astype(MXU_DTYPE)).astype(F32))
                invs.append(inv)
                pk.append(jnp.exp2(sk - m) * inv)
                qms.append(qm)
            gt = own[:, 512 + gi * 128:512 + (gi + 1) * 128]
            sg = _sig(gt)
            dy = dy_ref[:, sl].astype(F32)
            dgt = dy * og * _dsilu(gt, sg)
            dog = dy * gt * sg
            dqn = jnp.zeros((tm, 128), F32)
            for e in range(2):
                half = lo if e == 0 else ~lo
                dog_m = jnp.where(half, dog, 0.0)
                dl = jnp.sum(dog_m * og, axis=1, keepdims=True)
                dp = _dot_nt(dog_m, vuse[j][e])
                ds = ps[e] * ((dp - dl) * (invs[e] * (1.0 / LOG2E)))
                hsk = 2 * gi + e
                dsk_ref[hsk:hsk + 1, :] += jnp.zeros((1, 128), F32) - jnp.sum(pk[e] * dl, axis=0, keepdims=True)
                dqn = dqn + _dot(ds, kuse[j][e])
                dk_e = _dot_tn(qms[e], ds)
                dv_e = _dot_tn(dog_m * invs[e], ps[e])
                if j != e:
                    dk_e = pltpu.roll(dk_e, ATT_HEAD_DIM, 0)
                    dv_e = pltpu.roll(dv_e, ATT_HEAD_DIM, 0)
                dkn_t = dkn_t + dk_e
                dvn_t = dvn_t + dv_e
            dqn = dqn * ATT_SCALE2
            dqg_ref[...] += jnp.sum(dqn * qxh, axis=0, keepdims=True)
            dqx = dqn * qg_ref[...]
            dq = qr * (dqx - qxh * _half_sum(dqx * qxh, lo) * (1.0 / ATT_HEAD_DIM))
            du_ref[:, sl] = dq.astype(MXU_DTYPE)
            du_ref[:, 512 + gi * 128:512 + (gi + 1) * 128] = dgt.astype(MXU_DTYPE)

        dkn, dvn = dkn_t.T, dvn_t.T
        macc_ref[...] += jnp.concatenate([dkn[0:CHUNK], dvn[0:CHUNK]], axis=1)
        own0 = N_KEYS - tm
        tot = jnp.concatenate([dkn[own0:], dvn[own0:]], axis=1) + carry_ref[...]
        if tm > PREV_ROWS:
            carry_ref[0:tm - PREV_ROWS, :] = jnp.zeros((tm - PREV_ROWS, 2 * D_KV), F32)
        prev0 = own0 - PREV_ROWS
        carry_ref[tm - PREV_ROWS:tm, :] = jnp.concatenate([dkn[prev0:own0], dvn[prev0:own0]], axis=1)
        first = jnp.where((i == 0) & (_iota((tm, 1), 0) < CHUNK), 1.0, 0.0)
        tot = tot + first * jnp.concatenate([macc_ref[...], jnp.zeros((tm - CHUNK, 2 * D_KV), F32)], axis=0)
        dkn_own, dv_own = tot[:, :D_KV], tot[:, D_KV:]
        kx = k_raw[own0:] * krs[own0:]
        dkg_ref[...] += jnp.sum(dkn_own * kx, axis=0, keepdims=True)
        dkx = dkn_own * kg_ref[...]
        dk = krs[own0:] * (dkx - kx * _half_sum(dkx * kx, lo) * (1.0 / ATT_HEAD_DIM))
        du_ref[:, 1024:1152] = dk.astype(MXU_DTYPE)
        du_ref[:, 1152:1280] = dv_own.astype(MXU_DTYPE)
        du_ref[:, 1280:W_C] = jnp.zeros((tm, W_C - 1280), MXU_DTYPE)

    return pl.pallas_call(
        body, name="swa_bwd", grid=(nb,),
        in_specs=_swa_specs(tm, order) + [pl.BlockSpec((tm, D_ATT), lambda ii: (order(ii), 0)),
                                          pl.BlockSpec(memory_space=pl.ANY),
                                          pl.BlockSpec((tm, D_ATT), lambda ii: (order(ii), 0)),
                                          pl.BlockSpec((tm, 128), lambda ii: (order(ii), 0))],
        out_specs=[pl.BlockSpec((tm, W_C), lambda ii: (order(ii), OFF_C // W_C)), _full_spec((1, 128)),
                   _full_spec((1, 128)), _full_spec((ATT_Q_HEADS, 128))],
        out_shape=[jax.ShapeDtypeStruct(du.shape, du.dtype), jax.ShapeDtypeStruct((1, 128), F32),
                   jax.ShapeDtypeStruct((1, 128), F32), jax.ShapeDtypeStruct((ATT_Q_HEADS, 128), F32)],
        scratch_shapes=[pltpu.VMEM((tm, 2 * D_KV), F32), pltpu.VMEM((CHUNK, 2 * D_KV), F32)],
        input_output_aliases={7: 0},
        compiler_params=_cp(("arbitrary",)),
    )(u, u, u, qg, kg, sinks, dyc, du, og_att, stats)


def _load_once(pairs, first):
    @pl.when(first)
    def _():
        for src, dst in pairs:
            pltpu.sync_copy(src, dst)


def _mix_fwd(h, u, ya, yb, yc, wa, wb, wc, wo):
    lp = h.shape[0]
    tm = TM_MIX
    assert lp % tm == 0

    def body(h_ref, g_ref, ya_ref, yb_ref, yc_ref, wa_hbm, wb_hbm, wc_hbm, wo_hbm, out_ref, wa_ref, wb_ref, wc_ref,
             wo_ref):
        _load_once(((wa_hbm, wa_ref), (wb_hbm, wb_ref), (wc_hbm, wc_ref), (wo_hbm, wo_ref)), pl.program_id(0) == 0)
        mixed = jnp.zeros((tm, D_MODEL), F32)
        for n, (y_ref, w_ref) in enumerate(((ya_ref, wa_ref), (yb_ref, wb_ref), (yc_ref, wc_ref))):
            z = jnp.dot(y_ref[...], w_ref[...], preferred_element_type=F32)
            mixed = mixed + _sig(g_ref[:, n * D_MODEL:(n + 1) * D_MODEL].astype(F32)) * z
        out_ref[...] = h_ref[...] + _dot(mixed, wo_ref[...])

    ybs = pl.BlockSpec((tm, 512), lambda i: (i, 0))
    anyspec = pl.BlockSpec(memory_space=pl.ANY)
    return pl.pallas_call(
        body, name="mix_fwd", grid=(lp // tm,),
        in_specs=[pl.BlockSpec((tm, D_MODEL), lambda i: (i, 0)), pl.BlockSpec((tm, W_G), lambda i: (i, OFF_G // W_G)),
                  ybs, ybs, ybs, anyspec, anyspec, anyspec, anyspec],
        out_specs=pl.BlockSpec((tm, D_MODEL), lambda i: (i, 0)),
        out_shape=jax.ShapeDtypeStruct((lp, D_MODEL), F32),
        scratch_shapes=[pltpu.VMEM((512, D_MODEL), MXU_DTYPE)] * 3 + [pltpu.VMEM((D_MODEL, D_MODEL), MXU_DTYPE)],
        compiler_params=_cp(("arbitrary",)),
    )(h, u, ya, yb, yc, wa, wb, wc, wo)


def _mix_bwd(dh, u, ya, yb, yc, wa, wb, wc, wo):
    lp = dh.shape[0]
    tm = TM_BR
    nb = lp // tm

    def body(dh_ref, g_ref, ya_ref, yb_ref, yc_ref, wa_hbm, wb_hbm, wc_hbm, wo_hbm,
             du_ref, dya_ref, dyb_ref, dyc_ref, dwa_hbm, dwb_hbm, dwc_hbm, dwo_hbm,
             wa_ref, wb_ref, wc_ref, wo_ref, dwa_ref, dwb_ref, dwc_ref, dwo_ref):
        i = pl.program_id(0)
        _load_once(((wa_hbm, wa_ref), (wb_hbm, wb_ref), (wc_hbm, wc_ref), (wo_hbm, wo_ref)), i == 0)

        @pl.when(i == 0)
        def _():
            for r in (dwa_ref, dwb_ref, dwc_ref, dwo_ref):
                r[...] = jnp.zeros_like(r)

        dh_b = dh_ref[...].astype(MXU_DTYPE)
        dmixed = _dot_nt(dh_b, wo_ref[...])
        mixed = jnp.zeros((tm, D_MODEL), F32)
        for n, (y_ref, w_ref, dy_ref, dw_ref) in enumerate(((ya_ref, wa_ref, dya_ref, dwa_ref),
                                                            (yb_ref, wb_ref, dyb_ref, dwb_ref),
                                                            (yc_ref, wc_ref, dyc_ref, dwc_ref))):
            y = y_ref[...]
            z = jnp.dot(y, w_ref[...], preferred_element_type=F32)
            gate = _sig(g_ref[:, n * D_MODEL:(n + 1) * D_MODEL].astype(F32))
            mixed = mixed + gate * z
            du_ref[:, n * D_MODEL:(n + 1) * D_MODEL] = (z * dmixed * gate * (1.0 - gate)).astype(MXU_DTYPE)
            dz = (gate * dmixed).astype(MXU_DTYPE)
            dy_ref[...] = _dot_nt(dz, w_ref[...]).astype(MXU_DTYPE)
            dw_ref[...] += _dot_tn(y, dz)
        dwo_ref[...] += _dot_tn(mixed, dh_b)

        @pl.when(i == nb - 1)
        def _():
            for src, dst in ((dwa_ref, dwa_hbm), (dwb_ref, dwb_hbm), (dwc_ref, dwc_hbm), (dwo_ref, dwo_hbm)):
                pltpu.sync_copy(src, dst)

    ybs = pl.BlockSpec((tm, 512), lambda i: (i, 0))
    anyspec = pl.BlockSpec(memory_space=pl.ANY)
    wsh = jax.ShapeDtypeStruct((512, D_MODEL), F32)
    return pl.pallas_call(
        body, name="mix_bwd", grid=(nb,),
        in_specs=[pl.BlockSpec((tm, D_MODEL), lambda i: (i, 0)), pl.BlockSpec((tm, W_G), lambda i: (i, OFF_G // W_G)),
                  ybs, ybs, ybs, anyspec, anyspec, anyspec, anyspec],
        out_specs=[pl.BlockSpec((tm, W_G), lambda i: (i, OFF_G // W_G)), ybs, ybs, ybs, anyspec, anyspec, anyspec, anyspec],
        out_shape=[jax.ShapeDtypeStruct((lp, NP), MXU_DTYPE)] + [jax.ShapeDtypeStruct((lp, 512), MXU_DTYPE)] * 3
        + [wsh, wsh, wsh, jax.ShapeDtypeStruct((D_MODEL, D_MODEL), F32)],
        scratch_shapes=[pltpu.VMEM((512, D_MODEL), MXU_DTYPE)] * 3 + [pltpu.VMEM((D_MODEL, D_MODEL), MXU_DTYPE)]
        + [pltpu.VMEM((512, D_MODEL), F32)] * 3 + [pltpu.VMEM((D_MODEL, D_MODEL), F32)],
        compiler_params=_cp(("arbitrary",)),
    )(dh, u, ya, yb, yc, wa, wb, wc, wo)


def _loss_head(h, target_p, seq):
    lp = h.shape[0]
    tm = TM_MIX
    assert lp % tm == 0

    def body(h_ref, t_ref, dh_ref, loss_ref):
        i = pl.program_id(0)

        @pl.when(i == 0)
        def _():
            loss_ref[...] = jnp.zeros_like(loss_ref)

        rows = i * tm + _iota((tm, 1), 0)
        e = jnp.where((rows >= CHUNK) & (rows < CHUNK + seq), h_ref[...] - t_ref[...], 0.0)
        dh_ref[...] = e * (1.0 / D_MODEL)
        part = jnp.sum(jnp.mean(e * e, axis=-1, keepdims=True), axis=0, keepdims=True)
        loss_ref[...] += 0.5 * part

    return pl.pallas_call(
        body, name="loss_head", grid=(lp // tm,),
        in_specs=[pl.BlockSpec((tm, D_MODEL), lambda i: (i, 0))] * 2,
        out_specs=[pl.BlockSpec((tm, D_MODEL), lambda i: (i, 0)), _full_spec((1, 128))],
        out_shape=[jax.ShapeDtypeStruct((lp, D_MODEL), F32), jax.ShapeDtypeStruct((1, 128), F32)],
        compiler_params=_cp(("arbitrary",)),
    )(h, target_p)


def _lb_rows(p_ref):
    depth = p_ref.shape[0]
    rows = [p_ref[l:l + 1, :] for l in range(depth)]
    mx = functools.reduce(jnp.maximum, rows)
    ex = [jnp.exp(r - mx) for r in rows]
    tot = functools.reduce(jnp.add, ex)
    sm = [e / tot for e in ex]
    cs, run = [], jnp.zeros_like(sm[0])
    for l in range(depth):
        run = run + sm[l]
        cs.append(run)
    return sm, [c - sm[0] for c in cs]


def _lb_fwd(p):
    def body(p_ref, o_ref):
        _, xs = _lb_rows(p_ref)
        for l, xl in enumerate(xs):
            o_ref[l:l + 1, :] = jnp.clip(xl, 0.0, 1.0)

    return pl.pallas_call(body, name="lb_fwd", out_shape=jax.ShapeDtypeStruct(p.shape, F32))(p)


def _lb_bwd(p, dlb):
    def body(p_ref, d_ref, o_ref):
        sm, xs = _lb_rows(p_ref)
        depth = len(xs)
        dx = []
        for l in range(depth):
            x = xs[l]
            g0 = jnp.where(x > 0.0, 1.0, jnp.where(x == 0.0, 0.5, 0.0))
            y = jnp.maximum(x, 0.0)
            g1 = jnp.where(y < 1.0, 1.0, jnp.where(y == 1.0, 0.5, 0.0))
            dx.append(d_ref[l:l + 1, :] * g0 * g1)
        dsm = [functools.reduce(jnp.add, dx[jj:]) for jj in range(depth)]
        dsm[0] = dsm[0] - functools.reduce(jnp.add, dx)
        inner = functools.reduce(jnp.add, [a * b for a, b in zip(sm, dsm)])
        for l in range(depth):
            o_ref[l:l + 1, :] = sm[l] * (dsm[l] - inner)

    return pl.pallas_call(body, name="lb_bwd", out_shape=jax.ShapeDtypeStruct(p.shape, F32))(p, dlb)


def _exchange(gather, scatter, name):
    ng, ns = len(gather), len(scatter)
    n = ng + ns

    def body(*refs):
        x_refs, o_refs, sems = refs[:n], refs[n:2 * n], refs[2 * n:]
        exs = []
        if ng:
            exs.append(_Exchange(x_refs[:ng], o_refs[:ng], *sems[:3], scatter=False))
        if ns:
            exs.append(_Exchange(x_refs[ng:], o_refs[ng:], *sems[-3:], scatter=True))
        for ex in exs:
            ex.start()
        for ex in exs:
            ex.finish()

    out_shape = [jax.ShapeDtypeStruct((N_DEV,) + x.shape, x.dtype) for x in gather]
    out_shape += [jax.ShapeDtypeStruct(x.shape, x.dtype) for x in scatter]
    return pl.pallas_call(
        body, name=name, in_specs=[_ANY] * n, out_specs=[_ANY] * n, out_shape=out_shape,
        scratch_shapes=(_exchange_sems(ng) if ng else []) + (_exchange_sems(ns) if ns else []),
        compiler_params=pltpu.CompilerParams(has_side_effects=True),
    )(*gather, *scatter)


def _gather_two_level(xs, name):
    n = len(xs)

    def body(*refs):
        x_refs, o_refs = refs[:n], refs[n:2 * n]
        send_sems, recv_sems, loc_sems = refs[2 * n:]
        x, y, c = lax.axis_index("x"), lax.axis_index("y"), lax.axis_index("c")
        chips = [(1 - x, y), (x, 1 - y), (1 - x, 1 - y)]
        idx = lambda px, py, pc: 4 * px + 2 * py + pc

        def copy(a, k, block, to, src=None):
            slot = o_refs[a].at[idx(*block)]
            return pltpu.make_async_remote_copy(src_ref=slot if src is None else src, dst_ref=slot,
                                                send_sem=send_sems.at[a, k], recv_sem=recv_sems.at[a, k],
                                                device_id=to, device_id_type=pl.DeviceIdType.MESH)

        me, sib = (x, y, c), (x, y, 1 - c)
        local = [pltpu.make_async_copy(x_refs[a], o_refs[a].at[idx(*me)], loc_sems.at[a]) for a in range(n)]
        first = [copy(a, 0, me, sib, src=x_refs[a]) for a in range(n)]
        first += [copy(a, 1 + j, me, (*chip, c), src=x_refs[a]) for j, chip in enumerate(chips) for a in range(n)]
        for cp in local + first:
            cp.start()
        passed = []
        for j, chip in enumerate(chips):
            for a in range(n):
                copy(a, 1 + j, (*chip, c), me).wait_recv()
                cp = copy(a, 4 + j, (*chip, c), sib)
                cp.start()
                passed.append(cp)
        for a in range(n):
            copy(a, 0, sib, me).wait_recv()
            for j, chip in enumerate(chips):
                copy(a, 4 + j, (*chip, 1 - c), me).wait_recv()
        for cp in first + passed:
            cp.wait_send()
        for cp in local:
            cp.wait()

    return pl.pallas_call(
        body, name=name, in_specs=[_ANY] * n, out_specs=[_ANY] * n,
        out_shape=[jax.ShapeDtypeStruct((N_DEV,) + x.shape, x.dtype) for x in xs],
        scratch_shapes=_exchange_sems(n), compiler_params=pltpu.CompilerParams(has_side_effects=True),
    )(*xs)


def _adamw(gp, w, m, v, name):
    r, cc = w.shape
    tr = 256 if r % 256 == 0 else r

    def body(g_ref, w_ref, m_ref, v_ref, go_ref, d_ref, mo_ref, vo_ref):
        g = g_ref[0].astype(F32)
        for s in range(1, N_DEV):
            g = g + g_ref[s].astype(F32)
        go_ref[...] = g
        mn = ADAM_B1 * m_ref[...] + (1.0 - ADAM_B1) * g
        vn = ADAM_B2 * v_ref[...] + (1.0 - ADAM_B2) * (g * g)
        m_hat = mn / (1.0 - ADAM_B1 ** ADAM_STEP)
        v_hat = vn / (1.0 - ADAM_B2 ** ADAM_STEP)
        d_ref[...] = -ADAM_LR * (m_hat / (jnp.sqrt(v_hat) + ADAM_EPS) + ADAM_WD * w_ref[...])
        mo_ref[...] = mn
        vo_ref[...] = vn

    bs = pl.BlockSpec((tr, cc), lambda i: (i, 0))
    sh = jax.ShapeDtypeStruct((r, cc), F32)
    return pl.pallas_call(
        body, name=name, grid=(r // tr,),
        in_specs=[pl.BlockSpec((N_DEV, tr, cc), lambda i: (0, i, 0)), bs, bs, bs],
        out_specs=[bs, bs, bs, bs], out_shape=[sh, sh, sh, sh],
        compiler_params=_cp(("parallel",)),
    )(gp, w, m, v)


def _pack_cols(w):
    parts, pos = [], 0
    for pstart, ostart, width in _PACK:
        if pstart != pos:
            parts.append(jnp.zeros(w.shape[:-1] + (pstart - pos,), w.dtype))
        parts.append(w[..., ostart:ostart + width])
        pos = pstart + width
    return jnp.concatenate(parts, axis=-1)


def _unpack_cols(wp):
    by_orig = sorted(_PACK, key=lambda t: t[1])
    return jnp.concatenate([wp[..., p:p + wd] for p, _, wd in by_orig], axis=-1)


_LAYER_SHARDED = ("w_in", "conv_w", "w_conv_out", "w_hg_out", "w_att_out", "w_out")
_NARROW = ("w_in", "w_conv_out", "w_hg_out", "w_att_out", "w_out")
_REPLICATED = ("norm_g", "conv_b", "conv_ln_g", "conv_ln_b", "hg_lower_bounds", "hg_norm_g", "q_norm_g", "k_norm_g",
               "attn_sinks")
_WEIGHTS = ("meta_tokens", "norm_g", "w_in", "conv_w", "conv_b", "conv_ln_g", "conv_ln_b", "w_conv_out",
            "hg_lower_bounds", "hg_norm_g", "w_hg_out", "q_norm_g", "k_norm_g", "attn_sinks", "w_att_out", "w_out")
_ROW_SHARDED = ("w_out",)


def _assemble(name, g):
    if name in _ROW_SHARDED:
        return g.reshape((N_DEV * g.shape[1],) + g.shape[2:])
    full = jnp.moveaxis(g, 0, -2)
    return full.reshape(full.shape[:-2] + (N_DEV * full.shape[-1],))


def _split(name, full):
    if name in _ROW_SHARDED:
        return full.reshape((N_DEV, full.shape[0] // N_DEV) + full.shape[1:])
    c = full.shape[-1] // N_DEV
    return jnp.moveaxis(full.reshape(full.shape[:-1] + (N_DEV, c)), -2, 0)


def _layer_weights(gathered):
    full = {k: _assemble(k, g) for k, g in zip(_LAYER_SHARDED, gathered)}
    wp = _pack_cols(full["w_in"])
    return dict(wp=wp, wpt=wp.T, cw=full["conv_w"], wa=full["w_conv_out"], wb=full["w_hg_out"],
                wc=full["w_att_out"], wo=full["w_out"])


def _layer_fwd(h, lw, sp, gather):
    u, hn, gathered = _inproj_fwd(h, sp["norm_g"], lw["wp"], gather)
    ya, y_conv, yb, states = _conv_hgrn_fwd(u, lw["cw"], sp["conv_b"], sp["conv_ln_g"], sp["conv_ln_b"], sp["lb"],
                                            sp["hg_norm_g"])
    yc, og_att, st_att = _swa_fwd(u, sp["qg"], sp["kg"], sp["sinks"])
    h_next = _mix_fwd(h, u, ya, yb, yc, lw["wa"], lw["wb"], lw["wc"], lw["wo"])
    return h_next, (h, u, hn, ya, yb, yc, states, y_conv, og_att, st_att), gathered


def _layer_bwd(dh, saved, lw, sp, stacked, layer, depth):
    h_l, u, hn, ya, yb, yc, states, y_conv, og_att, st_att = saved
    du, dya, dyb, dyc, dwa, dwb, dwc, dwo = _mix_bwd(dh, u, ya, yb, yc, lw["wa"], lw["wb"], lw["wc"], lw["wo"])
    du, dy, dlg, dlb_ = _conv_bwd1(u, y_conv, dya, du, sp["conv_ln_g"], sp["conv_ln_b"])
    du, dcw, dcb = _conv_bwd2(u, dy, du, lw["cw"])
    du, dlbl, dgg = _hgrn_bwd(u, dyb, states, du, sp["lb"], sp["hg_norm_g"])
    du, dqg, dkg, dsk = _swa_bwd(u, dyc, du, sp["qg"], sp["kg"], sp["sinks"], og_att, st_att)
    dwp = _inproj_bwd_dw(hn, du)
    full = dict(w_in=_unpack_cols(dwp), conv_w=dcw, w_conv_out=dwa, w_hg_out=dwb, w_att_out=dwc, w_out=dwo)
    pieces = [_split(k, full[k]).astype(WIRE_DTYPE) for k in _LAYER_SHARDED]
    dh, dng, stacked = _inproj_bwd_dh(du, lw["wpt"], h_l, sp["norm_g"], dh, pieces, stacked, layer, depth)
    fold = lambda a: a[0, :ATT_HEAD_DIM] + a[0, ATT_HEAD_DIM:]
    small = dict(norm_g=dng[0], conv_b=dcb[0], conv_ln_g=dlg[0], conv_ln_b=dlb_[0], hg_lower_bounds=dlbl[0],
                 hg_norm_g=dgg[0], q_norm_g=fold(dqg), k_norm_g=fold(dkg), attn_sinks=dsk[:, 0])
    return dh, small, stacked


def _as2d(a):
    return a.reshape((-1, a.shape[-1]))


def kernel(x, meta_tokens, norm_g, w_in, conv_w, conv_b, conv_ln_g, conv_ln_b, w_conv_out, hg_lower_bounds, hg_norm_g, w_hg_out, q_norm_g, k_norm_g, attn_sinks, w_att_out, w_out, loss_target, m_meta_tokens, m_norm_g, m_w_in, m_conv_w, m_conv_b, m_conv_ln_g, m_conv_ln_b, m_w_conv_out, m_hg_lower_bounds, m_hg_norm_g, m_w_hg_out, m_q_norm_g, m_k_norm_g, m_attn_sinks, m_w_att_out, m_w_out, v_meta_tokens, v_norm_g, v_w_in, v_conv_w, v_conv_b, v_conv_ln_g, v_conv_ln_b, v_w_conv_out, v_hg_lower_bounds, v_hg_norm_g, v_w_hg_out, v_q_norm_g, v_k_norm_g, v_attn_sinks, v_w_att_out, v_w_out):
    w = dict(meta_tokens=meta_tokens, norm_g=norm_g, w_in=w_in, conv_w=conv_w, conv_b=conv_b, conv_ln_g=conv_ln_g,
             conv_ln_b=conv_ln_b, w_conv_out=w_conv_out, hg_lower_bounds=hg_lower_bounds, hg_norm_g=hg_norm_g,
             w_hg_out=w_hg_out, q_norm_g=q_norm_g, k_norm_g=k_norm_g, attn_sinks=attn_sinks, w_att_out=w_att_out,
             w_out=w_out)
    m = dict(meta_tokens=m_meta_tokens, norm_g=m_norm_g, w_in=m_w_in, conv_w=m_conv_w, conv_b=m_conv_b,
             conv_ln_g=m_conv_ln_g, conv_ln_b=m_conv_ln_b, w_conv_out=m_w_conv_out, hg_lower_bounds=m_hg_lower_bounds,
             hg_norm_g=m_hg_norm_g, w_hg_out=m_w_hg_out, q_norm_g=m_q_norm_g, k_norm_g=m_k_norm_g,
             attn_sinks=m_attn_sinks, w_att_out=m_w_att_out, w_out=m_w_out)
    v = dict(meta_tokens=v_meta_tokens, norm_g=v_norm_g, w_in=v_w_in, conv_w=v_conv_w, conv_b=v_conv_b,
             conv_ln_g=v_conv_ln_g, conv_ln_b=v_conv_ln_b, w_conv_out=v_w_conv_out, hg_lower_bounds=v_hg_lower_bounds,
             hg_norm_g=v_hg_norm_g, w_hg_out=v_w_hg_out, q_norm_g=v_q_norm_g, k_norm_g=v_k_norm_g,
             attn_sinks=v_attn_sinks, w_att_out=v_w_att_out, w_out=v_w_out)

    depth = norm_g.shape[0]
    seq = x.shape[1]
    lp = -(-(seq + CHUNK) // TM_MM) * TM_MM
    tail = lp - seq - CHUNK
    zeros = lambda n: jnp.zeros((n, D_MODEL), F32)

    def shards(l):
        return [w[k][l].astype(MXU_DTYPE) if k in _NARROW else w[k][l] for k in _LAYER_SHARDED]

    first = _gather_two_level(shards(0) + [meta_tokens], "gather_first")
    gathered, meta_full = first[:-1], _assemble("meta_tokens", first[-1])
    h = jnp.concatenate([zeros(META_PAD), meta_full, x[0], zeros(tail)], axis=0)
    target_p = jnp.concatenate([zeros(CHUNK), loss_target[0], zeros(tail)], axis=0)

    lb_all = _lb_fwd(hg_lower_bounds)
    tile2 = lambda a: jnp.concatenate([a, a], axis=-1)
    row = lambda a, l: a[l][None, :]

    def small_rows(l):
        sp = {k: row(w[k], l) for k in ("norm_g", "conv_b", "conv_ln_g", "conv_ln_b", "hg_norm_g")}
        sp.update(lb=row(lb_all, l), qg=tile2(row(q_norm_g, l)), kg=tile2(row(k_norm_g, l)), sinks=attn_sinks[l])
        return sp

    layer_w, saved = [], []
    for l in range(depth):
        layer_w.append(_layer_weights(gathered))
        h, sv, gathered = _layer_fwd(h, layer_w[l], small_rows(l), shards(l + 1) if l + 1 < depth else [])
        saved.append(sv)

    dh, loss_row = _loss_head(h, target_p, seq)
    loss = lax.psum(loss_row[0, 0], ("x", "y", "c"))

    stacked, small_grads = None, [None] * depth
    for l in reversed(range(depth)):
        dh, small_grads[l], stacked = _layer_bwd(dh, saved[l], layer_w[l], small_rows(l), stacked, l, depth)
    grad_x = dh[CHUNK:CHUNK + seq]
    grads = {k: jnp.stack([small_grads[l][k] for l in range(depth)]) for k in _REPLICATED}
    grads["hg_lower_bounds"] = _lb_bwd(hg_lower_bounds, grads["hg_lower_bounds"])

    small = jnp.concatenate([grads[k].reshape(-1) for k in _REPLICATED])
    small = jnp.concatenate([small, jnp.zeros((-small.shape[0] % 128,), F32)]).reshape(-1, 128)
    small_all, meta_pieces = _exchange([small], [_split("meta_tokens", dh[META_PAD:CHUNK])], "exchange_small_grads")
    small_all = small_all.reshape(N_DEV, -1)

    out_g, out_d, out_m, out_v = {}, {}, {}, {}
    for k, gp in zip(("meta_tokens",) + _LAYER_SHARDED, [meta_pieces] + stacked):
        shp = w[k].shape
        res = _adamw(gp.reshape((N_DEV,) + _as2d(w[k]).shape), _as2d(w[k]), _as2d(m[k]), _as2d(v[k]), "adamw_" + k)
        out_g[k], out_d[k], out_m[k], out_v[k] = (r.reshape(shp) for r in res)
    off = 0
    for k in _REPLICATED:
        shp = w[k].shape
        n = w[k].size
        gp = small_all[:, off:off + n].reshape((N_DEV,) + shp)
        off += n
        res = _adamw(gp, w[k], m[k], v[k], "adamw_" + k)
        out_g[k], out_d[k], out_m[k], out_v[k] = res

    return (loss, grad_x[None], *[out_g[k] for k in _WEIGHTS], *[out_d[k] for k in _WEIGHTS],
            *[out_m[k] for k in _WEIGHTS], *[out_v[k] for k in _WEIGHTS])
```
